```python
import math
import jax, jax.numpy as jnp
from jax import lax
import numpy as np

D_MODEL = 1024
BATCH = 8
SEQ = 2048
DEPTH = 2
DEC_BATCH = 128
DEC_SEQ = 8
PAST_LEN = 16384
PAGE_SIZE = 128

GLA_WIDTH = D_MODEL // 2
CONV_WIDTH = D_MODEL - GLA_WIDTH
GLA_HEADS = 4
GLA_DV = GLA_WIDTH // GLA_HEADS
GLA_DK = GLA_DV // 2
GLA_LOWRANK = 16
GLA_TAU = 16.0
GLA_CHUNK = 64
CONV_K = 31
PLE_DIM = 256
N_GROUPS = 4
EXPERTS_PER_GROUP = 8
EXPERT_FF = 256
TOP_K = 2
EPS = 1e-6

Q_COLS = GLA_HEADS * GLA_DK
V_COLS = GLA_HEADS * GLA_DV
SPLITS = (Q_COLS, 2 * Q_COLS, 2 * Q_COLS + V_COLS, 2 * Q_COLS + 2 * V_COLS,
          2 * Q_COLS + 2 * V_COLS + GLA_LOWRANK)
IN_COLS = 2 * Q_COLS + 2 * V_COLS + GLA_LOWRANK + 2 * CONV_WIDTH

kernel_name = "hymba_gla_conformer_hmoe_step"


def rmsnorm(x, g):
    xf = x.astype(jnp.float32)
    y = xf * lax.rsqrt(jnp.mean(xf * xf, axis=-1, keepdims=True) + EPS)
    return (y * g.astype(jnp.float32)).astype(x.dtype)


def layernorm(x, g, b):
    xf = x.astype(jnp.float32)
    mu = jnp.mean(xf, axis=-1, keepdims=True)
    xc = xf - mu
    y = xc * lax.rsqrt(jnp.mean(xc * xc, axis=-1, keepdims=True) + EPS)
    return (y * g.astype(jnp.float32) + b.astype(jnp.float32)).astype(x.dtype)


def gla_chunked(q, k, v, log_a, s0):
    B, T, H, _ = q.shape
    C = math.gcd(T, GLA_CHUNK)
    n = T // C

    def to_chunks(a):
        return jnp.moveaxis(a.astype(jnp.float32).reshape(B, n, C, H, a.shape[-1]), 1, 0)

    causal = jnp.tril(jnp.ones((C, C), dtype=bool))[None, :, :, None, None]

    def step(S, blk):
        qb, kb, vb, ab = blk
        b = jnp.cumsum(ab, axis=1)
        decay = jnp.exp(jnp.where(causal, b[:, :, None] - b[:, None, :], -jnp.inf))
        scores = jnp.sum(qb[:, :, None] * decay * kb[:, None], axis=-1)
        o_intra = jnp.einsum('btsh,bshv->bthv', scores, vb)
        o_inter = jnp.einsum('bthk,bhkv->bthv', qb * jnp.exp(b), S)
        b_last = b[:, -1]
        S_new = jnp.exp(b_last)[..., None] * S + jnp.einsum(
            'bshk,bshv->bhkv', kb * jnp.exp(b_last[:, None] - b), vb)
        return S_new, o_intra + o_inter

    S, o = lax.scan(step, s0.astype(jnp.float32),
                    (to_chunks(q), to_chunks(k), to_chunks(v), to_chunks(log_a)))
    return jnp.moveaxis(o, 0, 1).reshape(B, T, H, -1), S


def causal_depthwise_conv(u, buf, w, b):
    full = jnp.concatenate([buf.astype(u.dtype), u], axis=1)
    y = lax.conv_general_dilated(full, w[:, None, :].astype(u.dtype), window_strides=(1,),
                                 padding='VALID', dimension_numbers=('NWC', 'WIO', 'NWC'),
                                 feature_group_count=u.shape[-1])
    return y + b, full[:, -(CONV_K - 1):]


def hier_moe(x, w_grp, b_grp, w_er, b_er, w_gate, w_up, w_down):
    N = x.shape[0]
    xf = x.astype(jnp.float32)
    grp_logits = xf @ w_grp.astype(jnp.float32) + b_grp.astype(jnp.float32)
    grp_probs = jax.nn.softmax(grp_logits, axis=-1)
    g_sel = jnp.argmax(grp_logits, axis=-1)
    g_w = jnp.take_along_axis(grp_probs, g_sel[:, None], axis=-1)
    exp_logits = jnp.einsum('nd,gde->nge', xf, w_er.astype(jnp.float32)) + b_er.astype(jnp.float32)
    sel_logits = jnp.take_along_axis(exp_logits, g_sel[:, None, None], axis=1)[:, 0]
    top_v, top_i = lax.top_k(sel_logits, TOP_K)
    top_w = jax.nn.softmax(top_v, axis=-1) * g_w
    within = jnp.sum(jax.nn.one_hot(top_i, EXPERTS_PER_GROUP) * top_w[..., None], axis=1)
    combine = (jax.nn.one_hot(g_sel, N_GROUPS)[:, :, None] * within[:, None, :]).astype(x.dtype)
    y = jnp.zeros_like(x)
    for gi in range(N_GROUPS):
        hg = jax.nn.silu(jnp.einsum('nd,edf->nef', x, w_gate[gi])) * jnp.einsum('nd,edf->nef', x, w_up[gi])
        y = y + jnp.einsum('nef,efd->nd', hg * combine[:, gi, :, None], w_down[gi])
    return y


def layer(h, p_l, s0, buf0, g_mix, w_in, w_forget_up, b_forget, g_gla_out, w_conv, b_conv,
          g_conv_ln, b_conv_ln, w_out, g_ffn, w_grp_router, b_grp_router, w_exp_router,
          b_exp_router, w_exp_gate, w_exp_up, w_exp_down, g_ple, w_ple_gate, w_ple_proj):
    B, T, _ = h.shape
    xn = rmsnorm(h, g_mix)
    z = xn @ w_in
    q, k, v, og, lr, cv = jnp.split(z, SPLITS, axis=-1)
    q = q.reshape(B, T, GLA_HEADS, GLA_DK) * (GLA_DK ** -0.5)
    k = k.reshape(B, T, GLA_HEADS, GLA_DK)
    v = v.reshape(B, T, GLA_HEADS, GLA_DV)
    log_a = jax.nn.log_sigmoid((lr @ w_forget_up + b_forget).astype(jnp.float32)) / GLA_TAU
    o, s_new = gla_chunked(q, k, v, log_a.reshape(B, T, GLA_HEADS, GLA_DK), s0)
    o = rmsnorm(o.astype(h.dtype), g_gla_out) * jax.nn.silu(og.reshape(B, T, GLA_HEADS, GLA_DV))
    o = o.reshape(B, T, GLA_WIDTH)
    a, gate = jnp.split(cv, 2, axis=-1)
    u = a * jax.nn.sigmoid(gate)
    c, buf_new = causal_depthwise_conv(u, buf0, w_conv, b_conv)
    c = jax.nn.silu(layernorm(c, g_conv_ln, b_conv_ln))
    h = h + jnp.concatenate([o, c], axis=-1) @ w_out
    xn = rmsnorm(h, g_ffn).reshape(B * T, D_MODEL)
    h = h + hier_moe(xn, w_grp_router, b_grp_router, w_exp_router, b_exp_router,
                     w_exp_gate, w_exp_up, w_exp_down).reshape(B, T, D_MODEL)
    xn = rmsnorm(h, g_ple)
    h = h + jax.nn.sigmoid(xn @ w_ple_gate) * (p_l @ w_ple_proj)
    return h, s_new, buf_new


def trunk(x, p, s_init, buf_init, layer_weights, g_final):
    h = x
    states, bufs = [], []
    for l in range(DEPTH):
        h, s, buf = layer(h, p[l], s_init[l], buf_init[l], *(w[l] for w in layer_weights))
        states.append(s)
        bufs.append(buf)
    return rmsnorm(h, g_final), jnp.stack(states), jnp.stack(bufs)


def setup_inputs(seed: int = 0) -> dict:
    key = jax.random.key(seed)
    ks = iter(jax.random.split(key, 40))

    def nrm(shape, scale=1.0):
        return jax.random.normal(next(ks), shape, jnp.float32) * scale

    def gain(shape):
        return 1.0 + nrm(shape, 0.01)

    G, E, F, D = N_GROUPS, EXPERTS_PER_GROUP, EXPERT_FF, D_MODEL
    return {
        "x_prompt": nrm((BATCH, SEQ, D)),
        "x_sample": nrm((DEC_BATCH, DEC_SEQ, D)),
        "state_gla": nrm((DEPTH, DEC_BATCH, GLA_HEADS, GLA_DK, GLA_DV)),
        "state_conv": nrm((DEPTH, DEC_BATCH, CONV_K - 1, CONV_WIDTH), 0.5),
        "p_prompt": nrm((DEPTH, BATCH, SEQ, PLE_DIM)),
        "p_sample": nrm((DEPTH, DEC_BATCH, DEC_SEQ, PLE_DIM)),
        "g_mix": gain((DEPTH, D)),
        "w_in": nrm((DEPTH, D, IN_COLS), D ** -0.5),
        "w_forget_up": nrm((DEPTH, GLA_LOWRANK, Q_COLS), GLA_LOWRANK ** -0.5),
        "b_forget": nrm((DEPTH, Q_COLS), 0.1),
        "g_gla_out": gain((DEPTH, GLA_DV)),
        "w_conv": nrm((DEPTH, CONV_K, CONV_WIDTH), CONV_K ** -0.5),
        "b_conv": nrm((DEPTH, CONV_WIDTH), 0.01),
        "g_conv_ln": gain((DEPTH, CONV_WIDTH)),
        "b_conv_ln": nrm((DEPTH, CONV_WIDTH), 0.01),
        "w_out": nrm((DEPTH, D, D), D ** -0.5),
        "g_ffn": gain((DEPTH, D)),
        "w_grp_router": nrm((DEPTH, D, G), D ** -0.5),
        "b_grp_router": nrm((DEPTH, G), 0.01),
        "w_exp_router": nrm((DEPTH, G, D, E), D ** -0.5),
        "b_exp_router": nrm((DEPTH, G, E), 0.01),
        "w_exp_gate": nrm((DEPTH, G, E, D, F), D ** -0.5),
        "w_exp_up": nrm((DEPTH, G, E, D, F), D ** -0.5),
        "w_exp_down": nrm((DEPTH, G, E, F, D), F ** -0.5),
        "g_ple": gain((DEPTH, D)),
        "w_ple_gate": nrm((DEPTH, D, D), D ** -0.5),
        "w_ple_proj": nrm((DEPTH, PLE_DIM, D), PLE_DIM ** -0.5),
        "g_final": gain((D,)),
    }


def reference(x_prompt, x_sample, state_gla, state_conv, p_prompt, p_sample, g_mix, w_in,
              w_forget_up, b_forget, g_gla_out, w_conv, b_conv, g_conv_ln, b_conv_ln, w_out,
              g_ffn, w_grp_router, b_grp_router, w_exp_router, b_exp_router, w_exp_gate,
              w_exp_up, w_exp_down, g_ple, w_ple_gate, w_ple_proj, g_final):
    layer_weights = (g_mix, w_in, w_forget_up, b_forget, g_gla_out, w_conv, b_conv, g_conv_ln,
                     b_conv_ln, w_out, g_ffn, w_grp_router, b_grp_router, w_exp_router,
                     b_exp_router, w_exp_gate, w_exp_up, w_exp_down, g_ple, w_ple_gate, w_ple_proj)
    n_prompt = x_prompt.shape[0]
    s_zero = jnp.zeros((DEPTH, n_prompt, GLA_HEADS, GLA_DK, GLA_DV), jnp.float32)
    buf_zero = jnp.zeros((DEPTH, n_prompt, CONV_K - 1, CONV_WIDTH), x_prompt.dtype)
    y_prompt, sg_prompt, sc_prompt = trunk(x_prompt, p_prompt, s_zero, buf_zero, layer_weights, g_final)
    y_sample, sg_sample, sc_sample = trunk(x_sample, p_sample, state_gla, state_conv, layer_weights, g_final)
    return (y_prompt, y_sample, sg_prompt, sg_sample, sc_prompt, sc_sample)
```

```python
import functools

import jax
import jax.numpy as jnp
from jax import lax
from jax.experimental import pallas as pl
from jax.experimental.pallas import tpu as pltpu

D_MODEL = 1024
GLA_HEADS = 4
GLA_DK = 64
GLA_DV = 128
QK_COLS = GLA_HEADS * GLA_DK
V_COLS = GLA_HEADS * GLA_DV
CONV_WIDTH = 512
CONV_K = 31
GLA_LOWRANK = 16
GLA_TAU = 16.0
GLA_CHUNK = 64
PLE_DIM = 256
N_GROUPS = 4
EXPERTS_PER_GROUP = 8
N_EXPERTS = N_GROUPS * EXPERTS_PER_GROUP
EXPERT_FF = 256
EPS = 1e-6

LANES = 128
SUBLANES = 8
CONV_PAD = 32
CONV_OFF = CONV_PAD - (CONV_K - 1)
VMEM_LIMIT = 56 * 1024 * 1024

F32 = jnp.float32
BF16 = jnp.bfloat16
HI = lax.Precision.HIGHEST


def _sigmoid(x):
    return 1.0 / (1.0 + jnp.exp(-x))


def _silu(x):
    return x * _sigmoid(x)


def _log_sigmoid(x):
    return jnp.minimum(x, 0.0) - jnp.log(1.0 + jnp.exp(-jnp.abs(x)))


def _rmsnorm(x, g):
    return x * lax.rsqrt(jnp.mean(x * x, axis=-1, keepdims=True) + EPS) * g


def _dot(a, b):
    return jnp.dot(a.astype(BF16), b.astype(BF16), preferred_element_type=F32)


def _dot_t(a, b):
    return lax.dot_general(a.astype(BF16), b.astype(BF16), (((1,), (1,)), ((), ())),
                           preferred_element_type=F32)


def _dot_hi(a, b):
    return jnp.dot(a, b, preferred_element_type=F32, precision=HI)


def _project(x, g_mix, w_main, w_lr, w_fu, b_f, w_cv):
    xn = _rmsnorm(x, g_mix).astype(BF16)
    z = jnp.dot(xn, w_main, preferred_element_type=F32)
    q = z[:, :QK_COLS] * (GLA_DK ** -0.5)
    k = z[:, QK_COLS:2 * QK_COLS]
    v = z[:, 2 * QK_COLS:2 * QK_COLS + V_COLS]
    og = z[:, 2 * QK_COLS + V_COLS:]
    lr = jnp.dot(xn, w_lr, preferred_element_type=F32)
    zf = _dot(lr, w_fu) + b_f
    la = _log_sigmoid(zf) * (1.0 / GLA_TAU)
    cv = jnp.dot(xn, w_cv, preferred_element_type=F32)
    u = cv[:, :CONV_WIDTH] * _sigmoid(cv[:, CONV_WIDTH:])
    return q, k, v, og, la, u


def _stack_heads(qd):
    lane = lax.broadcasted_iota(jnp.int32, qd.shape, 1)
    return jnp.concatenate(
        [jnp.where((lane >= h * GLA_DK) & (lane < (h + 1) * GLA_DK), qd, 0.0) for h in range(GLA_HEADS)],
        axis=0)


def _gated_head_norm(o, og, g_gla):
    outs = []
    for h in range(GLA_HEADS):
        sl = slice(h * GLA_DV, (h + 1) * GLA_DV)
        outs.append(_rmsnorm(o[:, sl], g_gla) * _silu(og[:, sl]))
    return jnp.concatenate(outs, axis=1)


def _causal_conv(win, w_conv, b_conv, n):
    acc = jnp.broadcast_to(b_conv, (n, CONV_WIDTH))
    for s in range(SUBLANES):
        taps = [j for j in range(CONV_K) if (CONV_OFF + j) % SUBLANES == s]
        if not taps:
            continue
        rows = n if s == 0 else n + SUBLANES
        part = None
        for j in taps:
            a = (CONV_OFF + j) - s
            term = w_conv[j:j + 1, :] * win[a:a + rows, :]
            part = term if part is None else part + term
        acc = acc + part[s:s + n, :]
    return acc


def _conv_ln_act(acc, g_ln, b_ln):
    mu = jnp.mean(acc, axis=-1, keepdims=True)
    xc = acc - mu
    y = xc * lax.rsqrt(jnp.mean(xc * xc, axis=-1, keepdims=True) + EPS) * g_ln + b_ln
    return _silu(y)


def _head_diag(upd):
    return jnp.concatenate([upd[h * GLA_DK:(h + 1) * GLA_DK, h * GLA_DV:(h + 1) * GLA_DV]
                            for h in range(GLA_HEADS)], axis=0)


def _const_spec(shape):
    nd = len(shape)
    return pl.BlockSpec(shape, lambda *_: (0,) * nd)


_MIXER_WEIGHTS = ("g_mix", "w_main", "w_lr", "w_fu", "b_f", "w_cv", "g_gla", "w_conv", "b_conv", "g_ln",
                  "b_ln", "w_out")


def _mixer_prompt_kernel(x_ref, g_mix_ref, w_main_ref, w_lr_ref, w_fu_ref, b_f_ref, w_cv_ref,
                         g_gla_ref, w_conv_ref, b_conv_ref, g_ln_ref, b_ln_ref, w_out_ref,
                         h_ref, sg_ref, sc_ref,
                         s_ref, ubuf_ref, q_ref, k_ref, la_ref, v_ref, og_ref, mix_ref, *, tt):
    t = pl.program_id(1)
    nt = pl.num_programs(1)
    C = GLA_CHUNK

    @pl.when(t == 0)
    def _():
        s_ref[...] = jnp.zeros_like(s_ref)
        ubuf_ref[0:CONV_PAD, :] = jnp.zeros((CONV_PAD, CONV_WIDTH), F32)

    x = x_ref[...]
    q, k, v, og, la, u = _project(x, g_mix_ref[...], w_main_ref[...], w_lr_ref[...], w_fu_ref[...],
                                  b_f_ref[...], w_cv_ref[...])
    q_ref[...] = q
    k_ref[...] = k
    la_ref[...] = la
    v_ref[...] = v
    og_ref[...] = og
    ubuf_ref[CONV_PAD:CONV_PAD + tt, :] = u

    row = lax.broadcasted_iota(jnp.int32, (C, C), 0)
    col = lax.broadcasted_iota(jnp.int32, (C, C), 1)
    tri = (col <= row).astype(F32)
    r4 = lax.broadcasted_iota(jnp.int32, (GLA_HEADS * C, C), 0)
    c4 = lax.broadcasted_iota(jnp.int32, (GLA_HEADS * C, C), 1)
    causal4 = c4 <= (r4 % C)
    g_gla = g_gla_ref[...]
    w_conv = w_conv_ref[...]
    b_conv = b_conv_ref[...]
    g_ln = g_ln_ref[...]
    b_ln = b_ln_ref[...]

    def chunk(c, carry):
        r0 = pl.multiple_of(c * C, C)
        rows = pl.ds(r0, C)
        qc = q_ref[rows, :]
        kc = k_ref[rows, :]
        lac = la_ref[rows, :]
        vc = v_ref[rows, :]
        ogc = og_ref[rows, :]
        s_old = s_ref[...]

        b = _dot_hi(tri, lac)
        b_last = b[C - 1:C, :]
        qd = qc * jnp.exp(b)
        kd = kc * jnp.exp(-b)
        kl = kc * jnp.exp(b_last - b)
        qs = _stack_heads(qd).astype(BF16)
        scores = jnp.where(causal4, _dot_t(qs, kd), 0.0)
        o_inter = jnp.dot(qs, s_old.astype(BF16), preferred_element_type=F32)
        klt = jnp.concatenate([kl, jnp.broadcast_to(jnp.exp(b_last), (C, QK_COLS))], axis=0).T
        decay = klt[:, C:C + 1]
        upd = _dot(klt[:, :C], vc)
        s_ref[...] = decay * s_old + _head_diag(upd)
        o_parts = []
        for h in range(GLA_HEADS):
            vh = vc[:, h * GLA_DV:(h + 1) * GLA_DV]
            o_parts.append(_dot(scores[h * C:(h + 1) * C, :], vh) + o_inter[h * C:(h + 1) * C, :])
        o = jnp.concatenate(o_parts, axis=1)
        mix_ref[rows, 0:V_COLS] = _gated_head_norm(o, ogc, g_gla).astype(BF16)

        win = ubuf_ref[pl.ds(r0, C + CONV_PAD), :]
        acc = _causal_conv(win, w_conv, b_conv, C)
        mix_ref[rows, V_COLS:] = _conv_ln_act(acc, g_ln, b_ln).astype(BF16)
        return carry

    lax.fori_loop(0, tt // C, chunk, 0)

    h_ref[...] = x + jnp.dot(mix_ref[...], w_out_ref[...], preferred_element_type=F32)
    tail = ubuf_ref[tt:tt + CONV_PAD, :]
    ubuf_ref[0:CONV_PAD, :] = tail

    @pl.when(t == nt - 1)
    def _():
        sg_ref[0] = s_ref[...]
        sc_ref[0] = tail[CONV_OFF:, :]


def _mixer_prompt(x, row_off, nb, seq, lw, *, tt=512):
    nt = seq // tt
    blk_off = row_off // tt
    weights = [lw[n] for n in _MIXER_WEIGHTS]
    in_specs = [pl.BlockSpec((tt, D_MODEL), lambda b, t: (blk_off + b * nt + t, 0))]
    in_specs += [_const_spec(w.shape) for w in weights]
    out_shape = (jax.ShapeDtypeStruct((nb * seq, D_MODEL), F32),
                 jax.ShapeDtypeStruct((nb, QK_COLS, GLA_DV), F32),
                 jax.ShapeDtypeStruct((nb, CONV_K - 1, CONV_WIDTH), F32))
    out_specs = (pl.BlockSpec((tt, D_MODEL), lambda b, t: (b * nt + t, 0)),
                 pl.BlockSpec((1, QK_COLS, GLA_DV), lambda b, t: (b, 0, 0)),
                 pl.BlockSpec((1, CONV_K - 1, CONV_WIDTH), lambda b, t: (b, 0, 0)))
    scratch = [pltpu.VMEM((QK_COLS, GLA_DV), F32),
               pltpu.VMEM((CONV_PAD + tt, CONV_WIDTH), F32),
               pltpu.VMEM((tt, QK_COLS), F32), pltpu.VMEM((tt, QK_COLS), F32), pltpu.VMEM((tt, QK_COLS), F32),
               pltpu.VMEM((tt, V_COLS), F32), pltpu.VMEM((tt, V_COLS), F32),
               pltpu.VMEM((tt, D_MODEL), BF16)]
    return pl.pallas_call(
        functools.partial(_mixer_prompt_kernel, tt=tt),
        grid=(nb, nt), in_specs=in_specs, out_specs=out_specs, out_shape=out_shape,
        scratch_shapes=scratch,
        compiler_params=pltpu.CompilerParams(dimension_semantics=("arbitrary", "arbitrary"),
                                             vmem_limit_bytes=VMEM_LIMIT),
        name="mixer_prompt",
    )(x, *weights)


def _mixer_sample_kernel(x_ref, s_in_ref, c_in_ref, g_mix_ref, w_main_ref, w_lr_ref, w_fu_ref, b_f_ref,
                         w_cv_ref, g_gla_ref, w_conv_ref, b_conv_ref, g_ln_ref, b_ln_ref, w_out_ref,
                         h_ref, sg_ref, sc_ref,
                         qs_ref, klt_ref, dect_ref, v_ref, u_ref, oi_ref, cbuf_ref, cacc_ref, *, sb, seq):
    R = sb * seq
    x = x_ref[...]
    q, k, v, og, la, u = _project(x, g_mix_ref[...], w_main_ref[...], w_lr_ref[...], w_fu_ref[...],
                                  b_f_ref[...], w_cv_ref[...])
    row = lax.broadcasted_iota(jnp.int32, (R, R), 0)
    col = lax.broadcasted_iota(jnp.int32, (R, R), 1)
    same = (row // seq) == (col // seq)
    b = _dot_hi((same & (col <= row)).astype(F32), la)
    b_tot = _dot_hi(same.astype(F32), la)
    qd = q * jnp.exp(b)
    kd = k * jnp.exp(-b)
    kl = k * jnp.exp(b_tot - b)
    qs = _stack_heads(qd)
    r4 = lax.broadcasted_iota(jnp.int32, (GLA_HEADS * R, R), 0) % R
    c4 = lax.broadcasted_iota(jnp.int32, (GLA_HEADS * R, R), 1)
    mask4 = ((r4 // seq) == (c4 // seq)) & (c4 <= r4)
    scores = jnp.where(mask4, _dot_t(qs, kd), 0.0)

    qs_ref[...] = qs
    klt_ref[...] = kl.T
    dect_ref[...] = jnp.exp(b_tot).T
    v_ref[...] = v
    u_ref[...] = u

    lane_r = lax.broadcasted_iota(jnp.int32, (QK_COLS, R), 1)
    w_conv = w_conv_ref[...]
    b_conv = b_conv_ref[...]

    def one_seq(i, carry):
        lo = i * seq
        r0 = pl.multiple_of(lo, seq)
        s_old = s_in_ref[i]
        qsel = jnp.concatenate([qs_ref[pl.ds(pl.multiple_of(h * R + lo, seq), seq), :]
                                for h in range(GLA_HEADS)], axis=0)
        oi = _dot(qsel, s_old)
        for h in range(GLA_HEADS):
            oi_ref[h, pl.ds(r0, seq), :] = oi[h * seq:(h + 1) * seq, :]
        lmask = (lane_r >= lo) & (lane_r < lo + seq)
        upd = _dot(jnp.where(lmask, klt_ref[...], 0.0), v_ref[...])
        dec = jnp.sum(jnp.where(lmask, dect_ref[...], 0.0), axis=1, keepdims=True) * (1.0 / seq)
        sg_ref[i] = dec * s_old + _head_diag(upd)
        cbuf_ref[CONV_OFF:CONV_PAD, :] = c_in_ref[i]
        cbuf_ref[CONV_PAD:CONV_PAD + seq, :] = u_ref[pl.ds(r0, seq), :]
        cacc_ref[pl.ds(r0, seq), :] = _causal_conv(cbuf_ref[...], w_conv, b_conv, seq)
        sc_ref[i] = cbuf_ref[CONV_OFF + seq:CONV_PAD + seq, :]
        return carry

    lax.fori_loop(0, sb, one_seq, 0)

    o_parts = []
    for h in range(GLA_HEADS):
        vh = v[:, h * GLA_DV:(h + 1) * GLA_DV]
        o_parts.append(_dot(scores[h * R:(h + 1) * R, :], vh) + oi_ref[h])
    o = jnp.concatenate(o_parts, axis=1)
    mix = jnp.concatenate([_gated_head_norm(o, og, g_gla_ref[...]),
                           _conv_ln_act(cacc_ref[...], g_ln_ref[...], b_ln_ref[...])], axis=1)
    h_ref[...] = x + _dot(mix, w_out_ref[...])


def _mixer_sample(x, row_off, nb, seq, s_in, c_in, lw, *, sb=16):
    R = sb * seq
    blk_off = row_off // R
    weights = [lw[n] for n in _MIXER_WEIGHTS]
    in_specs = [pl.BlockSpec((R, D_MODEL), lambda i: (blk_off + i, 0)),
                pl.BlockSpec((sb, QK_COLS, GLA_DV), lambda i: (i, 0, 0)),
                pl.BlockSpec((sb, CONV_K - 1, CONV_WIDTH), lambda i: (i, 0, 0))]
    in_specs += [_const_spec(w.shape) for w in weights]
    out_shape = (jax.ShapeDtypeStruct((nb * seq, D_MODEL), F32),
                 jax.ShapeDtypeStruct((nb, QK_COLS, GLA_DV), F32),
                 jax.ShapeDtypeStruct((nb, CONV_K - 1, CONV_WIDTH), F32))
    out_specs = (pl.BlockSpec((R, D_MODEL), lambda i: (i, 0)),
                 pl.BlockSpec((sb, QK_COLS, GLA_DV), lambda i: (i, 0, 0)),
                 pl.BlockSpec((sb, CONV_K - 1, CONV_WIDTH), lambda i: (i, 0, 0)))
    scratch = [pltpu.VMEM((GLA_HEADS * R, QK_COLS), F32),
               pltpu.VMEM((QK_COLS, R), F32), pltpu.VMEM((QK_COLS, R), F32),
               pltpu.VMEM((R, V_COLS), F32), pltpu.VMEM((R, CONV_WIDTH), F32),
               pltpu.VMEM((GLA_HEADS, R, GLA_DV), F32),
               pltpu.VMEM((CONV_PAD + seq, CONV_WIDTH), F32),
               pltpu.VMEM((R, CONV_WIDTH), F32)]
    return pl.pallas_call(
        functools.partial(_mixer_sample_kernel, sb=sb, seq=seq),
        grid=(nb // sb,), in_specs=in_specs, out_specs=out_specs, out_shape=out_shape,
        scratch_shapes=scratch,
        compiler_params=pltpu.CompilerParams(dimension_semantics=("arbitrary",),
                                             vmem_limit_bytes=VMEM_LIMIT),
        name="mixer_sample",
    )(x, s_in, c_in, *weights)


def _route(logits):
    lane = lax.broadcasted_iota(jnp.int32, logits.shape, 1)
    lane_f = lane.astype(F32)
    neg = jnp.float32(-jnp.inf)
    big = jnp.float32(1e9)
    is_grp = (lane >= N_EXPERTS) & (lane < N_EXPERTS + N_GROUPS)
    gl = jnp.where(is_grp, logits, neg)
    gmax = jnp.max(gl, axis=-1, keepdims=True)
    gidx = jnp.min(jnp.where(is_grp & (gl == gmax), lane_f - N_EXPERTS, big), axis=-1, keepdims=True)
    gsum = jnp.sum(jnp.where(is_grp, jnp.exp(gl - gmax), 0.0), axis=-1, keepdims=True)
    g_w = 1.0 / gsum
    grp_of_lane = jnp.floor(lane_f * (1.0 / EXPERTS_PER_GROUP))
    in_grp = (lane < N_EXPERTS) & (grp_of_lane == gidx)
    ml = jnp.where(in_grp, logits, neg)
    v1 = jnp.max(ml, axis=-1, keepdims=True)
    i1 = jnp.min(jnp.where(in_grp & (ml == v1), lane_f, big), axis=-1, keepdims=True)
    ml2 = jnp.where(lane_f == i1, neg, ml)
    v2 = jnp.max(ml2, axis=-1, keepdims=True)
    i2 = jnp.min(jnp.where(in_grp & (ml2 == v2), lane_f, big), axis=-1, keepdims=True)
    e2 = jnp.exp(v2 - v1)
    w1 = g_w / (1.0 + e2)
    w2 = g_w * e2 / (1.0 + e2)
    return jnp.where(lane_f == i1, w1, 0.0) + jnp.where(lane_f == i2, w2, 0.0)


def _ffn_kernel(h_ref, p_ref, g_ffn_ref, w_rt_ref, b_rt_ref, wg_ref, wu_ref, wd_ref,
                g_ple_ref, w_pg_ref, w_pp_ref, g_fin_ref,
                o_ref, xn_ref, comb_ref, acc_ref, *, final):
    e = pl.program_id(1)
    ne = pl.num_programs(1)

    @pl.when(e == 0)
    def _():
        xn = _rmsnorm(h_ref[...], g_ffn_ref[...])
        xn_ref[...] = xn.astype(BF16)
        comb_ref[...] = _route(_dot_hi(xn, w_rt_ref[...]) + b_rt_ref[...])
        acc_ref[...] = jnp.zeros_like(acc_ref)

    xn = xn_ref[...]
    comb = comb_ref[...]
    lane = lax.broadcasted_iota(jnp.int32, comb.shape, 1)
    c_e = jnp.sum(jnp.where(lane == e, comb, 0.0), axis=-1, keepdims=True)
    hg = _silu(jnp.dot(xn, wg_ref[0], preferred_element_type=F32)) * jnp.dot(xn, wu_ref[0],
                                                                            preferred_element_type=F32)
    acc_ref[...] += jnp.dot((hg * c_e).astype(BF16), wd_ref[0], preferred_element_type=F32)

    @pl.when(e == ne - 1)
    def _():
        h2 = h_ref[...] + acc_ref[...]
        xn2 = _rmsnorm(h2, g_ple_ref[...])
        gate = _sigmoid(_dot(xn2, w_pg_ref[...]))
        h3 = h2 + gate * _dot(p_ref[...], w_pp_ref[...])
        if final:
            h3 = _rmsnorm(h3, g_fin_ref[...])
        o_ref[...] = h3


def _ffn(h, p, lw, g_final, *, final, tm=1024):
    n = h.shape[0]
    weights_c = [lw["g_ffn"], lw["w_rt"], lw["b_rt"]]
    weights_t = [lw["g_ple"], lw["w_pg"], lw["w_pp"], g_final]
    in_specs = [pl.BlockSpec((tm, D_MODEL), lambda i, e: (i, 0)),
                pl.BlockSpec((tm, PLE_DIM), lambda i, e: (i, 0))]
    in_specs += [_const_spec(w.shape) for w in weights_c]
    in_specs += [pl.BlockSpec((1, D_MODEL, EXPERT_FF), lambda i, e: (e, 0, 0)),
                 pl.BlockSpec((1, D_MODEL, EXPERT_FF), lambda i, e: (e, 0, 0)),
                 pl.BlockSpec((1, EXPERT_FF, D_MODEL), lambda i, e: (e, 0, 0))]
    in_specs += [_const_spec(w.shape) for w in weights_t]
    scratch = [pltpu.VMEM((tm, D_MODEL), BF16), pltpu.VMEM((tm, LANES), F32), pltpu.VMEM((tm, D_MODEL), F32)]
    return pl.pallas_call(
        functools.partial(_ffn_kernel, final=final),
        grid=(n // tm, N_EXPERTS), in_specs=in_specs,
        out_specs=pl.BlockSpec((tm, D_MODEL), lambda i, e: (i, 0)),
        out_shape=jax.ShapeDtypeStruct((n, D_MODEL), F32),
        scratch_shapes=scratch,
        compiler_params=pltpu.CompilerParams(dimension_semantics=("arbitrary", "arbitrary"),
                                             vmem_limit_bytes=VMEM_LIMIT),
        name="ffn_final" if final else "ffn",
    )(h, p, *weights_c, lw["wg"], lw["wu"], lw["wd"], *weights_t)


def _row(v):
    return v.reshape(1, -1).astype(F32)


def _layer_weights(l, g_mix, w_in, w_forget_up, b_forget, g_gla_out, w_conv, b_conv, g_conv_ln, b_conv_ln,
                   w_out, g_ffn, w_grp_router, b_grp_router, w_exp_router, b_exp_router, w_exp_gate,
                   w_exp_up, w_exp_down, g_ple, w_ple_gate, w_ple_proj):
    n_main = 2 * QK_COLS + 2 * V_COLS
    wi = w_in[l]
    w_lr = jnp.zeros((D_MODEL, LANES), BF16).at[:, :GLA_LOWRANK].set(
        wi[:, n_main:n_main + GLA_LOWRANK].astype(BF16))
    w_fu = jnp.zeros((LANES, QK_COLS), BF16).at[:GLA_LOWRANK].set(w_forget_up[l].astype(BF16))
    w_er = jnp.transpose(w_exp_router[l], (1, 0, 2)).reshape(D_MODEL, N_EXPERTS)
    w_rt = jnp.zeros((D_MODEL, LANES), F32).at[:, :N_EXPERTS].set(w_er)
    w_rt = w_rt.at[:, N_EXPERTS:N_EXPERTS + N_GROUPS].set(w_grp_router[l])
    b_rt = jnp.zeros((1, LANES), F32).at[0, :N_EXPERTS].set(b_exp_router[l].reshape(-1))
    b_rt = b_rt.at[0, N_EXPERTS:N_EXPERTS + N_GROUPS].set(b_grp_router[l])
    return {
        "g_mix": _row(g_mix[l]),
        "w_main": wi[:, :n_main].astype(BF16),
        "w_lr": w_lr,
        "w_fu": w_fu,
        "b_f": _row(b_forget[l]),
        "w_cv": wi[:, n_main + GLA_LOWRANK:].astype(BF16),
        "g_gla": _row(g_gla_out[l]),
        "w_conv": w_conv[l].astype(F32),
        "b_conv": _row(b_conv[l]),
        "g_ln": _row(g_conv_ln[l]),
        "b_ln": _row(b_conv_ln[l]),
        "w_out": w_out[l].astype(BF16),
        "g_ffn": _row(g_ffn[l]),
        "w_rt": w_rt,
        "b_rt": b_rt,
        "wg": w_exp_gate[l].reshape(N_EXPERTS, D_MODEL, EXPERT_FF).astype(BF16),
        "wu": w_exp_up[l].reshape(N_EXPERTS, D_MODEL, EXPERT_FF).astype(BF16),
        "wd": w_exp_down[l].reshape(N_EXPERTS, EXPERT_FF, D_MODEL).astype(BF16),
        "g_ple": _row(g_ple[l]),
        "w_pg": w_ple_gate[l].astype(BF16),
        "w_pp": w_ple_proj[l].astype(BF16),
    }


def kernel(x_prompt, x_sample, state_gla, state_conv, p_prompt, p_sample, g_mix, w_in, w_forget_up, b_forget,
           g_gla_out, w_conv, b_conv, g_conv_ln, b_conv_ln, w_out, g_ffn, w_grp_router, b_grp_router,
           w_exp_router, b_exp_router, w_exp_gate, w_exp_up, w_exp_down, g_ple, w_ple_gate, w_ple_proj, g_final):
    depth = w_in.shape[0]
    nbp, seq_p, _ = x_prompt.shape
    nbs, seq_s, _ = x_sample.shape
    n_p = nbp * seq_p
    n_s = nbs * seq_s
    layer_args = (g_mix, w_in, w_forget_up, b_forget, g_gla_out, w_conv, b_conv, g_conv_ln, b_conv_ln, w_out,
                  g_ffn, w_grp_router, b_grp_router, w_exp_router, b_exp_router, w_exp_gate, w_exp_up,
                  w_exp_down, g_ple, w_ple_gate, w_ple_proj)
    g_fin = _row(g_final)

    xp = x_prompt.reshape(n_p, D_MODEL)
    xs = x_sample.reshape(n_s, D_MODEL)
    p_all = jnp.concatenate([p_prompt.reshape(depth, n_p, PLE_DIM), p_sample.reshape(depth, n_s, PLE_DIM)],
                            axis=1)
    s_in = state_gla.reshape(depth, nbs, QK_COLS, GLA_DV)

    h = None
    sg_p, sg_s, sc_p, sc_s = [], [], [], []
    for l in range(depth):
        lw = _layer_weights(l, *layer_args)
        if l == 0:
            hp, sgp, scp = _mixer_prompt(xp, 0, nbp, seq_p, lw)
            hs, sgs, scs = _mixer_sample(xs, 0, nbs, seq_s, s_in[l], state_conv[l], lw)
        else:
            hp, sgp, scp = _mixer_prompt(h, 0, nbp, seq_p, lw)
            hs, sgs, scs = _mixer_sample(h, n_p, nbs, seq_s, s_in[l], state_conv[l], lw)
        h1 = jnp.concatenate([hp, hs], axis=0)
        h = _ffn(h1, p_all[l], lw, g_fin, final=(l == depth - 1))
        sg_p.append(sgp.reshape(nbp, GLA_HEADS, GLA_DK, GLA_DV))
        sg_s.append(sgs.reshape(nbs, GLA_HEADS, GLA_DK, GLA_DV))
        sc_p.append(scp)
        sc_s.append(scs)

    y_prompt = h[:n_p].reshape(nbp, seq_p, D_MODEL)
    y_sample = h[n_p:].reshape(nbs, seq_s, D_MODEL)
    return (y_prompt, y_sample, jnp.stack(sg_p), jnp.stack(sg_s), jnp.stack(sc_p), jnp.stack(sc_s))
```

```python
import functools

import jax
import jax.numpy as jnp
from jax import lax
from jax.experimental import pallas as pl
from jax.experimental.pallas import tpu as pltpu

D_MODEL = 1024
GLA_HEADS = 4
GLA_DK = 64
GLA_DV = 128
QK_COLS = GLA_HEADS * GLA_DK
V_COLS = GLA_HEADS * GLA_DV
CONV_WIDTH = 512
CONV_K = 31
GLA_LOWRANK = 16
GLA_TAU = 16.0
GLA_CHUNK = 64
PLE_DIM = 256
N_GROUPS = 4
EXPERTS_PER_GROUP = 8
N_EXPERTS = N_GROUPS * EXPERTS_PER_GROUP
EXPERT_FF = 256
EPS = 1e-6

LANES = 128
SUBLANES = 8
CONV_PAD = 32
CONV_OFF = CONV_PAD - (CONV_K - 1)
VMEM_LIMIT = 56 * 1024 * 1024
TOP_K = 2
TOK_TILE = 512
CHUNK = 16
EXP_TILE = 256
CBUF_ROWS = -(-(TOP_K * TOK_TILE + N_EXPERTS * (CHUNK - 1)) // LANES) * LANES
XS_COLS = D_MODEL + LANES

F32 = jnp.float32
BF16 = jnp.bfloat16
HI = lax.Precision.HIGHEST


def _sigmoid(x):
    return 1.0 / (1.0 + jnp.exp(-x))


def _silu(x):
    return x * _sigmoid(x)


def _log_sigmoid(x):
    return jnp.minimum(x, 0.0) - jnp.log(1.0 + jnp.exp(-jnp.abs(x)))


def _rmsnorm(x, g):
    return x * lax.rsqrt(jnp.mean(x * x, axis=-1, keepdims=True) + EPS) * g


def _dot(a, b):
    return jnp.dot(a.astype(BF16), b.astype(BF16), preferred_element_type=F32)


def _dot_t(a, b):
    return lax.dot_general(a.astype(BF16), b.astype(BF16), (((1,), (1,)), ((), ())),
                           preferred_element_type=F32)


def _dot_hi(a, b):
    return jnp.dot(a, b, preferred_element_type=F32, precision=HI)


def _project(x, g_mix, w_main, w_lr, w_fu, b_f, w_cv):
    xn = _rmsnorm(x, g_mix).astype(BF16)
    z = jnp.dot(xn, w_main, preferred_element_type=F32)
    q = z[:, :QK_COLS] * (GLA_DK ** -0.5)
    k = z[:, QK_COLS:2 * QK_COLS]
    v = z[:, 2 * QK_COLS:2 * QK_COLS + V_COLS]
    og = z[:, 2 * QK_COLS + V_COLS:]
    lr = jnp.dot(xn, w_lr, preferred_element_type=F32)
    zf = _dot(lr, w_fu) + b_f
    la = _log_sigmoid(zf) * (1.0 / GLA_TAU)
    cv = jnp.dot(xn, w_cv, preferred_element_type=F32)
    u = cv[:, :CONV_WIDTH] * _sigmoid(cv[:, CONV_WIDTH:])
    return q, k, v, og, la, u


def _stack_heads(qd):
    lane = lax.broadcasted_iota(jnp.int32, qd.shape, 1)
    return jnp.concatenate(
        [jnp.where((lane >= h * GLA_DK) & (lane < (h + 1) * GLA_DK), qd, 0.0) for h in range(GLA_HEADS)],
        axis=0)


def _gated_head_norm(o, og, g_gla):
    outs = []
    for h in range(GLA_HEADS):
        sl = slice(h * GLA_DV, (h + 1) * GLA_DV)
        outs.append(_rmsnorm(o[:, sl], g_gla) * _silu(og[:, sl]))
    return jnp.concatenate(outs, axis=1)


def _causal_conv(win, w_conv, b_conv, n):
    acc = jnp.broadcast_to(b_conv, (n, CONV_WIDTH))
    for s in range(SUBLANES):
        taps = [j for j in range(CONV_K) if (CONV_OFF + j) % SUBLANES == s]
        if not taps:
            continue
        rows = n if s == 0 else n + SUBLANES
        part = None
        for j in taps:
            a = (CONV_OFF + j) - s
            term = w_conv[j:j + 1, :] * win[a:a + rows, :]
            part = term if part is None else part + term
        acc = acc + part[s:s + n, :]
    return acc


def _conv_ln_act(acc, g_ln, b_ln):
    mu = jnp.mean(acc, axis=-1, keepdims=True)
    xc = acc - mu
    y = xc * lax.rsqrt(jnp.mean(xc * xc, axis=-1, keepdims=True) + EPS) * g_ln + b_ln
    return _silu(y)


def _head_diag(upd):
    return jnp.concatenate([upd[h * GLA_DK:(h + 1) * GLA_DK, h * GLA_DV:(h + 1) * GLA_DV]
                            for h in range(GLA_HEADS)], axis=0)


def _const_spec(shape):
    nd = len(shape)
    return pl.BlockSpec(shape, lambda *_: (0,) * nd)


_MIXER_WEIGHTS = ("g_mix", "w_main", "w_lr", "w_fu", "b_f", "w_cv", "g_gla", "w_conv", "b_conv", "g_ln",
                  "b_ln", "w_out")


def _mixer_prompt_kernel(x_ref, g_mix_ref, w_main_ref, w_lr_ref, w_fu_ref, b_f_ref, w_cv_ref,
                         g_gla_ref, w_conv_ref, b_conv_ref, g_ln_ref, b_ln_ref, w_out_ref,
                         h_ref, sg_ref, sc_ref,
                         s_ref, ubuf_ref, q_ref, k_ref, la_ref, v_ref, og_ref, mix_ref, *, tt):
    t = pl.program_id(1)
    nt = pl.num_programs(1)
    C = GLA_CHUNK

    @pl.when(t == 0)
    def _():
        s_ref[...] = jnp.zeros_like(s_ref)
        ubuf_ref[0:CONV_PAD, :] = jnp.zeros((CONV_PAD, CONV_WIDTH), F32)

    x = x_ref[...]
    q, k, v, og, la, u = _project(x, g_mix_ref[...], w_main_ref[...], w_lr_ref[...], w_fu_ref[...],
                                  b_f_ref[...], w_cv_ref[...])
    q_ref[...] = q
    k_ref[...] = k
    la_ref[...] = la
    v_ref[...] = v
    og_ref[...] = og
    ubuf_ref[CONV_PAD:CONV_PAD + tt, :] = u

    row = lax.broadcasted_iota(jnp.int32, (C, C), 0)
    col = lax.broadcasted_iota(jnp.int32, (C, C), 1)
    tri = (col <= row).astype(F32)
    r4 = lax.broadcasted_iota(jnp.int32, (GLA_HEADS * C, C), 0)
    c4 = lax.broadcasted_iota(jnp.int32, (GLA_HEADS * C, C), 1)
    causal4 = c4 <= (r4 % C)
    g_gla = g_gla_ref[...]
    w_conv = w_conv_ref[...]
    b_conv = b_conv_ref[...]
    g_ln = g_ln_ref[...]
    b_ln = b_ln_ref[...]

    def chunk(c, carry):
        r0 = pl.multiple_of(c * C, C)
        rows = pl.ds(r0, C)
        qc = q_ref[rows, :]
        kc = k_ref[rows, :]
        lac = la_ref[rows, :]
        vc = v_ref[rows, :]
        ogc = og_ref[rows, :]
        s_old = s_ref[...]

        b = _dot_hi(tri, lac)
        b_last = b[C - 1:C, :]
        qd = qc * jnp.exp(b)
        kd = kc * jnp.exp(-b)
        kl = kc * jnp.exp(b_last - b)
        qs = _stack_heads(qd).astype(BF16)
        scores = jnp.where(causal4, _dot_t(qs, kd), 0.0)
        o_inter = jnp.dot(qs, s_old.astype(BF16), preferred_element_type=F32)
        klt = jnp.concatenate([kl, jnp.broadcast_to(jnp.exp(b_last), (C, QK_COLS))], axis=0).T
        decay = klt[:, C:C + 1]
        upd = _dot(klt[:, :C], vc)
        s_ref[...] = decay * s_old + _head_diag(upd)
        o_parts = []
        for h in range(GLA_HEADS):
            vh = vc[:, h * GLA_DV:(h + 1) * GLA_DV]
            o_parts.append(_dot(scores[h * C:(h + 1) * C, :], vh) + o_inter[h * C:(h + 1) * C, :])
        o = jnp.concatenate(o_parts, axis=1)
        mix_ref[rows, 0:V_COLS] = _gated_head_norm(o, ogc, g_gla).astype(BF16)

        win = ubuf_ref[pl.ds(r0, C + CONV_PAD), :]
        acc = _causal_conv(win, w_conv, b_conv, C)
        mix_ref[rows, V_COLS:] = _conv_ln_act(acc, g_ln, b_ln).astype(BF16)
        return carry

    lax.fori_loop(0, tt // C, chunk, 0)

    h_ref[...] = x + jnp.dot(mix_ref[...], w_out_ref[...], preferred_element_type=F32)
    tail = ubuf_ref[tt:tt + CONV_PAD, :]
    ubuf_ref[0:CONV_PAD, :] = tail

    @pl.when(t == nt - 1)
    def _():
        sg_ref[0] = s_ref[...]
        sc_ref[0] = tail[CONV_OFF:, :]


def _mixer_prompt(x, row_off, nb, seq, lw, *, tt=512):
    nt = seq // tt
    blk_off = row_off // tt
    weights = [lw[n] for n in _MIXER_WEIGHTS]
    in_specs = [pl.BlockSpec((tt, D_MODEL), lambda b, t: (blk_off + b * nt + t, 0))]
    in_specs += [_const_spec(w.shape) for w in weights]
    out_shape = (jax.ShapeDtypeStruct((nb * seq, D_MODEL), F32),
                 jax.ShapeDtypeStruct((nb, QK_COLS, GLA_DV), F32),
                 jax.ShapeDtypeStruct((nb, CONV_K - 1, CONV_WIDTH), F32))
    out_specs = (pl.BlockSpec((tt, D_MODEL), lambda b, t: (b * nt + t, 0)),
                 pl.BlockSpec((1, QK_COLS, GLA_DV), lambda b, t: (b, 0, 0)),
                 pl.BlockSpec((1, CONV_K - 1, CONV_WIDTH), lambda b, t: (b, 0, 0)))
    scratch = [pltpu.VMEM((QK_COLS, GLA_DV), F32),
               pltpu.VMEM((CONV_PAD + tt, CONV_WIDTH), F32),
               pltpu.VMEM((tt, QK_COLS), F32), pltpu.VMEM((tt, QK_COLS), F32), pltpu.VMEM((tt, QK_COLS), F32),
               pltpu.VMEM((tt, V_COLS), F32), pltpu.VMEM((tt, V_COLS), F32),
               pltpu.VMEM((tt, D_MODEL), BF16)]
    return pl.pallas_call(
        functools.partial(_mixer_prompt_kernel, tt=tt),
        grid=(nb, nt), in_specs=in_specs, out_specs=out_specs, out_shape=out_shape,
        scratch_shapes=scratch,
        compiler_params=pltpu.CompilerParams(dimension_semantics=("arbitrary", "arbitrary"),
                                             vmem_limit_bytes=VMEM_LIMIT),
        name="mixer_prompt",
    )(x, *weights)


def _mixer_sample_kernel(x_ref, s_in_ref, c_in_ref, g_mix_ref, w_main_ref, w_lr_ref, w_fu_ref, b_f_ref,
                         w_cv_ref, g_gla_ref, w_conv_ref, b_conv_ref, g_ln_ref, b_ln_ref, w_out_ref,
                         h_ref, sg_ref, sc_ref,
                         qs_ref, klt_ref, dect_ref, v_ref, u_ref, oi_ref, cbuf_ref, cacc_ref, *, sb, seq):
    R = sb * seq
    x = x_ref[...]
    q, k, v, og, la, u = _project(x, g_mix_ref[...], w_main_ref[...], w_lr_ref[...], w_fu_ref[...],
                                  b_f_ref[...], w_cv_ref[...])
    row = lax.broadcasted_iota(jnp.int32, (R, R), 0)
    col = lax.broadcasted_iota(jnp.int32, (R, R), 1)
    same = (row // seq) == (col // seq)
    b = _dot_hi((same & (col <= row)).astype(F32), la)
    b_tot = _dot_hi(same.astype(F32), la)
    qd = q * jnp.exp(b)
    kd = k * jnp.exp(-b)
    kl = k * jnp.exp(b_tot - b)
    qs = _stack_heads(qd)
    r4 = lax.broadcasted_iota(jnp.int32, (GLA_HEADS * R, R), 0) % R
    c4 = lax.broadcasted_iota(jnp.int32, (GLA_HEADS * R, R), 1)
    mask4 = ((r4 // seq) == (c4 // seq)) & (c4 <= r4)
    scores = jnp.where(mask4, _dot_t(qs, kd), 0.0)

    qs_ref[...] = qs
    klt_ref[...] = kl.T
    dect_ref[...] = jnp.exp(b_tot).T
    v_ref[...] = v
    u_ref[...] = u

    lane_r = lax.broadcasted_iota(jnp.int32, (QK_COLS, R), 1)
    w_conv = w_conv_ref[...]
    b_conv = b_conv_ref[...]

    def one_seq(i, carry):
        lo = i * seq
        r0 = pl.multiple_of(lo, seq)
        s_old = s_in_ref[i]
        qsel = jnp.concatenate([qs_ref[pl.ds(pl.multiple_of(h * R + lo, seq), seq), :]
                                for h in range(GLA_HEADS)], axis=0)
        oi = _dot(qsel, s_old)
        for h in range(GLA_HEADS):
            oi_ref[h, pl.ds(r0, seq), :] = oi[h * seq:(h + 1) * seq, :]
        lmask = (lane_r >= lo) & (lane_r < lo + seq)
        upd = _dot(jnp.where(lmask, klt_ref[...], 0.0), v_ref[...])
        dec = jnp.sum(jnp.where(lmask, dect_ref[...], 0.0), axis=1, keepdims=True) * (1.0 / seq)
        sg_ref[i] = dec * s_old + _head_diag(upd)
        cbuf_ref[CONV_OFF:CONV_PAD, :] = c_in_ref[i]
        cbuf_ref[CONV_PAD:CONV_PAD + seq, :] = u_ref[pl.ds(r0, seq), :]
        cacc_ref[pl.ds(r0, seq), :] = _causal_conv(cbuf_ref[...], w_conv, b_conv, seq)
        sc_ref[i] = cbuf_ref[CONV_OFF + seq:CONV_PAD + seq, :]
        return carry

    lax.fori_loop(0, sb, one_seq, 0)

    o_parts = []
    for h in range(GLA_HEADS):
        vh = v[:, h * GLA_DV:(h + 1) * GLA_DV]
        o_parts.append(_dot(scores[h * R:(h + 1) * R, :], vh) + oi_ref[h])
    o = jnp.concatenate(o_parts, axis=1)
    mix = jnp.concatenate([_gated_head_norm(o, og, g_gla_ref[...]),
                           _conv_ln_act(cacc_ref[...], g_ln_ref[...], b_ln_ref[...])], axis=1)
    h_ref[...] = x + _dot(mix, w_out_ref[...])


def _mixer_sample(x, row_off, nb, seq, s_in, c_in, lw, *, sb=16):
    R = sb * seq
    blk_off = row_off // R
    weights = [lw[n] for n in _MIXER_WEIGHTS]
    in_specs = [pl.BlockSpec((R, D_MODEL), lambda i: (blk_off + i, 0)),
                pl.BlockSpec((sb, QK_COLS, GLA_DV), lambda i: (i, 0, 0)),
                pl.BlockSpec((sb, CONV_K - 1, CONV_WIDTH), lambda i: (i, 0, 0))]
    in_specs += [_const_spec(w.shape) for w in weights]
    out_shape = (jax.ShapeDtypeStruct((nb * seq, D_MODEL), F32),
                 jax.ShapeDtypeStruct((nb, QK_COLS, GLA_DV), F32),
                 jax.ShapeDtypeStruct((nb, CONV_K - 1, CONV_WIDTH), F32))
    out_specs = (pl.BlockSpec((R, D_MODEL), lambda i: (i, 0)),
                 pl.BlockSpec((sb, QK_COLS, GLA_DV), lambda i: (i, 0, 0)),
                 pl.BlockSpec((sb, CONV_K - 1, CONV_WIDTH), lambda i: (i, 0, 0)))
    scratch = [pltpu.VMEM((GLA_HEADS * R, QK_COLS), F32),
               pltpu.VMEM((QK_COLS, R), F32), pltpu.VMEM((QK_COLS, R), F32),
               pltpu.VMEM((R, V_COLS), F32), pltpu.VMEM((R, CONV_WIDTH), F32),
               pltpu.VMEM((GLA_HEADS, R, GLA_DV), F32),
               pltpu.VMEM((CONV_PAD + seq, CONV_WIDTH), F32),
               pltpu.VMEM((R, CONV_WIDTH), F32)]
    return pl.pallas_call(
        functools.partial(_mixer_sample_kernel, sb=sb, seq=seq),
        grid=(nb // sb,), in_specs=in_specs, out_specs=out_specs, out_shape=out_shape,
        scratch_shapes=scratch,
        compiler_params=pltpu.CompilerParams(dimension_semantics=("arbitrary",),
                                             vmem_limit_bytes=VMEM_LIMIT),
        name="mixer_sample",
    )(x, s_in, c_in, *weights)


def _route(logits):
    lane = lax.broadcasted_iota(jnp.int32, logits.shape, 1)
    lane_f = lane.astype(F32)
    neg = jnp.float32(-jnp.inf)
    big = jnp.float32(1e9)
    is_grp = (lane >= N_EXPERTS) & (lane < N_EXPERTS + N_GROUPS)
    gl = jnp.where(is_grp, logits, neg)
    gmax = jnp.max(gl, axis=-1, keepdims=True)
    gidx = jnp.min(jnp.where(is_grp & (gl == gmax), lane_f - N_EXPERTS, big), axis=-1, keepdims=True)
    gsum = jnp.sum(jnp.where(is_grp, jnp.exp(gl - gmax), 0.0), axis=-1, keepdims=True)
    g_w = 1.0 / gsum
    grp_of_lane = jnp.floor(lane_f * (1.0 / EXPERTS_PER_GROUP))
    in_grp = (lane < N_EXPERTS) & (grp_of_lane == gidx)
    ml = jnp.where(in_grp, logits, neg)
    v1 = jnp.max(ml, axis=-1, keepdims=True)
    i1 = jnp.min(jnp.where(in_grp & (ml == v1), lane_f, big), axis=-1, keepdims=True)
    ml2 = jnp.where(lane_f == i1, neg, ml)
    v2 = jnp.max(ml2, axis=-1, keepdims=True)
    i2 = jnp.min(jnp.where(in_grp & (ml2 == v2), lane_f, big), axis=-1, keepdims=True)
    e2 = jnp.exp(v2 - v1)
    w1 = g_w / (1.0 + e2)
    w2 = g_w * e2 / (1.0 + e2)
    return i1, i2, w1, w2


def _iota_f32(shape, dim):
    return lax.broadcasted_iota(jnp.int32, shape, dim).astype(F32)


def _route_kernel(h_ref, g_ffn_ref, w_rt_ref, b_rt_ref, pos_ref, cnt_ref):
    T = TOK_TILE
    xn = _rmsnorm(h_ref[...], g_ffn_ref[...])
    i1, i2, w1, w2 = _route(_dot_hi(xn, w_rt_ref[...]) + b_rt_ref[...])
    lane = _iota_f32((T, LANES), 1)
    a0 = (lane == i1).astype(F32)
    a1 = (lane == i2).astype(F32)
    a = a0 + a1
    cnt = jnp.sum(a, axis=0, keepdims=True)
    earlier = (_iota_f32((T, T), 1) < _iota_f32((T, T), 0)).astype(BF16)
    rank = jnp.dot(earlier, a.astype(BF16), preferred_element_type=F32)
    cnt_pad = jnp.ceil(cnt * (1.0 / CHUNK)) * CHUNK
    below = (_iota_f32((LANES, LANES), 0) < _iota_f32((LANES, LANES), 1)).astype(F32)
    first = _dot_hi(jnp.broadcast_to(cnt_pad, (SUBLANES, LANES)), below)[0:1, :]
    base = first + rank
    pos0 = jnp.sum(a0 * base, axis=1, keepdims=True)
    pos1 = jnp.sum(a1 * base, axis=1, keepdims=True)
    pos_ref[...] = jnp.where(lane == 0.0, pos0, jnp.where(lane == 1.0, pos1, jnp.where(
        lane == 2.0, w1, jnp.where(lane == 3.0, w2, 0.0))))
    cnt_ref[0] = jnp.broadcast_to(cnt, (SUBLANES, LANES))


def _route_call(h, lw):
    n = h.shape[0]
    nt = n // TOK_TILE
    weights = [lw["g_ffn"], lw["w_rt"], lw["b_rt"]]
    return pl.pallas_call(
        _route_kernel, grid=(nt,),
        in_specs=[pl.BlockSpec((TOK_TILE, D_MODEL), lambda t: (t, 0))] + [_const_spec(w.shape) for w in weights],
        out_specs=(pl.BlockSpec((TOK_TILE, LANES), lambda t: (t, 0)),
                   pl.BlockSpec((1, SUBLANES, LANES), lambda t: (t, 0, 0))),
        out_shape=(jax.ShapeDtypeStruct((n, LANES), F32), jax.ShapeDtypeStruct((nt, SUBLANES, LANES), F32)),
        compiler_params=pltpu.CompilerParams(dimension_semantics=("arbitrary",), vmem_limit_bytes=VMEM_LIMIT),
        name="moe_route",
    )(h, *weights)


def _chunk_plan(cnt, n_row_tiles):
    n16 = (cnt + (CHUNK - 1)) // CHUNK
    lofs16 = jnp.cumsum(n16, axis=1) - n16
    tile_pref16 = jnp.cumsum(n16, axis=0) - n16
    tot16 = jnp.sum(n16, axis=0)
    per_tile = EXP_TILE // CHUNK
    seg16 = ((tot16 + per_tile - 1) // per_tile) * per_tile
    seg_end16 = jnp.cumsum(seg16)
    dst16 = (seg_end16 - seg16)[None, :] + tile_pref16
    n_tot = jnp.sum(n16, axis=1)
    gap16 = seg_end16 - seg16 + tot16
    gapn16 = seg16 - tot16
    tile_start16 = jnp.arange(n_row_tiles, dtype=jnp.int32) * per_tile
    n_valid = seg_end16[-1] // per_tile
    misc = jnp.stack([n_valid, jnp.sum(gapn16)])
    exp_of_tile = jnp.minimum(jnp.sum(seg_end16[None, :] <= tile_start16[:, None], axis=1), N_EXPERTS - 1)
    i32 = lambda a: a.astype(jnp.int32).reshape(-1)
    return i32(dst16), i32(n16), i32(lofs16), i32(n_tot), i32(gap16), i32(gapn16), i32(misc), i32(exp_of_tile)


def _chunk_copy(src, dst, src_chunk, dst_chunk, sem):
    return pltpu.make_async_copy(src.at[pl.ds(pl.multiple_of(src_chunk * CHUNK, CHUNK), CHUNK), :],
                                 dst.at[pl.ds(pl.multiple_of(dst_chunk * CHUNK, CHUNK), CHUNK), :], sem)


def _wait_chunks(src, dst, sem, n):
    def body(_, carry):
        _chunk_copy(src, dst, 0, 0, sem).wait()
        return carry
    lax.fori_loop(0, n, body, 0)


def _tile_copy(src, dst, dst_tile, sem):
    return pltpu.make_async_copy(src, dst.at[pl.ds(pl.multiple_of(dst_tile * EXP_TILE, EXP_TILE), EXP_TILE), :], sem)


def _dispatch_kernel(dst16_ref, n16_ref, lofs16_ref, ntot_ref, gap16_ref, gapn16_ref, misc_ref,
                     h_ref, pos_ref, g_ffn_ref, xs_hbm, cbuf, zbuf, sem, *, n_row_tiles):
    t = pl.program_id(0)
    nt = pl.num_programs(0)
    slot = t % 2
    T = TOK_TILE
    n_tail = n_row_tiles - misc_ref[0]

    @pl.when(t == 0)
    def _():
        zbuf[...] = jnp.zeros_like(zbuf)
        for e in range(N_EXPERTS):
            g0 = gap16_ref[e]

            def fill(c, carry, g0=g0):
                _chunk_copy(zbuf, xs_hbm, 0, g0 + c, sem.at[2]).start()
                return carry
            lax.fori_loop(0, gapn16_ref[e], fill, 0)

        def fill_tile(i, carry):
            _tile_copy(zbuf, xs_hbm, misc_ref[0] + i, sem.at[2]).start()
            return carry
        lax.fori_loop(0, n_tail, fill_tile, 0)
    xn = _rmsnorm(h_ref[...], g_ffn_ref[...]).astype(BF16)
    pos = pos_ref[...]
    pos_t = pos.T
    rows = _iota_f32((CBUF_ROWS, T), 0)
    sel0 = rows == pos_t[0:1, :]
    sel1 = rows == pos_t[1:2, :]
    lane = lax.broadcasted_iota(jnp.int32, (T, LANES), 1)

    def weight_cols(c):
        hi = c.astype(BF16).astype(F32)
        mid = (c - hi).astype(BF16).astype(F32)
        lo = c - hi - mid
        return jnp.where(lane == 0, hi, jnp.where(lane == 1, mid, jnp.where(lane == 2, lo, 0.0))).astype(BF16)

    p0 = jnp.where(sel0, 1.0, 0.0).astype(BF16)
    p1 = jnp.where(sel1, 1.0, 0.0).astype(BF16)
    cbuf[slot, :, 0:D_MODEL] = jnp.dot(p0 + p1, xn, preferred_element_type=F32).astype(BF16)
    cbuf[slot, :, D_MODEL:] = (jnp.dot(p0, weight_cols(pos[:, 2:3]), preferred_element_type=F32)
                               + jnp.dot(p1, weight_cols(pos[:, 3:4]), preferred_element_type=F32)).astype(BF16)

    src = cbuf.at[slot]
    for e in range(N_EXPERTS):
        k = t * N_EXPERTS + e
        l0 = lofs16_ref[k]
        d0 = dst16_ref[k]

        def issue(c, carry, l0=l0, d0=d0):
            _chunk_copy(src, xs_hbm, l0 + c, d0 + c, sem.at[slot]).start()
            return carry
        lax.fori_loop(0, n16_ref[k], issue, 0)

    @pl.when(t > 0)
    def _():
        _wait_chunks(cbuf.at[1 - slot], xs_hbm, sem.at[1 - slot], ntot_ref[t - 1])

    @pl.when(t == nt - 1)
    def _():
        _wait_chunks(src, xs_hbm, sem.at[slot], ntot_ref[t])
        _wait_chunks(zbuf, xs_hbm, sem.at[2], misc_ref[1])

        def wait_tile(_, carry):
            _tile_copy(zbuf, xs_hbm, 0, sem.at[2]).wait()
            return carry
        lax.fori_loop(0, n_tail, wait_tile, 0)


def _dispatch_call(plan, h, pos, lw, n_sorted):
    nt = h.shape[0] // TOK_TILE
    grid_spec = pltpu.PrefetchScalarGridSpec(
        num_scalar_prefetch=7, grid=(nt,),
        in_specs=[pl.BlockSpec((TOK_TILE, D_MODEL), lambda t, *_: (t, 0)),
                  pl.BlockSpec((TOK_TILE, LANES), lambda t, *_: (t, 0)),
                  pl.BlockSpec(lw["g_ffn"].shape, lambda t, *_: (0, 0))],
        out_specs=pl.BlockSpec(memory_space=pl.ANY),
        scratch_shapes=[pltpu.VMEM((2, CBUF_ROWS, XS_COLS), BF16), pltpu.VMEM((EXP_TILE, XS_COLS), BF16),
                        pltpu.SemaphoreType.DMA((3,))])
    return pl.pallas_call(
        functools.partial(_dispatch_kernel, n_row_tiles=n_sorted // EXP_TILE), grid_spec=grid_spec,
        out_shape=jax.ShapeDtypeStruct((n_sorted, XS_COLS), BF16),
        compiler_params=pltpu.CompilerParams(dimension_semantics=("arbitrary",), vmem_limit_bytes=VMEM_LIMIT),
        name="moe_dispatch",
    )(*plan[:7], h, pos, lw["g_ffn"])


def _expert_kernel(eot_ref, misc_ref, xs_ref, wg_ref, wu_ref, wd_ref, y_ref):
    @pl.when(pl.program_id(0) >= misc_ref[0])
    def _():
        y_ref[...] = jnp.zeros_like(y_ref)

    @pl.when(pl.program_id(0) < misc_ref[0])
    def _():
        xs = xs_ref[...]
        x = xs[:, :D_MODEL]
        c = jnp.sum(xs[:, D_MODEL:].astype(F32), axis=-1, keepdims=True)
        hg = _silu(jnp.dot(x, wg_ref[0], preferred_element_type=F32)) * jnp.dot(x, wu_ref[0],
                                                                               preferred_element_type=F32)
        y_ref[...] = jnp.dot((hg * c).astype(BF16), wd_ref[0], preferred_element_type=F32).astype(BF16)


def _expert_call(plan, xs, lw):
    misc, exp_of_tile = plan[6:]
    n_row_tiles = xs.shape[0] // EXP_TILE

    def last_valid(i, nv):
        return jnp.maximum(jnp.minimum(i, nv[0] - 1), 0)

    def row_map(i, eot, nv):
        return (last_valid(i, nv), 0)

    def w_map(i, eot, nv):
        return (eot[last_valid(i, nv)], 0, 0)

    grid_spec = pltpu.PrefetchScalarGridSpec(
        num_scalar_prefetch=2, grid=(n_row_tiles,),
        in_specs=[pl.BlockSpec((EXP_TILE, XS_COLS), row_map),
                  pl.BlockSpec((1, D_MODEL, EXPERT_FF), w_map),
                  pl.BlockSpec((1, D_MODEL, EXPERT_FF), w_map),
                  pl.BlockSpec((1, EXPERT_FF, D_MODEL), w_map)],
        out_specs=pl.BlockSpec((EXP_TILE, D_MODEL), lambda i, eot, nv: (i, 0)))
    return pl.pallas_call(
        _expert_kernel, grid_spec=grid_spec,
        out_shape=jax.ShapeDtypeStruct((xs.shape[0], D_MODEL), BF16),
        compiler_params=pltpu.CompilerParams(dimension_semantics=("arbitrary",), vmem_limit_bytes=VMEM_LIMIT),
        name="moe_experts",
    )(exp_of_tile, misc, xs, lw["wg"], lw["wu"], lw["wd"])


def _combine_kernel(dst16_ref, n16_ref, lofs16_ref, ntot_ref, h_ref, p_ref, pos_ref, y_hbm,
                    g_ple_ref, w_pg_ref, w_pp_ref, g_fin_ref, o_ref, ybuf, sem, *, final):
    t = pl.program_id(0)
    nt = pl.num_programs(0)
    slot = t % 2

    def fetch(tile, sl):
        dst = ybuf.at[sl]
        for e in range(N_EXPERTS):
            k = tile * N_EXPERTS + e
            l0 = lofs16_ref[k]
            d0 = dst16_ref[k]

            def issue(c, carry, l0=l0, d0=d0):
                _chunk_copy(y_hbm, dst, d0 + c, l0 + c, sem.at[sl]).start()
                return carry
            lax.fori_loop(0, n16_ref[k], issue, 0)

    @pl.when(t == 0)
    def _():
        ybuf[...] = jnp.zeros_like(ybuf)
        fetch(t, slot)

    @pl.when(t + 1 < nt)
    def _():
        fetch(t + 1, 1 - slot)

    _wait_chunks(y_hbm, ybuf.at[slot], sem.at[slot], ntot_ref[t])

    pos = pos_ref[...]
    cols = _iota_f32((TOK_TILE, CBUF_ROWS), 1)
    pick = jnp.where((cols == pos[:, 0:1]) | (cols == pos[:, 1:2]), 1.0, 0.0).astype(BF16)
    h2 = h_ref[...] + jnp.dot(pick, ybuf[slot], preferred_element_type=F32)
    xn2 = _rmsnorm(h2, g_ple_ref[...])
    gate = _sigmoid(_dot(xn2, w_pg_ref[...]))
    h3 = h2 + gate * _dot(p_ref[...], w_pp_ref[...])
    if final:
        h3 = _rmsnorm(h3, g_fin_ref[...])
    o_ref[...] = h3


def _combine_call(plan, h, p, pos, y, lw, g_final, *, final):
    n = h.shape[0]
    nt = n // TOK_TILE
    dst16, n16, lofs16, n_tot = plan[:4]
    weights = [lw["g_ple"], lw["w_pg"], lw["w_pp"], g_final]
    grid_spec = pltpu.PrefetchScalarGridSpec(
        num_scalar_prefetch=4, grid=(nt,),
        in_specs=[pl.BlockSpec((TOK_TILE, D_MODEL), lambda t, *_: (t, 0)),
                  pl.BlockSpec((TOK_TILE, PLE_DIM), lambda t, *_: (t, 0)),
                  pl.BlockSpec((TOK_TILE, LANES), lambda t, *_: (t, 0)),
                  pl.BlockSpec(memory_space=pl.ANY)]
        + [pl.BlockSpec(w.shape, lambda t, *_: (0, 0)) for w in weights],
        out_specs=pl.BlockSpec((TOK_TILE, D_MODEL), lambda t, *_: (t, 0)),
        scratch_shapes=[pltpu.VMEM((2, CBUF_ROWS, D_MODEL), BF16), pltpu.SemaphoreType.DMA((2,))])
    return pl.pallas_call(
        functools.partial(_combine_kernel, final=final), grid_spec=grid_spec,
        out_shape=jax.ShapeDtypeStruct((n, D_MODEL), F32),
        compiler_params=pltpu.CompilerParams(dimension_semantics=("arbitrary",), vmem_limit_bytes=VMEM_LIMIT),
        name="moe_combine_final" if final else "moe_combine",
    )(dst16, n16, lofs16, n_tot, h, p, pos, y, *weights)


def _ffn(h, p, lw, g_final, *, final):
    n = h.shape[0]
    nt = n // TOK_TILE
    bound = TOP_K * n + nt * N_EXPERTS * (CHUNK - 1) + N_EXPERTS * (EXP_TILE - 1)
    n_sorted = -(-bound // EXP_TILE) * EXP_TILE
    pos, cnt = _route_call(h, lw)
    plan = _chunk_plan(cnt[:, 0, :N_EXPERTS].astype(jnp.int32), n_sorted // EXP_TILE)
    xs = _dispatch_call(plan, h, pos, lw, n_sorted)
    y = _expert_call(plan, xs, lw)
    return _combine_call(plan, h, p, pos, y, lw, g_final, final=final)


def _row(v):
    return v.reshape(1, -1).astype(F32)


def _layer_weights(l, g_mix, w_in, w_forget_up, b_forget, g_gla_out, w_conv, b_conv, g_conv_ln, b_conv_ln,
                   w_out, g_ffn, w_grp_router, b_grp_router, w_exp_router, b_exp_router, w_exp_gate,
                   w_exp_up, w_exp_down, g_ple, w_ple_gate, w_ple_proj):
    n_main = 2 * QK_COLS + 2 * V_COLS
    wi = w_in[l]
    w_lr = jnp.zeros((D_MODEL, LANES), BF16).at[:, :GLA_LOWRANK].set(
        wi[:, n_main:n_main + GLA_LOWRANK].astype(BF16))
    w_fu = jnp.zeros((LANES, QK_COLS), BF16).at[:GLA_LOWRANK].set(w_forget_up[l].astype(BF16))
    w_er = jnp.transpose(w_exp_router[l], (1, 0, 2)).reshape(D_MODEL, N_EXPERTS)
    w_rt = jnp.zeros((D_MODEL, LANES), F32).at[:, :N_EXPERTS].set(w_er)
    w_rt = w_rt.at[:, N_EXPERTS:N_EXPERTS + N_GROUPS].set(w_grp_router[l])
    b_rt = jnp.zeros((1, LANES), F32).at[0, :N_EXPERTS].set(b_exp_router[l].reshape(-1))
    b_rt = b_rt.at[0, N_EXPERTS:N_EXPERTS + N_GROUPS].set(b_grp_router[l])
    return {
        "g_mix": _row(g_mix[l]),
        "w_main": wi[:, :n_main].astype(BF16),
        "w_lr": w_lr,
        "w_fu": w_fu,
        "b_f": _row(b_forget[l]),
        "w_cv": wi[:, n_main + GLA_LOWRANK:].astype(BF16),
        "g_gla": _row(g_gla_out[l]),
        "w_conv": w_conv[l].astype(F32),
        "b_conv": _row(b_conv[l]),
        "g_ln": _row(g_conv_ln[l]),
        "b_ln": _row(b_conv_ln[l]),
        "w_out": w_out[l].astype(BF16),
        "g_ffn": _row(g_ffn[l]),
        "w_rt": w_rt,
        "b_rt": b_rt,
        "wg": w_exp_gate[l].reshape(N_EXPERTS, D_MODEL, EXPERT_FF).astype(BF16),
        "wu": w_exp_up[l].reshape(N_EXPERTS, D_MODEL, EXPERT_FF).astype(BF16),
        "wd": w_exp_down[l].reshape(N_EXPERTS, EXPERT_FF, D_MODEL).astype(BF16),
        "g_ple": _row(g_ple[l]),
        "w_pg": w_ple_gate[l].astype(BF16),
        "w_pp": w_ple_proj[l].astype(BF16),
    }


def kernel(x_prompt, x_sample, state_gla, state_conv, p_prompt, p_sample, g_mix, w_in, w_forget_up, b_forget,
           g_gla_out, w_conv, b_conv, g_conv_ln, b_conv_ln, w_out, g_ffn, w_grp_router, b_grp_router,
           w_exp_router, b_exp_router, w_exp_gate, w_exp_up, w_exp_down, g_ple, w_ple_gate, w_ple_proj, g_final):
    depth = w_in.shape[0]
    nbp, seq_p, _ = x_prompt.shape
    nbs, seq_s, _ = x_sample.shape
    n_p = nbp * seq_p
    n_s = nbs * seq_s
    layer_args = (g_mix, w_in, w_forget_up, b_forget, g_gla_out, w_conv, b_conv, g_conv_ln, b_conv_ln, w_out,
                  g_ffn, w_grp_router, b_grp_router, w_exp_router, b_exp_router, w_exp_gate, w_exp_up,
                  w_exp_down, g_ple, w_ple_gate, w_ple_proj)
    g_fin = _row(g_final)

    xp = x_prompt.reshape(n_p, D_MODEL)
    xs = x_sample.reshape(n_s, D_MODEL)
    p_all = jnp.concatenate([p_prompt.reshape(depth, n_p, PLE_DIM), p_sample.reshape(depth, n_s, PLE_DIM)],
                            axis=1)
    s_in = state_gla.reshape(depth, nbs, QK_COLS, GLA_DV)

    h = None
    sg_p, sg_s, sc_p, sc_s = [], [], [], []
    for l in range(depth):
        lw = _layer_weights(l, *layer_args)
        if l == 0:
            hp, sgp, scp = _mixer_prompt(xp, 0, nbp, seq_p, lw)
            hs, sgs, scs = _mixer_sample(xs, 0, nbs, seq_s, s_in[l], state_conv[l], lw)
        else:
            hp, sgp, scp = _mixer_prompt(h, 0, nbp, seq_p, lw)
            hs, sgs, scs = _mixer_sample(h, n_p, nbs, seq_s, s_in[l], state_conv[l], lw)
        h1 = jnp.concatenate([hp, hs], axis=0)
        h = _ffn(h1, p_all[l], lw, g_fin, final=(l == depth - 1))
        sg_p.append(sgp.reshape(nbp, GLA_HEADS, GLA_DK, GLA_DV))
        sg_s.append(sgs.reshape(nbs, GLA_HEADS, GLA_DK, GLA_DV))
        sc_p.append(scp)
        sc_s.append(scs)

    y_prompt = h[:n_p].reshape(nbp, seq_p, D_MODEL)
    y_sample = h[n_p:].reshape(nbs, seq_s, D_MODEL)
    return (y_prompt, y_sample, jnp.stack(sg_p), jnp.stack(sg_s), jnp.stack(sc_p), jnp.stack(sc_s))
```

```python
import functools

import jax
import jax.numpy as jnp
from jax import lax
from jax.experimental import pallas as pl
from jax.experimental.pallas import tpu as pltpu

D_MODEL = 1024
GLA_HEADS = 4
GLA_DK = 64
GLA_DV = 128
QK_COLS = GLA_HEADS * GLA_DK
V_COLS = GLA_HEADS * GLA_DV
CONV_WIDTH = 512
CONV_K = 31
GLA_LOWRANK = 16
GLA_TAU = 16.0
GLA_CHUNK = 64
PLE_DIM = 256
N_GROUPS = 4
EXPERTS_PER_GROUP = 8
N_EXPERTS = N_GROUPS * EXPERTS_PER_GROUP
EXPERT_FF = 256
TOP_K = 2
EPS = 1e-6
N_MAIN = 2 * QK_COLS + 2 * V_COLS

LANES = 128
SUBLANES = 8
CONV_PAD = 32
CONV_OFF = CONV_PAD - (CONV_K - 1)
VMEM_LIMIT = 56 * 1024 * 1024
TOK_TILE = 512
CHUNK = 16
EXP_TILE = 256
CBUF_ROWS = -(-(TOP_K * TOK_TILE + N_EXPERTS * (CHUNK - 1)) // LANES) * LANES
XS_COLS = D_MODEL + LANES

F32 = jnp.float32
BF16 = jnp.bfloat16
HI = lax.Precision.HIGHEST


def _sigmoid(x):
    return 1.0 / (1.0 + jnp.exp(-x))


def _silu(x):
    return x * _sigmoid(x)


def _log_sigmoid(x):
    return jnp.minimum(x, 0.0) - jnp.log(1.0 + jnp.exp(-jnp.abs(x)))


def _rmsnorm(x, g):
    return x * lax.rsqrt(jnp.mean(x * x, axis=-1, keepdims=True) + EPS) * g


def _dot(a, b):
    return jnp.dot(a.astype(BF16), b.astype(BF16), preferred_element_type=F32)


def _dot_t(a, b):
    return lax.dot_general(a.astype(BF16), b.astype(BF16), (((1,), (1,)), ((), ())),
                           preferred_element_type=F32)


def _dot_hi(a, b):
    return jnp.dot(a, b, preferred_element_type=F32, precision=HI)


def _iota_f32(shape, dim):
    return lax.broadcasted_iota(jnp.int32, shape, dim).astype(F32)


def _const_spec(shape):
    nd = len(shape)
    return pl.BlockSpec(shape, lambda *_: (0,) * nd)


def _layer_spec(arr, l, cols=None):
    nd = arr.ndim - 1
    shape = arr.shape[1:] if cols is None else arr.shape[1:-1] + (cols,)
    return pl.BlockSpec((None,) + shape, lambda *_: (l,) + (0,) * nd, pipeline_mode=pl.Buffered(1))


def _cast_mixer_weights(w_in_ref, w_cv_ref, w_fu_ref, w_out_ref, wmain_s, wlr_s, wcv_s, wfu_s, wout_s):
    wmain_s[...] = w_in_ref[:, 0:N_MAIN].astype(BF16)
    lane = lax.broadcasted_iota(jnp.int32, (D_MODEL, LANES), 1)
    wlr_s[...] = jnp.where(lane < GLA_LOWRANK, w_in_ref[:, N_MAIN:N_MAIN + LANES], 0.0).astype(BF16)
    wcv_s[...] = w_cv_ref[...].astype(BF16)
    wfu_s[...] = jnp.zeros_like(wfu_s)
    wfu_s[0:GLA_LOWRANK, :] = w_fu_ref[...].astype(BF16)
    wout_s[...] = w_out_ref[...].astype(BF16)


def _project(x, g_mix, wmain_s, wlr_s, wfu_s, b_f, wcv_s):
    xn = _rmsnorm(x, g_mix).astype(BF16)
    z = jnp.dot(xn, wmain_s[...], preferred_element_type=F32)
    q = z[:, :QK_COLS] * (GLA_DK ** -0.5)
    k = z[:, QK_COLS:2 * QK_COLS]
    v = z[:, 2 * QK_COLS:2 * QK_COLS + V_COLS]
    og = z[:, 2 * QK_COLS + V_COLS:]
    lr = jnp.dot(xn, wlr_s[...], preferred_element_type=F32)
    zf = _dot(lr, wfu_s[...]) + b_f
    la = _log_sigmoid(zf) * (1.0 / GLA_TAU)
    cv = jnp.dot(xn, wcv_s[...], preferred_element_type=F32)
    u = cv[:, :CONV_WIDTH] * _sigmoid(cv[:, CONV_WIDTH:])
    return q, k, v, og, la, u


def _stack_heads(qd):
    lane = lax.broadcasted_iota(jnp.int32, qd.shape, 1)
    return jnp.concatenate(
        [jnp.where((lane >= h * GLA_DK) & (lane < (h + 1) * GLA_DK), qd, 0.0) for h in range(GLA_HEADS)],
        axis=0)


def _gated_head_norm(o, og, g_gla):
    outs = []
    for h in range(GLA_HEADS):
        sl = slice(h * GLA_DV, (h + 1) * GLA_DV)
        outs.append(_rmsnorm(o[:, sl], g_gla) * _silu(og[:, sl]))
    return jnp.concatenate(outs, axis=1)


def _causal_conv(win, w_conv, b_conv, n):
    acc = jnp.broadcast_to(b_conv, (n, CONV_WIDTH))
    for s in range(SUBLANES):
        taps = [j for j in range(CONV_K) if (CONV_OFF + j) % SUBLANES == s]
        if not taps:
            continue
        rows = n if s == 0 else n + SUBLANES
        part = None
        for j in taps:
            a = (CONV_OFF + j) - s
            term = w_conv[j:j + 1, :] * win[a:a + rows, :]
            part = term if part is None else part + term
        acc = acc + part[s:s + n, :]
    return acc


def _conv_ln_act(acc, g_ln, b_ln):
    mu = jnp.mean(acc, axis=-1, keepdims=True)
    xc = acc - mu
    y = xc * lax.rsqrt(jnp.mean(xc * xc, axis=-1, keepdims=True) + EPS) * g_ln + b_ln
    return _silu(y)


def _head_diag(upd):
    return jnp.concatenate([upd[h * GLA_DK:(h + 1) * GLA_DK, h * GLA_DV:(h + 1) * GLA_DV]
                            for h in range(GLA_HEADS)], axis=0)


def _mixer_weight_args(l, P):
    names = ("g_mix", "w_in", "w_cv", "w_forget_up", "b_forget", "g_gla_out", "w_conv", "b_conv", "g_conv_ln",
             "b_conv_ln", "w_out")
    arrs = [P[n] for n in names]
    return arrs, [_layer_spec(a, l, cols=N_MAIN + LANES if n == "w_in" else None) for n, a in zip(names, arrs)]


_MIXER_WEIGHT_SCRATCH = [pltpu.VMEM((D_MODEL, N_MAIN), BF16), pltpu.VMEM((D_MODEL, LANES), BF16),
                         pltpu.VMEM((D_MODEL, 2 * CONV_WIDTH), BF16), pltpu.VMEM((LANES, QK_COLS), BF16),
                         pltpu.VMEM((D_MODEL, D_MODEL), BF16)]


def _mixer_prompt_kernel(x_ref, g_mix_ref, w_in_ref, w_cv_ref, w_fu_ref, b_f_ref, g_gla_ref, w_conv_ref,
                         b_conv_ref, g_ln_ref, b_ln_ref, w_out_ref,
                         h_ref, sg_ref, sc_ref,
                         wmain_s, wlr_s, wcv_s, wfu_s, wout_s,
                         s_ref, ubuf_ref, q_ref, k_ref, la_ref, v_ref, og_ref, mix_ref, *, tt):
    t = pl.program_id(1)
    nt = pl.num_programs(1)
    C = GLA_CHUNK

    @pl.when((pl.program_id(0) == 0) & (t == 0))
    def _():
        _cast_mixer_weights(w_in_ref, w_cv_ref, w_fu_ref, w_out_ref, wmain_s, wlr_s, wcv_s, wfu_s, wout_s)

    @pl.when(t == 0)
    def _():
        s_ref[...] = jnp.zeros_like(s_ref)
        ubuf_ref[0:CONV_PAD, :] = jnp.zeros((CONV_PAD, CONV_WIDTH), F32)

    x = x_ref[...]
    q, k, v, og, la, u = _project(x, g_mix_ref[...], wmain_s, wlr_s, wfu_s, b_f_ref[...], wcv_s)
    q_ref[...] = q
    k_ref[...] = k
    la_ref[...] = la
    v_ref[...] = v
    og_ref[...] = og
    ubuf_ref[CONV_PAD:CONV_PAD + tt, :] = u

    row = lax.broadcasted_iota(jnp.int32, (C, C), 0)
    col = lax.broadcasted_iota(jnp.int32, (C, C), 1)
    tri = (col <= row).astype(F32)
    r4 = lax.broadcasted_iota(jnp.int32, (GLA_HEADS * C, C), 0)
    c4 = lax.broadcasted_iota(jnp.int32, (GLA_HEADS * C, C), 1)
    causal4 = c4 <= (r4 % C)
    g_gla = g_gla_ref[...]
    w_conv = w_conv_ref[...]
    b_conv = b_conv_ref[...]
    g_ln = g_ln_ref[...]
    b_ln = b_ln_ref[...]

    def chunk(c, carry):
        r0 = pl.multiple_of(c * C, C)
        rows = pl.ds(r0, C)
        qc = q_ref[rows, :]
        kc = k_ref[rows, :]
        lac = la_ref[rows, :]
        vc = v_ref[rows, :]
        ogc = og_ref[rows, :]
        s_old = s_ref[...]

        b = _dot_hi(tri, lac)
        b_last = b[C - 1:C, :]
        qd = qc * jnp.exp(b)
        kd = kc * jnp.exp(-b)
        kl = kc * jnp.exp(b_last - b)
        qs = _stack_heads(qd).astype(BF16)
        scores = jnp.where(causal4, _dot_t(qs, kd), 0.0)
        o_inter = jnp.dot(qs, s_old.astype(BF16), preferred_element_type=F32)
        klt = jnp.concatenate([kl, jnp.broadcast_to(jnp.exp(b_last), (C, QK_COLS))], axis=0).T
        decay = klt[:, C:C + 1]
        upd = _dot(klt[:, :C], vc)
        s_ref[...] = decay * s_old + _head_diag(upd)
        o_parts = []
        for h in range(GLA_HEADS):
            vh = vc[:, h * GLA_DV:(h + 1) * GLA_DV]
            o_parts.append(_dot(scores[h * C:(h + 1) * C, :], vh) + o_inter[h * C:(h + 1) * C, :])
        o = jnp.concatenate(o_parts, axis=1)
        mix_ref[rows, 0:V_COLS] = _gated_head_norm(o, ogc, g_gla).astype(BF16)

        win = ubuf_ref[pl.ds(r0, C + CONV_PAD), :]
        acc = _causal_conv(win, w_conv, b_conv, C)
        mix_ref[rows, V_COLS:] = _conv_ln_act(acc, g_ln, b_ln).astype(BF16)
        return carry

    lax.fori_loop(0, tt // C, chunk, 0)

    h_ref[...] = x + jnp.dot(mix_ref[...], wout_s[...], preferred_element_type=F32)
    tail = ubuf_ref[tt:tt + CONV_PAD, :]
    ubuf_ref[0:CONV_PAD, :] = tail

    @pl.when(t == nt - 1)
    def _():
        sg_ref[0] = s_ref[...]
        sc_ref[0] = tail[CONV_OFF:, :]


def _mixer_prompt(x, row_off, nb, seq, l, P, *, tt=512):
    nt = seq // tt
    blk_off = row_off // tt
    weights, w_specs = _mixer_weight_args(l, P)
    in_specs = [pl.BlockSpec((tt, D_MODEL), lambda b, t: (blk_off + b * nt + t, 0))] + w_specs
    out_shape = (jax.ShapeDtypeStruct((nb * seq, D_MODEL), F32),
                 jax.ShapeDtypeStruct((nb, QK_COLS, GLA_DV), F32),
                 jax.ShapeDtypeStruct((nb, CONV_K - 1, CONV_WIDTH), F32))
    out_specs = (pl.BlockSpec((tt, D_MODEL), lambda b, t: (b * nt + t, 0)),
                 pl.BlockSpec((1, QK_COLS, GLA_DV), lambda b, t: (b, 0, 0)),
                 pl.BlockSpec((1, CONV_K - 1, CONV_WIDTH), lambda b, t: (b, 0, 0)))
    scratch = _MIXER_WEIGHT_SCRATCH + [
        pltpu.VMEM((QK_COLS, GLA_DV), F32),
        pltpu.VMEM((CONV_PAD + tt, CONV_WIDTH), F32),
        pltpu.VMEM((tt, QK_COLS), F32), pltpu.VMEM((tt, QK_COLS), F32), pltpu.VMEM((tt, QK_COLS), F32),
        pltpu.VMEM((tt, V_COLS), F32), pltpu.VMEM((tt, V_COLS), F32),
        pltpu.VMEM((tt, D_MODEL), BF16)]
    return pl.pallas_call(
        functools.partial(_mixer_prompt_kernel, tt=tt),
        grid=(nb, nt), in_specs=in_specs, out_specs=out_specs, out_shape=out_shape,
        scratch_shapes=scratch,
        compiler_params=pltpu.CompilerParams(dimension_semantics=("arbitrary", "arbitrary"),
                                             vmem_limit_bytes=VMEM_LIMIT),
        name="mixer_prompt",
    )(x, *weights)


def _mixer_sample_kernel(x_ref, s_in_ref, c_in_ref, g_mix_ref, w_in_ref, w_cv_ref, w_fu_ref, b_f_ref,
                         g_gla_ref, w_conv_ref, b_conv_ref, g_ln_ref, b_ln_ref, w_out_ref,
                         h_ref, sg_ref, sc_ref,
                         wmain_s, wlr_s, wcv_s, wfu_s, wout_s,
                         qs_ref, klt_ref, dect_ref, v_ref, u_ref, oi_ref, cbuf_ref, cacc_ref, *, sb, seq):
    R = sb * seq

    @pl.when(pl.program_id(0) == 0)
    def _():
        _cast_mixer_weights(w_in_ref, w_cv_ref, w_fu_ref, w_out_ref, wmain_s, wlr_s, wcv_s, wfu_s, wout_s)

    x = x_ref[...]
    q, k, v, og, la, u = _project(x, g_mix_ref[...], wmain_s, wlr_s, wfu_s, b_f_ref[...], wcv_s)
    row = lax.broadcasted_iota(jnp.int32, (R, R), 0)
    col = lax.broadcasted_iota(jnp.int32, (R, R), 1)
    same = (row // seq) == (col // seq)
    b = _dot_hi((same & (col <= row)).astype(F32), la)
    b_tot = _dot_hi(same.astype(F32), la)
    qd = q * jnp.exp(b)
    kd = k * jnp.exp(-b)
    kl = k * jnp.exp(b_tot - b)
    qs = _stack_heads(qd)
    r4 = lax.broadcasted_iota(jnp.int32, (GLA_HEADS * R, R), 0) % R
    c4 = lax.broadcasted_iota(jnp.int32, (GLA_HEADS * R, R), 1)
    mask4 = ((r4 // seq) == (c4 // seq)) & (c4 <= r4)
    scores = jnp.where(mask4, _dot_t(qs, kd), 0.0)

    qs_ref[...] = qs
    klt_ref[...] = kl.T
    dect_ref[...] = jnp.exp(b_tot).T
    v_ref[...] = v
    u_ref[...] = u

    lane_r = lax.broadcasted_iota(jnp.int32, (QK_COLS, R), 1)
    w_conv = w_conv_ref[...]
    b_conv = b_conv_ref[...]

    def one_seq(i, carry):
        lo = i * seq
        r0 = pl.multiple_of(lo, seq)
        s_old = s_in_ref[i]
        qsel = jnp.concatenate([qs_ref[pl.ds(pl.multiple_of(h * R + lo, seq), seq), :]
                                for h in range(GLA_HEADS)], axis=0)
        oi = _dot(qsel, s_old)
        for h in range(GLA_HEADS):
            oi_ref[h, pl.ds(r0, seq), :] = oi[h * seq:(h + 1) * seq, :]
        lmask = (lane_r >= lo) & (lane_r < lo + seq)
        upd = _dot(jnp.where(lmask, klt_ref[...], 0.0), v_ref[...])
        dec = jnp.sum(jnp.where(lmask, dect_ref[...], 0.0), axis=1, keepdims=True) * (1.0 / seq)
        sg_ref[i] = dec * s_old + _head_diag(upd)
        cbuf_ref[CONV_OFF:CONV_PAD, :] = c_in_ref[i]
        cbuf_ref[CONV_PAD:CONV_PAD + seq, :] = u_ref[pl.ds(r0, seq), :]
        cacc_ref[pl.ds(r0, seq), :] = _causal_conv(cbuf_ref[...], w_conv, b_conv, seq)
        sc_ref[i] = cbuf_ref[CONV_OFF + seq:CONV_PAD + seq, :]
        return carry

    lax.fori_loop(0, sb, one_seq, 0)

    o_parts = []
    for h in range(GLA_HEADS):
        vh = v[:, h * GLA_DV:(h + 1) * GLA_DV]
        o_parts.append(_dot(scores[h * R:(h + 1) * R, :], vh) + oi_ref[h])
    o = jnp.concatenate(o_parts, axis=1)
    mix = jnp.concatenate([_gated_head_norm(o, og, g_gla_ref[...]),
                           _conv_ln_act(cacc_ref[...], g_ln_ref[...], b_ln_ref[...])], axis=1)
    h_ref[...] = x + jnp.dot(mix.astype(BF16), wout_s[...], preferred_element_type=F32)


def _mixer_sample(x, row_off, nb, seq, l, s_in, c_in, P, *, sb=16):
    R = sb * seq
    blk_off = row_off // R
    weights, w_specs = _mixer_weight_args(l, P)
    in_specs = [pl.BlockSpec((R, D_MODEL), lambda i: (blk_off + i, 0)),
                pl.BlockSpec((None, sb, QK_COLS, GLA_DV), lambda i: (l, i, 0, 0)),
                pl.BlockSpec((None, sb, CONV_K - 1, CONV_WIDTH), lambda i: (l, i, 0, 0))] + w_specs
    out_shape = (jax.ShapeDtypeStruct((nb * seq, D_MODEL), F32),
                 jax.ShapeDtypeStruct((nb, QK_COLS, GLA_DV), F32),
                 jax.ShapeDtypeStruct((nb, CONV_K - 1, CONV_WIDTH), F32))
    out_specs = (pl.BlockSpec((R, D_MODEL), lambda i: (i, 0)),
                 pl.BlockSpec((sb, QK_COLS, GLA_DV), lambda i: (i, 0, 0)),
                 pl.BlockSpec((sb, CONV_K - 1, CONV_WIDTH), lambda i: (i, 0, 0)))
    scratch = _MIXER_WEIGHT_SCRATCH + [
        pltpu.VMEM((GLA_HEADS * R, QK_COLS), F32),
        pltpu.VMEM((QK_COLS, R), F32), pltpu.VMEM((QK_COLS, R), F32),
        pltpu.VMEM((R, V_COLS), F32), pltpu.VMEM((R, CONV_WIDTH), F32),
        pltpu.VMEM((GLA_HEADS, R, GLA_DV), F32),
        pltpu.VMEM((CONV_PAD + seq, CONV_WIDTH), F32),
        pltpu.VMEM((R, CONV_WIDTH), F32)]
    return pl.pallas_call(
        functools.partial(_mixer_sample_kernel, sb=sb, seq=seq),
        grid=(nb // sb,), in_specs=in_specs, out_specs=out_specs, out_shape=out_shape,
        scratch_shapes=scratch,
        compiler_params=pltpu.CompilerParams(dimension_semantics=("arbitrary",),
                                             vmem_limit_bytes=VMEM_LIMIT),
        name="mixer_sample",
    )(x, s_in, c_in, *weights)


def _pair_specs(n_first_tiles, width):
    return [pl.BlockSpec((TOK_TILE, width), lambda t, *_: (jnp.minimum(t, n_first_tiles - 1), 0)),
            pl.BlockSpec((TOK_TILE, width), lambda t, *_: (jnp.maximum(t - n_first_tiles, 0), 0))]


def _pick(t, n_first_tiles, a_ref, b_ref):
    return jnp.where(t < n_first_tiles, a_ref[...], b_ref[...])


def _route(logits):
    lane = lax.broadcasted_iota(jnp.int32, logits.shape, 1)
    lane_f = lane.astype(F32)
    neg = jnp.float32(-jnp.inf)
    big = jnp.float32(1e9)
    is_grp = (lane >= N_EXPERTS) & (lane < N_EXPERTS + N_GROUPS)
    gl = jnp.where(is_grp, logits, neg)
    gmax = jnp.max(gl, axis=-1, keepdims=True)
    gidx = jnp.min(jnp.where(is_grp & (gl == gmax), lane_f - N_EXPERTS, big), axis=-1, keepdims=True)
    gsum = jnp.sum(jnp.where(is_grp, jnp.exp(gl - gmax), 0.0), axis=-1, keepdims=True)
    g_w = 1.0 / gsum
    grp_of_lane = jnp.floor(lane_f * (1.0 / EXPERTS_PER_GROUP))
    in_grp = (lane < N_EXPERTS) & (grp_of_lane == gidx)
    ml = jnp.where(in_grp, logits, neg)
    v1 = jnp.max(ml, axis=-1, keepdims=True)
    i1 = jnp.min(jnp.where(in_grp & (ml == v1), lane_f, big), axis=-1, keepdims=True)
    ml2 = jnp.where(lane_f == i1, neg, ml)
    v2 = jnp.max(ml2, axis=-1, keepdims=True)
    i2 = jnp.min(jnp.where(in_grp & (ml2 == v2), lane_f, big), axis=-1, keepdims=True)
    e2 = jnp.exp(v2 - v1)
    w1 = g_w / (1.0 + e2)
    w2 = g_w * e2 / (1.0 + e2)
    return i1, i2, w1, w2


def _route_kernel(hp_ref, hs_ref, g_ffn_ref, w_rt_ref, b_rt_ref, pos_ref, cnt_ref, *, npt):
    T = TOK_TILE
    xn = _rmsnorm(_pick(pl.program_id(0), npt, hp_ref, hs_ref), g_ffn_ref[...])
    i1, i2, w1, w2 = _route(_dot_hi(xn, w_rt_ref[...]) + b_rt_ref[...])
    lane = _iota_f32((T, LANES), 1)
    a0 = (lane == i1).astype(F32)
    a1 = (lane == i2).astype(F32)
    a = a0 + a1
    cnt = jnp.sum(a, axis=0, keepdims=True)
    earlier = (_iota_f32((T, T), 1) < _iota_f32((T, T), 0)).astype(BF16)
    rank = jnp.dot(earlier, a.astype(BF16), preferred_element_type=F32)
    cnt_pad = jnp.ceil(cnt * (1.0 / CHUNK)) * CHUNK
    below = (_iota_f32((LANES, LANES), 0) < _iota_f32((LANES, LANES), 1)).astype(F32)
    first = _dot_hi(jnp.broadcast_to(cnt_pad, (SUBLANES, LANES)), below)[0:1, :]
    base = first + rank
    pos0 = jnp.sum(a0 * base, axis=1, keepdims=True)
    pos1 = jnp.sum(a1 * base, axis=1, keepdims=True)
    pos_ref[...] = jnp.where(lane == 0.0, pos0, jnp.where(lane == 1.0, pos1, jnp.where(
        lane == 2.0, w1, jnp.where(lane == 3.0, w2, 0.0))))
    cnt_ref[0] = jnp.broadcast_to(cnt, (SUBLANES, LANES))


def _route_call(hp, hs, l, P, w_rt, b_rt):
    npt = hp.shape[0] // TOK_TILE
    nt = npt + hs.shape[0] // TOK_TILE
    return pl.pallas_call(
        functools.partial(_route_kernel, npt=npt), grid=(nt,),
        in_specs=_pair_specs(npt, D_MODEL) + [_layer_spec(P["g_ffn"], l), _const_spec(w_rt.shape),
                                              _const_spec(b_rt.shape)],
        out_specs=(pl.BlockSpec((TOK_TILE, LANES), lambda t: (t, 0)),
                   pl.BlockSpec((1, SUBLANES, LANES), lambda t: (t, 0, 0))),
        out_shape=(jax.ShapeDtypeStruct((nt * TOK_TILE, LANES), F32),
                   jax.ShapeDtypeStruct((nt, SUBLANES, LANES), F32)),
        compiler_params=pltpu.CompilerParams(dimension_semantics=("arbitrary",), vmem_limit_bytes=VMEM_LIMIT),
        name="moe_route",
    )(hp, hs, P["g_ffn"], w_rt, b_rt)


def _chunk_plan(cnt, n_row_tiles):
    n16 = (cnt + (CHUNK - 1)) // CHUNK
    lofs16 = jnp.cumsum(n16, axis=1) - n16
    tile_pref16 = jnp.cumsum(n16, axis=0) - n16
    tot16 = jnp.sum(n16, axis=0)
    per_tile = EXP_TILE // CHUNK
    seg16 = ((tot16 + per_tile - 1) // per_tile) * per_tile
    seg_end16 = jnp.cumsum(seg16)
    dst16 = (seg_end16 - seg16)[None, :] + tile_pref16
    n_tot = jnp.sum(n16, axis=1)
    gap16 = seg_end16 - seg16 + tot16
    gapn16 = seg16 - tot16
    tile_start16 = jnp.arange(n_row_tiles, dtype=jnp.int32) * per_tile
    n_valid = seg_end16[-1] // per_tile
    misc = jnp.stack([n_valid, jnp.sum(gapn16)])
    exp_of_tile = jnp.minimum(jnp.sum(seg_end16[None, :] <= tile_start16[:, None], axis=1), N_EXPERTS - 1)
    i32 = lambda a: a.astype(jnp.int32).reshape(-1)
    return i32(dst16), i32(n16), i32(lofs16), i32(n_tot), i32(gap16), i32(gapn16), i32(misc), i32(exp_of_tile)


def _chunk_copy(src, dst, src_chunk, dst_chunk, sem):
    return pltpu.make_async_copy(src.at[pl.ds(pl.multiple_of(src_chunk * CHUNK, CHUNK), CHUNK), :],
                                 dst.at[pl.ds(pl.multiple_of(dst_chunk * CHUNK, CHUNK), CHUNK), :], sem)


def _tile_copy(src, dst, dst_tile, sem):
    return pltpu.make_async_copy(src, dst.at[pl.ds(pl.multiple_of(dst_tile * EXP_TILE, EXP_TILE), EXP_TILE), :], sem)


def _wait_chunks(src, dst, sem, n):
    def body(_, carry):
        _chunk_copy(src, dst, 0, 0, sem).wait()
        return carry
    lax.fori_loop(0, n, body, 0)


def _dispatch_kernel(dst16_ref, n16_ref, lofs16_ref, ntot_ref, gap16_ref, gapn16_ref, misc_ref,
                     hp_ref, hs_ref, pos_ref, g_ffn_ref, xs_hbm, cbuf, zbuf, sem, *, n_row_tiles, npt):
    t = pl.program_id(0)
    nt = pl.num_programs(0)
    slot = t % 2
    T = TOK_TILE
    n_tail = n_row_tiles - misc_ref[0]

    @pl.when(t == 0)
    def _():
        zbuf[...] = jnp.zeros_like(zbuf)
        for e in range(N_EXPERTS):
            g0 = gap16_ref[e]

            def fill(c, carry, g0=g0):
                _chunk_copy(zbuf, xs_hbm, 0, g0 + c, sem.at[2]).start()
                return carry
            lax.fori_loop(0, gapn16_ref[e], fill, 0)

        def fill_tile(i, carry):
            _tile_copy(zbuf, xs_hbm, misc_ref[0] + i, sem.at[2]).start()
            return carry
        lax.fori_loop(0, n_tail, fill_tile, 0)

    xn = _rmsnorm(_pick(t, npt, hp_ref, hs_ref), g_ffn_ref[...]).astype(BF16)
    pos = pos_ref[...]
    pos_t = pos.T
    rows = _iota_f32((CBUF_ROWS, T), 0)
    sel0 = rows == pos_t[0:1, :]
    sel1 = rows == pos_t[1:2, :]
    lane = lax.broadcasted_iota(jnp.int32, (T, LANES), 1)

    def weight_cols(c):
        hi = c.astype(BF16).astype(F32)
        mid = (c - hi).astype(BF16).astype(F32)
        lo = c - hi - mid
        return jnp.where(lane == 0, hi, jnp.where(lane == 1, mid, jnp.where(lane == 2, lo, 0.0))).astype(BF16)

    p0 = jnp.where(sel0, 1.0, 0.0).astype(BF16)
    p1 = jnp.where(sel1, 1.0, 0.0).astype(BF16)
    cbuf[slot, :, 0:D_MODEL] = jnp.dot(p0 + p1, xn, preferred_element_type=F32).astype(BF16)
    cbuf[slot, :, D_MODEL:] = (jnp.dot(p0, weight_cols(pos[:, 2:3]), preferred_element_type=F32)
                               + jnp.dot(p1, weight_cols(pos[:, 3:4]), preferred_element_type=F32)).astype(BF16)

    src = cbuf.at[slot]
    for e in range(N_EXPERTS):
        k = t * N_EXPERTS + e
        l0 = lofs16_ref[k]
        d0 = dst16_ref[k]

        def issue(c, carry, l0=l0, d0=d0):
            _chunk_copy(src, xs_hbm, l0 + c, d0 + c, sem.at[slot]).start()
            return carry
        lax.fori_loop(0, n16_ref[k], issue, 0)

    @pl.when(t > 0)
    def _():
        _wait_chunks(cbuf.at[1 - slot], xs_hbm, sem.at[1 - slot], ntot_ref[t - 1])

    @pl.when(t == nt - 1)
    def _():
        _wait_chunks(src, xs_hbm, sem.at[slot], ntot_ref[t])
        _wait_chunks(zbuf, xs_hbm, sem.at[2], misc_ref[1])

        def wait_tile(_, carry):
            _tile_copy(zbuf, xs_hbm, 0, sem.at[2]).wait()
            return carry
        lax.fori_loop(0, n_tail, wait_tile, 0)


def _dispatch_call(plan, hp, hs, pos, l, P, n_sorted):
    npt = hp.shape[0] // TOK_TILE
    nt = npt + hs.shape[0] // TOK_TILE
    g_ffn = P["g_ffn"]
    grid_spec = pltpu.PrefetchScalarGridSpec(
        num_scalar_prefetch=7, grid=(nt,),
        in_specs=_pair_specs(npt, D_MODEL) + [
            pl.BlockSpec((TOK_TILE, LANES), lambda t, *_: (t, 0)),
            pl.BlockSpec((None,) + g_ffn.shape[1:], lambda t, *_: (l, 0, 0))],
        out_specs=pl.BlockSpec(memory_space=pl.ANY),
        scratch_shapes=[pltpu.VMEM((2, CBUF_ROWS, XS_COLS), BF16), pltpu.VMEM((EXP_TILE, XS_COLS), BF16),
                        pltpu.SemaphoreType.DMA((3,))])
    return pl.pallas_call(
        functools.partial(_dispatch_kernel, n_row_tiles=n_sorted // EXP_TILE, npt=npt), grid_spec=grid_spec,
        out_shape=jax.ShapeDtypeStruct((n_sorted, XS_COLS), BF16),
        compiler_params=pltpu.CompilerParams(dimension_semantics=("arbitrary",), vmem_limit_bytes=VMEM_LIMIT),
        name="moe_dispatch",
    )(*plan[:7], hp, hs, pos, g_ffn)


def _expert_kernel(eot_ref, misc_ref, xs_ref, wg_ref, wu_ref, wd_ref, y_ref, wg_s, wu_s, wd_s):
    i = pl.program_id(0)
    valid = i < misc_ref[0]

    @pl.when(jnp.logical_not(valid))
    def _():
        y_ref[...] = jnp.zeros_like(y_ref)

    @pl.when(valid & ((i == 0) | (eot_ref[i] != eot_ref[jnp.maximum(i - 1, 0)])))
    def _():
        wg_s[...] = wg_ref[...].astype(BF16)
        wu_s[...] = wu_ref[...].astype(BF16)
        wd_s[...] = wd_ref[...].astype(BF16)

    @pl.when(valid)
    def _():
        xs = xs_ref[...]
        x = xs[:, :D_MODEL]
        c = jnp.sum(xs[:, D_MODEL:].astype(F32), axis=-1, keepdims=True)
        hg = _silu(jnp.dot(x, wg_s[...], preferred_element_type=F32)) * jnp.dot(x, wu_s[...],
                                                                               preferred_element_type=F32)
        y_ref[...] = jnp.dot((hg * c).astype(BF16), wd_s[...], preferred_element_type=F32).astype(BF16)


def _expert_call(plan, xs, l, wg, wu, wd):
    misc, exp_of_tile = plan[6:]
    n_row_tiles = xs.shape[0] // EXP_TILE

    def last_valid(i, nv):
        return jnp.maximum(jnp.minimum(i, nv[0] - 1), 0)

    def row_map(i, eot, nv):
        return (last_valid(i, nv), 0)

    def w_map(i, eot, nv):
        return (l * N_EXPERTS + eot[last_valid(i, nv)], 0, 0)

    grid_spec = pltpu.PrefetchScalarGridSpec(
        num_scalar_prefetch=2, grid=(n_row_tiles,),
        in_specs=[pl.BlockSpec((EXP_TILE, XS_COLS), row_map),
                  pl.BlockSpec((None, D_MODEL, EXPERT_FF), w_map),
                  pl.BlockSpec((None, D_MODEL, EXPERT_FF), w_map),
                  pl.BlockSpec((None, EXPERT_FF, D_MODEL), w_map)],
        out_specs=pl.BlockSpec((EXP_TILE, D_MODEL), lambda i, eot, nv: (i, 0)),
        scratch_shapes=[pltpu.VMEM((D_MODEL, EXPERT_FF), BF16), pltpu.VMEM((D_MODEL, EXPERT_FF), BF16),
                        pltpu.VMEM((EXPERT_FF, D_MODEL), BF16)])
    return pl.pallas_call(
        _expert_kernel, grid_spec=grid_spec,
        out_shape=jax.ShapeDtypeStruct((xs.shape[0], D_MODEL), BF16),
        compiler_params=pltpu.CompilerParams(dimension_semantics=("arbitrary",), vmem_limit_bytes=VMEM_LIMIT),
        name="moe_experts",
    )(exp_of_tile, misc, xs, wg, wu, wd)


def _combine_kernel(dst16_ref, n16_ref, lofs16_ref, ntot_ref, hp_ref, hs_ref, pp_ref, ps_ref, pos_ref, y_hbm,
                    g_ple_ref, w_pg_ref, w_pp_ref, g_fin_ref, *rest, final, npt):
    if final:
        op_ref, os_ref, ybuf, wpg_s, wpp_s, sem = rest
    else:
        o_ref, ybuf, wpg_s, wpp_s, sem = rest
    t = pl.program_id(0)
    nt = pl.num_programs(0)
    slot = t % 2

    def fetch(tile, sl):
        dst = ybuf.at[sl]
        for e in range(N_EXPERTS):
            k = tile * N_EXPERTS + e
            l0 = lofs16_ref[k]
            d0 = dst16_ref[k]

            def issue(c, carry, l0=l0, d0=d0):
                _chunk_copy(y_hbm, dst, d0 + c, l0 + c, sem.at[sl]).start()
                return carry
            lax.fori_loop(0, n16_ref[k], issue, 0)

    @pl.when(t == 0)
    def _():
        ybuf[...] = jnp.zeros_like(ybuf)
        fetch(t, slot)
        wpg_s[...] = w_pg_ref[...].astype(BF16)
        wpp_s[...] = w_pp_ref[...].astype(BF16)

    @pl.when(t + 1 < nt)
    def _():
        fetch(t + 1, 1 - slot)

    _wait_chunks(y_hbm, ybuf.at[slot], sem.at[slot], ntot_ref[t])

    pos = pos_ref[...]
    cols = _iota_f32((TOK_TILE, CBUF_ROWS), 1)
    pick = jnp.where((cols == pos[:, 0:1]) | (cols == pos[:, 1:2]), 1.0, 0.0).astype(BF16)
    h2 = _pick(t, npt, hp_ref, hs_ref) + jnp.dot(pick, ybuf[slot], preferred_element_type=F32)
    xn2 = _rmsnorm(h2, g_ple_ref[...])
    gate = _sigmoid(_dot(xn2, wpg_s[...]))
    p = jnp.where(t < npt, pp_ref[...], ps_ref[...])
    h3 = h2 + gate * _dot(p, wpp_s[...])
    if final:
        h3 = _rmsnorm(h3, g_fin_ref[...])

        @pl.when(t < npt)
        def _():
            op_ref[...] = h3

        @pl.when(t >= npt)
        def _():
            os_ref[...] = h3
    else:
        o_ref[...] = h3


def _combine_call(plan, hp, hs, pp, ps, pos, y, l, P, g_final, *, final):
    npt = hp.shape[0] // TOK_TILE
    nst = hs.shape[0] // TOK_TILE
    nt = npt + nst

    def lmap(t, *_):
        return (l, 0, 0)

    in_specs = _pair_specs(npt, D_MODEL) + [
        pl.BlockSpec((None, TOK_TILE, PLE_DIM), lambda t, *_: (l, jnp.minimum(t, npt - 1), 0)),
        pl.BlockSpec((None, TOK_TILE, PLE_DIM), lambda t, *_: (l, jnp.maximum(t - npt, 0), 0)),
        pl.BlockSpec((TOK_TILE, LANES), lambda t, *_: (t, 0)),
        pl.BlockSpec(memory_space=pl.ANY),
        pl.BlockSpec((None,) + P["g_ple"].shape[1:], lmap),
        pl.BlockSpec((None,) + P["w_ple_gate"].shape[1:], lmap),
        pl.BlockSpec((None,) + P["w_ple_proj"].shape[1:], lmap),
        pl.BlockSpec(g_final.shape, lambda t, *_: (0, 0))]
    if final:
        out_specs = (pl.BlockSpec((TOK_TILE, D_MODEL), lambda t, *_: (jnp.minimum(t, npt - 1), 0)),
                     pl.BlockSpec((TOK_TILE, D_MODEL), lambda t, *_: (jnp.maximum(t - npt, 0), 0)))
        out_shape = (jax.ShapeDtypeStruct(hp.shape, F32), jax.ShapeDtypeStruct(hs.shape, F32))
    else:
        out_specs = pl.BlockSpec((TOK_TILE, D_MODEL), lambda t, *_: (t, 0))
        out_shape = jax.ShapeDtypeStruct((nt * TOK_TILE, D_MODEL), F32)
    grid_spec = pltpu.PrefetchScalarGridSpec(
        num_scalar_prefetch=4, grid=(nt,), in_specs=in_specs, out_specs=out_specs,
        scratch_shapes=[pltpu.VMEM((2, CBUF_ROWS, D_MODEL), BF16), pltpu.VMEM((D_MODEL, D_MODEL), BF16),
                        pltpu.VMEM((PLE_DIM, D_MODEL), BF16), pltpu.SemaphoreType.DMA((2,))])
    return pl.pallas_call(
        functools.partial(_combine_kernel, final=final, npt=npt), grid_spec=grid_spec, out_shape=out_shape,
        compiler_params=pltpu.CompilerParams(dimension_semantics=("arbitrary",), vmem_limit_bytes=VMEM_LIMIT),
        name="moe_combine_final" if final else "moe_combine",
    )(*plan[:4], hp, hs, pp, ps, pos, y, P["g_ple"], P["w_ple_gate"], P["w_ple_proj"], g_final)


def _ffn(hp, hs, pp, ps, l, P, w_rt, b_rt, g_final, *, final):
    n = hp.shape[0] + hs.shape[0]
    nt = n // TOK_TILE
    bound = TOP_K * n + nt * N_EXPERTS * (CHUNK - 1) + N_EXPERTS * (EXP_TILE - 1)
    n_sorted = -(-bound // EXP_TILE) * EXP_TILE
    pos, cnt = _route_call(hp, hs, l, P, w_rt, b_rt)
    plan = _chunk_plan(cnt[:, 0, :N_EXPERTS].astype(jnp.int32), n_sorted // EXP_TILE)
    xs = _dispatch_call(plan, hp, hs, pos, l, P, n_sorted)
    y = _expert_call(plan, xs, l, P["wg"], P["wu"], P["wd"])
    return _combine_call(plan, hp, hs, pp, ps, pos, y, l, P, g_final, final=final)


def _router_weights(l, w_grp_router, b_grp_router, w_exp_router, b_exp_router):
    w_er = jnp.transpose(w_exp_router[l], (1, 0, 2)).reshape(D_MODEL, N_EXPERTS)
    w_rt = jnp.zeros((D_MODEL, LANES), F32).at[:, :N_EXPERTS].set(w_er)
    w_rt = w_rt.at[:, N_EXPERTS:N_EXPERTS + N_GROUPS].set(w_grp_router[l])
    b_rt = jnp.zeros((1, LANES), F32).at[0, :N_EXPERTS].set(b_exp_router[l].reshape(-1))
    b_rt = b_rt.at[0, N_EXPERTS:N_EXPERTS + N_GROUPS].set(b_grp_router[l])
    return w_rt, b_rt


def kernel(x_prompt, x_sample, state_gla, state_conv, p_prompt, p_sample, g_mix, w_in, w_forget_up, b_forget,
           g_gla_out, w_conv, b_conv, g_conv_ln, b_conv_ln, w_out, g_ffn, w_grp_router, b_grp_router,
           w_exp_router, b_exp_router, w_exp_gate, w_exp_up, w_exp_down, g_ple, w_ple_gate, w_ple_proj, g_final):
    depth = w_in.shape[0]
    nbp, seq_p, _ = x_prompt.shape
    nbs, seq_s, _ = x_sample.shape
    n_p = nbp * seq_p
    n_s = nbs * seq_s

    def rows(v):
        return v.reshape(depth, 1, -1)

    P = {
        "g_mix": rows(g_mix), "w_in": w_in, "w_cv": w_in[:, :, N_MAIN + GLA_LOWRANK:],
        "w_forget_up": w_forget_up, "b_forget": rows(b_forget), "g_gla_out": rows(g_gla_out),
        "w_conv": w_conv, "b_conv": rows(b_conv), "g_conv_ln": rows(g_conv_ln), "b_conv_ln": rows(b_conv_ln),
        "w_out": w_out, "g_ffn": rows(g_ffn), "g_ple": rows(g_ple),
        "w_ple_gate": w_ple_gate, "w_ple_proj": w_ple_proj,
        "wg": w_exp_gate.reshape(depth * N_EXPERTS, D_MODEL, EXPERT_FF),
        "wu": w_exp_up.reshape(depth * N_EXPERTS, D_MODEL, EXPERT_FF),
        "wd": w_exp_down.reshape(depth * N_EXPERTS, EXPERT_FF, D_MODEL),
    }
    g_fin = g_final.reshape(1, -1)
    xp = x_prompt.reshape(n_p, D_MODEL)
    xs = x_sample.reshape(n_s, D_MODEL)
    pp = p_prompt.reshape(depth, n_p, PLE_DIM)
    ps = p_sample.reshape(depth, n_s, PLE_DIM)
    s_in = state_gla.reshape(depth, nbs, QK_COLS, GLA_DV)

    h = None
    sg_p, sg_s, sc_p, sc_s = [], [], [], []
    for l in range(depth):
        if l == 0:
            hp, sgp, scp = _mixer_prompt(xp, 0, nbp, seq_p, l, P)
            hs, sgs, scs = _mixer_sample(xs, 0, nbs, seq_s, l, s_in, state_conv, P)
        else:
            hp, sgp, scp = _mixer_prompt(h, 0, nbp, seq_p, l, P)
            hs, sgs, scs = _mixer_sample(h, n_p, nbs, seq_s, l, s_in, state_conv, P)
        w_rt, b_rt = _router_weights(l, w_grp_router, b_grp_router, w_exp_router, b_exp_router)
        h = _ffn(hp, hs, pp, ps, l, P, w_rt, b_rt, g_fin, final=(l == depth - 1))
        sg_p.append(sgp.reshape(nbp, GLA_HEADS, GLA_DK, GLA_DV))
        sg_s.append(sgs.reshape(nbs, GLA_HEADS, GLA_DK, GLA_DV))
        sc_p.append(scp)
        sc_s.append(scs)

    y_prompt = h[0].reshape(nbp, seq_p, D_MODEL)
    y_sample = h[1].reshape(nbs, seq_s, D_MODEL)
    return (y_prompt, y_sample, jnp.stack(sg_p), jnp.stack(sg_s), jnp.stack(sc_p), jnp.stack(sc_s))
```

```python
import functools

import jax
import jax.numpy as jnp
from jax import lax
from jax.experimental import pallas as pl
from jax.experimental.pallas import tpu as pltpu

D_MODEL = 1024
GLA_HEADS = 4
GLA_DK = 64
GLA_DV = 128
QK_COLS = GLA_HEADS * GLA_DK
V_COLS = GLA_HEADS * GLA_DV
CONV_WIDTH = 512
CONV_K = 31
GLA_LOWRANK = 16
GLA_TAU = 16.0
GLA_CHUNK = 64
PLE_DIM = 256
N_GROUPS = 4
EXPERTS_PER_GROUP = 8
N_EXPERTS = N_GROUPS * EXPERTS_PER_GROUP
EXPERT_FF = 256
TOP_K = 2
EPS = 1e-6
N_MAIN = 2 * QK_COLS + 2 * V_COLS

LANES = 128
SUBLANES = 8
CONV_PAD = 32
CONV_OFF = CONV_PAD - (CONV_K - 1)
VMEM_LIMIT = 56 * 1024 * 1024
TOK_TILE = 512
CHUNK = 16
EXP_TILE = 512
CBUF_ROWS = -(-(TOP_K * TOK_TILE + N_EXPERTS * (CHUNK - 1)) // LANES) * LANES
CBUF_CHUNKS = CBUF_ROWS // CHUNK
XS_COLS = D_MODEL + LANES

F32 = jnp.float32
BF16 = jnp.bfloat16
HI = lax.Precision.HIGHEST


def _sigmoid(x):
    return 1.0 / (1.0 + jnp.exp(-x))


def _silu(x):
    return x * _sigmoid(x)


def _log_sigmoid(x):
    return jnp.minimum(x, 0.0) - jnp.log(1.0 + jnp.exp(-jnp.abs(x)))


def _rmsnorm(x, g):
    return x * lax.rsqrt(jnp.mean(x * x, axis=-1, keepdims=True) + EPS) * g


def _dot(a, b):
    return jnp.dot(a.astype(BF16), b.astype(BF16), preferred_element_type=F32)


def _dot_t(a, b):
    return lax.dot_general(a.astype(BF16), b.astype(BF16), (((1,), (1,)), ((), ())),
                           preferred_element_type=F32)


def _dot_hi(a, b):
    return jnp.dot(a, b, preferred_element_type=F32, precision=HI)


def _iota_f32(shape, dim):
    return lax.broadcasted_iota(jnp.int32, shape, dim).astype(F32)


def _const_spec(shape):
    nd = len(shape)
    return pl.BlockSpec(shape, lambda *_: (0,) * nd)


def _layer_spec(arr, l, cols=None):
    nd = arr.ndim - 1
    shape = arr.shape[1:] if cols is None else arr.shape[1:-1] + (cols,)
    return pl.BlockSpec((None,) + shape, lambda *_: (l,) + (0,) * nd, pipeline_mode=pl.Buffered(1))


def _cast_mixer_weights(w_in_ref, w_cv_ref, w_fu_ref, w_out_ref, wmain_s, wlr_s, wcv_s, wfu_s, wout_s):
    wmain_s[...] = w_in_ref[:, 0:N_MAIN].astype(BF16)
    lane = lax.broadcasted_iota(jnp.int32, (D_MODEL, LANES), 1)
    wlr_s[...] = jnp.where(lane < GLA_LOWRANK, w_in_ref[:, N_MAIN:N_MAIN + LANES], 0.0).astype(BF16)
    wcv_s[...] = w_cv_ref[...].astype(BF16)
    wfu_s[...] = jnp.zeros_like(wfu_s)
    wfu_s[0:GLA_LOWRANK, :] = w_fu_ref[...].astype(BF16)
    wout_s[...] = w_out_ref[...].astype(BF16)


def _project(x, g_mix, wmain_s, wlr_s, wfu_s, b_f, wcv_s):
    xn = _rmsnorm(x, g_mix).astype(BF16)
    z = jnp.dot(xn, wmain_s[...], preferred_element_type=F32)
    q = z[:, :QK_COLS] * (GLA_DK ** -0.5)
    k = z[:, QK_COLS:2 * QK_COLS]
    v = z[:, 2 * QK_COLS:2 * QK_COLS + V_COLS]
    og = z[:, 2 * QK_COLS + V_COLS:]
    lr = jnp.dot(xn, wlr_s[...], preferred_element_type=F32)
    zf = _dot(lr, wfu_s[...]) + b_f
    la = _log_sigmoid(zf) * (1.0 / GLA_TAU)
    cv = jnp.dot(xn, wcv_s[...], preferred_element_type=F32)
    u = cv[:, :CONV_WIDTH] * _sigmoid(cv[:, CONV_WIDTH:])
    return q, k, v, og, la, u


def _stack_heads(qd):
    lane = lax.broadcasted_iota(jnp.int32, qd.shape, 1)
    return jnp.concatenate(
        [jnp.where((lane >= h * GLA_DK) & (lane < (h + 1) * GLA_DK), qd, 0.0) for h in range(GLA_HEADS)],
        axis=0)


def _gated_head_norm(o, og, g_gla):
    outs = []
    for h in range(GLA_HEADS):
        sl = slice(h * GLA_DV, (h + 1) * GLA_DV)
        outs.append(_rmsnorm(o[:, sl], g_gla) * _silu(og[:, sl]))
    return jnp.concatenate(outs, axis=1)


def _causal_conv(win, w_conv, b_conv, n):
    acc = jnp.broadcast_to(b_conv, (n, CONV_WIDTH))
    for s in range(SUBLANES):
        taps = [j for j in range(CONV_K) if (CONV_OFF + j) % SUBLANES == s]
        if not taps:
            continue
        rows = n if s == 0 else n + SUBLANES
        part = None
        for j in taps:
            a = (CONV_OFF + j) - s
            term = w_conv[j:j + 1, :] * win[a:a + rows, :]
            part = term if part is None else part + term
        acc = acc + part[s:s + n, :]
    return acc


def _conv_ln_act(acc, g_ln, b_ln):
    mu = jnp.mean(acc, axis=-1, keepdims=True)
    xc = acc - mu
    y = xc * lax.rsqrt(jnp.mean(xc * xc, axis=-1, keepdims=True) + EPS) * g_ln + b_ln
    return _silu(y)


def _head_diag(upd):
    return jnp.concatenate([upd[h * GLA_DK:(h + 1) * GLA_DK, h * GLA_DV:(h + 1) * GLA_DV]
                            for h in range(GLA_HEADS)], axis=0)


def _mixer_weight_args(l, P):
    names = ("g_mix", "w_in", "w_cv", "w_forget_up", "b_forget", "g_gla_out", "w_conv", "b_conv", "g_conv_ln",
             "b_conv_ln", "w_out")
    arrs = [P[n] for n in names]
    return arrs, [_layer_spec(a, l, cols=N_MAIN + LANES if n == "w_in" else None) for n, a in zip(names, arrs)]


_MIXER_WEIGHT_SCRATCH = [pltpu.VMEM((D_MODEL, N_MAIN), BF16), pltpu.VMEM((D_MODEL, LANES), BF16),
                         pltpu.VMEM((D_MODEL, 2 * CONV_WIDTH), BF16), pltpu.VMEM((LANES, QK_COLS), BF16),
                         pltpu.VMEM((D_MODEL, D_MODEL), BF16)]


def _mixer_prompt_kernel(x_ref, g_mix_ref, w_in_ref, w_cv_ref, w_fu_ref, b_f_ref, g_gla_ref, w_conv_ref,
                         b_conv_ref, g_ln_ref, b_ln_ref, w_out_ref,
                         h_ref, sg_ref, sc_ref,
                         wmain_s, wlr_s, wcv_s, wfu_s, wout_s,
                         s_ref, ubuf_ref, qs_ref, kd_ref, klt_ref, dec_ref, v_ref, og_ref, mix_ref, *, tt):
    t = pl.program_id(1)
    nt = pl.num_programs(1)
    C = GLA_CHUNK
    n_chunks = tt // C

    @pl.when((pl.program_id(0) == 0) & (t == 0))
    def _():
        _cast_mixer_weights(w_in_ref, w_cv_ref, w_fu_ref, w_out_ref, wmain_s, wlr_s, wcv_s, wfu_s, wout_s)

    @pl.when(t == 0)
    def _():
        s_ref[...] = jnp.zeros_like(s_ref)
        ubuf_ref[0:CONV_PAD, :] = jnp.zeros((CONV_PAD, CONV_WIDTH), F32)

    x = x_ref[...]
    q, k, v, og, la, u = _project(x, g_mix_ref[...], wmain_s, wlr_s, wfu_s, b_f_ref[...], wcv_s)
    ubuf_ref[CONV_PAD:CONV_PAD + tt, :] = u
    v_ref[...] = v.astype(BF16)
    og_ref[...] = _silu(og)

    w_conv = w_conv_ref[...]
    b_conv = b_conv_ref[...]
    g_ln = g_ln_ref[...]
    b_ln = b_ln_ref[...]
    for c in range(n_chunks):
        win = ubuf_ref[c * C:c * C + C + CONV_PAD, :]
        mix_ref[c * C:(c + 1) * C, V_COLS:] = _conv_ln_act(_causal_conv(win, w_conv, b_conv, C), g_ln, b_ln).astype(BF16)

    row = lax.broadcasted_iota(jnp.int32, (C, C), 0)
    col = lax.broadcasted_iota(jnp.int32, (C, C), 1)
    tri = (col <= row).astype(F32)
    for c in range(n_chunks):
        rows = slice(c * C, (c + 1) * C)
        b = _dot_hi(tri, la[rows, :])
        b_last = b[C - 1:C, :]
        qs_ref[c] = _stack_heads(q[rows, :] * jnp.exp(b)).astype(BF16)
        kd_ref[rows, :] = (k[rows, :] * jnp.exp(-b)).astype(BF16)
        kl = k[rows, :] * jnp.exp(b_last - b)
        klt = jnp.concatenate([kl, jnp.broadcast_to(jnp.exp(b_last), (C, QK_COLS))], axis=0).T
        klt_ref[c] = klt.astype(BF16)
        dec_ref[c] = jnp.broadcast_to(klt[:, C:C + 1], (QK_COLS, GLA_DV))

    r4 = lax.broadcasted_iota(jnp.int32, (GLA_HEADS * C, C), 0)
    c4 = lax.broadcasted_iota(jnp.int32, (GLA_HEADS * C, C), 1)
    causal4 = c4 <= (r4 % C)
    g_gla = g_gla_ref[...]
    s = s_ref[...]
    for c in range(n_chunks):
        rows = slice(c * C, (c + 1) * C)
        qs = qs_ref[c]
        vc = v_ref[rows, :]
        scores = jnp.where(causal4, _dot_t(qs, kd_ref[rows, :]), 0.0).astype(BF16)
        o_inter = jnp.dot(qs, s.astype(BF16), preferred_element_type=F32)
        upd = jnp.dot(klt_ref[c][:, :C], vc, preferred_element_type=F32)
        s = dec_ref[c] * s + _head_diag(upd)
        o_parts = []
        for h in range(GLA_HEADS):
            vh = vc[:, h * GLA_DV:(h + 1) * GLA_DV]
            o_parts.append(jnp.dot(scores[h * C:(h + 1) * C, :], vh, preferred_element_type=F32)
                           + o_inter[h * C:(h + 1) * C, :])
        o = jnp.concatenate(o_parts, axis=1)
        gated = []
        for h in range(GLA_HEADS):
            sl = slice(h * GLA_DV, (h + 1) * GLA_DV)
            gated.append(_rmsnorm(o[:, sl], g_gla) * og_ref[rows, sl])
        mix_ref[rows, 0:V_COLS] = jnp.concatenate(gated, axis=1).astype(BF16)
    s_ref[...] = s

    h_ref[...] = x + jnp.dot(mix_ref[...], wout_s[...], preferred_element_type=F32)
    tail = ubuf_ref[tt:tt + CONV_PAD, :]
    ubuf_ref[0:CONV_PAD, :] = tail

    @pl.when(t == nt - 1)
    def _():
        sg_ref[0] = s
        sc_ref[0] = tail[CONV_OFF:, :]


def _mixer_prompt(x, row_off, nb, seq, l, P, *, tt=512):
    nt = seq // tt
    n_chunks = tt // GLA_CHUNK
    blk_off = row_off // tt
    weights, w_specs = _mixer_weight_args(l, P)
    in_specs = [pl.BlockSpec((tt, D_MODEL), lambda b, t: (blk_off + b * nt + t, 0))] + w_specs
    out_shape = (jax.ShapeDtypeStruct((nb * seq, D_MODEL), F32),
                 jax.ShapeDtypeStruct((nb, QK_COLS, GLA_DV), F32),
                 jax.ShapeDtypeStruct((nb, CONV_K - 1, CONV_WIDTH), F32))
    out_specs = (pl.BlockSpec((tt, D_MODEL), lambda b, t: (b * nt + t, 0)),
                 pl.BlockSpec((1, QK_COLS, GLA_DV), lambda b, t: (b, 0, 0)),
                 pl.BlockSpec((1, CONV_K - 1, CONV_WIDTH), lambda b, t: (b, 0, 0)))
    scratch = _MIXER_WEIGHT_SCRATCH + [
        pltpu.VMEM((QK_COLS, GLA_DV), F32),
        pltpu.VMEM((CONV_PAD + tt, CONV_WIDTH), F32),
        pltpu.VMEM((n_chunks, GLA_HEADS * GLA_CHUNK, QK_COLS), BF16), pltpu.VMEM((tt, QK_COLS), BF16),
        pltpu.VMEM((n_chunks, QK_COLS, 2 * GLA_CHUNK), BF16), pltpu.VMEM((n_chunks, QK_COLS, GLA_DV), F32),
        pltpu.VMEM((tt, V_COLS), BF16), pltpu.VMEM((tt, V_COLS), F32),
        pltpu.VMEM((tt, D_MODEL), BF16)]
    return pl.pallas_call(
        functools.partial(_mixer_prompt_kernel, tt=tt),
        grid=(nb, nt), in_specs=in_specs, out_specs=out_specs, out_shape=out_shape,
        scratch_shapes=scratch,
        compiler_params=pltpu.CompilerParams(dimension_semantics=("arbitrary", "arbitrary"),
                                             vmem_limit_bytes=VMEM_LIMIT),
        name="mixer_prompt",
    )(x, *weights)


def _mixer_sample_kernel(x_ref, s_in_ref, c_in_ref, g_mix_ref, w_in_ref, w_cv_ref, w_fu_ref, b_f_ref,
                         g_gla_ref, w_conv_ref, b_conv_ref, g_ln_ref, b_ln_ref, w_out_ref,
                         h_ref, sg_ref, sc_ref,
                         wmain_s, wlr_s, wcv_s, wfu_s, wout_s,
                         qs_ref, klt_ref, dect_ref, v_ref, u_ref, oi_ref, cbuf_ref, cacc_ref, *, sb, seq):
    R = sb * seq

    @pl.when(pl.program_id(0) == 0)
    def _():
        _cast_mixer_weights(w_in_ref, w_cv_ref, w_fu_ref, w_out_ref, wmain_s, wlr_s, wcv_s, wfu_s, wout_s)

    x = x_ref[...]
    q, k, v, og, la, u = _project(x, g_mix_ref[...], wmain_s, wlr_s, wfu_s, b_f_ref[...], wcv_s)
    row = lax.broadcasted_iota(jnp.int32, (R, R), 0)
    col = lax.broadcasted_iota(jnp.int32, (R, R), 1)
    same = (row // seq) == (col // seq)
    b = _dot_hi((same & (col <= row)).astype(F32), la)
    b_tot = _dot_hi(same.astype(F32), la)
    qd = q * jnp.exp(b)
    kd = k * jnp.exp(-b)
    kl = k * jnp.exp(b_tot - b)
    qs = _stack_heads(qd)
    r4 = lax.broadcasted_iota(jnp.int32, (GLA_HEADS * R, R), 0) % R
    c4 = lax.broadcasted_iota(jnp.int32, (GLA_HEADS * R, R), 1)
    mask4 = ((r4 // seq) == (c4 // seq)) & (c4 <= r4)
    scores = jnp.where(mask4, _dot_t(qs, kd), 0.0)

    qs_ref[...] = qs
    klt_ref[...] = kl.T
    dect_ref[...] = jnp.exp(b_tot).T
    v_ref[...] = v
    u_ref[...] = u

    lane_r = lax.broadcasted_iota(jnp.int32, (QK_COLS, R), 1)
    w_conv = w_conv_ref[...]
    b_conv = b_conv_ref[...]

    def one_seq(i, carry):
        lo = i * seq
        r0 = pl.multiple_of(lo, seq)
        s_old = s_in_ref[i]
        qsel = jnp.concatenate([qs_ref[pl.ds(pl.multiple_of(h * R + lo, seq), seq), :]
                                for h in range(GLA_HEADS)], axis=0)
        oi = _dot(qsel, s_old)
        for h in range(GLA_HEADS):
            oi_ref[h, pl.ds(r0, seq), :] = oi[h * seq:(h + 1) * seq, :]
        lmask = (lane_r >= lo) & (lane_r < lo + seq)
        upd = _dot(jnp.where(lmask, klt_ref[...], 0.0), v_ref[...])
        dec = jnp.sum(jnp.where(lmask, dect_ref[...], 0.0), axis=1, keepdims=True) * (1.0 / seq)
        sg_ref[i] = dec * s_old + _head_diag(upd)
        cbuf_ref[CONV_OFF:CONV_PAD, :] = c_in_ref[i]
        cbuf_ref[CONV_PAD:CONV_PAD + seq, :] = u_ref[pl.ds(r0, seq), :]
        cacc_ref[pl.ds(r0, seq), :] = _causal_conv(cbuf_ref[...], w_conv, b_conv, seq)
        sc_ref[i] = cbuf_ref[CONV_OFF + seq:CONV_PAD + seq, :]
        return carry

    lax.fori_loop(0, sb, one_seq, 0)

    o_parts = []
    for h in range(GLA_HEADS):
        vh = v[:, h * GLA_DV:(h + 1) * GLA_DV]
        o_parts.append(_dot(scores[h * R:(h + 1) * R, :], vh) + oi_ref[h])
    o = jnp.concatenate(o_parts, axis=1)
    mix = jnp.concatenate([_gated_head_norm(o, og, g_gla_ref[...]),
                           _conv_ln_act(cacc_ref[...], g_ln_ref[...], b_ln_ref[...])], axis=1)
    h_ref[...] = x + jnp.dot(mix.astype(BF16), wout_s[...], preferred_element_type=F32)


def _mixer_sample(x, row_off, nb, seq, l, s_in, c_in, P, *, sb=16):
    R = sb * seq
    blk_off = row_off // R
    weights, w_specs = _mixer_weight_args(l, P)
    in_specs = [pl.BlockSpec((R, D_MODEL), lambda i: (blk_off + i, 0)),
                pl.BlockSpec((None, sb, QK_COLS, GLA_DV), lambda i: (l, i, 0, 0)),
                pl.BlockSpec((None, sb, CONV_K - 1, CONV_WIDTH), lambda i: (l, i, 0, 0))] + w_specs
    out_shape = (jax.ShapeDtypeStruct((nb * seq, D_MODEL), F32),
                 jax.ShapeDtypeStruct((nb, QK_COLS, GLA_DV), F32),
                 jax.ShapeDtypeStruct((nb, CONV_K - 1, CONV_WIDTH), F32))
    out_specs = (pl.BlockSpec((R, D_MODEL), lambda i: (i, 0)),
                 pl.BlockSpec((sb, QK_COLS, GLA_DV), lambda i: (i, 0, 0)),
                 pl.BlockSpec((sb, CONV_K - 1, CONV_WIDTH), lambda i: (i, 0, 0)))
    scratch = _MIXER_WEIGHT_SCRATCH + [
        pltpu.VMEM((GLA_HEADS * R, QK_COLS), F32),
        pltpu.VMEM((QK_COLS, R), F32), pltpu.VMEM((QK_COLS, R), F32),
        pltpu.VMEM((R, V_COLS), F32), pltpu.VMEM((R, CONV_WIDTH), F32),
        pltpu.VMEM((GLA_HEADS, R, GLA_DV), F32),
        pltpu.VMEM((CONV_PAD + seq, CONV_WIDTH), F32),
        pltpu.VMEM((R, CONV_WIDTH), F32)]
    return pl.pallas_call(
        functools.partial(_mixer_sample_kernel, sb=sb, seq=seq),
        grid=(nb // sb,), in_specs=in_specs, out_specs=out_specs, out_shape=out_shape,
        scratch_shapes=scratch,
        compiler_params=pltpu.CompilerParams(dimension_semantics=("arbitrary",),
                                             vmem_limit_bytes=VMEM_LIMIT),
        name="mixer_sample",
    )(x, s_in, c_in, *weights)


def _pair_specs(n_first_tiles, width):
    return [pl.BlockSpec((TOK_TILE, width), lambda t, *_: (jnp.minimum(t, n_first_tiles - 1), 0)),
            pl.BlockSpec((TOK_TILE, width), lambda t, *_: (jnp.maximum(t - n_first_tiles, 0), 0))]


def _pick(t, n_first_tiles, a_ref, b_ref):
    return jnp.where(t < n_first_tiles, a_ref[...], b_ref[...])


def _route(logits):
    lane = lax.broadcasted_iota(jnp.int32, logits.shape, 1)
    lane_f = lane.astype(F32)
    neg = jnp.float32(-jnp.inf)
    big = jnp.float32(1e9)
    is_grp = (lane >= N_EXPERTS) & (lane < N_EXPERTS + N_GROUPS)
    gl = jnp.where(is_grp, logits, neg)
    gmax = jnp.max(gl, axis=-1, keepdims=True)
    gidx = jnp.min(jnp.where(is_grp & (gl == gmax), lane_f - N_EXPERTS, big), axis=-1, keepdims=True)
    gsum = jnp.sum(jnp.where(is_grp, jnp.exp(gl - gmax), 0.0), axis=-1, keepdims=True)
    g_w = 1.0 / gsum
    grp_of_lane = jnp.floor(lane_f * (1.0 / EXPERTS_PER_GROUP))
    in_grp = (lane < N_EXPERTS) & (grp_of_lane == gidx)
    ml = jnp.where(in_grp, logits, neg)
    v1 = jnp.max(ml, axis=-1, keepdims=True)
    i1 = jnp.min(jnp.where(in_grp & (ml == v1), lane_f, big), axis=-1, keepdims=True)
    ml2 = jnp.where(lane_f == i1, neg, ml)
    v2 = jnp.max(ml2, axis=-1, keepdims=True)
    i2 = jnp.min(jnp.where(in_grp & (ml2 == v2), lane_f, big), axis=-1, keepdims=True)
    e2 = jnp.exp(v2 - v1)
    w1 = g_w / (1.0 + e2)
    w2 = g_w * e2 / (1.0 + e2)
    return i1, i2, w1, w2


def _route_kernel(hp_ref, hs_ref, g_ffn_ref, w_rt_ref, b_rt_ref, pos_ref, cnt_ref, *, npt):
    T = TOK_TILE
    xn = _rmsnorm(_pick(pl.program_id(0), npt, hp_ref, hs_ref), g_ffn_ref[...])
    x_hi = xn.astype(BF16)
    x_lo = (xn - x_hi.astype(F32)).astype(BF16)
    w_split = w_rt_ref[...]
    both = jnp.dot(x_hi, w_split, preferred_element_type=F32)
    logits = (both[:, :LANES] + both[:, LANES:] + jnp.dot(x_lo, w_split[:, :LANES], preferred_element_type=F32)
              + b_rt_ref[...])
    i1, i2, w1, w2 = _route(logits)
    lane = _iota_f32((T, LANES), 1)
    a0 = (lane == i1).astype(F32)
    a1 = (lane == i2).astype(F32)
    a = a0 + a1
    cnt = jnp.sum(a, axis=0, keepdims=True)
    earlier = (_iota_f32((T, T), 1) < _iota_f32((T, T), 0)).astype(BF16)
    rank = jnp.dot(earlier, a.astype(BF16), preferred_element_type=F32)
    cnt_pad = jnp.ceil(cnt * (1.0 / CHUNK)) * CHUNK
    below = (_iota_f32((LANES, LANES), 0) < _iota_f32((LANES, LANES), 1)).astype(F32)
    first = _dot_hi(jnp.broadcast_to(cnt_pad, (SUBLANES, LANES)), below)[0:1, :]
    base = first + rank
    pos0 = jnp.sum(a0 * base, axis=1, keepdims=True)
    pos1 = jnp.sum(a1 * base, axis=1, keepdims=True)
    pos_ref[...] = jnp.where(lane == 0.0, pos0, jnp.where(lane == 1.0, pos1, jnp.where(
        lane == 2.0, w1, jnp.where(lane == 3.0, w2, jnp.where(lane == 4.0, i1, 0.0)))))
    cnt_ref[0] = jnp.broadcast_to(cnt, (SUBLANES, LANES))


def _route_call(hp, hs, l, P, w_rt, b_rt):
    npt = hp.shape[0] // TOK_TILE
    nt = npt + hs.shape[0] // TOK_TILE
    return pl.pallas_call(
        functools.partial(_route_kernel, npt=npt), grid=(nt,),
        in_specs=_pair_specs(npt, D_MODEL) + [_layer_spec(P["g_ffn"], l), _const_spec(w_rt.shape),
                                              _const_spec(b_rt.shape)],
        out_specs=(pl.BlockSpec((TOK_TILE, LANES), lambda t: (t, 0)),
                   pl.BlockSpec((1, SUBLANES, LANES), lambda t: (t, 0, 0))),
        out_shape=(jax.ShapeDtypeStruct((nt * TOK_TILE, LANES), F32),
                   jax.ShapeDtypeStruct((nt, SUBLANES, LANES), F32)),
        compiler_params=pltpu.CompilerParams(dimension_semantics=("arbitrary",), vmem_limit_bytes=VMEM_LIMIT),
        name="moe_route",
    )(hp, hs, P["g_ffn"], w_rt, b_rt)


def _chunk_plan(cnt, n_row_tiles):
    n16 = (cnt + (CHUNK - 1)) // CHUNK
    lend16 = jnp.cumsum(n16, axis=1)
    tile_pref16 = jnp.cumsum(n16, axis=0) - n16
    tot16 = jnp.sum(n16, axis=0)
    per_tile = EXP_TILE // CHUNK
    seg16 = ((tot16 + per_tile - 1) // per_tile) * per_tile
    seg_end16 = jnp.cumsum(seg16)
    dst16 = (seg_end16 - seg16)[None, :] + tile_pref16
    j = jnp.arange(CBUF_CHUNKS, dtype=jnp.int32)
    owner = jnp.minimum(jnp.sum(lend16[:, None, :] <= j[None, :, None], axis=2), N_EXPERTS - 1)
    shift = dst16 - (lend16 - n16)
    dst_flat = j[None, :] + jnp.take_along_axis(shift, owner, axis=1)
    n_tot = lend16[:, -1]
    gap16 = seg_end16 - seg16 + tot16
    gapn16 = seg16 - tot16
    tile_start16 = jnp.arange(n_row_tiles, dtype=jnp.int32) * per_tile
    n_valid = seg_end16[-1] // per_tile
    misc = jnp.stack([n_valid, jnp.sum(gapn16)])
    exp_of_tile = jnp.minimum(jnp.sum(seg_end16[None, :] <= tile_start16[:, None], axis=1), N_EXPERTS - 1)
    i32 = lambda a: a.astype(jnp.int32).reshape(-1)
    return i32(dst_flat), i32(n_tot), i32(gap16), i32(gapn16), i32(misc), i32(exp_of_tile)


def _chunk_copy(src, dst, src_chunk, dst_chunk, sem):
    return pltpu.make_async_copy(src.at[pl.ds(pl.multiple_of(src_chunk * CHUNK, CHUNK), CHUNK), :],
                                 dst.at[pl.ds(pl.multiple_of(dst_chunk * CHUNK, CHUNK), CHUNK), :], sem)


def _tile_copy(src, dst, dst_tile, sem):
    return pltpu.make_async_copy(src, dst.at[pl.ds(pl.multiple_of(dst_tile * EXP_TILE, EXP_TILE), EXP_TILE), :], sem)


def _wait_chunks(src, dst, sem, n):
    def body(_, carry):
        _chunk_copy(src, dst, 0, 0, sem).wait()
        return carry
    lax.fori_loop(0, n, body, 0)


def _dispatch_kernel(dst_ref, ntot_ref, gap16_ref, gapn16_ref, misc_ref,
                     hp_ref, hs_ref, pos_ref, g_ffn_ref, xs_hbm, cbuf, zbuf, sem, *, n_row_tiles, npt):
    t = pl.program_id(0)
    nt = pl.num_programs(0)
    slot = t % 2
    T = TOK_TILE
    n_tail = n_row_tiles - misc_ref[0]

    @pl.when(t == 0)
    def _():
        zbuf[...] = jnp.zeros_like(zbuf)
        for e in range(N_EXPERTS):
            g0 = gap16_ref[e]

            def fill(c, carry, g0=g0):
                _chunk_copy(zbuf, xs_hbm, 0, g0 + c, sem.at[2]).start()
                return carry
            lax.fori_loop(0, gapn16_ref[e], fill, 0)

        def fill_tile(i, carry):
            _tile_copy(zbuf, xs_hbm, misc_ref[0] + i, sem.at[2]).start()
            return carry
        lax.fori_loop(0, n_tail, fill_tile, 0)

    xn = _rmsnorm(_pick(t, npt, hp_ref, hs_ref), g_ffn_ref[...]).astype(BF16)
    pos = pos_ref[...]
    pos_t = pos.T
    rows = _iota_f32((CBUF_ROWS, T), 0)
    onehot = jnp.where((rows == pos_t[0:1, :]) | (rows == pos_t[1:2, :]), 1.0, 0.0).astype(BF16)
    lane = lax.broadcasted_iota(jnp.int32, (T, LANES), 1)
    extra = jnp.zeros((T, LANES), F32)
    for s in range(TOP_K):
        c = pos[:, 2 + s:3 + s]
        hi = c.astype(BF16).astype(F32)
        mid = (c - hi).astype(BF16).astype(F32)
        lo = c - hi - mid
        for j, piece in enumerate((hi, mid, lo)):
            extra = jnp.where(lane == 3 * s + j, piece, extra)
    extra = jnp.where(lane == 3 * TOP_K, pos[:, 4:5], extra)
    cbuf[slot] = jnp.dot(onehot, jnp.concatenate([xn, extra.astype(BF16)], axis=1),
                         preferred_element_type=F32).astype(BF16)

    src = cbuf.at[slot]

    def issue(j, carry):
        _chunk_copy(src, xs_hbm, j, dst_ref[t * CBUF_CHUNKS + j], sem.at[slot]).start()
        return carry
    lax.fori_loop(0, ntot_ref[t], issue, 0)

    @pl.when(t > 0)
    def _():
        _wait_chunks(cbuf.at[1 - slot], xs_hbm, sem.at[1 - slot], ntot_ref[t - 1])

    @pl.when(t == nt - 1)
    def _():
        _wait_chunks(src, xs_hbm, sem.at[slot], ntot_ref[t])
        _wait_chunks(zbuf, xs_hbm, sem.at[2], misc_ref[1])

        def wait_tile(_, carry):
            _tile_copy(zbuf, xs_hbm, 0, sem.at[2]).wait()
            return carry
        lax.fori_loop(0, n_tail, wait_tile, 0)


def _dispatch_call(plan, hp, hs, pos, l, P, n_sorted):
    npt = hp.shape[0] // TOK_TILE
    nt = npt + hs.shape[0] // TOK_TILE
    g_ffn = P["g_ffn"]
    grid_spec = pltpu.PrefetchScalarGridSpec(
        num_scalar_prefetch=5, grid=(nt,),
        in_specs=_pair_specs(npt, D_MODEL) + [
            pl.BlockSpec((TOK_TILE, LANES), lambda t, *_: (t, 0)),
            pl.BlockSpec((None,) + g_ffn.shape[1:], lambda t, *_: (l, 0, 0))],
        out_specs=pl.BlockSpec(memory_space=pl.ANY),
        scratch_shapes=[pltpu.VMEM((2, CBUF_ROWS, XS_COLS), BF16), pltpu.VMEM((EXP_TILE, XS_COLS), BF16),
                        pltpu.SemaphoreType.DMA((3,))])
    return pl.pallas_call(
        functools.partial(_dispatch_kernel, n_row_tiles=n_sorted // EXP_TILE, npt=npt), grid_spec=grid_spec,
        out_shape=jax.ShapeDtypeStruct((n_sorted, XS_COLS), BF16),
        compiler_params=pltpu.CompilerParams(dimension_semantics=("arbitrary",), vmem_limit_bytes=VMEM_LIMIT),
        name="moe_dispatch",
    )(*plan[:5], hp, hs, pos, g_ffn)


def _expert_kernel(eot_ref, misc_ref, xs_ref, wg_ref, wu_ref, wd_ref, y_ref, wg_s, wu_s, wd_s):
    i = pl.program_id(0)
    valid = i < misc_ref[0]

    @pl.when(jnp.logical_not(valid))
    def _():
        y_ref[...] = jnp.zeros_like(y_ref)

    @pl.when(valid & ((i == 0) | (eot_ref[i] != eot_ref[jnp.maximum(i - 1, 0)])))
    def _():
        wg_s[...] = wg_ref[...].astype(BF16)
        wu_s[...] = wu_ref[...].astype(BF16)
        wd_s[...] = wd_ref[...].astype(BF16)

    @pl.when(valid)
    def _():
        xs = xs_ref[...]
        x = xs[:, :D_MODEL]
        ex = xs[:, D_MODEL:].astype(F32)
        lane = lax.broadcasted_iota(jnp.int32, ex.shape, 1)
        id0 = jnp.sum(jnp.where(lane == 3 * TOP_K, ex, 0.0), axis=-1, keepdims=True)
        first = id0 == eot_ref[i].astype(F32)
        mine = (first & (lane < 3)) | (jnp.logical_not(first) & (lane >= 3) & (lane < 3 * TOP_K))
        c = jnp.sum(jnp.where(mine, ex, 0.0), axis=-1, keepdims=True)
        hg = _silu(jnp.dot(x, wg_s[...], preferred_element_type=F32)) * jnp.dot(x, wu_s[...],
                                                                               preferred_element_type=F32)
        y_ref[...] = jnp.dot((hg * c).astype(BF16), wd_s[...], preferred_element_type=F32).astype(BF16)


def _expert_call(plan, xs, l, wg, wu, wd):
    misc, exp_of_tile = plan[4:]
    n_row_tiles = xs.shape[0] // EXP_TILE

    def last_valid(i, nv):
        return jnp.maximum(jnp.minimum(i, nv[0] - 1), 0)

    def row_map(i, eot, nv):
        return (last_valid(i, nv), 0)

    def w_map(i, eot, nv):
        return (l * N_EXPERTS + eot[last_valid(i, nv)], 0, 0)

    grid_spec = pltpu.PrefetchScalarGridSpec(
        num_scalar_prefetch=2, grid=(n_row_tiles,),
        in_specs=[pl.BlockSpec((EXP_TILE, XS_COLS), row_map),
                  pl.BlockSpec((None, D_MODEL, EXPERT_FF), w_map),
                  pl.BlockSpec((None, D_MODEL, EXPERT_FF), w_map),
                  pl.BlockSpec((None, EXPERT_FF, D_MODEL), w_map)],
        out_specs=pl.BlockSpec((EXP_TILE, D_MODEL), lambda i, eot, nv: (i, 0)),
        scratch_shapes=[pltpu.VMEM((D_MODEL, EXPERT_FF), BF16), pltpu.VMEM((D_MODEL, EXPERT_FF), BF16),
                        pltpu.VMEM((EXPERT_FF, D_MODEL), BF16)])
    return pl.pallas_call(
        _expert_kernel, grid_spec=grid_spec,
        out_shape=jax.ShapeDtypeStruct((xs.shape[0], D_MODEL), BF16),
        compiler_params=pltpu.CompilerParams(dimension_semantics=("arbitrary",), vmem_limit_bytes=VMEM_LIMIT),
        name="moe_experts",
    )(exp_of_tile, misc, xs, wg, wu, wd)


def _combine_kernel(dst_ref, ntot_ref, hp_ref, hs_ref, pp_ref, ps_ref, pos_ref, y_hbm,
                    g_ple_ref, w_pg_ref, w_pp_ref, g_fin_ref, *rest, final, npt):
    if final:
        op_ref, os_ref, ybuf, wpg_s, wpp_s, sem = rest
    else:
        o_ref, ybuf, wpg_s, wpp_s, sem = rest
    t = pl.program_id(0)
    nt = pl.num_programs(0)
    slot = t % 2

    def fetch(tile, sl):
        dst = ybuf.at[sl]

        def issue(j, carry):
            _chunk_copy(y_hbm, dst, dst_ref[tile * CBUF_CHUNKS + j], j, sem.at[sl]).start()
            return carry
        lax.fori_loop(0, ntot_ref[tile], issue, 0)

    @pl.when(t == 0)
    def _():
        ybuf[...] = jnp.zeros_like(ybuf)
        fetch(t, slot)
        wpg_s[...] = w_pg_ref[...].astype(BF16)
        wpp_s[...] = w_pp_ref[...].astype(BF16)

    @pl.when(t + 1 < nt)
    def _():
        fetch(t + 1, 1 - slot)

    _wait_chunks(y_hbm, ybuf.at[slot], sem.at[slot], ntot_ref[t])

    pos = pos_ref[...]
    cols = _iota_f32((TOK_TILE, CBUF_ROWS), 1)
    pick = jnp.where((cols == pos[:, 0:1]) | (cols == pos[:, 1:2]), 1.0, 0.0).astype(BF16)
    h2 = _pick(t, npt, hp_ref, hs_ref) + jnp.dot(pick, ybuf[slot], preferred_element_type=F32)
    xn2 = _rmsnorm(h2, g_ple_ref[...])
    gate = _sigmoid(_dot(xn2, wpg_s[...]))
    p = jnp.where(t < npt, pp_ref[...], ps_ref[...])
    h3 = h2 + gate * _dot(p, wpp_s[...])
    if final:
        h3 = _rmsnorm(h3, g_fin_ref[...])

        @pl.when(t < npt)
        def _():
            op_ref[...] = h3

        @pl.when(t >= npt)
        def _():
            os_ref[...] = h3
    else:
        o_ref[...] = h3


def _combine_call(plan, hp, hs, pp, ps, pos, y, l, P, g_final, *, final):
    npt = hp.shape[0] // TOK_TILE
    nst = hs.shape[0] // TOK_TILE
    nt = npt + nst

    def lmap(t, *_):
        return (l, 0, 0)

    in_specs = _pair_specs(npt, D_MODEL) + [
        pl.BlockSpec((None, TOK_TILE, PLE_DIM), lambda t, *_: (l, jnp.minimum(t, npt - 1), 0)),
        pl.BlockSpec((None, TOK_TILE, PLE_DIM), lambda t, *_: (l, jnp.maximum(t - npt, 0), 0)),
        pl.BlockSpec((TOK_TILE, LANES), lambda t, *_: (t, 0)),
        pl.BlockSpec(memory_space=pl.ANY),
        pl.BlockSpec((None,) + P["g_ple"].shape[1:], lmap),
        pl.BlockSpec((None,) + P["w_ple_gate"].shape[1:], lmap),
        pl.BlockSpec((None,) + P["w_ple_proj"].shape[1:], lmap),
        pl.BlockSpec(g_final.shape, lambda t, *_: (0, 0))]
    if final:
        out_specs = (pl.BlockSpec((TOK_TILE, D_MODEL), lambda t, *_: (jnp.minimum(t, npt - 1), 0)),
                     pl.BlockSpec((TOK_TILE, D_MODEL), lambda t, *_: (jnp.maximum(t - npt, 0), 0)))
        out_shape = (jax.ShapeDtypeStruct(hp.shape, F32), jax.ShapeDtypeStruct(hs.shape, F32))
    else:
        out_specs = pl.BlockSpec((TOK_TILE, D_MODEL), lambda t, *_: (t, 0))
        out_shape = jax.ShapeDtypeStruct((nt * TOK_TILE, D_MODEL), F32)
    grid_spec = pltpu.PrefetchScalarGridSpec(
        num_scalar_prefetch=2, grid=(nt,), in_specs=in_specs, out_specs=out_specs,
        scratch_shapes=[pltpu.VMEM((2, CBUF_ROWS, D_MODEL), BF16), pltpu.VMEM((D_MODEL, D_MODEL), BF16),
                        pltpu.VMEM((PLE_DIM, D_MODEL), BF16), pltpu.SemaphoreType.DMA((2,))])
    return pl.pallas_call(
        functools.partial(_combine_kernel, final=final, npt=npt), grid_spec=grid_spec, out_shape=out_shape,
        compiler_params=pltpu.CompilerParams(dimension_semantics=("arbitrary",), vmem_limit_bytes=VMEM_LIMIT),
        name="moe_combine_final" if final else "moe_combine",
    )(*plan[:2], hp, hs, pp, ps, pos, y, P["g_ple"], P["w_ple_gate"], P["w_ple_proj"], g_final)


def _ffn(hp, hs, pp, ps, l, P, w_rt, b_rt, g_final, *, final):
    n = hp.shape[0] + hs.shape[0]
    nt = n // TOK_TILE
    bound = TOP_K * n + nt * N_EXPERTS * (CHUNK - 1) + N_EXPERTS * (EXP_TILE - 1)
    n_sorted = -(-bound // EXP_TILE) * EXP_TILE
    pos, cnt = _route_call(hp, hs, l, P, w_rt, b_rt)
    plan = _chunk_plan(cnt[:, 0, :N_EXPERTS].astype(jnp.int32), n_sorted // EXP_TILE)
    xs = _dispatch_call(plan, hp, hs, pos, l, P, n_sorted)
    y = _expert_call(plan, xs, l, P["wg"], P["wu"], P["wd"])
    return _combine_call(plan, hp, hs, pp, ps, pos, y, l, P, g_final, final=final)


def _router_weights(l, w_grp_router, b_grp_router, w_exp_router, b_exp_router):
    w_er = jnp.transpose(w_exp_router[l], (1, 0, 2)).reshape(D_MODEL, N_EXPERTS)
    w_rt = jnp.zeros((D_MODEL, LANES), F32).at[:, :N_EXPERTS].set(w_er)
    w_rt = w_rt.at[:, N_EXPERTS:N_EXPERTS + N_GROUPS].set(w_grp_router[l])
    b_rt = jnp.zeros((1, LANES), F32).at[0, :N_EXPERTS].set(b_exp_router[l].reshape(-1))
    b_rt = b_rt.at[0, N_EXPERTS:N_EXPERTS + N_GROUPS].set(b_grp_router[l])
    w_hi = w_rt.astype(BF16)
    w_lo = (w_rt - w_hi.astype(F32)).astype(BF16)
    return jnp.concatenate([w_hi, w_lo], axis=1), b_rt


def kernel(x_prompt, x_sample, state_gla, state_conv, p_prompt, p_sample, g_mix, w_in, w_forget_up, b_forget,
           g_gla_out, w_conv, b_conv, g_conv_ln, b_conv_ln, w_out, g_ffn, w_grp_router, b_grp_router,
           w_exp_router, b_exp_router, w_exp_gate, w_exp_up, w_exp_down, g_ple, w_ple_gate, w_ple_proj, g_final):
    depth = w_in.shape[0]
    nbp, seq_p, _ = x_prompt.shape
    nbs, seq_s, _ = x_sample.shape
    n_p = nbp * seq_p
    n_s = nbs * seq_s

    def rows(v):
        return v.reshape(depth, 1, -1)

    P = {
        "g_mix": rows(g_mix), "w_in": w_in, "w_cv": w_in[:, :, N_MAIN + GLA_LOWRANK:],
        "w_forget_up": w_forget_up, "b_forget": rows(b_forget), "g_gla_out": rows(g_gla_out),
        "w_conv": w_conv, "b_conv": rows(b_conv), "g_conv_ln": rows(g_conv_ln), "b_conv_ln": rows(b_conv_ln),
        "w_out": w_out, "g_ffn": rows(g_ffn), "g_ple": rows(g_ple),
        "w_ple_gate": w_ple_gate, "w_ple_proj": w_ple_proj,
        "wg": w_exp_gate.reshape(depth * N_EXPERTS, D_MODEL, EXPERT_FF),
        "wu": w_exp_up.reshape(depth * N_EXPERTS, D_MODEL, EXPERT_FF),
        "wd": w_exp_down.reshape(depth * N_EXPERTS, EXPERT_FF, D_MODEL),
    }
    g_fin = g_final.reshape(1, -1)
    xp = x_prompt.reshape(n_p, D_MODEL)
    xs = x_sample.reshape(n_s, D_MODEL)
    pp = p_prompt.reshape(depth, n_p, PLE_DIM)
    ps = p_sample.reshape(depth, n_s, PLE_DIM)
    s_in = state_gla.reshape(depth, nbs, QK_COLS, GLA_DV)

    h = None
    sg_p, sg_s, sc_p, sc_s = [], [], [], []
    for l in range(depth):
        if l == 0:
            hp, sgp, scp = _mixer_prompt(xp, 0, nbp, seq_p, l, P)
            hs, sgs, scs = _mixer_sample(xs, 0, nbs, seq_s, l, s_in, state_conv, P)
        else:
            hp, sgp, scp = _mixer_prompt(h, 0, nbp, seq_p, l, P)
            hs, sgs, scs = _mixer_sample(h, n_p, nbs, seq_s, l, s_in, state_conv, P)
        w_rt, b_rt = _router_weights(l, w_grp_router, b_grp_router, w_exp_router, b_exp_router)
        h = _ffn(hp, hs, pp, ps, l, P, w_rt, b_rt, g_fin, final=(l == depth - 1))
        sg_p.append(sgp.reshape(nbp, GLA_HEADS, GLA_DK, GLA_DV))
        sg_s.append(sgs.reshape(nbs, GLA_HEADS, GLA_DK, GLA_DV))
        sc_p.append(scp)
        sc_s.append(scs)

    y_prompt = h[0].reshape(nbp, seq_p, D_MODEL)
    y_sample = h[1].reshape(nbs, seq_s, D_MODEL)
    return (y_prompt, y_sample, jnp.stack(sg_p), jnp.stack(sg_s), jnp.stack(sc_p), jnp.stack(sc_s))
```

```python
import functools

import jax
import jax.numpy as jnp
from jax import lax
from jax.experimental import pallas as pl
from jax.experimental.pallas import tpu as pltpu

D_MODEL = 1024
GLA_HEADS = 4
GLA_DK = 64
GLA_DV = 128
QK_COLS = GLA_HEADS * GLA_DK
V_COLS = GLA_HEADS * GLA_DV
CONV_WIDTH = 512
CONV_K = 31
GLA_LOWRANK = 16
GLA_TAU = 16.0
GLA_CHUNK = 64
PLE_DIM = 256
N_GROUPS = 4
EXPERTS_PER_GROUP = 8
N_EXPERTS = N_GROUPS * EXPERTS_PER_GROUP
EXPERT_FF = 256
TOP_K = 2
EPS = 1e-6
N_MAIN = 2 * QK_COLS + 2 * V_COLS

LANES = 128
SUBLANES = 8
CONV_PAD = 32
CONV_OFF = CONV_PAD - (CONV_K - 1)
VMEM_LIMIT = 56 * 1024 * 1024
TOK_TILE = 512
CHUNK = 16
EXP_TILE = 512
CBUF_ROWS = -(-(TOP_K * TOK_TILE + N_EXPERTS * (CHUNK - 1)) // LANES) * LANES
XS_COLS = D_MODEL + LANES

F32 = jnp.float32
BF16 = jnp.bfloat16
HI = lax.Precision.HIGHEST


def _sigmoid(x):
    return 1.0 / (1.0 + jnp.exp(-x))


def _silu(x):
    return x * _sigmoid(x)


def _log_sigmoid(x):
    return jnp.minimum(x, 0.0) - jnp.log(1.0 + jnp.exp(-jnp.abs(x)))


def _rmsnorm(x, g):
    return x * lax.rsqrt(jnp.mean(x * x, axis=-1, keepdims=True) + EPS) * g


def _dot(a, b):
    return jnp.dot(a.astype(BF16), b.astype(BF16), preferred_element_type=F32)


def _dot_t(a, b):
    return lax.dot_general(a.astype(BF16), b.astype(BF16), (((1,), (1,)), ((), ())),
                           preferred_element_type=F32)


def _dot_hi(a, b):
    return jnp.dot(a, b, preferred_element_type=F32, precision=HI)


def _iota_f32(shape, dim):
    return lax.broadcasted_iota(jnp.int32, shape, dim).astype(F32)


def _const_spec(shape):
    nd = len(shape)
    return pl.BlockSpec(shape, lambda *_: (0,) * nd)


def _layer_spec(arr, l):
    nd = arr.ndim - 1
    return pl.BlockSpec((None,) + arr.shape[1:], lambda *_: (l,) + (0,) * nd, pipeline_mode=pl.Buffered(1))


def _cast_mixer_weights(w_int_ref, w_fu_ref, w_out_ref, wmain_s, wlr_s, wcv_s, wfu_s, wout_s):
    blk = 4 * LANES
    for r in range(0, N_MAIN, blk):
        wmain_s[:, r:r + blk] = w_int_ref[r:r + blk, :].T.astype(BF16)
    lane = lax.broadcasted_iota(jnp.int32, (D_MODEL, LANES), 1)
    wlr_s[...] = jnp.where(lane < GLA_LOWRANK, w_int_ref[N_MAIN:N_MAIN + LANES, :].T, 0.0).astype(BF16)
    cv0 = N_MAIN + GLA_LOWRANK
    for r in range(0, 2 * CONV_WIDTH, blk):
        wcv_s[:, r:r + blk] = w_int_ref[cv0 + r:cv0 + r + blk, :].T.astype(BF16)
    wfu_s[...] = jnp.zeros_like(wfu_s)
    wfu_s[0:GLA_LOWRANK, :] = w_fu_ref[...].astype(BF16)
    wout_s[...] = w_out_ref[...].astype(BF16)


def _project(x, g_mix, wmain_s, wlr_s, wfu_s, b_f, wcv_s):
    xn = _rmsnorm(x, g_mix).astype(BF16)
    z = jnp.dot(xn, wmain_s[...], preferred_element_type=F32)
    q = z[:, :QK_COLS] * (GLA_DK ** -0.5)
    k = z[:, QK_COLS:2 * QK_COLS]
    v = z[:, 2 * QK_COLS:2 * QK_COLS + V_COLS]
    og = z[:, 2 * QK_COLS + V_COLS:]
    lr = jnp.dot(xn, wlr_s[...], preferred_element_type=F32)
    zf = _dot(lr, wfu_s[...]) + b_f
    la = _log_sigmoid(zf) * (1.0 / GLA_TAU)
    cv = jnp.dot(xn, wcv_s[...], preferred_element_type=F32)
    u = cv[:, :CONV_WIDTH] * _sigmoid(cv[:, CONV_WIDTH:])
    return q, k, v, og, la, u


def _stack_heads(qd):
    lane = lax.broadcasted_iota(jnp.int32, qd.shape, 1)
    return jnp.concatenate(
        [jnp.where((lane >= h * GLA_DK) & (lane < (h + 1) * GLA_DK), qd, 0.0) for h in range(GLA_HEADS)],
        axis=0)


def _gated_head_norm(o, og, g_gla):
    outs = []
    for h in range(GLA_HEADS):
        sl = slice(h * GLA_DV, (h + 1) * GLA_DV)
        outs.append(_rmsnorm(o[:, sl], g_gla) * _silu(og[:, sl]))
    return jnp.concatenate(outs, axis=1)


def _causal_conv(win, w_conv, b_conv, n):
    acc = jnp.broadcast_to(b_conv, (n, CONV_WIDTH))
    for s in range(SUBLANES):
        taps = [j for j in range(CONV_K) if (CONV_OFF + j) % SUBLANES == s]
        if not taps:
            continue
        rows = n if s == 0 else n + SUBLANES
        part = None
        for j in taps:
            a = (CONV_OFF + j) - s
            term = w_conv[j:j + 1, :] * win[a:a + rows, :]
            part = term if part is None else part + term
        acc = acc + part[s:s + n, :]
    return acc


def _conv_ln_act(acc, g_ln, b_ln):
    mu = jnp.mean(acc, axis=-1, keepdims=True)
    xc = acc - mu
    y = xc * lax.rsqrt(jnp.mean(xc * xc, axis=-1, keepdims=True) + EPS) * g_ln + b_ln
    return _silu(y)


def _head_diag(upd):
    return jnp.concatenate([upd[h * GLA_DK:(h + 1) * GLA_DK, h * GLA_DV:(h + 1) * GLA_DV]
                            for h in range(GLA_HEADS)], axis=0)


def _mixer_weight_args(l, P):
    names = ("g_mix", "w_in_t", "w_forget_up", "b_forget", "g_gla_out", "w_conv", "b_conv", "g_conv_ln",
             "b_conv_ln", "w_out")
    arrs = [P[n] for n in names]
    return arrs, [_layer_spec(a, l) for a in arrs]


_MIXER_WEIGHT_SCRATCH = [pltpu.VMEM((D_MODEL, N_MAIN), BF16), pltpu.VMEM((D_MODEL, LANES), BF16),
                         pltpu.VMEM((D_MODEL, 2 * CONV_WIDTH), BF16), pltpu.VMEM((LANES, QK_COLS), BF16),
                         pltpu.VMEM((D_MODEL, D_MODEL), BF16)]


def _mixer_prompt_kernel(x_ref, g_mix_ref, w_int_ref, w_fu_ref, b_f_ref, g_gla_ref, w_conv_ref,
                         b_conv_ref, g_ln_ref, b_ln_ref, w_out_ref,
                         h_ref, sg_ref, sc_ref,
                         wmain_s, wlr_s, wcv_s, wfu_s, wout_s,
                         s_ref, ubuf_ref, qs_ref, kd_ref, klt_ref, dec_ref, v_ref, og_ref, mix_ref, *, tt):
    t = pl.program_id(1)
    nt = pl.num_programs(1)
    C = GLA_CHUNK
    n_chunks = tt // C

    @pl.when((pl.program_id(0) == 0) & (t == 0))
    def _():
        _cast_mixer_weights(w_int_ref, w_fu_ref, w_out_ref, wmain_s, wlr_s, wcv_s, wfu_s, wout_s)

    @pl.when(t == 0)
    def _():
        s_ref[...] = jnp.zeros_like(s_ref)
        ubuf_ref[0:CONV_PAD, :] = jnp.zeros((CONV_PAD, CONV_WIDTH), F32)

    x = x_ref[...]
    q, k, v, og, la, u = _project(x, g_mix_ref[...], wmain_s, wlr_s, wfu_s, b_f_ref[...], wcv_s)
    ubuf_ref[CONV_PAD:CONV_PAD + tt, :] = u
    v_ref[...] = v.astype(BF16)
    og_ref[...] = _silu(og)

    w_conv = w_conv_ref[...]
    b_conv = b_conv_ref[...]
    g_ln = g_ln_ref[...]
    b_ln = b_ln_ref[...]
    for c in range(n_chunks):
        win = ubuf_ref[c * C:c * C + C + CONV_PAD, :]
        mix_ref[c * C:(c + 1) * C, V_COLS:] = _conv_ln_act(_causal_conv(win, w_conv, b_conv, C), g_ln, b_ln).astype(BF16)

    row = lax.broadcasted_iota(jnp.int32, (C, C), 0)
    col = lax.broadcasted_iota(jnp.int32, (C, C), 1)
    tri = (col <= row).astype(F32)
    for c in range(n_chunks):
        rows = slice(c * C, (c + 1) * C)
        b = _dot_hi(tri, la[rows, :])
        b_last = b[C - 1:C, :]
        qs_ref[c] = _stack_heads(q[rows, :] * jnp.exp(b)).astype(BF16)
        kd_ref[rows, :] = (k[rows, :] * jnp.exp(-b)).astype(BF16)
        kl = k[rows, :] * jnp.exp(b_last - b)
        klt = jnp.concatenate([kl, jnp.broadcast_to(jnp.exp(b_last), (C, QK_COLS))], axis=0).T
        klt_ref[c] = klt.astype(BF16)
        dec_ref[c] = jnp.broadcast_to(klt[:, C:C + 1], (QK_COLS, GLA_DV))

    r4 = lax.broadcasted_iota(jnp.int32, (GLA_HEADS * C, C), 0)
    c4 = lax.broadcasted_iota(jnp.int32, (GLA_HEADS * C, C), 1)
    causal4 = c4 <= (r4 % C)
    g_gla = g_gla_ref[...]
    s = s_ref[...]
    for c in range(n_chunks):
        rows = slice(c * C, (c + 1) * C)
        qs = qs_ref[c]
        vc = v_ref[rows, :]
        scores = jnp.where(causal4, _dot_t(qs, kd_ref[rows, :]), 0.0).astype(BF16)
        o_inter = jnp.dot(qs, s.astype(BF16), preferred_element_type=F32)
        upd = jnp.dot(klt_ref[c][:, :C], vc, preferred_element_type=F32)
        s = dec_ref[c] * s + _head_diag(upd)
        o_parts = []
        for h in range(GLA_HEADS):
            vh = vc[:, h * GLA_DV:(h + 1) * GLA_DV]
            o_parts.append(jnp.dot(scores[h * C:(h + 1) * C, :], vh, preferred_element_type=F32)
                           + o_inter[h * C:(h + 1) * C, :])
        o = jnp.concatenate(o_parts, axis=1)
        gated = []
        for h in range(GLA_HEADS):
            sl = slice(h * GLA_DV, (h + 1) * GLA_DV)
            gated.append(_rmsnorm(o[:, sl], g_gla) * og_ref[rows, sl])
        mix_ref[rows, 0:V_COLS] = jnp.concatenate(gated, axis=1).astype(BF16)
    s_ref[...] = s

    h_ref[...] = x + jnp.dot(mix_ref[...], wout_s[...], preferred_element_type=F32)
    tail = ubuf_ref[tt:tt + CONV_PAD, :]
    ubuf_ref[0:CONV_PAD, :] = tail

    @pl.when(t == nt - 1)
    def _():
        sg_ref[0] = s
        sc_ref[0] = tail[CONV_OFF:, :]


def _mixer_prompt(x, row_off, nb, seq, l, P, *, tt=512):
    nt = seq // tt
    n_chunks = tt // GLA_CHUNK
    blk_off = row_off // tt
    weights, w_specs = _mixer_weight_args(l, P)
    in_specs = [pl.BlockSpec((tt, D_MODEL), lambda b, t: (blk_off + b * nt + t, 0))] + w_specs
    out_shape = (jax.ShapeDtypeStruct((nb * seq, D_MODEL), F32),
                 jax.ShapeDtypeStruct((nb, QK_COLS, GLA_DV), F32),
                 jax.ShapeDtypeStruct((nb, CONV_K - 1, CONV_WIDTH), F32))
    out_specs = (pl.BlockSpec((tt, D_MODEL), lambda b, t: (b * nt + t, 0)),
                 pl.BlockSpec((1, QK_COLS, GLA_DV), lambda b, t: (b, 0, 0)),
                 pl.BlockSpec((1, CONV_K - 1, CONV_WIDTH), lambda b, t: (b, 0, 0)))
    scratch = _MIXER_WEIGHT_SCRATCH + [
        pltpu.VMEM((QK_COLS, GLA_DV), F32),
        pltpu.VMEM((CONV_PAD + tt, CONV_WIDTH), F32),
        pltpu.VMEM((n_chunks, GLA_HEADS * GLA_CHUNK, QK_COLS), BF16), pltpu.VMEM((tt, QK_COLS), BF16),
        pltpu.VMEM((n_chunks, QK_COLS, 2 * GLA_CHUNK), BF16), pltpu.VMEM((n_chunks, QK_COLS, GLA_DV), F32),
        pltpu.VMEM((tt, V_COLS), BF16), pltpu.VMEM((tt, V_COLS), F32),
        pltpu.VMEM((tt, D_MODEL), BF16)]
    return pl.pallas_call(
        functools.partial(_mixer_prompt_kernel, tt=tt),
        grid=(nb, nt), in_specs=in_specs, out_specs=out_specs, out_shape=out_shape,
        scratch_shapes=scratch,
        compiler_params=pltpu.CompilerParams(dimension_semantics=("arbitrary", "arbitrary"),
                                             vmem_limit_bytes=VMEM_LIMIT),
        name="mixer_prompt",
    )(x, *weights)


def _mixer_sample_kernel(x_ref, s_in_ref, c_in_ref, g_mix_ref, w_int_ref, w_fu_ref, b_f_ref,
                         g_gla_ref, w_conv_ref, b_conv_ref, g_ln_ref, b_ln_ref, w_out_ref,
                         h_ref, sg_ref, sc_ref,
                         wmain_s, wlr_s, wcv_s, wfu_s, wout_s,
                         qs_ref, klt_ref, dect_ref, v_ref, u4_ref, oi_ref, cacc4_ref, *, sb, seq):
    R = sb * seq
    n_slabs = CONV_WIDTH // LANES

    @pl.when(pl.program_id(0) == 0)
    def _():
        _cast_mixer_weights(w_int_ref, w_fu_ref, w_out_ref, wmain_s, wlr_s, wcv_s, wfu_s, wout_s)

    x = x_ref[...]
    q, k, v, og, la, u = _project(x, g_mix_ref[...], wmain_s, wlr_s, wfu_s, b_f_ref[...], wcv_s)

    for kk in range(n_slabs):
        u4_ref[kk] = u[:, kk * LANES:(kk + 1) * LANES]
    full = [c_in_ref[j] for j in range(CONV_K - 1)]
    for t in range(seq):
        full.append(jnp.concatenate([u4_ref.at[kk][pl.ds(t, sb, stride=seq), :] for kk in range(n_slabs)], axis=1))
    w_conv = w_conv_ref[...]
    for t in range(seq):
        acc = jnp.broadcast_to(b_conv_ref[...], (sb, CONV_WIDTH))
        for j in range(CONV_K):
            acc = acc + w_conv[j:j + 1, :] * full[t + j]
        for kk in range(n_slabs):
            cacc4_ref.at[kk][pl.ds(t, sb, stride=seq), :] = acc[:, kk * LANES:(kk + 1) * LANES]
    for j in range(CONV_K - 1):
        sc_ref[j] = full[seq + j]

    row = lax.broadcasted_iota(jnp.int32, (R, R), 0)
    col = lax.broadcasted_iota(jnp.int32, (R, R), 1)
    same = (row // seq) == (col // seq)
    b = _dot_hi((same & (col <= row)).astype(F32), la)
    b_tot = _dot_hi(same.astype(F32), la)
    qd = q * jnp.exp(b)
    kd = k * jnp.exp(-b)
    kl = k * jnp.exp(b_tot - b)
    qs = _stack_heads(qd)
    r4 = lax.broadcasted_iota(jnp.int32, (GLA_HEADS * R, R), 0) % R
    c4 = lax.broadcasted_iota(jnp.int32, (GLA_HEADS * R, R), 1)
    mask4 = ((r4 // seq) == (c4 // seq)) & (c4 <= r4)
    scores = jnp.where(mask4, _dot_t(qs, kd), 0.0)

    qs_ref[...] = qs
    klt_ref[...] = kl.T
    dect_ref[...] = jnp.exp(b_tot).T
    v_ref[...] = v

    lane_r = lax.broadcasted_iota(jnp.int32, (QK_COLS, R), 1)

    def one_seq(i, carry):
        lo = i * seq
        r0 = pl.multiple_of(lo, seq)
        s_old = s_in_ref[i]
        qsel = jnp.concatenate([qs_ref[pl.ds(pl.multiple_of(h * R + lo, seq), seq), :]
                                for h in range(GLA_HEADS)], axis=0)
        oi = _dot(qsel, s_old)
        for h in range(GLA_HEADS):
            oi_ref[h, pl.ds(r0, seq), :] = oi[h * seq:(h + 1) * seq, :]
        lmask = (lane_r >= lo) & (lane_r < lo + seq)
        upd = _dot(jnp.where(lmask, klt_ref[...], 0.0), v_ref[...])
        dec = jnp.sum(jnp.where(lmask, dect_ref[...], 0.0), axis=1, keepdims=True) * (1.0 / seq)
        sg_ref[i] = dec * s_old + _head_diag(upd)
        return carry

    lax.fori_loop(0, sb, one_seq, 0)

    o_parts = []
    for h in range(GLA_HEADS):
        vh = v[:, h * GLA_DV:(h + 1) * GLA_DV]
        o_parts.append(_dot(scores[h * R:(h + 1) * R, :], vh) + oi_ref[h])
    o = jnp.concatenate(o_parts, axis=1)
    cacc = jnp.concatenate([cacc4_ref[kk] for kk in range(n_slabs)], axis=1)
    mix = jnp.concatenate([_gated_head_norm(o, og, g_gla_ref[...]),
                           _conv_ln_act(cacc, g_ln_ref[...], b_ln_ref[...])], axis=1)
    h_ref[...] = x + jnp.dot(mix.astype(BF16), wout_s[...], preferred_element_type=F32)


def _mixer_sample(x, row_off, nb, seq, l, s_in, c_in_t, P, *, sb=16):
    R = sb * seq
    blk_off = row_off // R
    n_slabs = CONV_WIDTH // LANES
    weights, w_specs = _mixer_weight_args(l, P)
    in_specs = [pl.BlockSpec((R, D_MODEL), lambda i: (blk_off + i, 0)),
                pl.BlockSpec((None, sb, QK_COLS, GLA_DV), lambda i: (l, i, 0, 0)),
                pl.BlockSpec((None, CONV_K - 1, sb, CONV_WIDTH), lambda i: (l, 0, i, 0))] + w_specs
    out_shape = (jax.ShapeDtypeStruct((nb * seq, D_MODEL), F32),
                 jax.ShapeDtypeStruct((nb, QK_COLS, GLA_DV), F32),
                 jax.ShapeDtypeStruct((CONV_K - 1, nb, CONV_WIDTH), F32))
    out_specs = (pl.BlockSpec((R, D_MODEL), lambda i: (i, 0)),
                 pl.BlockSpec((sb, QK_COLS, GLA_DV), lambda i: (i, 0, 0)),
                 pl.BlockSpec((CONV_K - 1, sb, CONV_WIDTH), lambda i: (0, i, 0)))
    scratch = _MIXER_WEIGHT_SCRATCH + [
        pltpu.VMEM((GLA_HEADS * R, QK_COLS), F32),
        pltpu.VMEM((QK_COLS, R), F32), pltpu.VMEM((QK_COLS, R), F32),
        pltpu.VMEM((R, V_COLS), F32), pltpu.VMEM((n_slabs, R, LANES), F32),
        pltpu.VMEM((GLA_HEADS, R, GLA_DV), F32),
        pltpu.VMEM((n_slabs, R, LANES), F32)]
    return pl.pallas_call(
        functools.partial(_mixer_sample_kernel, sb=sb, seq=seq),
        grid=(nb // sb,), in_specs=in_specs, out_specs=out_specs, out_shape=out_shape,
        scratch_shapes=scratch,
        compiler_params=pltpu.CompilerParams(dimension_semantics=("arbitrary",),
                                             vmem_limit_bytes=VMEM_LIMIT),
        name="mixer_sample",
    )(x, s_in, c_in_t, *weights)


def _pair_specs(n_first_tiles, width):
    return [pl.BlockSpec((TOK_TILE, width), lambda t, *_: (jnp.minimum(t, n_first_tiles - 1), 0)),
            pl.BlockSpec((TOK_TILE, width), lambda t, *_: (jnp.maximum(t - n_first_tiles, 0), 0))]


def _pick(t, n_first_tiles, a_ref, b_ref):
    return jnp.where(t < n_first_tiles, a_ref[...], b_ref[...])


def _route(logits):
    lane = lax.broadcasted_iota(jnp.int32, logits.shape, 1)
    lane_f = lane.astype(F32)
    neg = jnp.float32(-jnp.inf)
    big = jnp.float32(1e9)
    is_grp = (lane >= N_EXPERTS) & (lane < N_EXPERTS + N_GROUPS)
    gl = jnp.where(is_grp, logits, neg)
    gmax = jnp.max(gl, axis=-1, keepdims=True)
    gidx = jnp.min(jnp.where(is_grp & (gl == gmax), lane_f - N_EXPERTS, big), axis=-1, keepdims=True)
    gsum = jnp.sum(jnp.where(is_grp, jnp.exp(gl - gmax), 0.0), axis=-1, keepdims=True)
    g_w = 1.0 / gsum
    grp_of_lane = jnp.floor(lane_f * (1.0 / EXPERTS_PER_GROUP))
    in_grp = (lane < N_EXPERTS) & (grp_of_lane == gidx)
    ml = jnp.where(in_grp, logits, neg)
    v1 = jnp.max(ml, axis=-1, keepdims=True)
    i1 = jnp.min(jnp.where(in_grp & (ml == v1), lane_f, big), axis=-1, keepdims=True)
    ml2 = jnp.where(lane_f == i1, neg, ml)
    v2 = jnp.max(ml2, axis=-1, keepdims=True)
    i2 = jnp.min(jnp.where(in_grp & (ml2 == v2), lane_f, big), axis=-1, keepdims=True)
    e2 = jnp.exp(v2 - v1)
    w1 = g_w / (1.0 + e2)
    w2 = g_w * e2 / (1.0 + e2)
    return i1, i2, w1, w2


def _route_kernel(hp_ref, hs_ref, g_ffn_ref, w_rt_ref, b_rt_ref, pos_ref, cnt_ref, *, npt):
    T = TOK_TILE
    xn = _rmsnorm(_pick(pl.program_id(0), npt, hp_ref, hs_ref), g_ffn_ref[...])
    x_hi = xn.astype(BF16)
    x_lo = (xn - x_hi.astype(F32)).astype(BF16)
    w_split = w_rt_ref[...]
    both = jnp.dot(x_hi, w_split, preferred_element_type=F32)
    logits = (both[:, :LANES] + both[:, LANES:] + jnp.dot(x_lo, w_split[:, :LANES], preferred_element_type=F32)
              + b_rt_ref[...])
    i1, i2, w1, w2 = _route(logits)
    lane = _iota_f32((T, LANES), 1)
    a0 = (lane == i1).astype(F32)
    a1 = (lane == i2).astype(F32)
    a = a0 + a1
    cnt = jnp.sum(a, axis=0, keepdims=True)
    earlier = (_iota_f32((T, T), 1) < _iota_f32((T, T), 0)).astype(BF16)
    rank = jnp.dot(earlier, a.astype(BF16), preferred_element_type=F32)
    cnt_pad = jnp.ceil(cnt * (1.0 / CHUNK)) * CHUNK
    below = (_iota_f32((LANES, LANES), 0) < _iota_f32((LANES, LANES), 1)).astype(F32)
    first = _dot_hi(jnp.broadcast_to(cnt_pad, (SUBLANES, LANES)), below)[0:1, :]
    base = first + rank
    pos0 = jnp.sum(a0 * base, axis=1, keepdims=True)
    pos1 = jnp.sum(a1 * base, axis=1, keepdims=True)
    pos_ref[...] = jnp.where(lane == 0.0, pos0, jnp.where(lane == 1.0, pos1, jnp.where(
        lane == 2.0, w1, jnp.where(lane == 3.0, w2, jnp.where(lane == 4.0, i1, 0.0)))))
    cnt_ref[0] = jnp.broadcast_to(cnt, (SUBLANES, LANES))


def _route_call(hp, hs, l, P, w_rt, b_rt):
    npt = hp.shape[0] // TOK_TILE
    nt = npt + hs.shape[0] // TOK_TILE
    return pl.pallas_call(
        functools.partial(_route_kernel, npt=npt), grid=(nt,),
        in_specs=_pair_specs(npt, D_MODEL) + [_layer_spec(P["g_ffn"], l), _const_spec(w_rt.shape),
                                              _const_spec(b_rt.shape)],
        out_specs=(pl.BlockSpec((TOK_TILE, LANES), lambda t: (t, 0)),
                   pl.BlockSpec((1, SUBLANES, LANES), lambda t: (t, 0, 0))),
        out_shape=(jax.ShapeDtypeStruct((nt * TOK_TILE, LANES), F32),
                   jax.ShapeDtypeStruct((nt, SUBLANES, LANES), F32)),
        compiler_params=pltpu.CompilerParams(dimension_semantics=("arbitrary",), vmem_limit_bytes=VMEM_LIMIT),
        name="moe_route",
    )(hp, hs, P["g_ffn"], w_rt, b_rt)


def _chunk_plan(cnt, n_row_tiles):
    n16 = (cnt + (CHUNK - 1)) // CHUNK
    lofs16 = jnp.cumsum(n16, axis=1) - n16
    tile_pref16 = jnp.cumsum(n16, axis=0) - n16
    tot16 = jnp.sum(n16, axis=0)
    per_tile = EXP_TILE // CHUNK
    seg16 = ((tot16 + per_tile - 1) // per_tile) * per_tile
    seg_end16 = jnp.cumsum(seg16)
    dst16 = (seg_end16 - seg16)[None, :] + tile_pref16
    n_tot = jnp.sum(n16, axis=1)
    gap16 = seg_end16 - seg16 + tot16
    gapn16 = seg16 - tot16
    tile_start16 = jnp.arange(n_row_tiles, dtype=jnp.int32) * per_tile
    n_valid = seg_end16[-1] // per_tile
    misc = n_valid.reshape(1)
    exp_of_tile = jnp.minimum(jnp.sum(seg_end16[None, :] <= tile_start16[:, None], axis=1), N_EXPERTS - 1)
    i32 = lambda a: a.astype(jnp.int32).reshape(-1)
    return (i32(dst16), i32(n16), i32(lofs16), i32(n_tot), i32(gap16), i32(gapn16), i32(misc), i32(exp_of_tile))


def _chunk_copy(src, dst, src_chunk, dst_chunk, sem, n_chunks=1):
    rows = n_chunks * CHUNK
    return pltpu.make_async_copy(src.at[pl.ds(pl.multiple_of(src_chunk * CHUNK, CHUNK), rows), :],
                                 dst.at[pl.ds(pl.multiple_of(dst_chunk * CHUNK, CHUNK), rows), :], sem)


def _slab_copies(src, dst, sem, tile, src_ofs_ref, dst_ofs_ref, n16_ref):
    for e in range(N_EXPERTS):
        k = tile * N_EXPERTS + e
        n = n16_ref[k]

        @pl.when(n > 0)
        def _(k=k, n=n):
            _chunk_copy(src, dst, src_ofs_ref[k], dst_ofs_ref[k], sem, n).start()


def _wait_slabs(src, dst, sem, n_chunks):
    @pl.when(n_chunks > 0)
    def _():
        _chunk_copy(src, dst, 0, 0, sem, n_chunks).wait()


def _tile_copy(src, dst, dst_tile, sem):
    return pltpu.make_async_copy(src, dst.at[pl.ds(pl.multiple_of(dst_tile * EXP_TILE, EXP_TILE), EXP_TILE), :], sem)


def _dispatch_kernel(dst16_ref, n16_ref, lofs16_ref, ntot_ref, gap16_ref, gapn16_ref, misc_ref,
                     hp_ref, hs_ref, pos_ref, g_ffn_ref, xs_hbm, cbuf, zbuf, sem, *, n_row_tiles, npt):
    t = pl.program_id(0)
    nt = pl.num_programs(0)
    slot = t % 2
    T = TOK_TILE
    n_tail = n_row_tiles - misc_ref[0]

    @pl.when(t == 0)
    def _():
        zbuf[...] = jnp.zeros_like(zbuf)
        for e in range(N_EXPERTS):
            g = gapn16_ref[e]

            @pl.when(g > 0)
            def _(e=e, g=g):
                _chunk_copy(zbuf, xs_hbm, 0, gap16_ref[e], sem.at[2], g).start()

        def fill_tile(i, carry):
            _tile_copy(zbuf, xs_hbm, misc_ref[0] + i, sem.at[2]).start()
            return carry
        lax.fori_loop(0, n_tail, fill_tile, 0)

    xn = _rmsnorm(_pick(t, npt, hp_ref, hs_ref), g_ffn_ref[...]).astype(BF16)
    pos = pos_ref[...]
    pos_t = pos.T
    rows = _iota_f32((CBUF_ROWS, T), 0)
    onehot = jnp.where((rows == pos_t[0:1, :]) | (rows == pos_t[1:2, :]), 1.0, 0.0).astype(BF16)
    lane = lax.broadcasted_iota(jnp.int32, (T, LANES), 1)
    extra = jnp.zeros((T, LANES), F32)
    for s in range(TOP_K):
        c = pos[:, 2 + s:3 + s]
        hi = c.astype(BF16).astype(F32)
        mid = (c - hi).astype(BF16).astype(F32)
        lo = c - hi - mid
        for j, piece in enumerate((hi, mid, lo)):
            extra = jnp.where(lane == 3 * s + j, piece, extra)
    extra = jnp.where(lane == 3 * TOP_K, pos[:, 4:5], extra)
    cbuf[slot] = jnp.dot(onehot, jnp.concatenate([xn, extra.astype(BF16)], axis=1),
                         preferred_element_type=F32).astype(BF16)

    src = cbuf.at[slot]
    _slab_copies(src, xs_hbm, sem.at[slot], t, lofs16_ref, dst16_ref, n16_ref)

    @pl.when(t > 0)
    def _():
        _wait_slabs(cbuf.at[1 - slot], xs_hbm, sem.at[1 - slot], ntot_ref[t - 1])

    @pl.when(t == nt - 1)
    def _():
        _wait_slabs(src, xs_hbm, sem.at[slot], ntot_ref[t])
        for e in range(N_EXPERTS):
            _wait_slabs(zbuf, xs_hbm, sem.at[2], gapn16_ref[e])

        def wait_tile(_, carry):
            _tile_copy(zbuf, xs_hbm, 0, sem.at[2]).wait()
            return carry
        lax.fori_loop(0, n_tail, wait_tile, 0)


def _dispatch_call(plan, hp, hs, pos, l, P, n_sorted):
    npt = hp.shape[0] // TOK_TILE
    nt = npt + hs.shape[0] // TOK_TILE
    g_ffn = P["g_ffn"]
    grid_spec = pltpu.PrefetchScalarGridSpec(
        num_scalar_prefetch=7, grid=(nt,),
        in_specs=_pair_specs(npt, D_MODEL) + [
            pl.BlockSpec((TOK_TILE, LANES), lambda t, *_: (t, 0)),
            pl.BlockSpec((None,) + g_ffn.shape[1:], lambda t, *_: (l, 0, 0))],
        out_specs=pl.BlockSpec(memory_space=pl.ANY),
        scratch_shapes=[pltpu.VMEM((2, CBUF_ROWS, XS_COLS), BF16), pltpu.VMEM((EXP_TILE, XS_COLS), BF16),
                        pltpu.SemaphoreType.DMA((3,))])
    return pl.pallas_call(
        functools.partial(_dispatch_kernel, n_row_tiles=n_sorted // EXP_TILE, npt=npt), grid_spec=grid_spec,
        out_shape=jax.ShapeDtypeStruct((n_sorted, XS_COLS), BF16),
        compiler_params=pltpu.CompilerParams(dimension_semantics=("arbitrary",), vmem_limit_bytes=VMEM_LIMIT),
        name="moe_dispatch",
    )(*plan[:7], hp, hs, pos, g_ffn)


def _expert_kernel(eot_ref, misc_ref, xs_ref, wg_ref, wu_ref, wd_ref, y_ref, wg_s, wu_s, wd_s):
    i = pl.program_id(0)
    valid = i < misc_ref[0]

    @pl.when(jnp.logical_not(valid))
    def _():
        y_ref[...] = jnp.zeros_like(y_ref)

    @pl.when(valid & ((i == 0) | (eot_ref[i] != eot_ref[jnp.maximum(i - 1, 0)])))
    def _():
        wg_s[...] = wg_ref[...].astype(BF16)
        wu_s[...] = wu_ref[...].astype(BF16)
        wd_s[...] = wd_ref[...].astype(BF16)

    @pl.when(valid)
    def _():
        xs = xs_ref[...]
        x = xs[:, :D_MODEL]
        ex = xs[:, D_MODEL:].astype(F32)
        lane = lax.broadcasted_iota(jnp.int32, ex.shape, 1)
        id0 = jnp.sum(jnp.where(lane == 3 * TOP_K, ex, 0.0), axis=-1, keepdims=True)
        first = id0 == eot_ref[i].astype(F32)
        mine = (first & (lane < 3)) | (jnp.logical_not(first) & (lane >= 3) & (lane < 3 * TOP_K))
        c = jnp.sum(jnp.where(mine, ex, 0.0), axis=-1, keepdims=True)
        hg = _silu(jnp.dot(x, wg_s[...], preferred_element_type=F32)) * jnp.dot(x, wu_s[...],
                                                                               preferred_element_type=F32)
        y_ref[...] = jnp.dot((hg * c).astype(BF16), wd_s[...], preferred_element_type=F32).astype(BF16)


def _expert_call(plan, xs, l, wg, wu, wd):
    misc, exp_of_tile = plan[6:]
    n_row_tiles = xs.shape[0] // EXP_TILE

    def last_valid(i, nv):
        return jnp.maximum(jnp.minimum(i, nv[0] - 1), 0)

    def row_map(i, eot, nv):
        return (last_valid(i, nv), 0)

    def w_map(i, eot, nv):
        return (l * N_EXPERTS + eot[last_valid(i, nv)], 0, 0)

    grid_spec = pltpu.PrefetchScalarGridSpec(
        num_scalar_prefetch=2, grid=(n_row_tiles,),
        in_specs=[pl.BlockSpec((EXP_TILE, XS_COLS), row_map),
                  pl.BlockSpec((None, D_MODEL, EXPERT_FF), w_map),
                  pl.BlockSpec((None, D_MODEL, EXPERT_FF), w_map),
                  pl.BlockSpec((None, EXPERT_FF, D_MODEL), w_map)],
        out_specs=pl.BlockSpec((EXP_TILE, D_MODEL), lambda i, eot, nv: (i, 0)),
        scratch_shapes=[pltpu.VMEM((D_MODEL, EXPERT_FF), BF16), pltpu.VMEM((D_MODEL, EXPERT_FF), BF16),
                        pltpu.VMEM((EXPERT_FF, D_MODEL), BF16)])
    return pl.pallas_call(
        _expert_kernel, grid_spec=grid_spec,
        out_shape=jax.ShapeDtypeStruct((xs.shape[0], D_MODEL), BF16),
        compiler_params=pltpu.CompilerParams(dimension_semantics=("arbitrary",), vmem_limit_bytes=VMEM_LIMIT),
        name="moe_experts",
    )(exp_of_tile, misc, xs, wg, wu, wd)


def _combine_kernel(dst16_ref, n16_ref, lofs16_ref, ntot_ref, hp_ref, hs_ref, pp_ref, ps_ref, pos_ref, y_hbm,
                    g_ple_ref, w_pg_ref, w_pp_ref, g_fin_ref, *rest, final, npt):
    if final:
        op_ref, os_ref, ybuf, wpg_s, wpp_s, sem = rest
    else:
        o_ref, ybuf, wpg_s, wpp_s, sem = rest
    t = pl.program_id(0)
    nt = pl.num_programs(0)
    slot = t % 2

    def fetch(tile, sl):
        _slab_copies(y_hbm, ybuf.at[sl], sem.at[sl], tile, dst16_ref, lofs16_ref, n16_ref)

    @pl.when(t == 0)
    def _():
        ybuf[...] = jnp.zeros_like(ybuf)
        fetch(t, slot)
        wpg_s[...] = w_pg_ref[...].astype(BF16)
        wpp_s[...] = w_pp_ref[...].astype(BF16)

    @pl.when(t + 1 < nt)
    def _():
        fetch(t + 1, 1 - slot)

    _wait_slabs(y_hbm, ybuf.at[slot], sem.at[slot], ntot_ref[t])

    pos = pos_ref[...]
    cols = _iota_f32((TOK_TILE, CBUF_ROWS), 1)
    pick = jnp.where((cols == pos[:, 0:1]) | (cols == pos[:, 1:2]), 1.0, 0.0).astype(BF16)
    h2 = _pick(t, npt, hp_ref, hs_ref) + jnp.dot(pick, ybuf[slot], preferred_element_type=F32)
    xn2 = _rmsnorm(h2, g_ple_ref[...])
    gate = _sigmoid(_dot(xn2, wpg_s[...]))
    p = jnp.where(t < npt, pp_ref[...], ps_ref[...])
    h3 = h2 + gate * _dot(p, wpp_s[...])
    if final:
        h3 = _rmsnorm(h3, g_fin_ref[...])

        @pl.when(t < npt)
        def _():
            op_ref[...] = h3

        @pl.when(t >= npt)
        def _():
            os_ref[...] = h3
    else:
        o_ref[...] = h3


def _combine_call(plan, hp, hs, pp, ps, pos, y, l, P, g_final, *, final):
    npt = hp.shape[0] // TOK_TILE
    nst = hs.shape[0] // TOK_TILE
    nt = npt + nst

    def lmap(t, *_):
        return (l, 0, 0)

    in_specs = _pair_specs(npt, D_MODEL) + [
        pl.BlockSpec((None, TOK_TILE, PLE_DIM), lambda t, *_: (l, jnp.minimum(t, npt - 1), 0)),
        pl.BlockSpec((None, TOK_TILE, PLE_DIM), lambda t, *_: (l, jnp.maximum(t - npt, 0), 0)),
        pl.BlockSpec((TOK_TILE, LANES), lambda t, *_: (t, 0)),
        pl.BlockSpec(memory_space=pl.ANY),
        pl.BlockSpec((None,) + P["g_ple"].shape[1:], lmap),
        pl.BlockSpec((None,) + P["w_ple_gate"].shape[1:], lmap),
        pl.BlockSpec((None,) + P["w_ple_proj"].shape[1:], lmap),
        pl.BlockSpec(g_final.shape, lambda t, *_: (0, 0))]
    if final:
        out_specs = (pl.BlockSpec((TOK_TILE, D_MODEL), lambda t, *_: (jnp.minimum(t, npt - 1), 0)),
                     pl.BlockSpec((TOK_TILE, D_MODEL), lambda t, *_: (jnp.maximum(t - npt, 0), 0)))
        out_shape = (jax.ShapeDtypeStruct(hp.shape, F32), jax.ShapeDtypeStruct(hs.shape, F32))
    else:
        out_specs = pl.BlockSpec((TOK_TILE, D_MODEL), lambda t, *_: (t, 0))
        out_shape = jax.ShapeDtypeStruct((nt * TOK_TILE, D_MODEL), F32)
    grid_spec = pltpu.PrefetchScalarGridSpec(
        num_scalar_prefetch=4, grid=(nt,), in_specs=in_specs, out_specs=out_specs,
        scratch_shapes=[pltpu.VMEM((2, CBUF_ROWS, D_MODEL), BF16), pltpu.VMEM((D_MODEL, D_MODEL), BF16),
                        pltpu.VMEM((PLE_DIM, D_MODEL), BF16), pltpu.SemaphoreType.DMA((2,))])
    return pl.pallas_call(
        functools.partial(_combine_kernel, final=final, npt=npt), grid_spec=grid_spec, out_shape=out_shape,
        compiler_params=pltpu.CompilerParams(dimension_semantics=("arbitrary",), vmem_limit_bytes=VMEM_LIMIT),
        name="moe_combine_final" if final else "moe_combine",
    )(*plan[:4], hp, hs, pp, ps, pos, y, P["g_ple"], P["w_ple_gate"], P["w_ple_proj"], g_final)


def _ffn(hp, hs, pp, ps, l, P, w_rt, b_rt, g_final, *, final):
    n = hp.shape[0] + hs.shape[0]
    nt = n // TOK_TILE
    bound = TOP_K * n + nt * N_EXPERTS * (CHUNK - 1) + N_EXPERTS * (EXP_TILE - 1)
    n_sorted = -(-bound // EXP_TILE) * EXP_TILE
    pos, cnt = _route_call(hp, hs, l, P, w_rt, b_rt)
    plan = _chunk_plan(cnt[:, 0, :N_EXPERTS].astype(jnp.int32), n_sorted // EXP_TILE)
    xs = _dispatch_call(plan, hp, hs, pos, l, P, n_sorted)
    y = _expert_call(plan, xs, l, P["wg"], P["wu"], P["wd"])
    return _combine_call(plan, hp, hs, pp, ps, pos, y, l, P, g_final, final=final)


def _router_weights(l, w_grp_router, b_grp_router, w_exp_router, b_exp_router):
    w_er = jnp.transpose(w_exp_router[l], (1, 0, 2)).reshape(D_MODEL, N_EXPERTS)
    w_rt = jnp.zeros((D_MODEL, LANES), F32).at[:, :N_EXPERTS].set(w_er)
    w_rt = w_rt.at[:, N_EXPERTS:N_EXPERTS + N_GROUPS].set(w_grp_router[l])
    b_rt = jnp.zeros((1, LANES), F32).at[0, :N_EXPERTS].set(b_exp_router[l].reshape(-1))
    b_rt = b_rt.at[0, N_EXPERTS:N_EXPERTS + N_GROUPS].set(b_grp_router[l])
    w_hi = w_rt.astype(BF16)
    w_lo = (w_rt - w_hi.astype(F32)).astype(BF16)
    return jnp.concatenate([w_hi, w_lo], axis=1), b_rt


def kernel(x_prompt, x_sample, state_gla, state_conv, p_prompt, p_sample, g_mix, w_in, w_forget_up, b_forget,
           g_gla_out, w_conv, b_conv, g_conv_ln, b_conv_ln, w_out, g_ffn, w_grp_router, b_grp_router,
           w_exp_router, b_exp_router, w_exp_gate, w_exp_up, w_exp_down, g_ple, w_ple_gate, w_ple_proj, g_final):
    depth = w_in.shape[0]
    nbp, seq_p, _ = x_prompt.shape
    nbs, seq_s, _ = x_sample.shape
    n_p = nbp * seq_p
    n_s = nbs * seq_s

    def rows(v):
        return v.reshape(depth, 1, -1)

    P = {
        "g_mix": rows(g_mix), "w_in_t": jnp.swapaxes(w_in, 1, 2),
        "w_forget_up": w_forget_up, "b_forget": rows(b_forget), "g_gla_out": rows(g_gla_out),
        "w_conv": w_conv, "b_conv": rows(b_conv), "g_conv_ln": rows(g_conv_ln), "b_conv_ln": rows(b_conv_ln),
        "w_out": w_out, "g_ffn": rows(g_ffn), "g_ple": rows(g_ple),
        "w_ple_gate": w_ple_gate, "w_ple_proj": w_ple_proj,
        "wg": w_exp_gate.reshape(depth * N_EXPERTS, D_MODEL, EXPERT_FF),
        "wu": w_exp_up.reshape(depth * N_EXPERTS, D_MODEL, EXPERT_FF),
        "wd": w_exp_down.reshape(depth * N_EXPERTS, EXPERT_FF, D_MODEL),
    }
    g_fin = g_final.reshape(1, -1)
    xp = x_prompt.reshape(n_p, D_MODEL)
    xs = x_sample.reshape(n_s, D_MODEL)
    pp = p_prompt.reshape(depth, n_p, PLE_DIM)
    ps = p_sample.reshape(depth, n_s, PLE_DIM)
    s_in = state_gla.reshape(depth, nbs, QK_COLS, GLA_DV)
    c_in_t = jnp.swapaxes(state_conv, 1, 2)

    h = None
    sg_p, sg_s, sc_p, sc_s = [], [], [], []
    for l in range(depth):
        if l == 0:
            hp, sgp, scp = _mixer_prompt(xp, 0, nbp, seq_p, l, P)
            hs, sgs, scs = _mixer_sample(xs, 0, nbs, seq_s, l, s_in, c_in_t, P)
        else:
            hp, sgp, scp = _mixer_prompt(h, 0, nbp, seq_p, l, P)
            hs, sgs, scs = _mixer_sample(h, n_p, nbs, seq_s, l, s_in, c_in_t, P)
        w_rt, b_rt = _router_weights(l, w_grp_router, b_grp_router, w_exp_router, b_exp_router)
        h = _ffn(hp, hs, pp, ps, l, P, w_rt, b_rt, g_fin, final=(l == depth - 1))
        sg_p.append(sgp.reshape(nbp, GLA_HEADS, GLA_DK, GLA_DV))
        sg_s.append(sgs.reshape(nbs, GLA_HEADS, GLA_DK, GLA_DV))
        sc_p.append(scp)
        sc_s.append(scs)

    y_prompt = h[0].reshape(nbp, seq_p, D_MODEL)
    y_sample = h[1].reshape(nbs, seq_s, D_MODEL)
    return (y_prompt, y_sample, jnp.stack(sg_p), jnp.stack(sg_s), jnp.stack(sc_p), jnp.swapaxes(jnp.stack(sc_s), 1, 2))
```

```python
import functools

import jax
import jax.numpy as jnp
from jax import lax
from jax.experimental import pallas as pl
from jax.experimental.pallas import tpu as pltpu

D_MODEL = 1024
GLA_HEADS = 4
GLA_DK = 64
GLA_DV = 128
QK_COLS = GLA_HEADS * GLA_DK
V_COLS = GLA_HEADS * GLA_DV
CONV_WIDTH = 512
CONV_K = 31
GLA_LOWRANK = 16
GLA_TAU = 16.0
GLA_CHUNK = 64
PLE_DIM = 256
N_GROUPS = 4
EXPERTS_PER_GROUP = 8
N_EXPERTS = N_GROUPS * EXPERTS_PER_GROUP
EXPERT_FF = 256
TOP_K = 2
EPS = 1e-6
N_MAIN = 2 * QK_COLS + 2 * V_COLS

LANES = 128
SUBLANES = 8
CONV_PAD = 32
CONV_OFF = CONV_PAD - (CONV_K - 1)
VMEM_LIMIT = 56 * 1024 * 1024
TOK_TILE = 512
CHUNK = 16
EXP_TILE = 512
CBUF_ROWS = -(-(TOP_K * TOK_TILE + N_EXPERTS * (CHUNK - 1)) // LANES) * LANES
XS_COLS = D_MODEL + LANES

F32 = jnp.float32
BF16 = jnp.bfloat16
HI = lax.Precision.HIGHEST


def _sigmoid(x):
    return 1.0 / (1.0 + jnp.exp(-x))


def _silu(x):
    return x * _sigmoid(x)


def _log_sigmoid(x):
    return jnp.minimum(x, 0.0) - jnp.log(1.0 + jnp.exp(-jnp.abs(x)))


def _rmsnorm(x, g):
    return x * lax.rsqrt(jnp.mean(x * x, axis=-1, keepdims=True) + EPS) * g


def _dot(a, b):
    return jnp.dot(a.astype(BF16), b.astype(BF16), preferred_element_type=F32)


def _dot_t(a, b):
    return lax.dot_general(a.astype(BF16), b.astype(BF16), (((1,), (1,)), ((), ())),
                           preferred_element_type=F32)


def _dot_hi(a, b):
    return jnp.dot(a, b, preferred_element_type=F32, precision=HI)


def _iota_f32(shape, dim):
    return lax.broadcasted_iota(jnp.int32, shape, dim).astype(F32)


def _const_spec(shape):
    nd = len(shape)
    return pl.BlockSpec(shape, lambda *_: (0,) * nd)


def _layer_spec(arr, l):
    nd = arr.ndim - 1
    return pl.BlockSpec((None,) + arr.shape[1:], lambda *_: (l,) + (0,) * nd, pipeline_mode=pl.Buffered(1))


def _cast_mixer_weights(w_int_ref, w_fu_ref, w_out_ref, wmain_s, wlr_s, wcv_s, wfu_s, wout_s):
    blk = 4 * LANES
    for r in range(0, N_MAIN, blk):
        wmain_s[:, r:r + blk] = w_int_ref[r:r + blk, :].T.astype(BF16)
    lane = lax.broadcasted_iota(jnp.int32, (D_MODEL, LANES), 1)
    wlr_s[...] = jnp.where(lane < GLA_LOWRANK, w_int_ref[N_MAIN:N_MAIN + LANES, :].T, 0.0).astype(BF16)
    cv0 = N_MAIN + GLA_LOWRANK
    for r in range(0, 2 * CONV_WIDTH, blk):
        wcv_s[:, r:r + blk] = w_int_ref[cv0 + r:cv0 + r + blk, :].T.astype(BF16)
    wfu_s[...] = jnp.zeros_like(wfu_s)
    wfu_s[0:GLA_LOWRANK, :] = w_fu_ref[...].astype(BF16)
    wout_s[...] = w_out_ref[...].astype(BF16)


def _project(x, g_mix, wmain_s, wlr_s, wfu_s, b_f, wcv_s):
    xn = _rmsnorm(x, g_mix).astype(BF16)
    z = jnp.dot(xn, wmain_s[...], preferred_element_type=F32)
    q = z[:, :QK_COLS] * (GLA_DK ** -0.5)
    k = z[:, QK_COLS:2 * QK_COLS]
    v = z[:, 2 * QK_COLS:2 * QK_COLS + V_COLS]
    og = z[:, 2 * QK_COLS + V_COLS:]
    lr = jnp.dot(xn, wlr_s[...], preferred_element_type=F32)
    zf = _dot(lr, wfu_s[...]) + b_f
    la = _log_sigmoid(zf) * (1.0 / GLA_TAU)
    cv = jnp.dot(xn, wcv_s[...], preferred_element_type=F32)
    u = cv[:, :CONV_WIDTH] * _sigmoid(cv[:, CONV_WIDTH:])
    return q, k, v, og, la, u


def _stack_heads(qd):
    lane = lax.broadcasted_iota(jnp.int32, qd.shape, 1)
    return jnp.concatenate(
        [jnp.where((lane >= h * GLA_DK) & (lane < (h + 1) * GLA_DK), qd, 0.0) for h in range(GLA_HEADS)],
        axis=0)


def _gated_head_norm(o, og, g_gla):
    outs = []
    for h in range(GLA_HEADS):
        sl = slice(h * GLA_DV, (h + 1) * GLA_DV)
        outs.append(_rmsnorm(o[:, sl], g_gla) * _silu(og[:, sl]))
    return jnp.concatenate(outs, axis=1)


def _causal_conv(win, w_conv, b_conv, n):
    acc = jnp.broadcast_to(b_conv, (n, CONV_WIDTH))
    for s in range(SUBLANES):
        taps = [j for j in range(CONV_K) if (CONV_OFF + j) % SUBLANES == s]
        if not taps:
            continue
        rows = n if s == 0 else n + SUBLANES
        part = None
        for j in taps:
            a = (CONV_OFF + j) - s
            term = w_conv[j:j + 1, :] * win[a:a + rows, :]
            part = term if part is None else part + term
        acc = acc + part[s:s + n, :]
    return acc


def _conv_ln_act(acc, g_ln, b_ln):
    mu = jnp.mean(acc, axis=-1, keepdims=True)
    xc = acc - mu
    y = xc * lax.rsqrt(jnp.mean(xc * xc, axis=-1, keepdims=True) + EPS) * g_ln + b_ln
    return _silu(y)


def _head_diag(upd):
    return jnp.concatenate([upd[h * GLA_DK:(h + 1) * GLA_DK, h * GLA_DV:(h + 1) * GLA_DV]
                            for h in range(GLA_HEADS)], axis=0)


def _mixer_weight_args(l, P):
    names = ("g_mix", "w_in_t", "w_forget_up", "b_forget", "g_gla_out", "w_conv", "b_conv", "g_conv_ln",
             "b_conv_ln", "w_out")
    arrs = [P[n] for n in names]
    return arrs, [_layer_spec(a, l) for a in arrs]


_MIXER_WEIGHT_SCRATCH = [pltpu.VMEM((D_MODEL, N_MAIN), BF16), pltpu.VMEM((D_MODEL, LANES), BF16),
                         pltpu.VMEM((D_MODEL, 2 * CONV_WIDTH), BF16), pltpu.VMEM((LANES, QK_COLS), BF16),
                         pltpu.VMEM((D_MODEL, D_MODEL), BF16)]


def _mixer_prompt_kernel(x_ref, g_mix_ref, w_int_ref, w_fu_ref, b_f_ref, g_gla_ref, w_conv_ref,
                         b_conv_ref, g_ln_ref, b_ln_ref, w_out_ref,
                         h_ref, sg_ref, sc_ref,
                         wmain_s, wlr_s, wcv_s, wfu_s, wout_s,
                         s_ref, ubuf_ref, qs_ref, kd_ref, klt_ref, dec_ref, mid_ref, v_ref, og_ref, mix_ref, *, tt):
    t = pl.program_id(1)
    nt = pl.num_programs(1)
    C = GLA_CHUNK
    n_chunks = tt // C

    @pl.when((pl.program_id(0) == 0) & (t == 0))
    def _():
        _cast_mixer_weights(w_int_ref, w_fu_ref, w_out_ref, wmain_s, wlr_s, wcv_s, wfu_s, wout_s)

    @pl.when(t == 0)
    def _():
        s_ref[...] = jnp.zeros_like(s_ref)
        ubuf_ref[0:CONV_PAD, :] = jnp.zeros((CONV_PAD, CONV_WIDTH), F32)

    x = x_ref[...]
    q, k, v, og, la, u = _project(x, g_mix_ref[...], wmain_s, wlr_s, wfu_s, b_f_ref[...], wcv_s)
    ubuf_ref[CONV_PAD:CONV_PAD + tt, :] = u
    v_ref[...] = v.astype(BF16)
    og_ref[...] = _silu(og)

    w_conv = w_conv_ref[...]
    b_conv = b_conv_ref[...]
    g_ln = g_ln_ref[...]
    b_ln = b_ln_ref[...]
    for c in range(n_chunks):
        win = ubuf_ref[c * C:c * C + C + CONV_PAD, :]
        mix_ref[c * C:(c + 1) * C, V_COLS:] = _conv_ln_act(_causal_conv(win, w_conv, b_conv, C), g_ln, b_ln).astype(BF16)

    row = lax.broadcasted_iota(jnp.int32, (C, C), 0)
    col = lax.broadcasted_iota(jnp.int32, (C, C), 1)
    tri = (col <= row).astype(F32)
    for c in range(n_chunks):
        rows = slice(c * C, (c + 1) * C)
        b = _dot_hi(tri, la[rows, :])
        b_last = b[C - 1:C, :]
        b_mid = b[C // 2 - 1:C // 2, :]
        qs_ref[c] = _stack_heads(q[rows, :] * jnp.exp(b - b_mid)).astype(BF16)
        kd_ref[rows, :] = (k[rows, :] * jnp.exp(b_mid - b)).astype(BF16)
        kl = k[rows, :] * jnp.exp(b_last - b)
        klt = jnp.concatenate([kl, jnp.broadcast_to(jnp.exp(b_last), (C // 2, QK_COLS)),
                               jnp.broadcast_to(jnp.exp(b_mid), (C // 2, QK_COLS))], axis=0).T
        klt_ref[c] = klt.astype(BF16)
        dec_ref[c] = jnp.broadcast_to(klt[:, C:C + 1], (QK_COLS, GLA_DV))
        mid_ref[c] = jnp.broadcast_to(klt[:, 3 * C // 2:3 * C // 2 + 1], (QK_COLS, GLA_DV))

    r4 = lax.broadcasted_iota(jnp.int32, (GLA_HEADS * C, C), 0)
    c4 = lax.broadcasted_iota(jnp.int32, (GLA_HEADS * C, C), 1)
    causal4 = c4 <= (r4 % C)
    g_gla = g_gla_ref[...]
    s = s_ref[...]
    for c in range(n_chunks):
        rows = slice(c * C, (c + 1) * C)
        qs = qs_ref[c]
        vc = v_ref[rows, :]
        scores = jnp.where(causal4, _dot_t(qs, kd_ref[rows, :]), 0.0).astype(BF16)
        o_inter = jnp.dot(qs, (mid_ref[c] * s).astype(BF16), preferred_element_type=F32)
        upd = jnp.dot(klt_ref[c][:, :C], vc, preferred_element_type=F32)
        s = dec_ref[c] * s + _head_diag(upd)
        o_parts = []
        for h in range(GLA_HEADS):
            vh = vc[:, h * GLA_DV:(h + 1) * GLA_DV]
            o_parts.append(jnp.dot(scores[h * C:(h + 1) * C, :], vh, preferred_element_type=F32)
                           + o_inter[h * C:(h + 1) * C, :])
        o = jnp.concatenate(o_parts, axis=1)
        gated = []
        for h in range(GLA_HEADS):
            sl = slice(h * GLA_DV, (h + 1) * GLA_DV)
            gated.append(_rmsnorm(o[:, sl], g_gla) * og_ref[rows, sl])
        mix_ref[rows, 0:V_COLS] = jnp.concatenate(gated, axis=1).astype(BF16)
    s_ref[...] = s

    h_ref[...] = x + jnp.dot(mix_ref[...], wout_s[...], preferred_element_type=F32)
    tail = ubuf_ref[tt:tt + CONV_PAD, :]
    ubuf_ref[0:CONV_PAD, :] = tail

    @pl.when(t == nt - 1)
    def _():
        sg_ref[0] = s
        sc_ref[0] = tail[CONV_OFF:, :]


def _mixer_prompt(x, row_off, nb, seq, l, P, *, tt=512):
    nt = seq // tt
    n_chunks = tt // GLA_CHUNK
    blk_off = row_off // tt
    weights, w_specs = _mixer_weight_args(l, P)
    in_specs = [pl.BlockSpec((tt, D_MODEL), lambda b, t: (blk_off + b * nt + t, 0))] + w_specs
    out_shape = (jax.ShapeDtypeStruct((nb * seq, D_MODEL), F32),
                 jax.ShapeDtypeStruct((nb, QK_COLS, GLA_DV), F32),
                 jax.ShapeDtypeStruct((nb, CONV_K - 1, CONV_WIDTH), F32))
    out_specs = (pl.BlockSpec((tt, D_MODEL), lambda b, t: (b * nt + t, 0)),
                 pl.BlockSpec((1, QK_COLS, GLA_DV), lambda b, t: (b, 0, 0)),
                 pl.BlockSpec((1, CONV_K - 1, CONV_WIDTH), lambda b, t: (b, 0, 0)))
    scratch = _MIXER_WEIGHT_SCRATCH + [
        pltpu.VMEM((QK_COLS, GLA_DV), F32),
        pltpu.VMEM((CONV_PAD + tt, CONV_WIDTH), F32),
        pltpu.VMEM((n_chunks, GLA_HEADS * GLA_CHUNK, QK_COLS), BF16), pltpu.VMEM((tt, QK_COLS), BF16),
        pltpu.VMEM((n_chunks, QK_COLS, 2 * GLA_CHUNK), BF16), pltpu.VMEM((n_chunks, QK_COLS, GLA_DV), F32),
        pltpu.VMEM((n_chunks, QK_COLS, GLA_DV), F32),
        pltpu.VMEM((tt, V_COLS), BF16), pltpu.VMEM((tt, V_COLS), F32),
        pltpu.VMEM((tt, D_MODEL), BF16)]
    return pl.pallas_call(
        functools.partial(_mixer_prompt_kernel, tt=tt),
        grid=(nb, nt), in_specs=in_specs, out_specs=out_specs, out_shape=out_shape,
        scratch_shapes=scratch,
        compiler_params=pltpu.CompilerParams(dimension_semantics=("arbitrary", "arbitrary"),
                                             vmem_limit_bytes=VMEM_LIMIT),
        name="mixer_prompt",
    )(x, *weights)


def _mixer_sample_kernel(x_ref, s_in_ref, c_in_ref, g_mix_ref, w_int_ref, w_fu_ref, b_f_ref,
                         g_gla_ref, w_conv_ref, b_conv_ref, g_ln_ref, b_ln_ref, w_out_ref,
                         h_ref, sg_ref, sc_ref,
                         wmain_s, wlr_s, wcv_s, wfu_s, wout_s,
                         u4_ref, oi_ref, cacc4_ref, *, sb, seq):
    R = sb * seq
    n_slabs = CONV_WIDTH // LANES

    @pl.when(pl.program_id(0) == 0)
    def _():
        _cast_mixer_weights(w_int_ref, w_fu_ref, w_out_ref, wmain_s, wlr_s, wcv_s, wfu_s, wout_s)

    x = x_ref[...]
    q, k, v, og, la, u = _project(x, g_mix_ref[...], wmain_s, wlr_s, wfu_s, b_f_ref[...], wcv_s)

    for kk in range(n_slabs):
        u4_ref[kk] = u[:, kk * LANES:(kk + 1) * LANES]
    full = [c_in_ref[j] for j in range(CONV_K - 1)]
    for t in range(seq):
        full.append(jnp.concatenate([u4_ref.at[kk][pl.ds(t, sb, stride=seq), :] for kk in range(n_slabs)], axis=1))
    w_conv = w_conv_ref[...]
    for t in range(seq):
        acc = jnp.broadcast_to(b_conv_ref[...], (sb, CONV_WIDTH))
        for j in range(CONV_K):
            acc = acc + w_conv[j:j + 1, :] * full[t + j]
        for kk in range(n_slabs):
            cacc4_ref.at[kk][pl.ds(t, sb, stride=seq), :] = acc[:, kk * LANES:(kk + 1) * LANES]
    for j in range(CONV_K - 1):
        sc_ref[j] = full[seq + j]

    row = lax.broadcasted_iota(jnp.int32, (R, R), 0)
    col = lax.broadcasted_iota(jnp.int32, (R, R), 1)
    same = (row // seq) == (col // seq)
    b = _dot_hi((same & (col <= row)).astype(F32), la)
    b_tot = _dot_hi(same.astype(F32), la)
    qd = q * jnp.exp(b)
    kd = k * jnp.exp(-b)
    kl = k * jnp.exp(b_tot - b)
    qs = _stack_heads(qd)
    r4 = lax.broadcasted_iota(jnp.int32, (GLA_HEADS * R, R), 0) % R
    c4 = lax.broadcasted_iota(jnp.int32, (GLA_HEADS * R, R), 1)
    mask4 = ((r4 // seq) == (c4 // seq)) & (c4 <= r4)
    scores = jnp.where(mask4, _dot_t(qs, kd), 0.0)

    klt = kl.T
    dect = jnp.exp(b_tot).T
    lane_h = lax.broadcasted_iota(jnp.int32, (GLA_DK, R), 1)
    lane_r = lax.broadcasted_iota(jnp.int32, (QK_COLS, R), 1)
    upd = []
    for h in range(GLA_HEADS):
        klt_h = klt[h * GLA_DK:(h + 1) * GLA_DK, :]
        lhs = jnp.concatenate([jnp.where((lane_h >= i * seq) & (lane_h < (i + 1) * seq), klt_h, 0.0)
                               for i in range(sb)], axis=0)
        upd.append(_dot(lhs, v[:, h * GLA_DV:(h + 1) * GLA_DV]))
    for i in range(sb):
        s_old = s_in_ref[i]
        qsel = jnp.concatenate([qs[h * R + i * seq:h * R + (i + 1) * seq, :] for h in range(GLA_HEADS)], axis=0)
        oi = _dot(qsel, s_old)
        for h in range(GLA_HEADS):
            oi_ref[h, i * seq:(i + 1) * seq, :] = oi[h * seq:(h + 1) * seq, :]
        smask = (lane_r >= i * seq) & (lane_r < (i + 1) * seq)
        dec = jnp.sum(jnp.where(smask, dect, 0.0), axis=1, keepdims=True) * (1.0 / seq)
        u_new = jnp.concatenate([upd[h][i * GLA_DK:(i + 1) * GLA_DK, :] for h in range(GLA_HEADS)], axis=0)
        sg_ref[i] = dec * s_old + u_new

    o_parts = []
    for h in range(GLA_HEADS):
        vh = v[:, h * GLA_DV:(h + 1) * GLA_DV]
        o_parts.append(_dot(scores[h * R:(h + 1) * R, :], vh) + oi_ref[h])
    o = jnp.concatenate(o_parts, axis=1)
    cacc = jnp.concatenate([cacc4_ref[kk] for kk in range(n_slabs)], axis=1)
    mix = jnp.concatenate([_gated_head_norm(o, og, g_gla_ref[...]),
                           _conv_ln_act(cacc, g_ln_ref[...], b_ln_ref[...])], axis=1)
    h_ref[...] = x + jnp.dot(mix.astype(BF16), wout_s[...], preferred_element_type=F32)


def _mixer_sample(x, row_off, nb, seq, l, s_in, c_in_t, P, *, sb=16):
    R = sb * seq
    blk_off = row_off // R
    n_slabs = CONV_WIDTH // LANES
    weights, w_specs = _mixer_weight_args(l, P)
    in_specs = [pl.BlockSpec((R, D_MODEL), lambda i: (blk_off + i, 0)),
                pl.BlockSpec((None, sb, QK_COLS, GLA_DV), lambda i: (l, i, 0, 0)),
                pl.BlockSpec((None, CONV_K - 1, sb, CONV_WIDTH), lambda i: (l, 0, i, 0))] + w_specs
    out_shape = (jax.ShapeDtypeStruct((nb * seq, D_MODEL), F32),
                 jax.ShapeDtypeStruct((nb, QK_COLS, GLA_DV), F32),
                 jax.ShapeDtypeStruct((CONV_K - 1, nb, CONV_WIDTH), F32))
    out_specs = (pl.BlockSpec((R, D_MODEL), lambda i: (i, 0)),
                 pl.BlockSpec((sb, QK_COLS, GLA_DV), lambda i: (i, 0, 0)),
                 pl.BlockSpec((CONV_K - 1, sb, CONV_WIDTH), lambda i: (0, i, 0)))
    scratch = _MIXER_WEIGHT_SCRATCH + [
        pltpu.VMEM((n_slabs, R, LANES), F32),
        pltpu.VMEM((GLA_HEADS, R, GLA_DV), F32),
        pltpu.VMEM((n_slabs, R, LANES), F32)]
    return pl.pallas_call(
        functools.partial(_mixer_sample_kernel, sb=sb, seq=seq),
        grid=(nb // sb,), in_specs=in_specs, out_specs=out_specs, out_shape=out_shape,
        scratch_shapes=scratch,
        compiler_params=pltpu.CompilerParams(dimension_semantics=("arbitrary",),
                                             vmem_limit_bytes=VMEM_LIMIT),
        name="mixer_sample",
    )(x, s_in, c_in_t, *weights)


def _pair_specs(n_first_tiles, width):
    return [pl.BlockSpec((TOK_TILE, width), lambda t, *_: (jnp.minimum(t, n_first_tiles - 1), 0)),
            pl.BlockSpec((TOK_TILE, width), lambda t, *_: (jnp.maximum(t - n_first_tiles, 0), 0))]


def _pick(t, n_first_tiles, a_ref, b_ref):
    return jnp.where(t < n_first_tiles, a_ref[...], b_ref[...])


def _route(logits):
    lane = lax.broadcasted_iota(jnp.int32, logits.shape, 1)
    lane_f = lane.astype(F32)
    neg = jnp.float32(-jnp.inf)
    big = jnp.float32(1e9)
    is_grp = (lane >= N_EXPERTS) & (lane < N_EXPERTS + N_GROUPS)
    gl = jnp.where(is_grp, logits, neg)
    gmax = jnp.max(gl, axis=-1, keepdims=True)
    gidx = jnp.min(jnp.where(is_grp & (gl == gmax), lane_f - N_EXPERTS, big), axis=-1, keepdims=True)
    gsum = jnp.sum(jnp.where(is_grp, jnp.exp(gl - gmax), 0.0), axis=-1, keepdims=True)
    g_w = 1.0 / gsum
    grp_of_lane = jnp.floor(lane_f * (1.0 / EXPERTS_PER_GROUP))
    in_grp = (lane < N_EXPERTS) & (grp_of_lane == gidx)
    ml = jnp.where(in_grp, logits, neg)
    v1 = jnp.max(ml, axis=-1, keepdims=True)
    i1 = jnp.min(jnp.where(in_grp & (ml == v1), lane_f, big), axis=-1, keepdims=True)
    ml2 = jnp.where(lane_f == i1, neg, ml)
    v2 = jnp.max(ml2, axis=-1, keepdims=True)
    i2 = jnp.min(jnp.where(in_grp & (ml2 == v2), lane_f, big), axis=-1, keepdims=True)
    e2 = jnp.exp(v2 - v1)
    w1 = g_w / (1.0 + e2)
    w2 = g_w * e2 / (1.0 + e2)
    return i1, i2, w1, w2


def _route_kernel(hp_ref, hs_ref, g_ffn_ref, w_rt_ref, b_rt_ref, pos_ref, cnt_ref, *, npt):
    T = TOK_TILE
    xn = _rmsnorm(_pick(pl.program_id(0), npt, hp_ref, hs_ref), g_ffn_ref[...])
    x_hi = xn.astype(BF16)
    x_lo = (xn - x_hi.astype(F32)).astype(BF16)
    w_split = w_rt_ref[...]
    both = jnp.dot(x_hi, w_split, preferred_element_type=F32)
    logits = (both[:, :LANES] + both[:, LANES:] + jnp.dot(x_lo, w_split[:, :LANES], preferred_element_type=F32)
              + b_rt_ref[...])
    i1, i2, w1, w2 = _route(logits)
    lane = _iota_f32((T, LANES), 1)
    a0 = (lane == i1).astype(F32)
    a1 = (lane == i2).astype(F32)
    a = a0 + a1
    cnt = jnp.sum(a, axis=0, keepdims=True)
    earlier = (_iota_f32((T, T), 1) < _iota_f32((T, T), 0)).astype(BF16)
    rank = jnp.dot(earlier, a.astype(BF16), preferred_element_type=F32)
    cnt_pad = jnp.ceil(cnt * (1.0 / CHUNK)) * CHUNK
    below = (_iota_f32((LANES, LANES), 0) < _iota_f32((LANES, LANES), 1)).astype(F32)
    first = _dot_hi(jnp.broadcast_to(cnt_pad, (SUBLANES, LANES)), below)[0:1, :]
    base = first + rank
    pos0 = jnp.sum(a0 * base, axis=1, keepdims=True)
    pos1 = jnp.sum(a1 * base, axis=1, keepdims=True)
    pos_ref[...] = jnp.where(lane == 0.0, pos0, jnp.where(lane == 1.0, pos1, jnp.where(
        lane == 2.0, w1, jnp.where(lane == 3.0, w2, jnp.where(lane == 4.0, i1, 0.0)))))
    cnt_ref[0] = jnp.broadcast_to(cnt, (SUBLANES, LANES))


def _route_call(hp, hs, l, P, w_rt, b_rt):
    npt = hp.shape[0] // TOK_TILE
    nt = npt + hs.shape[0] // TOK_TILE
    return pl.pallas_call(
        functools.partial(_route_kernel, npt=npt), grid=(nt,),
        in_specs=_pair_specs(npt, D_MODEL) + [_layer_spec(P["g_ffn"], l), _const_spec(w_rt.shape),
                                              _const_spec(b_rt.shape)],
        out_specs=(pl.BlockSpec((TOK_TILE, LANES), lambda t: (t, 0)),
                   pl.BlockSpec((1, SUBLANES, LANES), lambda t: (t, 0, 0))),
        out_shape=(jax.ShapeDtypeStruct((nt * TOK_TILE, LANES), F32),
                   jax.ShapeDtypeStruct((nt, SUBLANES, LANES), F32)),
        compiler_params=pltpu.CompilerParams(dimension_semantics=("arbitrary",), vmem_limit_bytes=VMEM_LIMIT),
        name="moe_route",
    )(hp, hs, P["g_ffn"], w_rt, b_rt)


def _chunk_plan(cnt, n_row_tiles):
    n16 = (cnt + (CHUNK - 1)) // CHUNK
    lofs16 = jnp.cumsum(n16, axis=1) - n16
    tile_pref16 = jnp.cumsum(n16, axis=0) - n16
    tot16 = jnp.sum(n16, axis=0)
    per_tile = EXP_TILE // CHUNK
    seg16 = ((tot16 + per_tile - 1) // per_tile) * per_tile
    seg_end16 = jnp.cumsum(seg16)
    dst16 = (seg_end16 - seg16)[None, :] + tile_pref16
    n_tot = jnp.sum(n16, axis=1)
    gap16 = seg_end16 - seg16 + tot16
    gapn16 = seg16 - tot16
    tile_start16 = jnp.arange(n_row_tiles, dtype=jnp.int32) * per_tile
    n_valid = seg_end16[-1] // per_tile
    misc = n_valid.reshape(1)
    exp_of_tile = jnp.minimum(jnp.sum(seg_end16[None, :] <= tile_start16[:, None], axis=1), N_EXPERTS - 1)
    i32 = lambda a: a.astype(jnp.int32).reshape(-1)
    return (i32(dst16), i32(n16), i32(lofs16), i32(n_tot), i32(gap16), i32(gapn16), i32(misc), i32(exp_of_tile))


def _chunk_copy(src, dst, src_chunk, dst_chunk, sem, n_chunks=1):
    rows = n_chunks * CHUNK
    return pltpu.make_async_copy(src.at[pl.ds(pl.multiple_of(src_chunk * CHUNK, CHUNK), rows), :],
                                 dst.at[pl.ds(pl.multiple_of(dst_chunk * CHUNK, CHUNK), rows), :], sem)


def _slab_copies(src, dst, sem, tile, src_ofs_ref, dst_ofs_ref, n16_ref):
    for e in range(N_EXPERTS):
        k = tile * N_EXPERTS + e
        n = n16_ref[k]

        @pl.when(n > 0)
        def _(k=k, n=n):
            _chunk_copy(src, dst, src_ofs_ref[k], dst_ofs_ref[k], sem, n).start()


def _wait_slabs(src, dst, sem, n_chunks):
    @pl.when(n_chunks > 0)
    def _():
        _chunk_copy(src, dst, 0, 0, sem, n_chunks).wait()


def _tile_copy(src, dst, dst_tile, sem):
    return pltpu.make_async_copy(src, dst.at[pl.ds(pl.multiple_of(dst_tile * EXP_TILE, EXP_TILE), EXP_TILE), :], sem)


def _dispatch_kernel(dst16_ref, n16_ref, lofs16_ref, ntot_ref, gap16_ref, gapn16_ref, misc_ref,
                     hp_ref, hs_ref, pos_ref, g_ffn_ref, xs_hbm, cbuf, zbuf, sem, *, n_row_tiles, npt):
    t = pl.program_id(0)
    nt = pl.num_programs(0)
    slot = t % 2
    T = TOK_TILE
    n_tail = n_row_tiles - misc_ref[0]

    @pl.when(t == 0)
    def _():
        zbuf[...] = jnp.zeros_like(zbuf)
        for e in range(N_EXPERTS):
            g = gapn16_ref[e]

            @pl.when(g > 0)
            def _(e=e, g=g):
                _chunk_copy(zbuf, xs_hbm, 0, gap16_ref[e], sem.at[2], g).start()

        def fill_tile(i, carry):
            _tile_copy(zbuf, xs_hbm, misc_ref[0] + i, sem.at[2]).start()
            return carry
        lax.fori_loop(0, n_tail, fill_tile, 0)

    xn = _rmsnorm(_pick(t, npt, hp_ref, hs_ref), g_ffn_ref[...]).astype(BF16)
    pos = pos_ref[...]
    pos_t = pos.T
    rows = _iota_f32((CBUF_ROWS, T), 0)
    onehot = jnp.where((rows == pos_t[0:1, :]) | (rows == pos_t[1:2, :]), 1.0, 0.0).astype(BF16)
    lane = lax.broadcasted_iota(jnp.int32, (T, LANES), 1)
    extra = jnp.zeros((T, LANES), F32)
    for s in range(TOP_K):
        c = pos[:, 2 + s:3 + s]
        hi = c.astype(BF16).astype(F32)
        mid = (c - hi).astype(BF16).astype(F32)
        lo = c - hi - mid
        for j, piece in enumerate((hi, mid, lo)):
            extra = jnp.where(lane == 3 * s + j, piece, extra)
    extra = jnp.where(lane == 3 * TOP_K, pos[:, 4:5], extra)
    cbuf[slot] = jnp.dot(onehot, jnp.concatenate([xn, extra.astype(BF16)], axis=1),
                         preferred_element_type=F32).astype(BF16)

    src = cbuf.at[slot]
    _slab_copies(src, xs_hbm, sem.at[slot], t, lofs16_ref, dst16_ref, n16_ref)

    @pl.when(t > 0)
    def _():
        _wait_slabs(cbuf.at[1 - slot], xs_hbm, sem.at[1 - slot], ntot_ref[t - 1])

    @pl.when(t == nt - 1)
    def _():
        _wait_slabs(src, xs_hbm, sem.at[slot], ntot_ref[t])
        for e in range(N_EXPERTS):
            _wait_slabs(zbuf, xs_hbm, sem.at[2], gapn16_ref[e])

        def wait_tile(_, carry):
            _tile_copy(zbuf, xs_hbm, 0, sem.at[2]).wait()
            return carry
        lax.fori_loop(0, n_tail, wait_tile, 0)


def _dispatch_call(plan, hp, hs, pos, l, P, n_sorted):
    npt = hp.shape[0] // TOK_TILE
    nt = npt + hs.shape[0] // TOK_TILE
    g_ffn = P["g_ffn"]
    grid_spec = pltpu.PrefetchScalarGridSpec(
        num_scalar_prefetch=7, grid=(nt,),
        in_specs=_pair_specs(npt, D_MODEL) + [
            pl.BlockSpec((TOK_TILE, LANES), lambda t, *_: (t, 0)),
            pl.BlockSpec((None,) + g_ffn.shape[1:], lambda t, *_: (l, 0, 0))],
        out_specs=pl.BlockSpec(memory_space=pl.ANY),
        scratch_shapes=[pltpu.VMEM((2, CBUF_ROWS, XS_COLS), BF16), pltpu.VMEM((EXP_TILE, XS_COLS), BF16),
                        pltpu.SemaphoreType.DMA((3,))])
    return pl.pallas_call(
        functools.partial(_dispatch_kernel, n_row_tiles=n_sorted // EXP_TILE, npt=npt), grid_spec=grid_spec,
        out_shape=jax.ShapeDtypeStruct((n_sorted, XS_COLS), BF16),
        compiler_params=pltpu.CompilerParams(dimension_semantics=("arbitrary",), vmem_limit_bytes=VMEM_LIMIT),
        name="moe_dispatch",
    )(*plan[:7], hp, hs, pos, g_ffn)


def _expert_kernel(eot_ref, misc_ref, xs_ref, wg_ref, wu_ref, wd_ref, y_ref, wg_s, wu_s, wd_s):
    i = pl.program_id(0)
    valid = i < misc_ref[0]

    @pl.when(jnp.logical_not(valid))
    def _():
        y_ref[...] = jnp.zeros_like(y_ref)

    @pl.when(valid & ((i == 0) | (eot_ref[i] != eot_ref[jnp.maximum(i - 1, 0)])))
    def _():
        wg_s[...] = wg_ref[...].astype(BF16)
        wu_s[...] = wu_ref[...].astype(BF16)
        wd_s[...] = wd_ref[...].astype(BF16)

    @pl.when(valid)
    def _():
        xs = xs_ref[...]
        x = xs[:, :D_MODEL]
        ex = xs[:, D_MODEL:].astype(F32)
        lane = lax.broadcasted_iota(jnp.int32, ex.shape, 1)
        id0 = jnp.sum(jnp.where(lane == 3 * TOP_K, ex, 0.0), axis=-1, keepdims=True)
        first = id0 == eot_ref[i].astype(F32)
        mine = (first & (lane < 3)) | (jnp.logical_not(first) & (lane >= 3) & (lane < 3 * TOP_K))
        c = jnp.sum(jnp.where(mine, ex, 0.0), axis=-1, keepdims=True)
        hg = _silu(jnp.dot(x, wg_s[...], preferred_element_type=F32)) * jnp.dot(x, wu_s[...],
                                                                               preferred_element_type=F32)
        y_ref[...] = jnp.dot((hg * c).astype(BF16), wd_s[...], preferred_element_type=F32).astype(BF16)


def _expert_call(plan, xs, l, wg, wu, wd):
    misc, exp_of_tile = plan[6:]
    n_row_tiles = xs.shape[0] // EXP_TILE

    def last_valid(i, nv):
        return jnp.maximum(jnp.minimum(i, nv[0] - 1), 0)

    def row_map(i, eot, nv):
        return (last_valid(i, nv), 0)

    def w_map(i, eot, nv):
        return (l * N_EXPERTS + eot[last_valid(i, nv)], 0, 0)

    grid_spec = pltpu.PrefetchScalarGridSpec(
        num_scalar_prefetch=2, grid=(n_row_tiles,),
        in_specs=[pl.BlockSpec((EXP_TILE, XS_COLS), row_map),
                  pl.BlockSpec((None, D_MODEL, EXPERT_FF), w_map),
                  pl.BlockSpec((None, D_MODEL, EXPERT_FF), w_map),
                  pl.BlockSpec((None, EXPERT_FF, D_MODEL), w_map)],
        out_specs=pl.BlockSpec((EXP_TILE, D_MODEL), lambda i, eot, nv: (i, 0)),
        scratch_shapes=[pltpu.VMEM((D_MODEL, EXPERT_FF), BF16), pltpu.VMEM((D_MODEL, EXPERT_FF), BF16),
                        pltpu.VMEM((EXPERT_FF, D_MODEL), BF16)])
    return pl.pallas_call(
        _expert_kernel, grid_spec=grid_spec,
        out_shape=jax.ShapeDtypeStruct((xs.shape[0], D_MODEL), BF16),
        compiler_params=pltpu.CompilerParams(dimension_semantics=("arbitrary",), vmem_limit_bytes=VMEM_LIMIT),
        name="moe_experts",
    )(exp_of_tile, misc, xs, wg, wu, wd)


def _combine_kernel(dst16_ref, n16_ref, lofs16_ref, ntot_ref, hp_ref, hs_ref, pp_ref, ps_ref, pos_ref, y_hbm,
                    g_ple_ref, w_pg_ref, w_pp_ref, g_fin_ref, *rest, final, npt):
    if final:
        op_ref, os_ref, ybuf, wpg_s, wpp_s, sem = rest
    else:
        o_ref, ybuf, wpg_s, wpp_s, sem = rest
    t = pl.program_id(0)
    nt = pl.num_programs(0)
    slot = t % 2

    def fetch(tile, sl):
        _slab_copies(y_hbm, ybuf.at[sl], sem.at[sl], tile, dst16_ref, lofs16_ref, n16_ref)

    @pl.when(t == 0)
    def _():
        ybuf[...] = jnp.zeros_like(ybuf)
        fetch(t, slot)
        wpg_s[...] = w_pg_ref[...].astype(BF16)
        wpp_s[...] = w_pp_ref[...].astype(BF16)

    @pl.when(t + 1 < nt)
    def _():
        fetch(t + 1, 1 - slot)

    _wait_slabs(y_hbm, ybuf.at[slot], sem.at[slot], ntot_ref[t])

    pos = pos_ref[...]
    cols = _iota_f32((TOK_TILE, CBUF_ROWS), 1)
    pick = jnp.where((cols == pos[:, 0:1]) | (cols == pos[:, 1:2]), 1.0, 0.0).astype(BF16)
    h2 = _pick(t, npt, hp_ref, hs_ref) + jnp.dot(pick, ybuf[slot], preferred_element_type=F32)
    xn2 = _rmsnorm(h2, g_ple_ref[...])
    gate = _sigmoid(_dot(xn2, wpg_s[...]))
    p = jnp.where(t < npt, pp_ref[...], ps_ref[...])
    h3 = h2 + gate * _dot(p, wpp_s[...])
    if final:
        h3 = _rmsnorm(h3, g_fin_ref[...])

        @pl.when(t < npt)
        def _():
            op_ref[...] = h3

        @pl.when(t >= npt)
        def _():
            os_ref[...] = h3
    else:
        o_ref[...] = h3


def _combine_call(plan, hp, hs, pp, ps, pos, y, l, P, g_final, *, final):
    npt = hp.shape[0] // TOK_TILE
    nst = hs.shape[0] // TOK_TILE
    nt = npt + nst

    def lmap(t, *_):
        return (l, 0, 0)

    in_specs = _pair_specs(npt, D_MODEL) + [
        pl.BlockSpec((None, TOK_TILE, PLE_DIM), lambda t, *_: (l, jnp.minimum(t, npt - 1), 0)),
        pl.BlockSpec((None, TOK_TILE, PLE_DIM), lambda t, *_: (l, jnp.maximum(t - npt, 0), 0)),
        pl.BlockSpec((TOK_TILE, LANES), lambda t, *_: (t, 0)),
        pl.BlockSpec(memory_space=pl.ANY),
        pl.BlockSpec((None,) + P["g_ple"].shape[1:], lmap),
        pl.BlockSpec((None,) + P["w_ple_gate"].shape[1:], lmap),
        pl.BlockSpec((None,) + P["w_ple_proj"].shape[1:], lmap),
        pl.BlockSpec(g_final.shape, lambda t, *_: (0, 0))]
    if final:
        out_specs = (pl.BlockSpec((TOK_TILE, D_MODEL), lambda t, *_: (jnp.minimum(t, npt - 1), 0)),
                     pl.BlockSpec((TOK_TILE, D_MODEL), lambda t, *_: (jnp.maximum(t - npt, 0), 0)))
        out_shape = (jax.ShapeDtypeStruct(hp.shape, F32), jax.ShapeDtypeStruct(hs.shape, F32))
    else:
        out_specs = pl.BlockSpec((TOK_TILE, D_MODEL), lambda t, *_: (t, 0))
        out_shape = jax.ShapeDtypeStruct((nt * TOK_TILE, D_MODEL), F32)
    grid_spec = pltpu.PrefetchScalarGridSpec(
        num_scalar_prefetch=4, grid=(nt,), in_specs=in_specs, out_specs=out_specs,
        scratch_shapes=[pltpu.VMEM((2, CBUF_ROWS, D_MODEL), BF16), pltpu.VMEM((D_MODEL, D_MODEL), BF16),
                        pltpu.VMEM((PLE_DIM, D_MODEL), BF16), pltpu.SemaphoreType.DMA((2,))])
    return pl.pallas_call(
        functools.partial(_combine_kernel, final=final, npt=npt), grid_spec=grid_spec, out_shape=out_shape,
        compiler_params=pltpu.CompilerParams(dimension_semantics=("arbitrary",), vmem_limit_bytes=VMEM_LIMIT),
        name="moe_combine_final" if final else "moe_combine",
    )(*plan[:4], hp, hs, pp, ps, pos, y, P["g_ple"], P["w_ple_gate"], P["w_ple_proj"], g_final)


def _ffn(hp, hs, pp, ps, l, P, w_rt, b_rt, g_final, *, final):
    n = hp.shape[0] + hs.shape[0]
    nt = n // TOK_TILE
    bound = TOP_K * n + nt * N_EXPERTS * (CHUNK - 1) + N_EXPERTS * (EXP_TILE - 1)
    n_sorted = -(-bound // EXP_TILE) * EXP_TILE
    pos, cnt = _route_call(hp, hs, l, P, w_rt, b_rt)
    plan = _chunk_plan(cnt[:, 0, :N_EXPERTS].astype(jnp.int32), n_sorted // EXP_TILE)
    xs = _dispatch_call(plan, hp, hs, pos, l, P, n_sorted)
    y = _expert_call(plan, xs, l, P["wg"], P["wu"], P["wd"])
    return _combine_call(plan, hp, hs, pp, ps, pos, y, l, P, g_final, final=final)


def _router_weights(l, w_grp_router, b_grp_router, w_exp_router, b_exp_router):
    w_er = jnp.transpose(w_exp_router[l], (1, 0, 2)).reshape(D_MODEL, N_EXPERTS)
    w_rt = jnp.zeros((D_MODEL, LANES), F32).at[:, :N_EXPERTS].set(w_er)
    w_rt = w_rt.at[:, N_EXPERTS:N_EXPERTS + N_GROUPS].set(w_grp_router[l])
    b_rt = jnp.zeros((1, LANES), F32).at[0, :N_EXPERTS].set(b_exp_router[l].reshape(-1))
    b_rt = b_rt.at[0, N_EXPERTS:N_EXPERTS + N_GROUPS].set(b_grp_router[l])
    w_hi = w_rt.astype(BF16)
    w_lo = (w_rt - w_hi.astype(F32)).astype(BF16)
    return jnp.concatenate([w_hi, w_lo], axis=1), b_rt


def kernel(x_prompt, x_sample, state_gla, state_conv, p_prompt, p_sample, g_mix, w_in, w_forget_up, b_forget,
           g_gla_out, w_conv, b_conv, g_conv_ln, b_conv_ln, w_out, g_ffn, w_grp_router, b_grp_router,
           w_exp_router, b_exp_router, w_exp_gate, w_exp_up, w_exp_down, g_ple, w_ple_gate, w_ple_proj, g_final):
    depth = w_in.shape[0]
    nbp, seq_p, _ = x_prompt.shape
    nbs, seq_s, _ = x_sample.shape
    n_p = nbp * seq_p
    n_s = nbs * seq_s

    def rows(v):
        return v.reshape(depth, 1, -1)

    P = {
        "g_mix": rows(g_mix), "w_in_t": jnp.swapaxes(w_in, 1, 2),
        "w_forget_up": w_forget_up, "b_forget": rows(b_forget), "g_gla_out": rows(g_gla_out),
        "w_conv": w_conv, "b_conv": rows(b_conv), "g_conv_ln": rows(g_conv_ln), "b_conv_ln": rows(b_conv_ln),
        "w_out": w_out, "g_ffn": rows(g_ffn), "g_ple": rows(g_ple),
        "w_ple_gate": w_ple_gate, "w_ple_proj": w_ple_proj,
        "wg": w_exp_gate.reshape(depth * N_EXPERTS, D_MODEL, EXPERT_FF),
        "wu": w_exp_up.reshape(depth * N_EXPERTS, D_MODEL, EXPERT_FF),
        "wd": w_exp_down.reshape(depth * N_EXPERTS, EXPERT_FF, D_MODEL),
    }
    g_fin = g_final.reshape(1, -1)
    xp = x_prompt.reshape(n_p, D_MODEL)
    xs = x_sample.reshape(n_s, D_MODEL)
    pp = p_prompt.reshape(depth, n_p, PLE_DIM)
    ps = p_sample.reshape(depth, n_s, PLE_DIM)
    s_in = state_gla.reshape(depth, nbs, QK_COLS, GLA_DV)
    c_in_t = jnp.swapaxes(state_conv, 1, 2)

    h = None
    sg_p, sg_s, sc_p, sc_s = [], [], [], []
    for l in range(depth):
        if l == 0:
            hp, sgp, scp = _mixer_prompt(xp, 0, nbp, seq_p, l, P)
            hs, sgs, scs = _mixer_sample(xs, 0, nbs, seq_s, l, s_in, c_in_t, P)
        else:
            hp, sgp, scp = _mixer_prompt(h, 0, nbp, seq_p, l, P)
            hs, sgs, scs = _mixer_sample(h, n_p, nbs, seq_s, l, s_in, c_in_t, P)
        w_rt, b_rt = _router_weights(l, w_grp_router, b_grp_router, w_exp_router, b_exp_router)
        h = _ffn(hp, hs, pp, ps, l, P, w_rt, b_rt, g_fin, final=(l == depth - 1))
        sg_p.append(sgp.reshape(nbp, GLA_HEADS, GLA_DK, GLA_DV))
        sg_s.append(sgs.reshape(nbs, GLA_HEADS, GLA_DK, GLA_DV))
        sc_p.append(scp)
        sc_s.append(scs)

    y_prompt = h[0].reshape(nbp, seq_p, D_MODEL)
    y_sample = h[1].reshape(nbs, seq_s, D_MODEL)
    return (y_prompt, y_sample, jnp.stack(sg_p), jnp.stack(sg_s), jnp.stack(sc_p), jnp.swapaxes(jnp.stack(sc_s), 1, 2))
```

```python
import functools

import jax
import jax.numpy as jnp
from jax import lax
from jax.experimental import pallas as pl
from jax.experimental.pallas import tpu as pltpu

D_MODEL = 1024
GLA_HEADS = 4
GLA_DK = 64
GLA_DV = 128
QK_COLS = GLA_HEADS * GLA_DK
V_COLS = GLA_HEADS * GLA_DV
CONV_WIDTH = 512
CONV_K = 31
GLA_LOWRANK = 16
GLA_TAU = 16.0
GLA_CHUNK = 64
PLE_DIM = 256
N_GROUPS = 4
EXPERTS_PER_GROUP = 8
N_EXPERTS = N_GROUPS * EXPERTS_PER_GROUP
EXPERT_FF = 256
TOP_K = 2
EPS = 1e-6
N_MAIN = 2 * QK_COLS + 2 * V_COLS

LANES = 128
SUBLANES = 8
CONV_PAD = 32
CONV_OFF = CONV_PAD - (CONV_K - 1)
VMEM_LIMIT = 56 * 1024 * 1024
TOK_TILE = 512
CHUNK = 16
EXP_TILE = 512
CBUF_ROWS = -(-(TOP_K * TOK_TILE + N_EXPERTS * (CHUNK - 1)) // LANES) * LANES
XS_COLS = D_MODEL + LANES

F32 = jnp.float32
BF16 = jnp.bfloat16
HI = lax.Precision.HIGHEST


def _sigmoid(x):
    return 1.0 / (1.0 + jnp.exp(-x))


def _silu(x):
    return x * _sigmoid(x)


def _log_sigmoid(x):
    return jnp.minimum(x, 0.0) - jnp.log(1.0 + jnp.exp(-jnp.abs(x)))


def _rmsnorm(x, g):
    return x * lax.rsqrt(jnp.mean(x * x, axis=-1, keepdims=True) + EPS) * g


def _dot(a, b):
    return jnp.dot(a.astype(BF16), b.astype(BF16), preferred_element_type=F32)


def _dot_t(a, b):
    return lax.dot_general(a.astype(BF16), b.astype(BF16), (((1,), (1,)), ((), ())),
                           preferred_element_type=F32)


def _dot_hi(a, b):
    return jnp.dot(a, b, preferred_element_type=F32, precision=HI)


def _iota_f32(shape, dim):
    return lax.broadcasted_iota(jnp.int32, shape, dim).astype(F32)


def _const_spec(shape):
    nd = len(shape)
    return pl.BlockSpec(shape, lambda *_: (0,) * nd)


def _layer_spec(arr, l):
    nd = arr.ndim - 1
    return pl.BlockSpec((None,) + arr.shape[1:], lambda *_: (l,) + (0,) * nd, pipeline_mode=pl.Buffered(1))


def _cast_mixer_weights(w_int_ref, w_fu_ref, w_out_ref, wmain_s, wlr_s, wcv_s, wfu_s, wout_s):
    blk = 4 * LANES
    for r in range(0, N_MAIN, blk):
        wmain_s[:, r:r + blk] = w_int_ref[r:r + blk, :].T.astype(BF16)
    lane = lax.broadcasted_iota(jnp.int32, (D_MODEL, LANES), 1)
    wlr_s[...] = jnp.where(lane < GLA_LOWRANK, w_int_ref[N_MAIN:N_MAIN + LANES, :].T, 0.0).astype(BF16)
    cv0 = N_MAIN + GLA_LOWRANK
    for r in range(0, 2 * CONV_WIDTH, blk):
        wcv_s[:, r:r + blk] = w_int_ref[cv0 + r:cv0 + r + blk, :].T.astype(BF16)
    wfu_s[...] = jnp.zeros_like(wfu_s)
    wfu_s[0:GLA_LOWRANK, :] = w_fu_ref[...].astype(BF16)
    wout_s[...] = w_out_ref[...].astype(BF16)


def _project(x, g_mix, wmain_s, wlr_s, wfu_s, b_f, wcv_s):
    xn = _rmsnorm(x, g_mix).astype(BF16)
    cv = jnp.dot(xn, wcv_s[...], preferred_element_type=F32)
    u = cv[:, :CONV_WIDTH] * _sigmoid(cv[:, CONV_WIDTH:])
    lr = jnp.dot(xn, wlr_s[...], preferred_element_type=F32)
    zf = _dot(lr, wfu_s[...]) + b_f
    la = _log_sigmoid(zf) * (1.0 / GLA_TAU)
    zqk = jnp.dot(xn, wmain_s[:, :2 * QK_COLS], preferred_element_type=F32)
    q = zqk[:, :QK_COLS] * (GLA_DK ** -0.5)
    k = zqk[:, QK_COLS:]
    zvo = jnp.dot(xn, wmain_s[:, 2 * QK_COLS:], preferred_element_type=F32)
    v = zvo[:, :V_COLS]
    og = zvo[:, V_COLS:]
    return q, k, v, og, la, u


def _stack_heads(qd):
    lane = lax.broadcasted_iota(jnp.int32, qd.shape, 1)
    return jnp.concatenate(
        [jnp.where((lane >= h * GLA_DK) & (lane < (h + 1) * GLA_DK), qd, 0.0) for h in range(GLA_HEADS)],
        axis=0)


def _gated_head_norm(o, og, g_gla):
    outs = []
    for h in range(GLA_HEADS):
        sl = slice(h * GLA_DV, (h + 1) * GLA_DV)
        outs.append(_rmsnorm(o[:, sl], g_gla) * _silu(og[:, sl]))
    return jnp.concatenate(outs, axis=1)


def _causal_conv(win, w_conv, b_conv, n):
    acc = jnp.broadcast_to(b_conv, (n, CONV_WIDTH))
    for s in range(SUBLANES):
        taps = [j for j in range(CONV_K) if (CONV_OFF + j) % SUBLANES == s]
        if not taps:
            continue
        rows = n if s == 0 else n + SUBLANES
        part = None
        for j in taps:
            a = (CONV_OFF + j) - s
            term = w_conv[j:j + 1, :] * win[a:a + rows, :]
            part = term if part is None else part + term
        acc = acc + part[s:s + n, :]
    return acc


def _conv_ln_act(acc, g_ln, b_ln):
    mu = jnp.mean(acc, axis=-1, keepdims=True)
    xc = acc - mu
    y = xc * lax.rsqrt(jnp.mean(xc * xc, axis=-1, keepdims=True) + EPS) * g_ln + b_ln
    return _silu(y)


def _head_diag(upd):
    return jnp.concatenate([upd[h * GLA_DK:(h + 1) * GLA_DK, h * GLA_DV:(h + 1) * GLA_DV]
                            for h in range(GLA_HEADS)], axis=0)


def _mixer_weight_args(l, P):
    names = ("g_mix", "w_in_t", "w_forget_up", "b_forget", "g_gla_out", "w_conv", "b_conv", "g_conv_ln",
             "b_conv_ln", "w_out")
    arrs = [P[n] for n in names]
    return arrs, [_layer_spec(a, l) for a in arrs]


_MIXER_WEIGHT_SCRATCH = [pltpu.VMEM((D_MODEL, N_MAIN), BF16), pltpu.VMEM((D_MODEL, LANES), BF16),
                         pltpu.VMEM((D_MODEL, 2 * CONV_WIDTH), BF16), pltpu.VMEM((LANES, QK_COLS), BF16),
                         pltpu.VMEM((D_MODEL, D_MODEL), BF16)]


def _mixer_prompt_kernel(x_ref, g_mix_ref, w_int_ref, w_fu_ref, b_f_ref, g_gla_ref, w_conv_ref,
                         b_conv_ref, g_ln_ref, b_ln_ref, w_out_ref,
                         h_ref, sg_ref, sc_ref,
                         wmain_s, wlr_s, wcv_s, wfu_s, wout_s,
                         s_ref, ubuf_ref, qs_ref, kd_ref, klt_ref, dec_ref, mid_ref, v_ref, og_ref, mix_ref, *, tt):
    t = pl.program_id(1)
    nt = pl.num_programs(1)
    C = GLA_CHUNK
    n_chunks = tt // C

    @pl.when((pl.program_id(0) == 0) & (t == 0))
    def _():
        _cast_mixer_weights(w_int_ref, w_fu_ref, w_out_ref, wmain_s, wlr_s, wcv_s, wfu_s, wout_s)

    @pl.when(t == 0)
    def _():
        s_ref[...] = jnp.zeros_like(s_ref)
        ubuf_ref[0:CONV_PAD, :] = jnp.zeros((CONV_PAD, CONV_WIDTH), F32)

    x = x_ref[...]
    q, k, v, og, la, u = _project(x, g_mix_ref[...], wmain_s, wlr_s, wfu_s, b_f_ref[...], wcv_s)
    ubuf_ref[CONV_PAD:CONV_PAD + tt, :] = u
    v_ref[...] = v.astype(BF16)
    og_ref[...] = _silu(og)

    w_conv = w_conv_ref[...]
    b_conv = b_conv_ref[...]
    g_ln = g_ln_ref[...]
    b_ln = b_ln_ref[...]
    row = lax.broadcasted_iota(jnp.int32, (C, C), 0)
    col = lax.broadcasted_iota(jnp.int32, (C, C), 1)
    tri = (col <= row).astype(F32)
    for c in range(n_chunks):
        rows = slice(c * C, (c + 1) * C)
        win = ubuf_ref[c * C:c * C + C + CONV_PAD, :]
        cact = _conv_ln_act(_causal_conv(win, w_conv, b_conv, C), g_ln, b_ln)
        mix_ref[rows, V_COLS:] = cact.astype(BF16)
        bits = pltpu.bitcast(cact[C - SUBLANES:C, 0:QK_COLS], jnp.uint32)
        zero = pltpu.bitcast(lax.shift_right_logical(lax.shift_right_logical(bits, jnp.uint32(16)), jnp.uint32(16)),
                             F32)[0:1, :]
        qc = q[rows, :] + zero
        b = _dot_hi(tri, la[rows, :])
        b_last = b[C - 1:C, :]
        b_mid = b[C // 2 - 1:C // 2, :]
        qs_ref[c] = _stack_heads(qc * jnp.exp(b - b_mid)).astype(BF16)
        kd_ref[rows, :] = (k[rows, :] * jnp.exp(b_mid - b)).astype(BF16)
        kl = k[rows, :] * jnp.exp(b_last - b)
        klt = jnp.concatenate([kl, jnp.broadcast_to(jnp.exp(b_last), (C // 2, QK_COLS)),
                               jnp.broadcast_to(jnp.exp(b_mid), (C // 2, QK_COLS))], axis=0).T
        klt_ref[c] = klt.astype(BF16)
        dec_ref[c] = jnp.broadcast_to(klt[:, C:C + 1], (QK_COLS, GLA_DV))
        mid_ref[c] = jnp.broadcast_to(klt[:, 3 * C // 2:3 * C // 2 + 1], (QK_COLS, GLA_DV))

    r4 = lax.broadcasted_iota(jnp.int32, (GLA_HEADS * C, C), 0)
    c4 = lax.broadcasted_iota(jnp.int32, (GLA_HEADS * C, C), 1)
    causal4 = c4 <= (r4 % C)
    g_gla = g_gla_ref[...]
    s = s_ref[...]
    for c in range(n_chunks):
        rows = slice(c * C, (c + 1) * C)
        qs = qs_ref[c]
        vc = v_ref[rows, :]
        scores = jnp.where(causal4, _dot_t(qs, kd_ref[rows, :]), 0.0).astype(BF16)
        o_inter = jnp.dot(qs, (mid_ref[c] * s).astype(BF16), preferred_element_type=F32)
        upd = jnp.dot(klt_ref[c][:, :C], vc, preferred_element_type=F32)
        s = dec_ref[c] * s + _head_diag(upd)
        o_parts = []
        for h in range(GLA_HEADS):
            vh = vc[:, h * GLA_DV:(h + 1) * GLA_DV]
            o_parts.append(jnp.dot(scores[h * C:(h + 1) * C, :], vh, preferred_element_type=F32)
                           + o_inter[h * C:(h + 1) * C, :])
        o = jnp.concatenate(o_parts, axis=1)
        gated = []
        for h in range(GLA_HEADS):
            sl = slice(h * GLA_DV, (h + 1) * GLA_DV)
            gated.append(_rmsnorm(o[:, sl], g_gla) * og_ref[rows, sl])
        mix_ref[rows, 0:V_COLS] = jnp.concatenate(gated, axis=1).astype(BF16)
    s_ref[...] = s

    h_ref[...] = x + jnp.dot(mix_ref[...], wout_s[...], preferred_element_type=F32)
    tail = ubuf_ref[tt:tt + CONV_PAD, :]
    ubuf_ref[0:CONV_PAD, :] = tail

    @pl.when(t == nt - 1)
    def _():
        sg_ref[0] = s
        sc_ref[0] = tail[CONV_OFF:, :]


def _mixer_prompt(x, row_off, nb, seq, l, P, *, tt=512):
    nt = seq // tt
    n_chunks = tt // GLA_CHUNK
    blk_off = row_off // tt
    weights, w_specs = _mixer_weight_args(l, P)
    in_specs = [pl.BlockSpec((tt, D_MODEL), lambda b, t: (blk_off + b * nt + t, 0))] + w_specs
    out_shape = (jax.ShapeDtypeStruct((nb * seq, D_MODEL), F32),
                 jax.ShapeDtypeStruct((nb, QK_COLS, GLA_DV), F32),
                 jax.ShapeDtypeStruct((nb, CONV_K - 1, CONV_WIDTH), F32))
    out_specs = (pl.BlockSpec((tt, D_MODEL), lambda b, t: (b * nt + t, 0)),
                 pl.BlockSpec((1, QK_COLS, GLA_DV), lambda b, t: (b, 0, 0)),
                 pl.BlockSpec((1, CONV_K - 1, CONV_WIDTH), lambda b, t: (b, 0, 0)))
    scratch = _MIXER_WEIGHT_SCRATCH + [
        pltpu.VMEM((QK_COLS, GLA_DV), F32),
        pltpu.VMEM((CONV_PAD + tt, CONV_WIDTH), F32),
        pltpu.VMEM((n_chunks, GLA_HEADS * GLA_CHUNK, QK_COLS), BF16), pltpu.VMEM((tt, QK_COLS), BF16),
        pltpu.VMEM((n_chunks, QK_COLS, 2 * GLA_CHUNK), BF16), pltpu.VMEM((n_chunks, QK_COLS, GLA_DV), F32),
        pltpu.VMEM((n_chunks, QK_COLS, GLA_DV), F32),
        pltpu.VMEM((tt, V_COLS), BF16), pltpu.VMEM((tt, V_COLS), F32),
        pltpu.VMEM((tt, D_MODEL), BF16)]
    return pl.pallas_call(
        functools.partial(_mixer_prompt_kernel, tt=tt),
        grid=(nb, nt), in_specs=in_specs, out_specs=out_specs, out_shape=out_shape,
        scratch_shapes=scratch,
        compiler_params=pltpu.CompilerParams(dimension_semantics=("arbitrary", "arbitrary"),
                                             vmem_limit_bytes=VMEM_LIMIT),
        name="mixer_prompt",
    )(x, *weights)


def _mixer_sample_kernel(x_ref, s_in_ref, c_in_ref, g_mix_ref, w_int_ref, w_fu_ref, b_f_ref,
                         g_gla_ref, w_conv_ref, b_conv_ref, g_ln_ref, b_ln_ref, w_out_ref,
                         h_ref, sg_ref, sc_ref,
                         wmain_s, wlr_s, wcv_s, wfu_s, wout_s,
                         u4_ref, oi_ref, cacc4_ref, *, sb, seq):
    R = sb * seq
    n_slabs = CONV_WIDTH // LANES

    @pl.when(pl.program_id(0) == 0)
    def _():
        _cast_mixer_weights(w_int_ref, w_fu_ref, w_out_ref, wmain_s, wlr_s, wcv_s, wfu_s, wout_s)

    x = x_ref[...]
    q, k, v, og, la, u = _project(x, g_mix_ref[...], wmain_s, wlr_s, wfu_s, b_f_ref[...], wcv_s)

    for kk in range(n_slabs):
        u4_ref[kk] = u[:, kk * LANES:(kk + 1) * LANES]
    full = [c_in_ref[j] for j in range(CONV_K - 1)]
    for t in range(seq):
        full.append(jnp.concatenate([u4_ref.at[kk][pl.ds(t, sb, stride=seq), :] for kk in range(n_slabs)], axis=1))
    w_conv = w_conv_ref[...]
    for t in range(seq):
        acc = jnp.broadcast_to(b_conv_ref[...], (sb, CONV_WIDTH))
        for j in range(CONV_K):
            acc = acc + w_conv[j:j + 1, :] * full[t + j]
        for kk in range(n_slabs):
            cacc4_ref.at[kk][pl.ds(t, sb, stride=seq), :] = acc[:, kk * LANES:(kk + 1) * LANES]
    for j in range(CONV_K - 1):
        sc_ref[j] = full[seq + j]

    row = lax.broadcasted_iota(jnp.int32, (R, R), 0)
    col = lax.broadcasted_iota(jnp.int32, (R, R), 1)
    same = (row // seq) == (col // seq)
    b = _dot_hi((same & (col <= row)).astype(F32), la)
    b_tot = _dot_hi(same.astype(F32), la)
    qd = q * jnp.exp(b)
    kd = k * jnp.exp(-b)
    kl = k * jnp.exp(b_tot - b)
    qs = _stack_heads(qd)
    r4 = lax.broadcasted_iota(jnp.int32, (GLA_HEADS * R, R), 0) % R
    c4 = lax.broadcasted_iota(jnp.int32, (GLA_HEADS * R, R), 1)
    mask4 = ((r4 // seq) == (c4 // seq)) & (c4 <= r4)
    scores = jnp.where(mask4, _dot_t(qs, kd), 0.0)

    klt = kl.T
    dect = jnp.exp(b_tot).T
    lane_h = lax.broadcasted_iota(jnp.int32, (GLA_DK, R), 1)
    lane_r = lax.broadcasted_iota(jnp.int32, (QK_COLS, R), 1)
    upd = []
    for h in range(GLA_HEADS):
        klt_h = klt[h * GLA_DK:(h + 1) * GLA_DK, :]
        lhs = jnp.concatenate([jnp.where((lane_h >= i * seq) & (lane_h < (i + 1) * seq), klt_h, 0.0)
                               for i in range(sb)], axis=0)
        upd.append(_dot(lhs, v[:, h * GLA_DV:(h + 1) * GLA_DV]))
    for i in range(sb):
        s_old = s_in_ref[i]
        qsel = jnp.concatenate([qs[h * R + i * seq:h * R + (i + 1) * seq, :] for h in range(GLA_HEADS)], axis=0)
        oi = _dot(qsel, s_old)
        for h in range(GLA_HEADS):
            oi_ref[h, i * seq:(i + 1) * seq, :] = oi[h * seq:(h + 1) * seq, :]
        smask = (lane_r >= i * seq) & (lane_r < (i + 1) * seq)
        dec = jnp.sum(jnp.where(smask, dect, 0.0), axis=1, keepdims=True) * (1.0 / seq)
        u_new = jnp.concatenate([upd[h][i * GLA_DK:(i + 1) * GLA_DK, :] for h in range(GLA_HEADS)], axis=0)
        sg_ref[i] = dec * s_old + u_new

    o_parts = []
    for h in range(GLA_HEADS):
        vh = v[:, h * GLA_DV:(h + 1) * GLA_DV]
        o_parts.append(_dot(scores[h * R:(h + 1) * R, :], vh) + oi_ref[h])
    o = jnp.concatenate(o_parts, axis=1)
    cacc = jnp.concatenate([cacc4_ref[kk] for kk in range(n_slabs)], axis=1)
    mix = jnp.concatenate([_gated_head_norm(o, og, g_gla_ref[...]),
                           _conv_ln_act(cacc, g_ln_ref[...], b_ln_ref[...])], axis=1)
    h_ref[...] = x + jnp.dot(mix.astype(BF16), wout_s[...], preferred_element_type=F32)


def _mixer_sample(x, row_off, nb, seq, l, s_in, c_in_t, P, *, sb=16):
    R = sb * seq
    blk_off = row_off // R
    n_slabs = CONV_WIDTH // LANES
    weights, w_specs = _mixer_weight_args(l, P)
    in_specs = [pl.BlockSpec((R, D_MODEL), lambda i: (blk_off + i, 0)),
                pl.BlockSpec((None, sb, QK_COLS, GLA_DV), lambda i: (l, i, 0, 0)),
                pl.BlockSpec((None, CONV_K - 1, sb, CONV_WIDTH), lambda i: (l, 0, i, 0))] + w_specs
    out_shape = (jax.ShapeDtypeStruct((nb * seq, D_MODEL), F32),
                 jax.ShapeDtypeStruct((nb, QK_COLS, GLA_DV), F32),
                 jax.ShapeDtypeStruct((CONV_K - 1, nb, CONV_WIDTH), F32))
    out_specs = (pl.BlockSpec((R, D_MODEL), lambda i: (i, 0)),
                 pl.BlockSpec((sb, QK_COLS, GLA_DV), lambda i: (i, 0, 0)),
                 pl.BlockSpec((CONV_K - 1, sb, CONV_WIDTH), lambda i: (0, i, 0)))
    scratch = _MIXER_WEIGHT_SCRATCH + [
        pltpu.VMEM((n_slabs, R, LANES), F32),
        pltpu.VMEM((GLA_HEADS, R, GLA_DV), F32),
        pltpu.VMEM((n_slabs, R, LANES), F32)]
    return pl.pallas_call(
        functools.partial(_mixer_sample_kernel, sb=sb, seq=seq),
        grid=(nb // sb,), in_specs=in_specs, out_specs=out_specs, out_shape=out_shape,
        scratch_shapes=scratch,
        compiler_params=pltpu.CompilerParams(dimension_semantics=("arbitrary",),
                                             vmem_limit_bytes=VMEM_LIMIT),
        name="mixer_sample",
    )(x, s_in, c_in_t, *weights)


def _pair_specs(n_first_tiles, width):
    return [pl.BlockSpec((TOK_TILE, width), lambda t, *_: (jnp.minimum(t, n_first_tiles - 1), 0)),
            pl.BlockSpec((TOK_TILE, width), lambda t, *_: (jnp.maximum(t - n_first_tiles, 0), 0))]


def _pick(t, n_first_tiles, a_ref, b_ref):
    return jnp.where(t < n_first_tiles, a_ref[...], b_ref[...])


def _route(logits):
    lane = lax.broadcasted_iota(jnp.int32, logits.shape, 1)
    lane_f = lane.astype(F32)
    neg = jnp.float32(-jnp.inf)
    big = jnp.float32(1e9)
    is_grp = (lane >= N_EXPERTS) & (lane < N_EXPERTS + N_GROUPS)
    gl = jnp.where(is_grp, logits, neg)
    gmax = jnp.max(gl, axis=-1, keepdims=True)
    gidx = jnp.min(jnp.where(is_grp & (gl == gmax), lane_f - N_EXPERTS, big), axis=-1, keepdims=True)
    gsum = jnp.sum(jnp.where(is_grp, jnp.exp(gl - gmax), 0.0), axis=-1, keepdims=True)
    g_w = 1.0 / gsum
    grp_of_lane = jnp.floor(lane_f * (1.0 / EXPERTS_PER_GROUP))
    in_grp = (lane < N_EXPERTS) & (grp_of_lane == gidx)
    ml = jnp.where(in_grp, logits, neg)
    v1 = jnp.max(ml, axis=-1, keepdims=True)
    i1 = jnp.min(jnp.where(in_grp & (ml == v1), lane_f, big), axis=-1, keepdims=True)
    ml2 = jnp.where(lane_f == i1, neg, ml)
    v2 = jnp.max(ml2, axis=-1, keepdims=True)
    i2 = jnp.min(jnp.where(in_grp & (ml2 == v2), lane_f, big), axis=-1, keepdims=True)
    e2 = jnp.exp(v2 - v1)
    w1 = g_w / (1.0 + e2)
    w2 = g_w * e2 / (1.0 + e2)
    return i1, i2, w1, w2


def _route_kernel(hp_ref, hs_ref, g_ffn_ref, w_rt_ref, b_rt_ref, pos_ref, cnt_ref, *, npt):
    T = TOK_TILE
    xn = _rmsnorm(_pick(pl.program_id(0), npt, hp_ref, hs_ref), g_ffn_ref[...])
    x_hi = xn.astype(BF16)
    x_lo = (xn - x_hi.astype(F32)).astype(BF16)
    w_split = w_rt_ref[...]
    both = jnp.dot(x_hi, w_split, preferred_element_type=F32)
    logits = (both[:, :LANES] + both[:, LANES:] + jnp.dot(x_lo, w_split[:, :LANES], preferred_element_type=F32)
              + b_rt_ref[...])
    i1, i2, w1, w2 = _route(logits)
    lane = _iota_f32((T, LANES), 1)
    a0 = (lane == i1).astype(F32)
    a1 = (lane == i2).astype(F32)
    a = a0 + a1
    cnt = jnp.sum(a, axis=0, keepdims=True)
    earlier = (_iota_f32((T, T), 1) < _iota_f32((T, T), 0)).astype(BF16)
    rank = jnp.dot(earlier, a.astype(BF16), preferred_element_type=F32)
    cnt_pad = jnp.ceil(cnt * (1.0 / CHUNK)) * CHUNK
    below = (_iota_f32((LANES, LANES), 0) < _iota_f32((LANES, LANES), 1)).astype(F32)
    first = _dot_hi(jnp.broadcast_to(cnt_pad, (SUBLANES, LANES)), below)[0:1, :]
    base = first + rank
    pos0 = jnp.sum(a0 * base, axis=1, keepdims=True)
    pos1 = jnp.sum(a1 * base, axis=1, keepdims=True)
    pos_ref[...] = jnp.where(lane == 0.0, pos0, jnp.where(lane == 1.0, pos1, jnp.where(
        lane == 2.0, w1, jnp.where(lane == 3.0, w2, jnp.where(lane == 4.0, i1, 0.0)))))
    cnt_ref[0] = jnp.broadcast_to(cnt, (SUBLANES, LANES))


def _route_call(hp, hs, l, P, w_rt, b_rt):
    npt = hp.shape[0] // TOK_TILE
    nt = npt + hs.shape[0] // TOK_TILE
    return pl.pallas_call(
        functools.partial(_route_kernel, npt=npt), grid=(nt,),
        in_specs=_pair_specs(npt, D_MODEL) + [_layer_spec(P["g_ffn"], l), _const_spec(w_rt.shape),
                                              _const_spec(b_rt.shape)],
        out_specs=(pl.BlockSpec((TOK_TILE, LANES), lambda t: (t, 0)),
                   pl.BlockSpec((1, SUBLANES, LANES), lambda t: (t, 0, 0))),
        out_shape=(jax.ShapeDtypeStruct((nt * TOK_TILE, LANES), F32),
                   jax.ShapeDtypeStruct((nt, SUBLANES, LANES), F32)),
        compiler_params=pltpu.CompilerParams(dimension_semantics=("arbitrary",), vmem_limit_bytes=VMEM_LIMIT),
        name="moe_route",
    )(hp, hs, P["g_ffn"], w_rt, b_rt)


def _chunk_plan(cnt, n_row_tiles):
    n16 = (cnt + (CHUNK - 1)) // CHUNK
    lofs16 = jnp.cumsum(n16, axis=1) - n16
    tile_pref16 = jnp.cumsum(n16, axis=0) - n16
    tot16 = jnp.sum(n16, axis=0)
    per_tile = EXP_TILE // CHUNK
    seg16 = ((tot16 + per_tile - 1) // per_tile) * per_tile
    seg_end16 = jnp.cumsum(seg16)
    dst16 = (seg_end16 - seg16)[None, :] + tile_pref16
    n_tot = jnp.sum(n16, axis=1)
    gap16 = seg_end16 - seg16 + tot16
    gapn16 = seg16 - tot16
    tile_start16 = jnp.arange(n_row_tiles, dtype=jnp.int32) * per_tile
    n_valid = seg_end16[-1] // per_tile
    misc = n_valid.reshape(1)
    exp_of_tile = jnp.minimum(jnp.sum(seg_end16[None, :] <= tile_start16[:, None], axis=1), N_EXPERTS - 1)
    i32 = lambda a: a.astype(jnp.int32).reshape(-1)
    return (i32(dst16), i32(n16), i32(lofs16), i32(n_tot), i32(gap16), i32(gapn16), i32(misc), i32(exp_of_tile))


def _chunk_copy(src, dst, src_chunk, dst_chunk, sem, n_chunks=1):
    rows = n_chunks * CHUNK
    return pltpu.make_async_copy(src.at[pl.ds(pl.multiple_of(src_chunk * CHUNK, CHUNK), rows), :],
                                 dst.at[pl.ds(pl.multiple_of(dst_chunk * CHUNK, CHUNK), rows), :], sem)


def _slab_copies(src, dst, sem, tile, src_ofs_ref, dst_ofs_ref, n16_ref):
    for e in range(N_EXPERTS):
        k = tile * N_EXPERTS + e
        n = n16_ref[k]

        @pl.when(n > 0)
        def _(k=k, n=n):
            _chunk_copy(src, dst, src_ofs_ref[k], dst_ofs_ref[k], sem, n).start()


def _wait_slabs(src, dst, sem, n_chunks):
    @pl.when(n_chunks > 0)
    def _():
        _chunk_copy(src, dst, 0, 0, sem, n_chunks).wait()


def _tile_copy(src, dst, dst_tile, sem):
    return pltpu.make_async_copy(src, dst.at[pl.ds(pl.multiple_of(dst_tile * EXP_TILE, EXP_TILE), EXP_TILE), :], sem)


def _dispatch_kernel(dst16_ref, n16_ref, lofs16_ref, ntot_ref, gap16_ref, gapn16_ref, misc_ref,
                     hp_ref, hs_ref, pos_ref, g_ffn_ref, xs_hbm, cbuf, zbuf, sem, *, n_row_tiles, npt):
    t = pl.program_id(0)
    nt = pl.num_programs(0)
    slot = t % 2
    T = TOK_TILE
    n_tail = n_row_tiles - misc_ref[0]

    @pl.when(t == 0)
    def _():
        zbuf[...] = jnp.zeros_like(zbuf)
        for e in range(N_EXPERTS):
            g = gapn16_ref[e]

            @pl.when(g > 0)
            def _(e=e, g=g):
                _chunk_copy(zbuf, xs_hbm, 0, gap16_ref[e], sem.at[2], g).start()

        def fill_tile(i, carry):
            _tile_copy(zbuf, xs_hbm, misc_ref[0] + i, sem.at[2]).start()
            return carry
        lax.fori_loop(0, n_tail, fill_tile, 0)

    xn = _rmsnorm(_pick(t, npt, hp_ref, hs_ref), g_ffn_ref[...]).astype(BF16)
    pos = pos_ref[...]
    pos_t = pos.T
    rows = _iota_f32((CBUF_ROWS, T), 0)
    onehot = jnp.where((rows == pos_t[0:1, :]) | (rows == pos_t[1:2, :]), 1.0, 0.0).astype(BF16)
    lane = lax.broadcasted_iota(jnp.int32, (T, LANES), 1)
    extra = jnp.zeros((T, LANES), F32)
    for s in range(TOP_K):
        c = pos[:, 2 + s:3 + s]
        hi = c.astype(BF16).astype(F32)
        mid = (c - hi).astype(BF16).astype(F32)
        lo = c - hi - mid
        for j, piece in enumerate((hi, mid, lo)):
            extra = jnp.where(lane == 3 * s + j, piece, extra)
    extra = jnp.where(lane == 3 * TOP_K, pos[:, 4:5], extra)
    cbuf[slot] = jnp.dot(onehot, jnp.concatenate([xn, extra.astype(BF16)], axis=1),
                         preferred_element_type=F32).astype(BF16)

    src = cbuf.at[slot]
    _slab_copies(src, xs_hbm, sem.at[slot], t, lofs16_ref, dst16_ref, n16_ref)

    @pl.when(t > 0)
    def _():
        _wait_slabs(cbuf.at[1 - slot], xs_hbm, sem.at[1 - slot], ntot_ref[t - 1])

    @pl.when(t == nt - 1)
    def _():
        _wait_slabs(src, xs_hbm, sem.at[slot], ntot_ref[t])
        for e in range(N_EXPERTS):
            _wait_slabs(zbuf, xs_hbm, sem.at[2], gapn16_ref[e])

        def wait_tile(_, carry):
            _tile_copy(zbuf, xs_hbm, 0, sem.at[2]).wait()
            return carry
        lax.fori_loop(0, n_tail, wait_tile, 0)


def _dispatch_call(plan, hp, hs, pos, l, P, n_sorted):
    npt = hp.shape[0] // TOK_TILE
    nt = npt + hs.shape[0] // TOK_TILE
    g_ffn = P["g_ffn"]
    grid_spec = pltpu.PrefetchScalarGridSpec(
        num_scalar_prefetch=7, grid=(nt,),
        in_specs=_pair_specs(npt, D_MODEL) + [
            pl.BlockSpec((TOK_TILE, LANES), lambda t, *_: (t, 0)),
            pl.BlockSpec((None,) + g_ffn.shape[1:], lambda t, *_: (l, 0, 0))],
        out_specs=pl.BlockSpec(memory_space=pl.ANY),
        scratch_shapes=[pltpu.VMEM((2, CBUF_ROWS, XS_COLS), BF16), pltpu.VMEM((EXP_TILE, XS_COLS), BF16),
                        pltpu.SemaphoreType.DMA((3,))])
    return pl.pallas_call(
        functools.partial(_dispatch_kernel, n_row_tiles=n_sorted // EXP_TILE, npt=npt), grid_spec=grid_spec,
        out_shape=jax.ShapeDtypeStruct((n_sorted, XS_COLS), BF16),
        compiler_params=pltpu.CompilerParams(dimension_semantics=("arbitrary",), vmem_limit_bytes=VMEM_LIMIT),
        name="moe_dispatch",
    )(*plan[:7], hp, hs, pos, g_ffn)


def _expert_kernel(eot_ref, misc_ref, xs_ref, wg_ref, wu_ref, wd_ref, y_ref, wg_s, wu_s, wd_s):
    i = pl.program_id(0)
    valid = i < misc_ref[0]

    @pl.when(jnp.logical_not(valid))
    def _():
        y_ref[...] = jnp.zeros_like(y_ref)

    @pl.when(valid & ((i == 0) | (eot_ref[i] != eot_ref[jnp.maximum(i - 1, 0)])))
    def _():
        wg_s[...] = wg_ref[...].astype(BF16)
        wu_s[...] = wu_ref[...].astype(BF16)
        wd_s[...] = wd_ref[...].astype(BF16)

    @pl.when(valid)
    def _():
        xs = xs_ref[...]
        x = xs[:, :D_MODEL]
        ex = xs[:, D_MODEL:].astype(F32)
        lane = lax.broadcasted_iota(jnp.int32, ex.shape, 1)
        id0 = jnp.sum(jnp.where(lane == 3 * TOP_K, ex, 0.0), axis=-1, keepdims=True)
        first = id0 == eot_ref[i].astype(F32)
        mine = (first & (lane < 3)) | (jnp.logical_not(first) & (lane >= 3) & (lane < 3 * TOP_K))
        c = jnp.sum(jnp.where(mine, ex, 0.0), axis=-1, keepdims=True)
        hg = _silu(jnp.dot(x, wg_s[...], preferred_element_type=F32)) * jnp.dot(x, wu_s[...],
                                                                               preferred_element_type=F32)
        y_ref[...] = jnp.dot((hg * c).astype(BF16), wd_s[...], preferred_element_type=F32).astype(BF16)


def _expert_call(plan, xs, l, wg, wu, wd):
    misc, exp_of_tile = plan[6:]
    n_row_tiles = xs.shape[0] // EXP_TILE

    def last_valid(i, nv):
        return jnp.maximum(jnp.minimum(i, nv[0] - 1), 0)

    def row_map(i, eot, nv):
        return (last_valid(i, nv), 0)

    def w_map(i, eot, nv):
        return (l * N_EXPERTS + eot[last_valid(i, nv)], 0, 0)

    grid_spec = pltpu.PrefetchScalarGridSpec(
        num_scalar_prefetch=2, grid=(n_row_tiles,),
        in_specs=[pl.BlockSpec((EXP_TILE, XS_COLS), row_map),
                  pl.BlockSpec((None, D_MODEL, EXPERT_FF), w_map),
                  pl.BlockSpec((None, D_MODEL, EXPERT_FF), w_map),
                  pl.BlockSpec((None, EXPERT_FF, D_MODEL), w_map)],
        out_specs=pl.BlockSpec((EXP_TILE, D_MODEL), lambda i, eot, nv: (i, 0)),
        scratch_shapes=[pltpu.VMEM((D_MODEL, EXPERT_FF), BF16), pltpu.VMEM((D_MODEL, EXPERT_FF), BF16),
                        pltpu.VMEM((EXPERT_FF, D_MODEL), BF16)])
    return pl.pallas_call(
        _expert_kernel, grid_spec=grid_spec,
        out_shape=jax.ShapeDtypeStruct((xs.shape[0], D_MODEL), BF16),
        compiler_params=pltpu.CompilerParams(dimension_semantics=("arbitrary",), vmem_limit_bytes=VMEM_LIMIT),
        name="moe_experts",
    )(exp_of_tile, misc, xs, wg, wu, wd)


def _combine_kernel(dst16_ref, n16_ref, lofs16_ref, ntot_ref, hp_ref, hs_ref, pp_ref, ps_ref, pos_ref, y_hbm,
                    g_ple_ref, w_pg_ref, w_pp_ref, g_fin_ref, *rest, final, npt):
    if final:
        op_ref, os_ref, ybuf, wpg_s, wpp_s, sem = rest
    else:
        o_ref, ybuf, wpg_s, wpp_s, sem = rest
    t = pl.program_id(0)
    nt = pl.num_programs(0)
    slot = t % 2

    def fetch(tile, sl):
        _slab_copies(y_hbm, ybuf.at[sl], sem.at[sl], tile, dst16_ref, lofs16_ref, n16_ref)

    @pl.when(t == 0)
    def _():
        ybuf[...] = jnp.zeros_like(ybuf)
        fetch(t, slot)
        wpg_s[...] = w_pg_ref[...].astype(BF16)
        wpp_s[...] = w_pp_ref[...].astype(BF16)

    @pl.when(t + 1 < nt)
    def _():
        fetch(t + 1, 1 - slot)

    _wait_slabs(y_hbm, ybuf.at[slot], sem.at[slot], ntot_ref[t])

    pos = pos_ref[...]
    cols = _iota_f32((TOK_TILE, CBUF_ROWS), 1)
    pick = jnp.where((cols == pos[:, 0:1]) | (cols == pos[:, 1:2]), 1.0, 0.0).astype(BF16)
    h2 = _pick(t, npt, hp_ref, hs_ref) + jnp.dot(pick, ybuf[slot], preferred_element_type=F32)
    xn2 = _rmsnorm(h2, g_ple_ref[...])
    gate = _sigmoid(_dot(xn2, wpg_s[...]))
    p = jnp.where(t < npt, pp_ref[...], ps_ref[...])
    h3 = h2 + gate * _dot(p, wpp_s[...])
    if final:
        h3 = _rmsnorm(h3, g_fin_ref[...])

        @pl.when(t < npt)
        def _():
            op_ref[...] = h3

        @pl.when(t >= npt)
        def _():
            os_ref[...] = h3
    else:
        o_ref[...] = h3


def _combine_call(plan, hp, hs, pp, ps, pos, y, l, P, g_final, *, final):
    npt = hp.shape[0] // TOK_TILE
    nst = hs.shape[0] // TOK_TILE
    nt = npt + nst

    def lmap(t, *_):
        return (l, 0, 0)

    in_specs = _pair_specs(npt, D_MODEL) + [
        pl.BlockSpec((None, TOK_TILE, PLE_DIM), lambda t, *_: (l, jnp.minimum(t, npt - 1), 0)),
        pl.BlockSpec((None, TOK_TILE, PLE_DIM), lambda t, *_: (l, jnp.maximum(t - npt, 0), 0)),
        pl.BlockSpec((TOK_TILE, LANES), lambda t, *_: (t, 0)),
        pl.BlockSpec(memory_space=pl.ANY),
        pl.BlockSpec((None,) + P["g_ple"].shape[1:], lmap),
        pl.BlockSpec((None,) + P["w_ple_gate"].shape[1:], lmap),
        pl.BlockSpec((None,) + P["w_ple_proj"].shape[1:], lmap),
        pl.BlockSpec(g_final.shape, lambda t, *_: (0, 0))]
    if final:
        out_specs = (pl.BlockSpec((TOK_TILE, D_MODEL), lambda t, *_: (jnp.minimum(t, npt - 1), 0)),
                     pl.BlockSpec((TOK_TILE, D_MODEL), lambda t, *_: (jnp.maximum(t - npt, 0), 0)))
        out_shape = (jax.ShapeDtypeStruct(hp.shape, F32), jax.ShapeDtypeStruct(hs.shape, F32))
    else:
        out_specs = pl.BlockSpec((TOK_TILE, D_MODEL), lambda t, *_: (t, 0))
        out_shape = jax.ShapeDtypeStruct((nt * TOK_TILE, D_MODEL), F32)
    grid_spec = pltpu.PrefetchScalarGridSpec(
        num_scalar_prefetch=4, grid=(nt,), in_specs=in_specs, out_specs=out_specs,
        scratch_shapes=[pltpu.VMEM((2, CBUF_ROWS, D_MODEL), BF16), pltpu.VMEM((D_MODEL, D_MODEL), BF16),
                        pltpu.VMEM((PLE_DIM, D_MODEL), BF16), pltpu.SemaphoreType.DMA((2,))])
    return pl.pallas_call(
        functools.partial(_combine_kernel, final=final, npt=npt), grid_spec=grid_spec, out_shape=out_shape,
        compiler_params=pltpu.CompilerParams(dimension_semantics=("arbitrary",), vmem_limit_bytes=VMEM_LIMIT),
        name="moe_combine_final" if final else "moe_combine",
    )(*plan[:4], hp, hs, pp, ps, pos, y, P["g_ple"], P["w_ple_gate"], P["w_ple_proj"], g_final)


def _ffn(hp, hs, pp, ps, l, P, w_rt, b_rt, g_final, *, final):
    n = hp.shape[0] + hs.shape[0]
    nt = n // TOK_TILE
    bound = TOP_K * n + nt * N_EXPERTS * (CHUNK - 1) + N_EXPERTS * (EXP_TILE - 1)
    n_sorted = -(-bound // EXP_TILE) * EXP_TILE
    pos, cnt = _route_call(hp, hs, l, P, w_rt, b_rt)
    plan = _chunk_plan(cnt[:, 0, :N_EXPERTS].astype(jnp.int32), n_sorted // EXP_TILE)
    xs = _dispatch_call(plan, hp, hs, pos, l, P, n_sorted)
    y = _expert_call(plan, xs, l, P["wg"], P["wu"], P["wd"])
    return _combine_call(plan, hp, hs, pp, ps, pos, y, l, P, g_final, final=final)


def _router_weights(l, w_grp_router, b_grp_router, w_exp_router, b_exp_router):
    w_er = jnp.transpose(w_exp_router[l], (1, 0, 2)).reshape(D_MODEL, N_EXPERTS)
    w_rt = jnp.zeros((D_MODEL, LANES), F32).at[:, :N_EXPERTS].set(w_er)
    w_rt = w_rt.at[:, N_EXPERTS:N_EXPERTS + N_GROUPS].set(w_grp_router[l])
    b_rt = jnp.zeros((1, LANES), F32).at[0, :N_EXPERTS].set(b_exp_router[l].reshape(-1))
    b_rt = b_rt.at[0, N_EXPERTS:N_EXPERTS + N_GROUPS].set(b_grp_router[l])
    w_hi = w_rt.astype(BF16)
    w_lo = (w_rt - w_hi.astype(F32)).astype(BF16)
    return jnp.concatenate([w_hi, w_lo], axis=1), b_rt


def kernel(x_prompt, x_sample, state_gla, state_conv, p_prompt, p_sample, g_mix, w_in, w_forget_up, b_forget,
           g_gla_out, w_conv, b_conv, g_conv_ln, b_conv_ln, w_out, g_ffn, w_grp_router, b_grp_router,
           w_exp_router, b_exp_router, w_exp_gate, w_exp_up, w_exp_down, g_ple, w_ple_gate, w_ple_proj, g_final):
    depth = w_in.shape[0]
    nbp, seq_p, _ = x_prompt.shape
    nbs, seq_s, _ = x_sample.shape
    n_p = nbp * seq_p
    n_s = nbs * seq_s

    def rows(v):
        return v.reshape(depth, 1, -1)

    P = {
        "g_mix": rows(g_mix), "w_in_t": jnp.swapaxes(w_in, 1, 2),
        "w_forget_up": w_forget_up, "b_forget": rows(b_forget), "g_gla_out": rows(g_gla_out),
        "w_conv": w_conv, "b_conv": rows(b_conv), "g_conv_ln": rows(g_conv_ln), "b_conv_ln": rows(b_conv_ln),
        "w_out": w_out, "g_ffn": rows(g_ffn), "g_ple": rows(g_ple),
        "w_ple_gate": w_ple_gate, "w_ple_proj": w_ple_proj,
        "wg": w_exp_gate.reshape(depth * N_EXPERTS, D_MODEL, EXPERT_FF),
        "wu": w_exp_up.reshape(depth * N_EXPERTS, D_MODEL, EXPERT_FF),
        "wd": w_exp_down.reshape(depth * N_EXPERTS, EXPERT_FF, D_MODEL),
    }
    g_fin = g_final.reshape(1, -1)
    xp = x_prompt.reshape(n_p, D_MODEL)
    xs = x_sample.reshape(n_s, D_MODEL)
    pp = p_prompt.reshape(depth, n_p, PLE_DIM)
    ps = p_sample.reshape(depth, n_s, PLE_DIM)
    s_in = state_gla.reshape(depth, nbs, QK_COLS, GLA_DV)
    c_in_t = jnp.swapaxes(state_conv, 1, 2)

    h = None
    sg_p, sg_s, sc_p, sc_s = [], [], [], []
    for l in range(depth):
        if l == 0:
            hp, sgp, scp = _mixer_prompt(xp, 0, nbp, seq_p, l, P)
            hs, sgs, scs = _mixer_sample(xs, 0, nbs, seq_s, l, s_in, c_in_t, P)
        else:
            hp, sgp, scp = _mixer_prompt(h, 0, nbp, seq_p, l, P)
            hs, sgs, scs = _mixer_sample(h, n_p, nbs, seq_s, l, s_in, c_in_t, P)
        w_rt, b_rt = _router_weights(l, w_grp_router, b_grp_router, w_exp_router, b_exp_router)
        h = _ffn(hp, hs, pp, ps, l, P, w_rt, b_rt, g_fin, final=(l == depth - 1))
        sg_p.append(sgp.reshape(nbp, GLA_HEADS, GLA_DK, GLA_DV))
        sg_s.append(sgs.reshape(nbs, GLA_HEADS, GLA_DK, GLA_DV))
        sc_p.append(scp)
        sc_s.append(scs)

    y_prompt = h[0].reshape(nbp, seq_p, D_MODEL)
    y_sample = h[1].reshape(nbs, seq_s, D_MODEL)
    return (y_prompt, y_sample, jnp.stack(sg_p), jnp.stack(sg_s), jnp.stack(sc_p), jnp.swapaxes(jnp.stack(sc_s), 1, 2))
```

```python
import functools

import jax
import jax.numpy as jnp
from jax import lax
from jax.experimental import pallas as pl
from jax.experimental.pallas import tpu as pltpu

D_MODEL = 1024
GLA_HEADS = 4
GLA_DK = 64
GLA_DV = 128
QK_COLS = GLA_HEADS * GLA_DK
V_COLS = GLA_HEADS * GLA_DV
CONV_WIDTH = 512
CONV_K = 31
GLA_LOWRANK = 16
GLA_TAU = 16.0
GLA_CHUNK = 64
PLE_DIM = 256
N_GROUPS = 4
EXPERTS_PER_GROUP = 8
N_EXPERTS = N_GROUPS * EXPERTS_PER_GROUP
EXPERT_FF = 256
TOP_K = 2
EPS = 1e-6
N_MAIN = 2 * QK_COLS + 2 * V_COLS

LANES = 128
SUBLANES = 8
CONV_PAD = 32
CONV_OFF = CONV_PAD - (CONV_K - 1)
VMEM_LIMIT = 56 * 1024 * 1024
TOK_TILE = 512
CHUNK = 16
EXP_TILE = 512
CBUF_ROWS = -(-(TOP_K * TOK_TILE + N_EXPERTS * (CHUNK - 1)) // LANES) * LANES
XS_COLS = D_MODEL + LANES
DECAY_SAFE = 80.0

F32 = jnp.float32
BF16 = jnp.bfloat16
HI = lax.Precision.HIGHEST


def _sigmoid(x):
    return 1.0 / (1.0 + jnp.exp(-x))


def _silu(x):
    return x * _sigmoid(x)


def _log_sigmoid(x):
    return jnp.minimum(x, 0.0) - jnp.log(1.0 + jnp.exp(-jnp.abs(x)))


def _rmsnorm(x, g):
    return x * lax.rsqrt(jnp.mean(x * x, axis=-1, keepdims=True) + EPS) * g


def _dot(a, b):
    return jnp.dot(a.astype(BF16), b.astype(BF16), preferred_element_type=F32)


def _dot_t(a, b):
    return lax.dot_general(a.astype(BF16), b.astype(BF16), (((1,), (1,)), ((), ())),
                           preferred_element_type=F32)


def _dot_hi(a, b):
    return jnp.dot(a, b, preferred_element_type=F32, precision=HI)


def _iota_f32(shape, dim):
    return lax.broadcasted_iota(jnp.int32, shape, dim).astype(F32)


def _const_spec(shape):
    nd = len(shape)
    return pl.BlockSpec(shape, lambda *_: (0,) * nd)


def _layer_spec(arr, l):
    nd = arr.ndim - 1
    return pl.BlockSpec((None,) + arr.shape[1:], lambda *_: (l,) + (0,) * nd, pipeline_mode=pl.Buffered(1))


def _cast_mixer_weights(w_int_ref, w_fu_ref, w_out_ref, wmain_s, wlr_s, wcv_s, wfu_s, wout_s):
    blk = 4 * LANES
    for r in range(0, N_MAIN, blk):
        wmain_s[:, r:r + blk] = w_int_ref[r:r + blk, :].T.astype(BF16)
    lane = lax.broadcasted_iota(jnp.int32, (D_MODEL, LANES), 1)
    wlr_s[...] = jnp.where(lane < GLA_LOWRANK, w_int_ref[N_MAIN:N_MAIN + LANES, :].T, 0.0).astype(BF16)
    cv0 = N_MAIN + GLA_LOWRANK
    for r in range(0, 2 * CONV_WIDTH, blk):
        wcv_s[:, r:r + blk] = w_int_ref[cv0 + r:cv0 + r + blk, :].T.astype(BF16)
    wfu_s[...] = jnp.zeros_like(wfu_s)
    wfu_s[0:GLA_LOWRANK, :] = w_fu_ref[...].astype(BF16)
    wout_s[...] = w_out_ref[...].astype(BF16)


def _project(x, g_mix, wmain_s, wlr_s, wfu_s, b_f, wcv_s):
    xn = _rmsnorm(x, g_mix).astype(BF16)
    cv = jnp.dot(xn, wcv_s[...], preferred_element_type=F32)
    u = cv[:, :CONV_WIDTH] * _sigmoid(cv[:, CONV_WIDTH:])
    lr = jnp.dot(xn, wlr_s[...], preferred_element_type=F32)
    zf = _dot(lr, wfu_s[...]) + b_f
    la = _log_sigmoid(zf) * (1.0 / GLA_TAU)
    zqk = jnp.dot(xn, wmain_s[:, :2 * QK_COLS], preferred_element_type=F32)
    q = zqk[:, :QK_COLS] * (GLA_DK ** -0.5)
    k = zqk[:, QK_COLS:]
    zvo = jnp.dot(xn, wmain_s[:, 2 * QK_COLS:], preferred_element_type=F32)
    v = zvo[:, :V_COLS]
    og = zvo[:, V_COLS:]
    return q, k, v, og, la, u


def _stack_heads(qd):
    lane = lax.broadcasted_iota(jnp.int32, qd.shape, 1)
    return jnp.concatenate(
        [jnp.where((lane >= h * GLA_DK) & (lane < (h + 1) * GLA_DK), qd, 0.0) for h in range(GLA_HEADS)],
        axis=0)


def _gated_head_norm(o, og, g_gla):
    outs = []
    for h in range(GLA_HEADS):
        sl = slice(h * GLA_DV, (h + 1) * GLA_DV)
        outs.append(_rmsnorm(o[:, sl], g_gla) * _silu(og[:, sl]))
    return jnp.concatenate(outs, axis=1)


def _causal_conv(win, w_conv, b_conv, n):
    acc = jnp.broadcast_to(b_conv, (n, CONV_WIDTH))
    for s in range(SUBLANES):
        taps = [j for j in range(CONV_K) if (CONV_OFF + j) % SUBLANES == s]
        if not taps:
            continue
        rows = n if s == 0 else n + SUBLANES
        part = None
        for j in taps:
            a = (CONV_OFF + j) - s
            term = w_conv[j:j + 1, :] * win[a:a + rows, :]
            part = term if part is None else part + term
        acc = acc + part[s:s + n, :]
    return acc


def _conv_ln_act(acc, g_ln, b_ln):
    mu = jnp.mean(acc, axis=-1, keepdims=True)
    xc = acc - mu
    y = xc * lax.rsqrt(jnp.mean(xc * xc, axis=-1, keepdims=True) + EPS) * g_ln + b_ln
    return _silu(y)


def _head_diag(upd):
    return jnp.concatenate([upd[h * GLA_DK:(h + 1) * GLA_DK, h * GLA_DV:(h + 1) * GLA_DV]
                            for h in range(GLA_HEADS)], axis=0)


def _mixer_weight_args(l, P):
    names = ("g_mix", "w_in_t", "w_forget_up", "b_forget", "g_gla_out", "w_conv", "b_conv", "g_conv_ln",
             "b_conv_ln", "w_out")
    arrs = [P[n] for n in names]
    return arrs, [_layer_spec(a, l) for a in arrs]


_MIXER_WEIGHT_SCRATCH = [pltpu.VMEM((D_MODEL, N_MAIN), BF16), pltpu.VMEM((D_MODEL, LANES), BF16),
                         pltpu.VMEM((D_MODEL, 2 * CONV_WIDTH), BF16), pltpu.VMEM((LANES, QK_COLS), BF16),
                         pltpu.VMEM((D_MODEL, D_MODEL), BF16)]


def _mixer_prompt_kernel(x_ref, g_mix_ref, w_int_ref, w_fu_ref, b_f_ref, g_gla_ref, w_conv_ref,
                         b_conv_ref, g_ln_ref, b_ln_ref, w_out_ref,
                         h_ref, sg_ref, sc_ref, risk_ref,
                         wmain_s, wlr_s, wcv_s, wfu_s, wout_s,
                         s_ref, ubuf_ref, qs_ref, kd_ref, klt_ref, dec_ref, mid_ref, v_ref, og_ref, mix_ref,
                         *stable_refs, tt, stable):
    t = pl.program_id(1)
    nt = pl.num_programs(1)
    C = GLA_CHUNK
    n_chunks = tt // C

    @pl.when((pl.program_id(0) == 0) & (t == 0))
    def _():
        _cast_mixer_weights(w_int_ref, w_fu_ref, w_out_ref, wmain_s, wlr_s, wcv_s, wfu_s, wout_s)

    @pl.when(t == 0)
    def _():
        s_ref[...] = jnp.zeros_like(s_ref)
        ubuf_ref[0:CONV_PAD, :] = jnp.zeros((CONV_PAD, CONV_WIDTH), F32)

    x = x_ref[...]
    q, k, v, og, la, u = _project(x, g_mix_ref[...], wmain_s, wlr_s, wfu_s, b_f_ref[...], wcv_s)
    ubuf_ref[CONV_PAD:CONV_PAD + tt, :] = u
    v_ref[...] = v.astype(BF16)
    og_ref[...] = _silu(og)

    w_conv = w_conv_ref[...]
    b_conv = b_conv_ref[...]
    g_ln = g_ln_ref[...]
    b_ln = b_ln_ref[...]
    row = lax.broadcasted_iota(jnp.int32, (C, C), 0)
    col = lax.broadcasted_iota(jnp.int32, (C, C), 1)
    tri = (col <= row).astype(F32)
    risk = jnp.zeros((1, QK_COLS), F32)
    for c in range(n_chunks):
        rows = slice(c * C, (c + 1) * C)
        win = ubuf_ref[c * C:c * C + C + CONV_PAD, :]
        cact = _conv_ln_act(_causal_conv(win, w_conv, b_conv, C), g_ln, b_ln)
        mix_ref[rows, V_COLS:] = cact.astype(BF16)
        bits = pltpu.bitcast(cact[C - SUBLANES:C, 0:QK_COLS], jnp.uint32)
        zero = pltpu.bitcast(lax.shift_right_logical(lax.shift_right_logical(bits, jnp.uint32(16)), jnp.uint32(16)),
                             F32)[0:1, :]
        qc = q[rows, :] + zero
        b = _dot_hi(tri, la[rows, :])
        b_last = b[C - 1:C, :]
        if stable:
            q_st, k_st, b_st = stable_refs
            q_st[rows, :] = qc
            k_st[rows, :] = k[rows, :]
            b_st[rows, :] = b
            b_mid = jnp.zeros_like(b_last)
        else:
            b_mid = b[C // 2 - 1:C // 2, :]
            risk = jnp.maximum(risk, jnp.maximum(-b_mid, b_mid - b_last))
        qs_ref[c] = _stack_heads(qc * jnp.exp(b - b_mid)).astype(BF16)
        kd_ref[rows, :] = (k[rows, :] * jnp.exp(jnp.minimum(b_mid - b, DECAY_SAFE))).astype(BF16)
        kl = k[rows, :] * jnp.exp(b_last - b)
        klt = jnp.concatenate([kl, jnp.broadcast_to(jnp.exp(b_last), (C // 2, QK_COLS)),
                               jnp.broadcast_to(jnp.exp(b_mid), (C // 2, QK_COLS))], axis=0).T
        klt_ref[c] = klt.astype(BF16)
        dec_ref[c] = jnp.broadcast_to(klt[:, C:C + 1], (QK_COLS, GLA_DV))
        mid_ref[c] = jnp.broadcast_to(klt[:, 3 * C // 2:3 * C // 2 + 1], (QK_COLS, GLA_DV))

    r4 = lax.broadcasted_iota(jnp.int32, (GLA_HEADS * C, C), 0)
    c4 = lax.broadcasted_iota(jnp.int32, (GLA_HEADS * C, C), 1)
    causal4 = c4 <= (r4 % C)
    g_gla = g_gla_ref[...]
    s = s_ref[...]
    for c in range(n_chunks):
        rows = slice(c * C, (c + 1) * C)
        qs = qs_ref[c]
        vc = v_ref[rows, :]
        if stable:
            scores = _direct_scores(*stable_refs, c * C, C).astype(BF16)
        else:
            scores = jnp.where(causal4, _dot_t(qs, kd_ref[rows, :]), 0.0).astype(BF16)
        o_inter = jnp.dot(qs, (mid_ref[c] * s).astype(BF16), preferred_element_type=F32)
        upd = jnp.dot(klt_ref[c][:, :C], vc, preferred_element_type=F32)
        s = dec_ref[c] * s + _head_diag(upd)
        o_parts = []
        for h in range(GLA_HEADS):
            vh = vc[:, h * GLA_DV:(h + 1) * GLA_DV]
            o_parts.append(jnp.dot(scores[h * C:(h + 1) * C, :], vh, preferred_element_type=F32)
                           + o_inter[h * C:(h + 1) * C, :])
        o = jnp.concatenate(o_parts, axis=1)
        gated = []
        for h in range(GLA_HEADS):
            sl = slice(h * GLA_DV, (h + 1) * GLA_DV)
            gated.append(_rmsnorm(o[:, sl], g_gla) * og_ref[rows, sl])
        mix_ref[rows, 0:V_COLS] = jnp.concatenate(gated, axis=1).astype(BF16)
    s_ref[...] = s

    h_ref[...] = x + jnp.dot(mix_ref[...], wout_s[...], preferred_element_type=F32)
    tail = ubuf_ref[tt:tt + CONV_PAD, :]
    ubuf_ref[0:CONV_PAD, :] = tail

    risk_ref[0] = jnp.broadcast_to(jnp.max(risk, axis=-1, keepdims=True), (SUBLANES, LANES))

    @pl.when(t == nt - 1)
    def _():
        sg_ref[0] = s
        sc_ref[0] = tail[CONV_OFF:, :]


def _direct_scores(q_st, k_st, b_st, r0, C):
    kc = k_st[r0:r0 + C, :]
    bc = b_st[r0:r0 + C, :]
    srow = lax.broadcasted_iota(jnp.int32, (C, QK_COLS), 0)
    lane = lax.broadcasted_iota(jnp.int32, (C, LANES), 1)
    head_of = lax.broadcasted_iota(jnp.int32, (QK_COLS, LANES), 0) // GLA_DK
    head_sum = (head_of == lax.broadcasted_iota(jnp.int32, (QK_COLS, LANES), 1)).astype(F32)

    def one_query(t, acc):
        d = b_st[pl.ds(r0 + t, 1), :] - bc
        w = jnp.where(srow <= t, jnp.exp(jnp.minimum(d, 0.0)), 0.0) * kc * q_st[pl.ds(r0 + t, 1), :]
        per_head = _dot_hi(w, head_sum)
        return tuple(jnp.where(lane == t, per_head[:, h:h + 1], acc[h]) for h in range(GLA_HEADS))

    acc = lax.fori_loop(0, C, one_query, tuple(jnp.zeros((C, LANES), F32) for _ in range(GLA_HEADS)))
    return jnp.concatenate([a.T[0:C, :] for a in acc], axis=0)


def _mixer_prompt(x, row_off, nb, seq, l, P, *, tt=512, stable=False):
    nt = seq // tt
    n_chunks = tt // GLA_CHUNK
    blk_off = row_off // tt
    weights, w_specs = _mixer_weight_args(l, P)
    in_specs = [pl.BlockSpec((tt, D_MODEL), lambda b, t: (blk_off + b * nt + t, 0))] + w_specs
    out_shape = (jax.ShapeDtypeStruct((nb * seq, D_MODEL), F32),
                 jax.ShapeDtypeStruct((nb, QK_COLS, GLA_DV), F32),
                 jax.ShapeDtypeStruct((nb, CONV_K - 1, CONV_WIDTH), F32),
                 jax.ShapeDtypeStruct((nb * nt, SUBLANES, LANES), F32))
    out_specs = (pl.BlockSpec((tt, D_MODEL), lambda b, t: (b * nt + t, 0)),
                 pl.BlockSpec((1, QK_COLS, GLA_DV), lambda b, t: (b, 0, 0)),
                 pl.BlockSpec((1, CONV_K - 1, CONV_WIDTH), lambda b, t: (b, 0, 0)),
                 pl.BlockSpec((1, SUBLANES, LANES), lambda b, t: (b * nt + t, 0, 0)))
    scratch = _MIXER_WEIGHT_SCRATCH + [
        pltpu.VMEM((QK_COLS, GLA_DV), F32),
        pltpu.VMEM((CONV_PAD + tt, CONV_WIDTH), F32),
        pltpu.VMEM((n_chunks, GLA_HEADS * GLA_CHUNK, QK_COLS), BF16), pltpu.VMEM((tt, QK_COLS), BF16),
        pltpu.VMEM((n_chunks, QK_COLS, 2 * GLA_CHUNK), BF16), pltpu.VMEM((n_chunks, QK_COLS, GLA_DV), F32),
        pltpu.VMEM((n_chunks, QK_COLS, GLA_DV), F32),
        pltpu.VMEM((tt, V_COLS), BF16), pltpu.VMEM((tt, V_COLS), F32),
        pltpu.VMEM((tt, D_MODEL), BF16)]
    if stable:
        scratch += [pltpu.VMEM((tt, QK_COLS), F32)] * 3
    return pl.pallas_call(
        functools.partial(_mixer_prompt_kernel, tt=tt, stable=stable),
        grid=(nb, nt), in_specs=in_specs, out_specs=out_specs, out_shape=out_shape,
        scratch_shapes=scratch,
        compiler_params=pltpu.CompilerParams(dimension_semantics=("arbitrary", "arbitrary"),
                                             vmem_limit_bytes=VMEM_LIMIT),
        name="mixer_prompt_stable" if stable else "mixer_prompt",
    )(x, *weights)


def _guarded(mixer):
    *outs, risk = mixer(stable=False)
    return lax.cond(jnp.max(risk) > DECAY_SAFE, lambda: tuple(mixer(stable=True)[:-1]), lambda: tuple(outs))


def _mixer_sample_kernel(x_ref, s_in_ref, c_in_ref, g_mix_ref, w_int_ref, w_fu_ref, b_f_ref,
                         g_gla_ref, w_conv_ref, b_conv_ref, g_ln_ref, b_ln_ref, w_out_ref,
                         h_ref, sg_ref, sc_ref, risk_ref,
                         wmain_s, wlr_s, wcv_s, wfu_s, wout_s,
                         u4_ref, oi_ref, cacc4_ref, *, sb, seq, stable):
    R = sb * seq
    n_slabs = CONV_WIDTH // LANES

    @pl.when(pl.program_id(0) == 0)
    def _():
        _cast_mixer_weights(w_int_ref, w_fu_ref, w_out_ref, wmain_s, wlr_s, wcv_s, wfu_s, wout_s)

    x = x_ref[...]
    q, k, v, og, la, u = _project(x, g_mix_ref[...], wmain_s, wlr_s, wfu_s, b_f_ref[...], wcv_s)

    for kk in range(n_slabs):
        u4_ref[kk] = u[:, kk * LANES:(kk + 1) * LANES]
    full = [c_in_ref[j] for j in range(CONV_K - 1)]
    for t in range(seq):
        full.append(jnp.concatenate([u4_ref.at[kk][pl.ds(t, sb, stride=seq), :] for kk in range(n_slabs)], axis=1))
    w_conv = w_conv_ref[...]
    for t in range(seq):
        acc = jnp.broadcast_to(b_conv_ref[...], (sb, CONV_WIDTH))
        for j in range(CONV_K):
            acc = acc + w_conv[j:j + 1, :] * full[t + j]
        for kk in range(n_slabs):
            cacc4_ref.at[kk][pl.ds(t, sb, stride=seq), :] = acc[:, kk * LANES:(kk + 1) * LANES]
    for j in range(CONV_K - 1):
        sc_ref[j] = full[seq + j]

    row = lax.broadcasted_iota(jnp.int32, (R, R), 0)
    col = lax.broadcasted_iota(jnp.int32, (R, R), 1)
    same = (row // seq) == (col // seq)
    b = _dot_hi((same & (col <= row)).astype(F32), la)
    b_tot = _dot_hi(same.astype(F32), la)
    qd = q * jnp.exp(b)
    kl = k * jnp.exp(b_tot - b)
    qs = _stack_heads(qd)
    risk_ref[0] = jnp.broadcast_to(jnp.max(jnp.max(-b_tot, axis=-1, keepdims=True), axis=0, keepdims=True),
                                   (SUBLANES, LANES))
    if stable:
        head_of = lax.broadcasted_iota(jnp.int32, (QK_COLS, LANES), 0) // GLA_DK
        head_sum = (head_of == lax.broadcasted_iota(jnp.int32, (QK_COLS, LANES), 1)).astype(F32)
        t_in_seq = lax.broadcasted_iota(jnp.int32, (R, QK_COLS), 0) % seq
        parts = [jnp.zeros((R, R), F32) for _ in range(GLA_HEADS)]
        for d in range(seq):
            k_d = jnp.concatenate([jnp.zeros((d, QK_COLS), F32), k[:R - d, :]], axis=0) if d else k
            b_d = jnp.concatenate([jnp.zeros((d, QK_COLS), F32), b[:R - d, :]], axis=0) if d else b
            w = jnp.where(t_in_seq >= d, jnp.exp(jnp.minimum(b - b_d, 0.0)), 0.0) * k_d * q
            per_head = _dot_hi(w, head_sum)
            for h in range(GLA_HEADS):
                parts[h] = jnp.where(col == row - d, per_head[:, h:h + 1], parts[h])
        scores = jnp.concatenate(parts, axis=0)
    else:
        kd = k * jnp.exp(jnp.minimum(-b, DECAY_SAFE))
        r4 = lax.broadcasted_iota(jnp.int32, (GLA_HEADS * R, R), 0) % R
        c4 = lax.broadcasted_iota(jnp.int32, (GLA_HEADS * R, R), 1)
        mask4 = ((r4 // seq) == (c4 // seq)) & (c4 <= r4)
        scores = jnp.where(mask4, _dot_t(qs, kd), 0.0)

    klt = kl.T
    dect = jnp.exp(b_tot).T
    lane_h = lax.broadcasted_iota(jnp.int32, (GLA_DK, R), 1)
    lane_r = lax.broadcasted_iota(jnp.int32, (QK_COLS, R), 1)
    upd = []
    for h in range(GLA_HEADS):
        klt_h = klt[h * GLA_DK:(h + 1) * GLA_DK, :]
        lhs = jnp.concatenate([jnp.where((lane_h >= i * seq) & (lane_h < (i + 1) * seq), klt_h, 0.0)
                               for i in range(sb)], axis=0)
        upd.append(_dot(lhs, v[:, h * GLA_DV:(h + 1) * GLA_DV]))
    for i in range(sb):
        s_old = s_in_ref[i]
        qsel = jnp.concatenate([qs[h * R + i * seq:h * R + (i + 1) * seq, :] for h in range(GLA_HEADS)], axis=0)
        oi = _dot(qsel, s_old)
        for h in range(GLA_HEADS):
            oi_ref[h, i * seq:(i + 1) * seq, :] = oi[h * seq:(h + 1) * seq, :]
        smask = (lane_r >= i * seq) & (lane_r < (i + 1) * seq)
        dec = jnp.sum(jnp.where(smask, dect, 0.0), axis=1, keepdims=True) * (1.0 / seq)
        u_new = jnp.concatenate([upd[h][i * GLA_DK:(i + 1) * GLA_DK, :] for h in range(GLA_HEADS)], axis=0)
        sg_ref[i] = dec * s_old + u_new

    o_parts = []
    for h in range(GLA_HEADS):
        vh = v[:, h * GLA_DV:(h + 1) * GLA_DV]
        o_parts.append(_dot(scores[h * R:(h + 1) * R, :], vh) + oi_ref[h])
    o = jnp.concatenate(o_parts, axis=1)
    cacc = jnp.concatenate([cacc4_ref[kk] for kk in range(n_slabs)], axis=1)
    mix = jnp.concatenate([_gated_head_norm(o, og, g_gla_ref[...]),
                           _conv_ln_act(cacc, g_ln_ref[...], b_ln_ref[...])], axis=1)
    h_ref[...] = x + jnp.dot(mix.astype(BF16), wout_s[...], preferred_element_type=F32)


def _mixer_sample(x, row_off, nb, seq, l, s_in, c_in_t, P, *, sb=16, stable=False):
    R = sb * seq
    blk_off = row_off // R
    n_slabs = CONV_WIDTH // LANES
    weights, w_specs = _mixer_weight_args(l, P)
    in_specs = [pl.BlockSpec((R, D_MODEL), lambda i: (blk_off + i, 0)),
                pl.BlockSpec((None, sb, QK_COLS, GLA_DV), lambda i: (l, i, 0, 0)),
                pl.BlockSpec((None, CONV_K - 1, sb, CONV_WIDTH), lambda i: (l, 0, i, 0))] + w_specs
    out_shape = (jax.ShapeDtypeStruct((nb * seq, D_MODEL), F32),
                 jax.ShapeDtypeStruct((nb, QK_COLS, GLA_DV), F32),
                 jax.ShapeDtypeStruct((CONV_K - 1, nb, CONV_WIDTH), F32),
                 jax.ShapeDtypeStruct((nb // sb, SUBLANES, LANES), F32))
    out_specs = (pl.BlockSpec((R, D_MODEL), lambda i: (i, 0)),
                 pl.BlockSpec((sb, QK_COLS, GLA_DV), lambda i: (i, 0, 0)),
                 pl.BlockSpec((CONV_K - 1, sb, CONV_WIDTH), lambda i: (0, i, 0)),
                 pl.BlockSpec((1, SUBLANES, LANES), lambda i: (i, 0, 0)))
    scratch = _MIXER_WEIGHT_SCRATCH + [
        pltpu.VMEM((n_slabs, R, LANES), F32),
        pltpu.VMEM((GLA_HEADS, R, GLA_DV), F32),
        pltpu.VMEM((n_slabs, R, LANES), F32)]
    return pl.pallas_call(
        functools.partial(_mixer_sample_kernel, sb=sb, seq=seq, stable=stable),
        grid=(nb // sb,), in_specs=in_specs, out_specs=out_specs, out_shape=out_shape,
        scratch_shapes=scratch,
        compiler_params=pltpu.CompilerParams(dimension_semantics=("arbitrary",),
                                             vmem_limit_bytes=VMEM_LIMIT),
        name="mixer_sample_stable" if stable else "mixer_sample",
    )(x, s_in, c_in_t, *weights)


def _pair_specs(n_first_tiles, width):
    return [pl.BlockSpec((TOK_TILE, width), lambda t, *_: (jnp.minimum(t, n_first_tiles - 1), 0)),
            pl.BlockSpec((TOK_TILE, width), lambda t, *_: (jnp.maximum(t - n_first_tiles, 0), 0))]


def _pick(t, n_first_tiles, a_ref, b_ref):
    return jnp.where(t < n_first_tiles, a_ref[...], b_ref[...])


def _route(logits):
    lane = lax.broadcasted_iota(jnp.int32, logits.shape, 1)
    lane_f = lane.astype(F32)
    neg = jnp.float32(-jnp.inf)
    big = jnp.float32(1e9)
    is_grp = (lane >= N_EXPERTS) & (lane < N_EXPERTS + N_GROUPS)
    gl = jnp.where(is_grp, logits, neg)
    gmax = jnp.max(gl, axis=-1, keepdims=True)
    gidx = jnp.min(jnp.where(is_grp & (gl == gmax), lane_f - N_EXPERTS, big), axis=-1, keepdims=True)
    gsum = jnp.sum(jnp.where(is_grp, jnp.exp(gl - gmax), 0.0), axis=-1, keepdims=True)
    g_w = 1.0 / gsum
    grp_of_lane = jnp.floor(lane_f * (1.0 / EXPERTS_PER_GROUP))
    in_grp = (lane < N_EXPERTS) & (grp_of_lane == gidx)
    ml = jnp.where(in_grp, logits, neg)
    v1 = jnp.max(ml, axis=-1, keepdims=True)
    i1 = jnp.min(jnp.where(in_grp & (ml == v1), lane_f, big), axis=-1, keepdims=True)
    ml2 = jnp.where(lane_f == i1, neg, ml)
    v2 = jnp.max(ml2, axis=-1, keepdims=True)
    i2 = jnp.min(jnp.where(in_grp & (ml2 == v2), lane_f, big), axis=-1, keepdims=True)
    e2 = jnp.exp(v2 - v1)
    w1 = g_w / (1.0 + e2)
    w2 = g_w * e2 / (1.0 + e2)
    return i1, i2, w1, w2


def _route_kernel(hp_ref, hs_ref, g_ffn_ref, w_rt_ref, b_rt_ref, pos_ref, cnt_ref, *, npt):
    T = TOK_TILE
    xn = _rmsnorm(_pick(pl.program_id(0), npt, hp_ref, hs_ref), g_ffn_ref[...])
    x_hi = xn.astype(BF16)
    x_lo = (xn - x_hi.astype(F32)).astype(BF16)
    w_split = w_rt_ref[...]
    both = jnp.dot(x_hi, w_split, preferred_element_type=F32)
    logits = (both[:, :LANES] + both[:, LANES:] + jnp.dot(x_lo, w_split[:, :LANES], preferred_element_type=F32)
              + b_rt_ref[...])
    i1, i2, w1, w2 = _route(logits)
    lane = _iota_f32((T, LANES), 1)
    a0 = (lane == i1).astype(F32)
    a1 = (lane == i2).astype(F32)
    a = a0 + a1
    cnt = jnp.sum(a, axis=0, keepdims=True)
    earlier = (_iota_f32((T, T), 1) < _iota_f32((T, T), 0)).astype(BF16)
    rank = jnp.dot(earlier, a.astype(BF16), preferred_element_type=F32)
    cnt_pad = jnp.ceil(cnt * (1.0 / CHUNK)) * CHUNK
    below = (_iota_f32((LANES, LANES), 0) < _iota_f32((LANES, LANES), 1)).astype(F32)
    first = _dot_hi(jnp.broadcast_to(cnt_pad, (SUBLANES, LANES)), below)[0:1, :]
    base = first + rank
    pos0 = jnp.sum(a0 * base, axis=1, keepdims=True)
    pos1 = jnp.sum(a1 * base, axis=1, keepdims=True)
    pos_ref[...] = jnp.where(lane == 0.0, pos0, jnp.where(lane == 1.0, pos1, jnp.where(
        lane == 2.0, w1, jnp.where(lane == 3.0, w2, jnp.where(lane == 4.0, i1, 0.0)))))
    cnt_ref[0] = jnp.broadcast_to(cnt, (SUBLANES, LANES))


def _route_call(hp, hs, l, P, w_rt, b_rt):
    npt = hp.shape[0] // TOK_TILE
    nt = npt + hs.shape[0] // TOK_TILE
    return pl.pallas_call(
        functools.partial(_route_kernel, npt=npt), grid=(nt,),
        in_specs=_pair_specs(npt, D_MODEL) + [_layer_spec(P["g_ffn"], l), _const_spec(w_rt.shape),
                                              _const_spec(b_rt.shape)],
        out_specs=(pl.BlockSpec((TOK_TILE, LANES), lambda t: (t, 0)),
                   pl.BlockSpec((1, SUBLANES, LANES), lambda t: (t, 0, 0))),
        out_shape=(jax.ShapeDtypeStruct((nt * TOK_TILE, LANES), F32),
                   jax.ShapeDtypeStruct((nt, SUBLANES, LANES), F32)),
        compiler_params=pltpu.CompilerParams(dimension_semantics=("arbitrary",), vmem_limit_bytes=VMEM_LIMIT),
        name="moe_route",
    )(hp, hs, P["g_ffn"], w_rt, b_rt)


def _chunk_plan(cnt, n_row_tiles):
    n16 = (cnt + (CHUNK - 1)) // CHUNK
    lofs16 = jnp.cumsum(n16, axis=1) - n16
    tile_pref16 = jnp.cumsum(n16, axis=0) - n16
    tot16 = jnp.sum(n16, axis=0)
    per_tile = EXP_TILE // CHUNK
    seg16 = ((tot16 + per_tile - 1) // per_tile) * per_tile
    seg_end16 = jnp.cumsum(seg16)
    dst16 = (seg_end16 - seg16)[None, :] + tile_pref16
    n_tot = jnp.sum(n16, axis=1)
    gap16 = seg_end16 - seg16 + tot16
    gapn16 = seg16 - tot16
    tile_start16 = jnp.arange(n_row_tiles, dtype=jnp.int32) * per_tile
    n_valid = seg_end16[-1] // per_tile
    misc = n_valid.reshape(1)
    exp_of_tile = jnp.minimum(jnp.sum(seg_end16[None, :] <= tile_start16[:, None], axis=1), N_EXPERTS - 1)
    i32 = lambda a: a.astype(jnp.int32).reshape(-1)
    return (i32(dst16), i32(n16), i32(lofs16), i32(n_tot), i32(gap16), i32(gapn16), i32(misc), i32(exp_of_tile))


def _chunk_copy(src, dst, src_chunk, dst_chunk, sem, n_chunks=1):
    rows = n_chunks * CHUNK
    return pltpu.make_async_copy(src.at[pl.ds(pl.multiple_of(src_chunk * CHUNK, CHUNK), rows), :],
                                 dst.at[pl.ds(pl.multiple_of(dst_chunk * CHUNK, CHUNK), rows), :], sem)


def _slab_copies(src, dst, sem, tile, src_ofs_ref, dst_ofs_ref, n16_ref):
    for e in range(N_EXPERTS):
        k = tile * N_EXPERTS + e
        n = n16_ref[k]

        @pl.when(n > 0)
        def _(k=k, n=n):
            _chunk_copy(src, dst, src_ofs_ref[k], dst_ofs_ref[k], sem, n).start()


def _wait_slabs(src, dst, sem, n_chunks):
    @pl.when(n_chunks > 0)
    def _():
        _chunk_copy(src, dst, 0, 0, sem, n_chunks).wait()


def _tile_copy(src, dst, dst_tile, sem):
    return pltpu.make_async_copy(src, dst.at[pl.ds(pl.multiple_of(dst_tile * EXP_TILE, EXP_TILE), EXP_TILE), :], sem)


def _dispatch_kernel(dst16_ref, n16_ref, lofs16_ref, ntot_ref, gap16_ref, gapn16_ref, misc_ref,
                     hp_ref, hs_ref, pos_ref, g_ffn_ref, xs_hbm, cbuf, zbuf, sem, *, n_row_tiles, npt):
    t = pl.program_id(0)
    nt = pl.num_programs(0)
    slot = t % 2
    T = TOK_TILE
    n_tail = n_row_tiles - misc_ref[0]

    @pl.when(t == 0)
    def _():
        zbuf[...] = jnp.zeros_like(zbuf)
        for e in range(N_EXPERTS):
            g = gapn16_ref[e]

            @pl.when(g > 0)
            def _(e=e, g=g):
                _chunk_copy(zbuf, xs_hbm, 0, gap16_ref[e], sem.at[2], g).start()

        def fill_tile(i, carry):
            _tile_copy(zbuf, xs_hbm, misc_ref[0] + i, sem.at[2]).start()
            return carry
        lax.fori_loop(0, n_tail, fill_tile, 0)

    xn = _rmsnorm(_pick(t, npt, hp_ref, hs_ref), g_ffn_ref[...]).astype(BF16)
    pos = pos_ref[...]
    pos_t = pos.T
    rows = _iota_f32((CBUF_ROWS, T), 0)
    onehot = jnp.where((rows == pos_t[0:1, :]) | (rows == pos_t[1:2, :]), 1.0, 0.0).astype(BF16)
    lane = lax.broadcasted_iota(jnp.int32, (T, LANES), 1)
    extra = jnp.zeros((T, LANES), F32)
    for s in range(TOP_K):
        c = pos[:, 2 + s:3 + s]
        hi = c.astype(BF16).astype(F32)
        mid = (c - hi).astype(BF16).astype(F32)
        lo = c - hi - mid
        for j, piece in enumerate((hi, mid, lo)):
            extra = jnp.where(lane == 3 * s + j, piece, extra)
    extra = jnp.where(lane == 3 * TOP_K, pos[:, 4:5], extra)
    cbuf[slot] = jnp.dot(onehot, jnp.concatenate([xn, extra.astype(BF16)], axis=1),
                         preferred_element_type=F32).astype(BF16)

    src = cbuf.at[slot]
    _slab_copies(src, xs_hbm, sem.at[slot], t, lofs16_ref, dst16_ref, n16_ref)

    @pl.when(t > 0)
    def _():
        _wait_slabs(cbuf.at[1 - slot], xs_hbm, sem.at[1 - slot], ntot_ref[t - 1])

    @pl.when(t == nt - 1)
    def _():
        _wait_slabs(src, xs_hbm, sem.at[slot], ntot_ref[t])
        for e in range(N_EXPERTS):
            _wait_slabs(zbuf, xs_hbm, sem.at[2], gapn16_ref[e])

        def wait_tile(_, carry):
            _tile_copy(zbuf, xs_hbm, 0, sem.at[2]).wait()
            return carry
        lax.fori_loop(0, n_tail, wait_tile, 0)


def _dispatch_call(plan, hp, hs, pos, l, P, n_sorted):
    npt = hp.shape[0] // TOK_TILE
    nt = npt + hs.shape[0] // TOK_TILE
    g_ffn = P["g_ffn"]
    grid_spec = pltpu.PrefetchScalarGridSpec(
        num_scalar_prefetch=7, grid=(nt,),
        in_specs=_pair_specs(npt, D_MODEL) + [
            pl.BlockSpec((TOK_TILE, LANES), lambda t, *_: (t, 0)),
            pl.BlockSpec((None,) + g_ffn.shape[1:], lambda t, *_: (l, 0, 0))],
        out_specs=pl.BlockSpec(memory_space=pl.ANY),
        scratch_shapes=[pltpu.VMEM((2, CBUF_ROWS, XS_COLS), BF16), pltpu.VMEM((EXP_TILE, XS_COLS), BF16),
                        pltpu.SemaphoreType.DMA((3,))])
    return pl.pallas_call(
        functools.partial(_dispatch_kernel, n_row_tiles=n_sorted // EXP_TILE, npt=npt), grid_spec=grid_spec,
        out_shape=jax.ShapeDtypeStruct((n_sorted, XS_COLS), BF16),
        compiler_params=pltpu.CompilerParams(dimension_semantics=("arbitrary",), vmem_limit_bytes=VMEM_LIMIT),
        name="moe_dispatch",
    )(*plan[:7], hp, hs, pos, g_ffn)


def _expert_kernel(eot_ref, misc_ref, xs_ref, wg_ref, wu_ref, wd_ref, y_ref, wg_s, wu_s, wd_s):
    i = pl.program_id(0)
    valid = i < misc_ref[0]

    @pl.when(jnp.logical_not(valid))
    def _():
        y_ref[...] = jnp.zeros_like(y_ref)

    @pl.when(valid & ((i == 0) | (eot_ref[i] != eot_ref[jnp.maximum(i - 1, 0)])))
    def _():
        wg_s[...] = wg_ref[...].astype(BF16)
        wu_s[...] = wu_ref[...].astype(BF16)
        wd_s[...] = wd_ref[...].astype(BF16)

    @pl.when(valid)
    def _():
        xs = xs_ref[...]
        x = xs[:, :D_MODEL]
        ex = xs[:, D_MODEL:].astype(F32)
        lane = lax.broadcasted_iota(jnp.int32, ex.shape, 1)
        id0 = jnp.sum(jnp.where(lane == 3 * TOP_K, ex, 0.0), axis=-1, keepdims=True)
        first = id0 == eot_ref[i].astype(F32)
        mine = (first & (lane < 3)) | (jnp.logical_not(first) & (lane >= 3) & (lane < 3 * TOP_K))
        c = jnp.sum(jnp.where(mine, ex, 0.0), axis=-1, keepdims=True)
        hg = _silu(jnp.dot(x, wg_s[...], preferred_element_type=F32)) * jnp.dot(x, wu_s[...],
                                                                               preferred_element_type=F32)
        y_ref[...] = jnp.dot((hg * c).astype(BF16), wd_s[...], preferred_element_type=F32).astype(BF16)


def _expert_call(plan, xs, l, wg, wu, wd):
    misc, exp_of_tile = plan[6:]
    n_row_tiles = xs.shape[0] // EXP_TILE

    def last_valid(i, nv):
        return jnp.maximum(jnp.minimum(i, nv[0] - 1), 0)

    def row_map(i, eot, nv):
        return (last_valid(i, nv), 0)

    def w_map(i, eot, nv):
        return (l * N_EXPERTS + eot[last_valid(i, nv)], 0, 0)

    grid_spec = pltpu.PrefetchScalarGridSpec(
        num_scalar_prefetch=2, grid=(n_row_tiles,),
        in_specs=[pl.BlockSpec((EXP_TILE, XS_COLS), row_map),
                  pl.BlockSpec((None, D_MODEL, EXPERT_FF), w_map),
                  pl.BlockSpec((None, D_MODEL, EXPERT_FF), w_map),
                  pl.BlockSpec((None, EXPERT_FF, D_MODEL), w_map)],
        out_specs=pl.BlockSpec((EXP_TILE, D_MODEL), lambda i, eot, nv: (i, 0)),
        scratch_shapes=[pltpu.VMEM((D_MODEL, EXPERT_FF), BF16), pltpu.VMEM((D_MODEL, EXPERT_FF), BF16),
                        pltpu.VMEM((EXPERT_FF, D_MODEL), BF16)])
    return pl.pallas_call(
        _expert_kernel, grid_spec=grid_spec,
        out_shape=jax.ShapeDtypeStruct((xs.shape[0], D_MODEL), BF16),
        compiler_params=pltpu.CompilerParams(dimension_semantics=("arbitrary",), vmem_limit_bytes=VMEM_LIMIT),
        name="moe_experts",
    )(exp_of_tile, misc, xs, wg, wu, wd)


def _combine_kernel(dst16_ref, n16_ref, lofs16_ref, ntot_ref, hp_ref, hs_ref, pp_ref, ps_ref, pos_ref, y_hbm,
                    g_ple_ref, w_pg_ref, w_pp_ref, g_fin_ref, *rest, final, npt):
    if final:
        op_ref, os_ref, ybuf, wpg_s, wpp_s, sem = rest
    else:
        o_ref, ybuf, wpg_s, wpp_s, sem = rest
    t = pl.program_id(0)
    nt = pl.num_programs(0)
    slot = t % 2

    def fetch(tile, sl):
        _slab_copies(y_hbm, ybuf.at[sl], sem.at[sl], tile, dst16_ref, lofs16_ref, n16_ref)

    @pl.when(t == 0)
    def _():
        ybuf[...] = jnp.zeros_like(ybuf)
        fetch(t, slot)
        wpg_s[...] = w_pg_ref[...].astype(BF16)
        wpp_s[...] = w_pp_ref[...].astype(BF16)

    @pl.when(t + 1 < nt)
    def _():
        fetch(t + 1, 1 - slot)

    _wait_slabs(y_hbm, ybuf.at[slot], sem.at[slot], ntot_ref[t])

    pos = pos_ref[...]
    cols = _iota_f32((TOK_TILE, CBUF_ROWS), 1)
    pick = jnp.where((cols == pos[:, 0:1]) | (cols == pos[:, 1:2]), 1.0, 0.0).astype(BF16)
    h2 = _pick(t, npt, hp_ref, hs_ref) + jnp.dot(pick, ybuf[slot], preferred_element_type=F32)
    xn2 = _rmsnorm(h2, g_ple_ref[...])
    gate = _sigmoid(_dot(xn2, wpg_s[...]))
    p = jnp.where(t < npt, pp_ref[...], ps_ref[...])
    h3 = h2 + gate * _dot(p, wpp_s[...])
    if final:
        h3 = _rmsnorm(h3, g_fin_ref[...])

        @pl.when(t < npt)
        def _():
            op_ref[...] = h3

        @pl.when(t >= npt)
        def _():
            os_ref[...] = h3
    else:
        o_ref[...] = h3


def _combine_call(plan, hp, hs, pp, ps, pos, y, l, P, g_final, *, final):
    npt = hp.shape[0] // TOK_TILE
    nst = hs.shape[0] // TOK_TILE
    nt = npt + nst

    def lmap(t, *_):
        return (l, 0, 0)

    in_specs = _pair_specs(npt, D_MODEL) + [
        pl.BlockSpec((None, TOK_TILE, PLE_DIM), lambda t, *_: (l, jnp.minimum(t, npt - 1), 0)),
        pl.BlockSpec((None, TOK_TILE, PLE_DIM), lambda t, *_: (l, jnp.maximum(t - npt, 0), 0)),
        pl.BlockSpec((TOK_TILE, LANES), lambda t, *_: (t, 0)),
        pl.BlockSpec(memory_space=pl.ANY),
        pl.BlockSpec((None,) + P["g_ple"].shape[1:], lmap),
        pl.BlockSpec((None,) + P["w_ple_gate"].shape[1:], lmap),
        pl.BlockSpec((None,) + P["w_ple_proj"].shape[1:], lmap),
        pl.BlockSpec(g_final.shape, lambda t, *_: (0, 0))]
    if final:
        out_specs = (pl.BlockSpec((TOK_TILE, D_MODEL), lambda t, *_: (jnp.minimum(t, npt - 1), 0)),
                     pl.BlockSpec((TOK_TILE, D_MODEL), lambda t, *_: (jnp.maximum(t - npt, 0), 0)))
        out_shape = (jax.ShapeDtypeStruct(hp.shape, F32), jax.ShapeDtypeStruct(hs.shape, F32))
    else:
        out_specs = pl.BlockSpec((TOK_TILE, D_MODEL), lambda t, *_: (t, 0))
        out_shape = jax.ShapeDtypeStruct((nt * TOK_TILE, D_MODEL), F32)
    grid_spec = pltpu.PrefetchScalarGridSpec(
        num_scalar_prefetch=4, grid=(nt,), in_specs=in_specs, out_specs=out_specs,
        scratch_shapes=[pltpu.VMEM((2, CBUF_ROWS, D_MODEL), BF16), pltpu.VMEM((D_MODEL, D_MODEL), BF16),
                        pltpu.VMEM((PLE_DIM, D_MODEL), BF16), pltpu.SemaphoreType.DMA((2,))])
    return pl.pallas_call(
        functools.partial(_combine_kernel, final=final, npt=npt), grid_spec=grid_spec, out_shape=out_shape,
        compiler_params=pltpu.CompilerParams(dimension_semantics=("arbitrary",), vmem_limit_bytes=VMEM_LIMIT),
        name="moe_combine_final" if final else "moe_combine",
    )(*plan[:4], hp, hs, pp, ps, pos, y, P["g_ple"], P["w_ple_gate"], P["w_ple_proj"], g_final)


def _ffn(hp, hs, pp, ps, l, P, w_rt, b_rt, g_final, *, final):
    n = hp.shape[0] + hs.shape[0]
    nt = n // TOK_TILE
    bound = TOP_K * n + nt * N_EXPERTS * (CHUNK - 1) + N_EXPERTS * (EXP_TILE - 1)
    n_sorted = -(-bound // EXP_TILE) * EXP_TILE
    pos, cnt = _route_call(hp, hs, l, P, w_rt, b_rt)
    plan = _chunk_plan(cnt[:, 0, :N_EXPERTS].astype(jnp.int32), n_sorted // EXP_TILE)
    xs = _dispatch_call(plan, hp, hs, pos, l, P, n_sorted)
    y = _expert_call(plan, xs, l, P["wg"], P["wu"], P["wd"])
    return _combine_call(plan, hp, hs, pp, ps, pos, y, l, P, g_final, final=final)


def _router_weights(l, w_grp_router, b_grp_router, w_exp_router, b_exp_router):
    w_er = jnp.transpose(w_exp_router[l], (1, 0, 2)).reshape(D_MODEL, N_EXPERTS)
    w_rt = jnp.zeros((D_MODEL, LANES), F32).at[:, :N_EXPERTS].set(w_er)
    w_rt = w_rt.at[:, N_EXPERTS:N_EXPERTS + N_GROUPS].set(w_grp_router[l])
    b_rt = jnp.zeros((1, LANES), F32).at[0, :N_EXPERTS].set(b_exp_router[l].reshape(-1))
    b_rt = b_rt.at[0, N_EXPERTS:N_EXPERTS + N_GROUPS].set(b_grp_router[l])
    w_hi = w_rt.astype(BF16)
    w_lo = (w_rt - w_hi.astype(F32)).astype(BF16)
    return jnp.concatenate([w_hi, w_lo], axis=1), b_rt


def kernel(x_prompt, x_sample, state_gla, state_conv, p_prompt, p_sample, g_mix, w_in, w_forget_up, b_forget,
           g_gla_out, w_conv, b_conv, g_conv_ln, b_conv_ln, w_out, g_ffn, w_grp_router, b_grp_router,
           w_exp_router, b_exp_router, w_exp_gate, w_exp_up, w_exp_down, g_ple, w_ple_gate, w_ple_proj, g_final):
    depth = w_in.shape[0]
    nbp, seq_p, _ = x_prompt.shape
    nbs, seq_s, _ = x_sample.shape
    n_p = nbp * seq_p
    n_s = nbs * seq_s

    def rows(v):
        return v.reshape(depth, 1, -1)

    P = {
        "g_mix": rows(g_mix), "w_in_t": jnp.swapaxes(w_in, 1, 2),
        "w_forget_up": w_forget_up, "b_forget": rows(b_forget), "g_gla_out": rows(g_gla_out),
        "w_conv": w_conv, "b_conv": rows(b_conv), "g_conv_ln": rows(g_conv_ln), "b_conv_ln": rows(b_conv_ln),
        "w_out": w_out, "g_ffn": rows(g_ffn), "g_ple": rows(g_ple),
        "w_ple_gate": w_ple_gate, "w_ple_proj": w_ple_proj,
        "wg": w_exp_gate.reshape(depth * N_EXPERTS, D_MODEL, EXPERT_FF),
        "wu": w_exp_up.reshape(depth * N_EXPERTS, D_MODEL, EXPERT_FF),
        "wd": w_exp_down.reshape(depth * N_EXPERTS, EXPERT_FF, D_MODEL),
    }
    g_fin = g_final.reshape(1, -1)
    xp = x_prompt.reshape(n_p, D_MODEL)
    xs = x_sample.reshape(n_s, D_MODEL)
    pp = p_prompt.reshape(depth, n_p, PLE_DIM)
    ps = p_sample.reshape(depth, n_s, PLE_DIM)
    s_in = state_gla.reshape(depth, nbs, QK_COLS, GLA_DV)
    c_in_t = jnp.swapaxes(state_conv, 1, 2)

    h = None
    sg_p, sg_s, sc_p, sc_s = [], [], [], []
    for l in range(depth):
        src_p, src_s, off_s = (xp, xs, 0) if l == 0 else (h, h, n_p)
        hp, sgp, scp = _guarded(functools.partial(_mixer_prompt, src_p, 0, nbp, seq_p, l, P))
        hs, sgs, scs = _guarded(functools.partial(_mixer_sample, src_s, off_s, nbs, seq_s, l, s_in, c_in_t, P))
        w_rt, b_rt = _router_weights(l, w_grp_router, b_grp_router, w_exp_router, b_exp_router)
        h = _ffn(hp, hs, pp, ps, l, P, w_rt, b_rt, g_fin, final=(l == depth - 1))
        sg_p.append(sgp.reshape(nbp, GLA_HEADS, GLA_DK, GLA_DV))
        sg_s.append(sgs.reshape(nbs, GLA_HEADS, GLA_DK, GLA_DV))
        sc_p.append(scp)
        sc_s.append(scs)

    y_prompt = h[0].reshape(nbp, seq_p, D_MODEL)
    y_sample = h[1].reshape(nbs, seq_s, D_MODEL)
    return (y_prompt, y_sample, jnp.stack(sg_p), jnp.stack(sg_s), jnp.stack(sc_p), jnp.swapaxes(jnp.stack(sc_s), 1, 2))
```

```python
import functools

import jax
import jax.numpy as jnp
from jax import lax
from jax.experimental import pallas as pl
from jax.experimental.pallas import tpu as pltpu

D_MODEL = 1024
GLA_HEADS = 4
GLA_DK = 64
GLA_DV = 128
QK_COLS = GLA_HEADS * GLA_DK
V_COLS = GLA_HEADS * GLA_DV
CONV_WIDTH = 512
CONV_K = 31
GLA_LOWRANK = 16
GLA_TAU = 16.0
GLA_CHUNK = 64
PLE_DIM = 256
N_GROUPS = 4
EXPERTS_PER_GROUP = 8
N_EXPERTS = N_GROUPS * EXPERTS_PER_GROUP
EXPERT_FF = 256
TOP_K = 2
EPS = 1e-6
N_MAIN = 2 * QK_COLS + 2 * V_COLS

LANES = 128
SUBLANES = 8
CONV_PAD = 32
CONV_OFF = CONV_PAD - (CONV_K - 1)
VMEM_LIMIT = 56 * 1024 * 1024
TOK_TILE = 512
CHUNK = 16
EXP_TILE = 512
CBUF_ROWS = -(-(TOP_K * TOK_TILE + N_EXPERTS * (CHUNK - 1)) // LANES) * LANES
XS_COLS = D_MODEL + LANES
DECAY_SAFE = 80.0

F32 = jnp.float32
BF16 = jnp.bfloat16
HI = lax.Precision.HIGHEST


def _sigmoid(x):
    return 1.0 / (1.0 + jnp.exp(-x))


def _silu(x):
    return x * _sigmoid(x)


def _log_sigmoid(x):
    return jnp.minimum(x, 0.0) - jnp.log(1.0 + jnp.exp(-jnp.abs(x)))


def _rmsnorm(x, g):
    return x * lax.rsqrt(jnp.mean(x * x, axis=-1, keepdims=True) + EPS) * g


def _dot(a, b):
    return jnp.dot(a.astype(BF16), b.astype(BF16), preferred_element_type=F32)


def _dot_t(a, b):
    return lax.dot_general(a.astype(BF16), b.astype(BF16), (((1,), (1,)), ((), ())),
                           preferred_element_type=F32)


def _dot_hi(a, b):
    return jnp.dot(a, b, preferred_element_type=F32, precision=HI)


def _iota_f32(shape, dim):
    return lax.broadcasted_iota(jnp.int32, shape, dim).astype(F32)


def _const_spec(shape):
    nd = len(shape)
    return pl.BlockSpec(shape, lambda *_: (0,) * nd)


def _layer_spec(arr, l):
    nd = arr.ndim - 1
    return pl.BlockSpec((None,) + arr.shape[1:], lambda *_: (l,) + (0,) * nd, pipeline_mode=pl.Buffered(1))


def _cast_mixer_weights(w_int_ref, w_fu_ref, w_out_ref, wmain_s, wlr_s, wcv_s, wfu_s, wout_s):
    blk = 4 * LANES
    for r in range(0, N_MAIN, blk):
        wmain_s[:, r:r + blk] = w_int_ref[r:r + blk, :].T.astype(BF16)
    lane = lax.broadcasted_iota(jnp.int32, (D_MODEL, LANES), 1)
    wlr_s[...] = jnp.where(lane < GLA_LOWRANK, w_int_ref[N_MAIN:N_MAIN + LANES, :].T, 0.0).astype(BF16)
    cv0 = N_MAIN + GLA_LOWRANK
    for r in range(0, 2 * CONV_WIDTH, blk):
        wcv_s[:, r:r + blk] = w_int_ref[cv0 + r:cv0 + r + blk, :].T.astype(BF16)
    wfu_s[...] = jnp.zeros_like(wfu_s)
    wfu_s[0:GLA_LOWRANK, :] = w_fu_ref[...].astype(BF16)
    wout_s[...] = w_out_ref[...].astype(BF16)


def _project(x, g_mix, wmain_s, wlr_s, wfu_s, b_f, wcv_s):
    xn = _rmsnorm(x, g_mix).astype(BF16)
    cv = jnp.dot(xn, wcv_s[...], preferred_element_type=F32)
    u = cv[:, :CONV_WIDTH] * _sigmoid(cv[:, CONV_WIDTH:])
    lr = jnp.dot(xn, wlr_s[...], preferred_element_type=F32)
    zf = _dot(lr, wfu_s[...]) + b_f
    la = _log_sigmoid(zf) * (1.0 / GLA_TAU)
    zqk = jnp.dot(xn, wmain_s[:, :2 * QK_COLS], preferred_element_type=F32)
    q = zqk[:, :QK_COLS] * (GLA_DK ** -0.5)
    k = zqk[:, QK_COLS:]
    zvo = jnp.dot(xn, wmain_s[:, 2 * QK_COLS:], preferred_element_type=F32)
    v = zvo[:, :V_COLS]
    og = zvo[:, V_COLS:]
    return q, k, v, og, la, u


def _stack_heads(qd):
    lane = lax.broadcasted_iota(jnp.int32, qd.shape, 1)
    return jnp.concatenate(
        [jnp.where((lane >= h * GLA_DK) & (lane < (h + 1) * GLA_DK), qd, 0.0) for h in range(GLA_HEADS)],
        axis=0)


def _gated_head_norm(o, og, g_gla):
    outs = []
    for h in range(GLA_HEADS):
        sl = slice(h * GLA_DV, (h + 1) * GLA_DV)
        outs.append(_rmsnorm(o[:, sl], g_gla) * _silu(og[:, sl]))
    return jnp.concatenate(outs, axis=1)


def _causal_conv(win, w_conv, b_conv, n):
    acc = jnp.broadcast_to(b_conv, (n, CONV_WIDTH))
    for s in range(SUBLANES):
        taps = [j for j in range(CONV_K) if (CONV_OFF + j) % SUBLANES == s]
        if not taps:
            continue
        rows = n if s == 0 else n + SUBLANES
        part = None
        for j in taps:
            a = (CONV_OFF + j) - s
            term = w_conv[j:j + 1, :] * win[a:a + rows, :]
            part = term if part is None else part + term
        acc = acc + part[s:s + n, :]
    return acc


def _conv_ln_act(acc, g_ln, b_ln):
    mu = jnp.mean(acc, axis=-1, keepdims=True)
    xc = acc - mu
    y = xc * lax.rsqrt(jnp.mean(xc * xc, axis=-1, keepdims=True) + EPS) * g_ln + b_ln
    return _silu(y)


def _head_diag(upd):
    return jnp.concatenate([upd[h * GLA_DK:(h + 1) * GLA_DK, h * GLA_DV:(h + 1) * GLA_DV]
                            for h in range(GLA_HEADS)], axis=0)


def _mixer_weight_args(l, P):
    names = ("g_mix", "w_in_t", "w_forget_up", "b_forget", "g_gla_out", "w_conv", "b_conv", "g_conv_ln",
             "b_conv_ln", "w_out")
    arrs = [P[n] for n in names]
    return arrs, [_layer_spec(a, l) for a in arrs]


_MIXER_WEIGHT_SCRATCH = [pltpu.VMEM((D_MODEL, N_MAIN), BF16), pltpu.VMEM((D_MODEL, LANES), BF16),
                         pltpu.VMEM((D_MODEL, 2 * CONV_WIDTH), BF16), pltpu.VMEM((LANES, QK_COLS), BF16),
                         pltpu.VMEM((D_MODEL, D_MODEL), BF16)]


def _mixer_prompt_kernel(x_ref, g_mix_ref, w_int_ref, w_fu_ref, b_f_ref, g_gla_ref, w_conv_ref,
                         b_conv_ref, g_ln_ref, b_ln_ref, w_out_ref,
                         h_ref, sg_ref, sc_ref, risk_ref,
                         wmain_s, wlr_s, wcv_s, wfu_s, wout_s,
                         s_ref, ubuf_ref, qs_ref, kd_ref, klt_ref, dec_ref, mid_ref, v_ref, og_ref, mix_ref,
                         *stable_refs, tt, stable):
    t = pl.program_id(1)
    nt = pl.num_programs(1)
    C = GLA_CHUNK
    n_chunks = tt // C

    @pl.when((pl.program_id(0) == 0) & (t == 0))
    def _():
        _cast_mixer_weights(w_int_ref, w_fu_ref, w_out_ref, wmain_s, wlr_s, wcv_s, wfu_s, wout_s)

    @pl.when(t == 0)
    def _():
        s_ref[...] = jnp.zeros_like(s_ref)
        ubuf_ref[0:CONV_PAD, :] = jnp.zeros((CONV_PAD, CONV_WIDTH), F32)

    x = x_ref[...]
    q, k, v, og, la, u = _project(x, g_mix_ref[...], wmain_s, wlr_s, wfu_s, b_f_ref[...], wcv_s)
    ubuf_ref[CONV_PAD:CONV_PAD + tt, :] = u
    v_ref[...] = v.astype(BF16)
    og_ref[...] = _silu(og)

    w_conv = w_conv_ref[...]
    b_conv = b_conv_ref[...]
    g_ln = g_ln_ref[...]
    b_ln = b_ln_ref[...]
    row = lax.broadcasted_iota(jnp.int32, (C, C), 0)
    col = lax.broadcasted_iota(jnp.int32, (C, C), 1)
    tri = (col <= row).astype(F32)
    risk = jnp.zeros((1, QK_COLS), F32)
    for c in range(n_chunks):
        rows = slice(c * C, (c + 1) * C)
        win = ubuf_ref[c * C:c * C + C + CONV_PAD, :]
        cact = _conv_ln_act(_causal_conv(win, w_conv, b_conv, C), g_ln, b_ln)
        mix_ref[rows, V_COLS:] = cact.astype(BF16)
        bits = pltpu.bitcast(cact[C - SUBLANES:C, 0:QK_COLS], jnp.uint32)
        zero = pltpu.bitcast(lax.shift_right_logical(lax.shift_right_logical(bits, jnp.uint32(16)), jnp.uint32(16)),
                             F32)[0:1, :]
        qc = q[rows, :] + zero
        b = _dot_hi(tri, la[rows, :])
        b_last = b[C - 1:C, :]
        if stable:
            q_st, k_st, b_st = stable_refs
            q_st[rows, :] = qc
            k_st[rows, :] = k[rows, :]
            b_st[rows, :] = b
            b_mid = jnp.zeros_like(b_last)
        else:
            b_mid = b[C // 2 - 1:C // 2, :]
            risk = jnp.maximum(risk, jnp.maximum(-b_mid, b_mid - b_last))
        qs_ref[c] = _stack_heads(qc * jnp.exp(b - b_mid)).astype(BF16)
        kd_ref[rows, :] = (k[rows, :] * jnp.exp(jnp.minimum(b_mid - b, DECAY_SAFE))).astype(BF16)
        kl = k[rows, :] * jnp.exp(b_last - b)
        klt = jnp.concatenate([kl, jnp.broadcast_to(jnp.exp(b_last), (C // 2, QK_COLS)),
                               jnp.broadcast_to(jnp.exp(b_mid), (C // 2, QK_COLS))], axis=0).T
        klt_ref[c] = klt.astype(BF16)
        dec_ref[c] = jnp.broadcast_to(klt[:, C:C + 1], (QK_COLS, GLA_DV))
        mid_ref[c] = jnp.broadcast_to(klt[:, 3 * C // 2:3 * C // 2 + 1], (QK_COLS, GLA_DV))

    r4 = lax.broadcasted_iota(jnp.int32, (GLA_HEADS * C, C), 0)
    c4 = lax.broadcasted_iota(jnp.int32, (GLA_HEADS * C, C), 1)
    causal4 = c4 <= (r4 % C)
    g_gla = g_gla_ref[...]
    s = s_ref[...]
    for c in range(n_chunks):
        rows = slice(c * C, (c + 1) * C)
        qs = qs_ref[c]
        vc = v_ref[rows, :]
        if stable:
            scores = _direct_scores(*stable_refs, c * C, C).astype(BF16)
        else:
            scores = jnp.where(causal4, _dot_t(qs, kd_ref[rows, :]), 0.0).astype(BF16)
        o_inter = jnp.dot(qs, (mid_ref[c] * s).astype(BF16), preferred_element_type=F32)
        upd = jnp.dot(klt_ref[c][:, :C], vc, preferred_element_type=F32)
        s = dec_ref[c] * s + _head_diag(upd)
        o_parts = []
        for h in range(GLA_HEADS):
            vh = vc[:, h * GLA_DV:(h + 1) * GLA_DV]
            o_parts.append(jnp.dot(scores[h * C:(h + 1) * C, :], vh, preferred_element_type=F32)
                           + o_inter[h * C:(h + 1) * C, :])
        o = jnp.concatenate(o_parts, axis=1)
        gated = []
        for h in range(GLA_HEADS):
            sl = slice(h * GLA_DV, (h + 1) * GLA_DV)
            gated.append(_rmsnorm(o[:, sl], g_gla) * og_ref[rows, sl])
        mix_ref[rows, 0:V_COLS] = jnp.concatenate(gated, axis=1).astype(BF16)
    s_ref[...] = s

    h_ref[...] = x + jnp.dot(mix_ref[...], wout_s[...], preferred_element_type=F32)
    tail = ubuf_ref[tt:tt + CONV_PAD, :]
    ubuf_ref[0:CONV_PAD, :] = tail

    risk_ref[0] = jnp.broadcast_to(jnp.max(risk, axis=-1, keepdims=True), (SUBLANES, LANES))

    @pl.when(t == nt - 1)
    def _():
        sg_ref[0] = s
        sc_ref[0] = tail[CONV_OFF:, :]


def _direct_scores(q_st, k_st, b_st, r0, C):
    kc = k_st[r0:r0 + C, :]
    bc = b_st[r0:r0 + C, :]
    srow = lax.broadcasted_iota(jnp.int32, (C, QK_COLS), 0)
    lane = lax.broadcasted_iota(jnp.int32, (C, LANES), 1)
    head_of = lax.broadcasted_iota(jnp.int32, (QK_COLS, LANES), 0) // GLA_DK
    head_sum = (head_of == lax.broadcasted_iota(jnp.int32, (QK_COLS, LANES), 1)).astype(F32)

    def one_query(t, acc):
        d = b_st[pl.ds(r0 + t, 1), :] - bc
        w = jnp.where(srow <= t, jnp.exp(jnp.minimum(d, 0.0)), 0.0) * kc * q_st[pl.ds(r0 + t, 1), :]
        per_head = _dot_hi(w, head_sum)
        return tuple(jnp.where(lane == t, per_head[:, h:h + 1], acc[h]) for h in range(GLA_HEADS))

    acc = lax.fori_loop(0, C, one_query, tuple(jnp.zeros((C, LANES), F32) for _ in range(GLA_HEADS)))
    return jnp.concatenate([a.T[0:C, :] for a in acc], axis=0)


def _mixer_prompt(x, row_off, nb, seq, l, P, *, tt=512, stable=False):
    nt = seq // tt
    n_chunks = tt // GLA_CHUNK
    blk_off = row_off // tt
    weights, w_specs = _mixer_weight_args(l, P)
    in_specs = [pl.BlockSpec((tt, D_MODEL), lambda b, t: (blk_off + b * nt + t, 0))] + w_specs
    out_shape = (jax.ShapeDtypeStruct((nb * seq, D_MODEL), F32),
                 jax.ShapeDtypeStruct((nb, QK_COLS, GLA_DV), F32),
                 jax.ShapeDtypeStruct((nb, CONV_K - 1, CONV_WIDTH), F32),
                 jax.ShapeDtypeStruct((nb * nt, SUBLANES, LANES), F32))
    out_specs = (pl.BlockSpec((tt, D_MODEL), lambda b, t: (b * nt + t, 0)),
                 pl.BlockSpec((1, QK_COLS, GLA_DV), lambda b, t: (b, 0, 0)),
                 pl.BlockSpec((1, CONV_K - 1, CONV_WIDTH), lambda b, t: (b, 0, 0)),
                 pl.BlockSpec((1, SUBLANES, LANES), lambda b, t: (b * nt + t, 0, 0)))
    scratch = _MIXER_WEIGHT_SCRATCH + [
        pltpu.VMEM((QK_COLS, GLA_DV), F32),
        pltpu.VMEM((CONV_PAD + tt, CONV_WIDTH), F32),
        pltpu.VMEM((n_chunks, GLA_HEADS * GLA_CHUNK, QK_COLS), BF16), pltpu.VMEM((tt, QK_COLS), BF16),
        pltpu.VMEM((n_chunks, QK_COLS, 2 * GLA_CHUNK), BF16), pltpu.VMEM((n_chunks, QK_COLS, GLA_DV), F32),
        pltpu.VMEM((n_chunks, QK_COLS, GLA_DV), F32),
        pltpu.VMEM((tt, V_COLS), BF16), pltpu.VMEM((tt, V_COLS), F32),
        pltpu.VMEM((tt, D_MODEL), BF16)]
    if stable:
        scratch += [pltpu.VMEM((tt, QK_COLS), F32)] * 3
    return pl.pallas_call(
        functools.partial(_mixer_prompt_kernel, tt=tt, stable=stable),
        grid=(nb, nt), in_specs=in_specs, out_specs=out_specs, out_shape=out_shape,
        scratch_shapes=scratch,
        compiler_params=pltpu.CompilerParams(dimension_semantics=("arbitrary", "arbitrary"),
                                             vmem_limit_bytes=VMEM_LIMIT),
        name="mixer_prompt_stable" if stable else "mixer_prompt",
    )(x, *weights)


def _guarded(mixer):
    *outs, risk = mixer(stable=False)
    return lax.cond(jnp.max(risk) > DECAY_SAFE, lambda: tuple(mixer(stable=True)[:-1]), lambda: tuple(outs))


def _mixer_sample_kernel(x_ref, s_in_ref, c_in_ref, g_mix_ref, w_int_ref, w_fu_ref, b_f_ref,
                         g_gla_ref, w_conv_ref, b_conv_ref, g_ln_ref, b_ln_ref, w_out_ref,
                         h_ref, sg_ref, sc_ref, risk_ref,
                         wmain_s, wlr_s, wcv_s, wfu_s, wout_s,
                         u4_ref, oi_ref, cacc4_ref, *, sb, seq, stable):
    R = sb * seq
    n_slabs = CONV_WIDTH // LANES

    @pl.when(pl.program_id(0) == 0)
    def _():
        _cast_mixer_weights(w_int_ref, w_fu_ref, w_out_ref, wmain_s, wlr_s, wcv_s, wfu_s, wout_s)

    x = x_ref[...]
    q, k, v, og, la, u = _project(x, g_mix_ref[...], wmain_s, wlr_s, wfu_s, b_f_ref[...], wcv_s)

    for kk in range(n_slabs):
        u4_ref[kk] = u[:, kk * LANES:(kk + 1) * LANES]
    full = [c_in_ref[j] for j in range(CONV_K - 1)]
    for t in range(seq):
        full.append(jnp.concatenate([u4_ref.at[kk][pl.ds(t, sb, stride=seq), :] for kk in range(n_slabs)], axis=1))
    w_conv = w_conv_ref[...]
    for t in range(seq):
        acc = jnp.broadcast_to(b_conv_ref[...], (sb, CONV_WIDTH))
        for j in range(CONV_K):
            acc = acc + w_conv[j:j + 1, :] * full[t + j]
        for kk in range(n_slabs):
            cacc4_ref.at[kk][pl.ds(t, sb, stride=seq), :] = acc[:, kk * LANES:(kk + 1) * LANES]
    for j in range(CONV_K - 1):
        sc_ref[j] = full[seq + j]

    row = lax.broadcasted_iota(jnp.int32, (R, R), 0)
    col = lax.broadcasted_iota(jnp.int32, (R, R), 1)
    same = (row // seq) == (col // seq)
    b = _dot_hi((same & (col <= row)).astype(F32), la)
    b_tot = _dot_hi(same.astype(F32), la)
    qd = q * jnp.exp(b)
    kl = k * jnp.exp(b_tot - b)
    qs = _stack_heads(qd)
    risk_ref[0] = jnp.broadcast_to(jnp.max(jnp.max(-b_tot, axis=-1, keepdims=True), axis=0, keepdims=True),
                                   (SUBLANES, LANES))
    if stable:
        head_of = lax.broadcasted_iota(jnp.int32, (QK_COLS, LANES), 0) // GLA_DK
        head_sum = (head_of == lax.broadcasted_iota(jnp.int32, (QK_COLS, LANES), 1)).astype(F32)
        t_in_seq = lax.broadcasted_iota(jnp.int32, (R, QK_COLS), 0) % seq
        parts = [jnp.zeros((R, R), F32) for _ in range(GLA_HEADS)]
        for d in range(seq):
            k_d = jnp.concatenate([jnp.zeros((d, QK_COLS), F32), k[:R - d, :]], axis=0) if d else k
            b_d = jnp.concatenate([jnp.zeros((d, QK_COLS), F32), b[:R - d, :]], axis=0) if d else b
            w = jnp.where(t_in_seq >= d, jnp.exp(jnp.minimum(b - b_d, 0.0)), 0.0) * k_d * q
            per_head = _dot_hi(w, head_sum)
            for h in range(GLA_HEADS):
                parts[h] = jnp.where(col == row - d, per_head[:, h:h + 1], parts[h])
        scores = jnp.concatenate(parts, axis=0)
    else:
        kd = k * jnp.exp(jnp.minimum(-b, DECAY_SAFE))
        r4 = lax.broadcasted_iota(jnp.int32, (GLA_HEADS * R, R), 0) % R
        c4 = lax.broadcasted_iota(jnp.int32, (GLA_HEADS * R, R), 1)
        mask4 = ((r4 // seq) == (c4 // seq)) & (c4 <= r4)
        scores = jnp.where(mask4, _dot_t(qs, kd), 0.0)

    klt = kl.T
    dect = jnp.exp(b_tot).T
    lane_h = lax.broadcasted_iota(jnp.int32, (GLA_DK, R), 1)
    lane_r = lax.broadcasted_iota(jnp.int32, (QK_COLS, R), 1)
    upd = []
    for h in range(GLA_HEADS):
        klt_h = klt[h * GLA_DK:(h + 1) * GLA_DK, :]
        lhs = jnp.concatenate([jnp.where((lane_h >= i * seq) & (lane_h < (i + 1) * seq), klt_h, 0.0)
                               for i in range(sb)], axis=0)
        upd.append(_dot(lhs, v[:, h * GLA_DV:(h + 1) * GLA_DV]))
    for i in range(sb):
        s_old = s_in_ref[i]
        qsel = jnp.concatenate([qs[h * R + i * seq:h * R + (i + 1) * seq, :] for h in range(GLA_HEADS)], axis=0)
        oi = _dot(qsel, s_old)
        for h in range(GLA_HEADS):
            oi_ref[h, i * seq:(i + 1) * seq, :] = oi[h * seq:(h + 1) * seq, :]
        smask = (lane_r >= i * seq) & (lane_r < (i + 1) * seq)
        dec = jnp.sum(jnp.where(smask, dect, 0.0), axis=1, keepdims=True) * (1.0 / seq)
        u_new = jnp.concatenate([upd[h][i * GLA_DK:(i + 1) * GLA_DK, :] for h in range(GLA_HEADS)], axis=0)
        sg_ref[i] = dec * s_old + u_new

    o_parts = []
    for h in range(GLA_HEADS):
        vh = v[:, h * GLA_DV:(h + 1) * GLA_DV]
        o_parts.append(_dot(scores[h * R:(h + 1) * R, :], vh) + oi_ref[h])
    o = jnp.concatenate(o_parts, axis=1)
    cacc = jnp.concatenate([cacc4_ref[kk] for kk in range(n_slabs)], axis=1)
    mix = jnp.concatenate([_gated_head_norm(o, og, g_gla_ref[...]),
                           _conv_ln_act(cacc, g_ln_ref[...], b_ln_ref[...])], axis=1)
    h_ref[...] = x + jnp.dot(mix.astype(BF16), wout_s[...], preferred_element_type=F32)


def _mixer_sample(x, row_off, nb, seq, l, s_in, c_in_t, P, *, sb=16, stable=False):
    R = sb * seq
    blk_off = row_off // R
    n_slabs = CONV_WIDTH // LANES
    weights, w_specs = _mixer_weight_args(l, P)
    in_specs = [pl.BlockSpec((R, D_MODEL), lambda i: (blk_off + i, 0)),
                pl.BlockSpec((None, sb, QK_COLS, GLA_DV), lambda i: (l, i, 0, 0)),
                pl.BlockSpec((None, CONV_K - 1, sb, CONV_WIDTH), lambda i: (l, 0, i, 0))] + w_specs
    out_shape = (jax.ShapeDtypeStruct((nb * seq, D_MODEL), F32),
                 jax.ShapeDtypeStruct((nb, QK_COLS, GLA_DV), F32),
                 jax.ShapeDtypeStruct((CONV_K - 1, nb, CONV_WIDTH), F32),
                 jax.ShapeDtypeStruct((nb // sb, SUBLANES, LANES), F32))
    out_specs = (pl.BlockSpec((R, D_MODEL), lambda i: (i, 0)),
                 pl.BlockSpec((sb, QK_COLS, GLA_DV), lambda i: (i, 0, 0)),
                 pl.BlockSpec((CONV_K - 1, sb, CONV_WIDTH), lambda i: (0, i, 0)),
                 pl.BlockSpec((1, SUBLANES, LANES), lambda i: (i, 0, 0)))
    scratch = _MIXER_WEIGHT_SCRATCH + [
        pltpu.VMEM((n_slabs, R, LANES), F32),
        pltpu.VMEM((GLA_HEADS, R, GLA_DV), F32),
        pltpu.VMEM((n_slabs, R, LANES), F32)]
    return pl.pallas_call(
        functools.partial(_mixer_sample_kernel, sb=sb, seq=seq, stable=stable),
        grid=(nb // sb,), in_specs=in_specs, out_specs=out_specs, out_shape=out_shape,
        scratch_shapes=scratch,
        compiler_params=pltpu.CompilerParams(dimension_semantics=("arbitrary",),
                                             vmem_limit_bytes=VMEM_LIMIT),
        name="mixer_sample_stable" if stable else "mixer_sample",
    )(x, s_in, c_in_t, *weights)


def _pair_specs(n_first_tiles, width):
    return [pl.BlockSpec((TOK_TILE, width), lambda t, *_: (jnp.minimum(t, n_first_tiles - 1), 0)),
            pl.BlockSpec((TOK_TILE, width), lambda t, *_: (jnp.maximum(t - n_first_tiles, 0), 0))]


def _pick(t, n_first_tiles, a_ref, b_ref):
    return jnp.where(t < n_first_tiles, a_ref[...], b_ref[...])


def _route(logits):
    lane = lax.broadcasted_iota(jnp.int32, logits.shape, 1)
    lane_f = lane.astype(F32)
    neg = jnp.float32(-jnp.inf)
    big = jnp.float32(1e9)
    is_grp = (lane >= N_EXPERTS) & (lane < N_EXPERTS + N_GROUPS)
    gl = jnp.where(is_grp, logits, neg)
    gmax = jnp.max(gl, axis=-1, keepdims=True)
    gidx = jnp.min(jnp.where(is_grp & (gl == gmax), lane_f - N_EXPERTS, big), axis=-1, keepdims=True)
    gsum = jnp.sum(jnp.where(is_grp, jnp.exp(gl - gmax), 0.0), axis=-1, keepdims=True)
    g_w = 1.0 / gsum
    grp_of_lane = jnp.floor(lane_f * (1.0 / EXPERTS_PER_GROUP))
    in_grp = (lane < N_EXPERTS) & (grp_of_lane == gidx)
    ml = jnp.where(in_grp, logits, neg)
    v1 = jnp.max(ml, axis=-1, keepdims=True)
    i1 = jnp.min(jnp.where(in_grp & (ml == v1), lane_f, big), axis=-1, keepdims=True)
    ml2 = jnp.where(lane_f == i1, neg, ml)
    v2 = jnp.max(ml2, axis=-1, keepdims=True)
    i2 = jnp.min(jnp.where(in_grp & (ml2 == v2), lane_f, big), axis=-1, keepdims=True)
    e2 = jnp.exp(v2 - v1)
    w1 = g_w / (1.0 + e2)
    w2 = g_w * e2 / (1.0 + e2)
    return i1, i2, w1, w2


def _route_kernel(hp_ref, hs_ref, g_ffn_ref, w_rt_ref, b_rt_ref, pos_ref, cnt_ref, *, npt):
    T = TOK_TILE
    xn = _rmsnorm(_pick(pl.program_id(0), npt, hp_ref, hs_ref), g_ffn_ref[...])
    x_hi = xn.astype(BF16)
    x_lo = (xn - x_hi.astype(F32)).astype(BF16)
    w_split = w_rt_ref[...]
    both = jnp.dot(x_hi, w_split, preferred_element_type=F32)
    logits = (both[:, :LANES] + both[:, LANES:] + jnp.dot(x_lo, w_split[:, :LANES], preferred_element_type=F32)
              + b_rt_ref[...])
    i1, i2, w1, w2 = _route(logits)
    lane = _iota_f32((T, LANES), 1)
    a0 = (lane == i1).astype(F32)
    a1 = (lane == i2).astype(F32)
    a = a0 + a1
    cnt = jnp.sum(a, axis=0, keepdims=True)
    earlier = (_iota_f32((T, T), 1) < _iota_f32((T, T), 0)).astype(BF16)
    rank = jnp.dot(earlier, a.astype(BF16), preferred_element_type=F32)
    cnt_pad = jnp.ceil(cnt * (1.0 / CHUNK)) * CHUNK
    below = (_iota_f32((LANES, LANES), 0) < _iota_f32((LANES, LANES), 1)).astype(F32)
    first = _dot_hi(jnp.broadcast_to(cnt_pad, (SUBLANES, LANES)), below)[0:1, :]
    base = first + rank
    pos0 = jnp.sum(a0 * base, axis=1, keepdims=True)
    pos1 = jnp.sum(a1 * base, axis=1, keepdims=True)
    pos_ref[...] = jnp.where(lane == 0.0, pos0, jnp.where(lane == 1.0, pos1, jnp.where(
        lane == 2.0, w1, jnp.where(lane == 3.0, w2, jnp.where(lane == 4.0, i1, 0.0)))))
    cnt_ref[0] = jnp.broadcast_to(cnt, (SUBLANES, LANES))


def _route_call(hp, hs, l, P, w_rt, b_rt):
    npt = hp.shape[0] // TOK_TILE
    nt = npt + hs.shape[0] // TOK_TILE
    return pl.pallas_call(
        functools.partial(_route_kernel, npt=npt), grid=(nt,),
        in_specs=_pair_specs(npt, D_MODEL) + [_layer_spec(P["g_ffn"], l), _const_spec(w_rt.shape),
                                              _const_spec(b_rt.shape)],
        out_specs=(pl.BlockSpec((TOK_TILE, LANES), lambda t: (t, 0)),
                   pl.BlockSpec((1, SUBLANES, LANES), lambda t: (t, 0, 0))),
        out_shape=(jax.ShapeDtypeStruct((nt * TOK_TILE, LANES), F32),
                   jax.ShapeDtypeStruct((nt, SUBLANES, LANES), F32)),
        compiler_params=pltpu.CompilerParams(dimension_semantics=("arbitrary",), vmem_limit_bytes=VMEM_LIMIT),
        name="moe_route",
    )(hp, hs, P["g_ffn"], w_rt, b_rt)


def _chunk_plan(cnt, n_row_tiles):
    n16 = (cnt + (CHUNK - 1)) // CHUNK
    lofs16 = jnp.cumsum(n16, axis=1) - n16
    tile_pref16 = jnp.cumsum(n16, axis=0) - n16
    tot16 = jnp.sum(n16, axis=0)
    per_tile = EXP_TILE // CHUNK
    seg16 = ((tot16 + per_tile - 1) // per_tile) * per_tile
    seg_end16 = jnp.cumsum(seg16)
    dst16 = (seg_end16 - seg16)[None, :] + tile_pref16
    n_tot = jnp.sum(n16, axis=1)
    gap16 = seg_end16 - seg16 + tot16
    gapn16 = seg16 - tot16
    tile_start16 = jnp.arange(n_row_tiles, dtype=jnp.int32) * per_tile
    n_valid = seg_end16[-1] // per_tile
    misc = n_valid.reshape(1)
    exp_of_tile = jnp.minimum(jnp.sum(seg_end16[None, :] <= tile_start16[:, None], axis=1), N_EXPERTS - 1)
    i32 = lambda a: a.astype(jnp.int32).reshape(-1)
    return (i32(dst16), i32(n16), i32(lofs16), i32(n_tot), i32(gap16), i32(gapn16), i32(misc), i32(exp_of_tile))


def _chunk_copy(src, dst, src_chunk, dst_chunk, sem, n_chunks=1):
    rows = n_chunks * CHUNK
    return pltpu.make_async_copy(src.at[pl.ds(pl.multiple_of(src_chunk * CHUNK, CHUNK), rows), :],
                                 dst.at[pl.ds(pl.multiple_of(dst_chunk * CHUNK, CHUNK), rows), :], sem)


def _slab_copies(src, dst, sem, tile, src_ofs_ref, dst_ofs_ref, n16_ref):
    for e in range(N_EXPERTS):
        k = tile * N_EXPERTS + e
        n = n16_ref[k]

        @pl.when(n > 0)
        def _(k=k, n=n):
            _chunk_copy(src, dst, src_ofs_ref[k], dst_ofs_ref[k], sem, n).start()


def _wait_slabs(src, dst, sem, n_chunks):
    @pl.when(n_chunks > 0)
    def _():
        _chunk_copy(src, dst, 0, 0, sem, n_chunks).wait()


def _tile_copy(src, dst, dst_tile, sem):
    return pltpu.make_async_copy(src, dst.at[pl.ds(pl.multiple_of(dst_tile * EXP_TILE, EXP_TILE), EXP_TILE), :], sem)


def _dispatch_kernel(dst16_ref, n16_ref, lofs16_ref, ntot_ref, gap16_ref, gapn16_ref, misc_ref,
                     hp_ref, hs_ref, pos_ref, g_ffn_ref, xs_hbm, cbuf, zbuf, sem, *, n_row_tiles, npt):
    t = pl.program_id(0)
    nt = pl.num_programs(0)
    slot = t % 2
    T = TOK_TILE
    n_tail = n_row_tiles - misc_ref[0]

    @pl.when(t == 0)
    def _():
        zbuf[...] = jnp.zeros_like(zbuf)
        for e in range(N_EXPERTS):
            g = gapn16_ref[e]

            @pl.when(g > 0)
            def _(e=e, g=g):
                _chunk_copy(zbuf, xs_hbm, 0, gap16_ref[e], sem.at[2], g).start()

        def fill_tile(i, carry):
            _tile_copy(zbuf, xs_hbm, misc_ref[0] + i, sem.at[2]).start()
            return carry
        lax.fori_loop(0, n_tail, fill_tile, 0)

    xn = _rmsnorm(_pick(t, npt, hp_ref, hs_ref), g_ffn_ref[...]).astype(BF16)
    pos = pos_ref[...]
    pos_t = pos.T
    rows = _iota_f32((CBUF_ROWS, T), 0)
    onehot = jnp.where((rows == pos_t[0:1, :]) | (rows == pos_t[1:2, :]), 1.0, 0.0).astype(BF16)
    lane = lax.broadcasted_iota(jnp.int32, (T, LANES), 1)
    extra = jnp.zeros((T, LANES), F32)
    for s in range(TOP_K):
        c = pos[:, 2 + s:3 + s]
        hi = c.astype(BF16).astype(F32)
        mid = (c - hi).astype(BF16).astype(F32)
        lo = c - hi - mid
        for j, piece in enumerate((hi, mid, lo)):
            extra = jnp.where(lane == 3 * s + j, piece, extra)
    extra = jnp.where(lane == 3 * TOP_K, pos[:, 4:5], extra)
    cbuf[slot] = jnp.dot(onehot, jnp.concatenate([xn, extra.astype(BF16)], axis=1),
                         preferred_element_type=F32).astype(BF16)

    src = cbuf.at[slot]
    _slab_copies(src, xs_hbm, sem.at[slot], t, lofs16_ref, dst16_ref, n16_ref)

    @pl.when(t > 0)
    def _():
        _wait_slabs(cbuf.at[1 - slot], xs_hbm, sem.at[1 - slot], ntot_ref[t - 1])

    @pl.when(t == nt - 1)
    def _():
        _wait_slabs(src, xs_hbm, sem.at[slot], ntot_ref[t])
        for e in range(N_EXPERTS):
            _wait_slabs(zbuf, xs_hbm, sem.at[2], gapn16_ref[e])

        def wait_tile(_, carry):
            _tile_copy(zbuf, xs_hbm, 0, sem.at[2]).wait()
            return carry
        lax.fori_loop(0, n_tail, wait_tile, 0)


def _dispatch_call(plan, hp, hs, pos, l, P, n_sorted):
    npt = hp.shape[0] // TOK_TILE
    nt = npt + hs.shape[0] // TOK_TILE
    g_ffn = P["g_ffn"]
    grid_spec = pltpu.PrefetchScalarGridSpec(
        num_scalar_prefetch=7, grid=(nt,),
        in_specs=_pair_specs(npt, D_MODEL) + [
            pl.BlockSpec((TOK_TILE, LANES), lambda t, *_: (t, 0)),
            pl.BlockSpec((None,) + g_ffn.shape[1:], lambda t, *_: (l, 0, 0))],
        out_specs=pl.BlockSpec(memory_space=pl.ANY),
        scratch_shapes=[pltpu.VMEM((2, CBUF_ROWS, XS_COLS), BF16), pltpu.VMEM((EXP_TILE, XS_COLS), BF16),
                        pltpu.SemaphoreType.DMA((3,))])
    return pl.pallas_call(
        functools.partial(_dispatch_kernel, n_row_tiles=n_sorted // EXP_TILE, npt=npt), grid_spec=grid_spec,
        out_shape=jax.ShapeDtypeStruct((n_sorted, XS_COLS), BF16),
        compiler_params=pltpu.CompilerParams(dimension_semantics=("arbitrary",), vmem_limit_bytes=VMEM_LIMIT),
        name="moe_dispatch",
    )(*plan[:7], hp, hs, pos, g_ffn)


def _expert_kernel(eot_ref, misc_ref, xs_ref, wg_ref, wu_ref, wd_ref, y_ref, wg_s, wu_s, wd_s):
    i = pl.program_id(0)
    valid = i < misc_ref[0]

    @pl.when(jnp.logical_not(valid))
    def _():
        y_ref[...] = jnp.zeros_like(y_ref)

    @pl.when(valid & ((i == 0) | (eot_ref[i] != eot_ref[jnp.maximum(i - 1, 0)])))
    def _():
        wg_s[...] = wg_ref[...].astype(BF16)
        wu_s[...] = wu_ref[...].astype(BF16)
        wd_s[...] = wd_ref[...].astype(BF16)

    @pl.when(valid)
    def _():
        xs = xs_ref[...]
        x = xs[:, :D_MODEL]
        ex = xs[:, D_MODEL:].astype(F32)
        lane = lax.broadcasted_iota(jnp.int32, ex.shape, 1)
        id0 = jnp.sum(jnp.where(lane == 3 * TOP_K, ex, 0.0), axis=-1, keepdims=True)
        first = id0 == eot_ref[i].astype(F32)
        mine = (first & (lane < 3)) | (jnp.logical_not(first) & (lane >= 3) & (lane < 3 * TOP_K))
        c = jnp.sum(jnp.where(mine, ex, 0.0), axis=-1, keepdims=True)
        hg = _silu(jnp.dot(x, wg_s[...], preferred_element_type=F32)) * jnp.dot(x, wu_s[...],
                                                                               preferred_element_type=F32)
        y_ref[...] = jnp.dot((hg * c).astype(BF16), wd_s[...], preferred_element_type=F32).astype(BF16)


def _expert_call(plan, xs, l, wg, wu, wd):
    misc, exp_of_tile = plan[6:]
    n_row_tiles = xs.shape[0] // EXP_TILE

    def last_valid(i, nv):
        return jnp.maximum(jnp.minimum(i, nv[0] - 1), 0)

    def row_map(i, eot, nv):
        return (last_valid(i, nv), 0)

    def w_map(i, eot, nv):
        return (l * N_EXPERTS + eot[last_valid(i, nv)], 0, 0)

    grid_spec = pltpu.PrefetchScalarGridSpec(
        num_scalar_prefetch=2, grid=(n_row_tiles,),
        in_specs=[pl.BlockSpec((EXP_TILE, XS_COLS), row_map),
                  pl.BlockSpec((None, D_MODEL, EXPERT_FF), w_map),
                  pl.BlockSpec((None, D_MODEL, EXPERT_FF), w_map),
                  pl.BlockSpec((None, EXPERT_FF, D_MODEL), w_map)],
        out_specs=pl.BlockSpec((EXP_TILE, D_MODEL), lambda i, eot, nv: (i, 0)),
        scratch_shapes=[pltpu.VMEM((D_MODEL, EXPERT_FF), BF16), pltpu.VMEM((D_MODEL, EXPERT_FF), BF16),
                        pltpu.VMEM((EXPERT_FF, D_MODEL), BF16)])
    return pl.pallas_call(
        _expert_kernel, grid_spec=grid_spec,
        out_shape=jax.ShapeDtypeStruct((xs.shape[0], D_MODEL), BF16),
        compiler_params=pltpu.CompilerParams(dimension_semantics=("arbitrary",), vmem_limit_bytes=VMEM_LIMIT),
        name="moe_experts",
    )(exp_of_tile, misc, xs, wg, wu, wd)


def _state_copies(srcs, dsts, sem):
    depth = len(srcs) // len(dsts)
    return [pltpu.make_async_copy(srcs[k * depth + l], dsts[k].at[l], sem.at[k * depth + l])
            for k in range(len(dsts)) for l in range(depth)]


def _combine_kernel(dst16_ref, n16_ref, lofs16_ref, ntot_ref, hp_ref, hs_ref, pp_ref, ps_ref, pos_ref, y_hbm,
                    g_ple_ref, w_pg_ref, w_pp_ref, g_fin_ref, *rest, final, npt, n_state):
    if final:
        st_in, (op_ref, os_ref, *st_out, ybuf, wpg_s, wpp_s, sem, st_sem) = rest[:n_state], rest[n_state:]
        state_copies = _state_copies(st_in, st_out, st_sem)
    else:
        o_ref, ybuf, wpg_s, wpp_s, sem = rest
        state_copies = []
    t = pl.program_id(0)
    nt = pl.num_programs(0)
    slot = t % 2

    def fetch(tile, sl):
        _slab_copies(y_hbm, ybuf.at[sl], sem.at[sl], tile, dst16_ref, lofs16_ref, n16_ref)

    @pl.when(t == 0)
    def _():
        ybuf[...] = jnp.zeros_like(ybuf)
        fetch(t, slot)
        for c in state_copies:
            c.start()
        wpg_s[...] = w_pg_ref[...].astype(BF16)
        wpp_s[...] = w_pp_ref[...].astype(BF16)

    @pl.when(t + 1 < nt)
    def _():
        fetch(t + 1, 1 - slot)

    _wait_slabs(y_hbm, ybuf.at[slot], sem.at[slot], ntot_ref[t])

    pos = pos_ref[...]
    cols = _iota_f32((TOK_TILE, CBUF_ROWS), 1)
    pick = jnp.where((cols == pos[:, 0:1]) | (cols == pos[:, 1:2]), 1.0, 0.0).astype(BF16)
    h2 = _pick(t, npt, hp_ref, hs_ref) + jnp.dot(pick, ybuf[slot], preferred_element_type=F32)
    xn2 = _rmsnorm(h2, g_ple_ref[...])
    gate = _sigmoid(_dot(xn2, wpg_s[...]))
    p = jnp.where(t < npt, pp_ref[...], ps_ref[...])
    h3 = h2 + gate * _dot(p, wpp_s[...])
    if final:
        h3 = _rmsnorm(h3, g_fin_ref[...])

        @pl.when(t < npt)
        def _():
            op_ref[...] = h3

        @pl.when(t >= npt)
        def _():
            os_ref[...] = h3

        @pl.when(t == nt - 1)
        def _():
            for c in state_copies:
                c.wait()
    else:
        o_ref[...] = h3


def _combine_call(plan, hp, hs, pp, ps, pos, y, l, P, g_final, *, states=None):
    final = states is not None
    st_in = [a for kind in states for a in kind] if final else []
    npt = hp.shape[0] // TOK_TILE
    nst = hs.shape[0] // TOK_TILE
    nt = npt + nst

    def lmap(t, *_):
        return (l, 0, 0)

    in_specs = _pair_specs(npt, D_MODEL) + [
        pl.BlockSpec((None, TOK_TILE, PLE_DIM), lambda t, *_: (l, jnp.minimum(t, npt - 1), 0)),
        pl.BlockSpec((None, TOK_TILE, PLE_DIM), lambda t, *_: (l, jnp.maximum(t - npt, 0), 0)),
        pl.BlockSpec((TOK_TILE, LANES), lambda t, *_: (t, 0)),
        pl.BlockSpec(memory_space=pl.ANY),
        pl.BlockSpec((None,) + P["g_ple"].shape[1:], lmap),
        pl.BlockSpec((None,) + P["w_ple_gate"].shape[1:], lmap),
        pl.BlockSpec((None,) + P["w_ple_proj"].shape[1:], lmap),
        pl.BlockSpec(g_final.shape, lambda t, *_: (0, 0))] + [pl.BlockSpec(memory_space=pl.ANY)] * len(st_in)
    scratch = [pltpu.VMEM((2, CBUF_ROWS, D_MODEL), BF16), pltpu.VMEM((D_MODEL, D_MODEL), BF16),
               pltpu.VMEM((PLE_DIM, D_MODEL), BF16), pltpu.SemaphoreType.DMA((2,))]
    if final:
        out_specs = (pl.BlockSpec((TOK_TILE, D_MODEL), lambda t, *_: (jnp.minimum(t, npt - 1), 0)),
                     pl.BlockSpec((TOK_TILE, D_MODEL), lambda t, *_: (jnp.maximum(t - npt, 0), 0)),
                     ) + (pl.BlockSpec(memory_space=pl.ANY),) * len(states)
        out_shape = (jax.ShapeDtypeStruct(hp.shape, F32), jax.ShapeDtypeStruct(hs.shape, F32),
                     ) + tuple(jax.ShapeDtypeStruct((len(kind),) + kind[0].shape, F32) for kind in states)
        scratch.append(pltpu.SemaphoreType.DMA((len(st_in),)))
    else:
        out_specs = pl.BlockSpec((TOK_TILE, D_MODEL), lambda t, *_: (t, 0))
        out_shape = jax.ShapeDtypeStruct((nt * TOK_TILE, D_MODEL), F32)
    grid_spec = pltpu.PrefetchScalarGridSpec(
        num_scalar_prefetch=4, grid=(nt,), in_specs=in_specs, out_specs=out_specs, scratch_shapes=scratch)
    return pl.pallas_call(
        functools.partial(_combine_kernel, final=final, npt=npt, n_state=len(st_in)),
        grid_spec=grid_spec, out_shape=out_shape,
        compiler_params=pltpu.CompilerParams(dimension_semantics=("arbitrary",), vmem_limit_bytes=VMEM_LIMIT),
        name="moe_combine_final" if final else "moe_combine",
    )(*plan[:4], hp, hs, pp, ps, pos, y, P["g_ple"], P["w_ple_gate"], P["w_ple_proj"], g_final, *st_in)


def _ffn(hp, hs, pp, ps, l, P, w_rt, b_rt, g_final, *, states=None):
    n = hp.shape[0] + hs.shape[0]
    nt = n // TOK_TILE
    bound = TOP_K * n + nt * N_EXPERTS * (CHUNK - 1) + N_EXPERTS * (EXP_TILE - 1)
    n_sorted = -(-bound // EXP_TILE) * EXP_TILE
    pos, cnt = _route_call(hp, hs, l, P, w_rt, b_rt)
    plan = _chunk_plan(cnt[:, 0, :N_EXPERTS].astype(jnp.int32), n_sorted // EXP_TILE)
    xs = _dispatch_call(plan, hp, hs, pos, l, P, n_sorted)
    y = _expert_call(plan, xs, l, P["wg"], P["wu"], P["wd"])
    return _combine_call(plan, hp, hs, pp, ps, pos, y, l, P, g_final, states=states)


def _router_weights(l, w_grp_router, b_grp_router, w_exp_router, b_exp_router):
    w_er = jnp.transpose(w_exp_router[l], (1, 0, 2)).reshape(D_MODEL, N_EXPERTS)
    w_rt = jnp.zeros((D_MODEL, LANES), F32).at[:, :N_EXPERTS].set(w_er)
    w_rt = w_rt.at[:, N_EXPERTS:N_EXPERTS + N_GROUPS].set(w_grp_router[l])
    b_rt = jnp.zeros((1, LANES), F32).at[0, :N_EXPERTS].set(b_exp_router[l].reshape(-1))
    b_rt = b_rt.at[0, N_EXPERTS:N_EXPERTS + N_GROUPS].set(b_grp_router[l])
    w_hi = w_rt.astype(BF16)
    w_lo = (w_rt - w_hi.astype(F32)).astype(BF16)
    return jnp.concatenate([w_hi, w_lo], axis=1), b_rt


def kernel(x_prompt, x_sample, state_gla, state_conv, p_prompt, p_sample, g_mix, w_in, w_forget_up, b_forget,
           g_gla_out, w_conv, b_conv, g_conv_ln, b_conv_ln, w_out, g_ffn, w_grp_router, b_grp_router,
           w_exp_router, b_exp_router, w_exp_gate, w_exp_up, w_exp_down, g_ple, w_ple_gate, w_ple_proj, g_final):
    depth = w_in.shape[0]
    nbp, seq_p, _ = x_prompt.shape
    nbs, seq_s, _ = x_sample.shape
    n_p = nbp * seq_p
    n_s = nbs * seq_s

    def rows(v):
        return v.reshape(depth, 1, -1)

    P = {
        "g_mix": rows(g_mix), "w_in_t": jnp.swapaxes(w_in, 1, 2),
        "w_forget_up": w_forget_up, "b_forget": rows(b_forget), "g_gla_out": rows(g_gla_out),
        "w_conv": w_conv, "b_conv": rows(b_conv), "g_conv_ln": rows(g_conv_ln), "b_conv_ln": rows(b_conv_ln),
        "w_out": w_out, "g_ffn": rows(g_ffn), "g_ple": rows(g_ple),
        "w_ple_gate": w_ple_gate, "w_ple_proj": w_ple_proj,
        "wg": w_exp_gate.reshape(depth * N_EXPERTS, D_MODEL, EXPERT_FF),
        "wu": w_exp_up.reshape(depth * N_EXPERTS, D_MODEL, EXPERT_FF),
        "wd": w_exp_down.reshape(depth * N_EXPERTS, EXPERT_FF, D_MODEL),
    }
    g_fin = g_final.reshape(1, -1)
    xp = x_prompt.reshape(n_p, D_MODEL)
    xs = x_sample.reshape(n_s, D_MODEL)
    pp = p_prompt.reshape(depth, n_p, PLE_DIM)
    ps = p_sample.reshape(depth, n_s, PLE_DIM)
    s_in = state_gla.reshape(depth, nbs, QK_COLS, GLA_DV)
    c_in_t = jnp.swapaxes(state_conv, 1, 2)

    h = None
    sg_p, sg_s, sc_p, sc_s = [], [], [], []
    for l in range(depth):
        src_p, src_s, off_s = (xp, xs, 0) if l == 0 else (h, h, n_p)
        hp, sgp, scp = _guarded(functools.partial(_mixer_prompt, src_p, 0, nbp, seq_p, l, P))
        hs, sgs, scs = _guarded(functools.partial(_mixer_sample, src_s, off_s, nbs, seq_s, l, s_in, c_in_t, P))
        sg_p.append(sgp)
        sg_s.append(sgs)
        sc_p.append(scp)
        sc_s.append(scs)
        w_rt, b_rt = _router_weights(l, w_grp_router, b_grp_router, w_exp_router, b_exp_router)
        h = _ffn(hp, hs, pp, ps, l, P, w_rt, b_rt, g_fin, states=(sg_p, sg_s, sc_s) if l == depth - 1 else None)

    y_prompt, y_sample, sg_p_all, sg_s_all, sc_s_all = h
    return (y_prompt.reshape(nbp, seq_p, D_MODEL), y_sample.reshape(nbs, seq_s, D_MODEL),
            sg_p_all.reshape(depth, nbp, GLA_HEADS, GLA_DK, GLA_DV), sg_s_all.reshape(depth, nbs, GLA_HEADS, GLA_DK, GLA_DV),
            jnp.stack(sc_p), jnp.swapaxes(sc_s_all, 1, 2))
```

```python
import functools

import jax
import jax.numpy as jnp
from jax import lax
from jax.experimental import pallas as pl
from jax.experimental.pallas import tpu as pltpu

D_MODEL = 1024
GLA_HEADS = 4
GLA_DK = 64
GLA_DV = 128
QK_COLS = GLA_HEADS * GLA_DK
V_COLS = GLA_HEADS * GLA_DV
CONV_WIDTH = 512
CONV_K = 31
GLA_LOWRANK = 16
GLA_TAU = 16.0
GLA_CHUNK = 64
PLE_DIM = 256
N_GROUPS = 4
EXPERTS_PER_GROUP = 8
N_EXPERTS = N_GROUPS * EXPERTS_PER_GROUP
EXPERT_FF = 256
TOP_K = 2
EPS = 1e-6
N_MAIN = 2 * QK_COLS + 2 * V_COLS

LANES = 128
SUBLANES = 8
CONV_PAD = 32
CONV_OFF = CONV_PAD - (CONV_K - 1)
VMEM_LIMIT = 56 * 1024 * 1024
TOK_TILE = 512
CHUNK = 16
EXP_TILE = 512
CBUF_ROWS = -(-(TOP_K * TOK_TILE + N_EXPERTS * (CHUNK - 1)) // LANES) * LANES
XS_COLS = D_MODEL + LANES
DECAY_SAFE = 80.0

F32 = jnp.float32
BF16 = jnp.bfloat16
HI = lax.Precision.HIGHEST


def _sigmoid(x):
    return 1.0 / (1.0 + jnp.exp(-x))


def _silu(x):
    return x * _sigmoid(x)


def _log_sigmoid(x):
    return jnp.minimum(x, 0.0) - jnp.log(1.0 + jnp.exp(-jnp.abs(x)))


def _rmsnorm(x, g):
    return x * lax.rsqrt(jnp.mean(x * x, axis=-1, keepdims=True) + EPS) * g


def _dot(a, b):
    return jnp.dot(a.astype(BF16), b.astype(BF16), preferred_element_type=F32)


def _dot_t(a, b):
    return lax.dot_general(a.astype(BF16), b.astype(BF16), (((1,), (1,)), ((), ())),
                           preferred_element_type=F32)


def _dot_hi(a, b):
    return jnp.dot(a, b, preferred_element_type=F32, precision=HI)


def _iota_f32(shape, dim):
    return lax.broadcasted_iota(jnp.int32, shape, dim).astype(F32)


def _const_spec(shape):
    nd = len(shape)
    return pl.BlockSpec(shape, lambda *_: (0,) * nd)


def _layer_spec(arr, l):
    nd = arr.ndim - 1
    return pl.BlockSpec((None,) + arr.shape[1:], lambda *_: (l,) + (0,) * nd, pipeline_mode=pl.Buffered(1))


def _cast_mixer_weights(w_int_ref, w_fu_ref, w_out_ref, wmain_s, wlr_s, wcv_s, wfu_s, wout_s):
    blk = 4 * LANES
    for r in range(0, N_MAIN, blk):
        wmain_s[:, r:r + blk] = w_int_ref[r:r + blk, :].T.astype(BF16)
    lane = lax.broadcasted_iota(jnp.int32, (D_MODEL, LANES), 1)
    wlr_s[...] = jnp.where(lane < GLA_LOWRANK, w_int_ref[N_MAIN:N_MAIN + LANES, :].T, 0.0).astype(BF16)
    cv0 = N_MAIN + GLA_LOWRANK
    for r in range(0, 2 * CONV_WIDTH, blk):
        wcv_s[:, r:r + blk] = w_int_ref[cv0 + r:cv0 + r + blk, :].T.astype(BF16)
    wfu_s[...] = jnp.zeros_like(wfu_s)
    wfu_s[0:GLA_LOWRANK, :] = w_fu_ref[...].astype(BF16)
    wout_s[...] = w_out_ref[...].astype(BF16)


def _project(x, g_mix, wmain_s, wlr_s, wfu_s, b_f, wcv_s):
    xn = _rmsnorm(x, g_mix).astype(BF16)
    cv = jnp.dot(xn, wcv_s[...], preferred_element_type=F32)
    u = cv[:, :CONV_WIDTH] * _sigmoid(cv[:, CONV_WIDTH:])
    lr = jnp.dot(xn, wlr_s[...], preferred_element_type=F32)
    zf = _dot(lr, wfu_s[...]) + b_f
    la = _log_sigmoid(zf) * (1.0 / GLA_TAU)
    zqk = jnp.dot(xn, wmain_s[:, :2 * QK_COLS], preferred_element_type=F32)
    q = zqk[:, :QK_COLS] * (GLA_DK ** -0.5)
    k = zqk[:, QK_COLS:]
    zvo = jnp.dot(xn, wmain_s[:, 2 * QK_COLS:], preferred_element_type=F32)
    v = zvo[:, :V_COLS]
    og = zvo[:, V_COLS:]
    return q, k, v, og, la, u


def _stack_heads(qd):
    lane = lax.broadcasted_iota(jnp.int32, qd.shape, 1)
    return jnp.concatenate(
        [jnp.where((lane >= h * GLA_DK) & (lane < (h + 1) * GLA_DK), qd, 0.0) for h in range(GLA_HEADS)],
        axis=0)


def _gated_head_norm(o, og, g_gla):
    outs = []
    for h in range(GLA_HEADS):
        sl = slice(h * GLA_DV, (h + 1) * GLA_DV)
        outs.append(_rmsnorm(o[:, sl], g_gla) * _silu(og[:, sl]))
    return jnp.concatenate(outs, axis=1)


def _causal_conv(win, w_conv, b_conv, n):
    acc = jnp.broadcast_to(b_conv, (n, CONV_WIDTH))
    for s in range(SUBLANES):
        taps = [j for j in range(CONV_K) if (CONV_OFF + j) % SUBLANES == s]
        if not taps:
            continue
        rows = n if s == 0 else n + SUBLANES
        part = None
        for j in taps:
            a = (CONV_OFF + j) - s
            term = w_conv[j:j + 1, :] * win[a:a + rows, :]
            part = term if part is None else part + term
        acc = acc + part[s:s + n, :]
    return acc


def _conv_ln_act(acc, g_ln, b_ln):
    mu = jnp.mean(acc, axis=-1, keepdims=True)
    xc = acc - mu
    y = xc * lax.rsqrt(jnp.mean(xc * xc, axis=-1, keepdims=True) + EPS) * g_ln + b_ln
    return _silu(y)


def _head_diag(upd):
    return jnp.concatenate([upd[h * GLA_DK:(h + 1) * GLA_DK, h * GLA_DV:(h + 1) * GLA_DV]
                            for h in range(GLA_HEADS)], axis=0)


def _mixer_weight_args(l, P):
    names = ("g_mix", "w_in_t", "w_forget_up", "b_forget", "g_gla_out", "w_conv", "b_conv", "g_conv_ln",
             "b_conv_ln", "w_out")
    arrs = [P[n] for n in names]
    return arrs, [_layer_spec(a, l) for a in arrs]


_MIXER_WEIGHT_SCRATCH = [pltpu.VMEM((D_MODEL, N_MAIN), BF16), pltpu.VMEM((D_MODEL, LANES), BF16),
                         pltpu.VMEM((D_MODEL, 2 * CONV_WIDTH), BF16), pltpu.VMEM((LANES, QK_COLS), BF16),
                         pltpu.VMEM((D_MODEL, D_MODEL), BF16)]


def _mixer_prompt_kernel(x_ref, g_mix_ref, w_int_ref, w_fu_ref, b_f_ref, g_gla_ref, w_conv_ref,
                         b_conv_ref, g_ln_ref, b_ln_ref, w_out_ref,
                         h_ref, sg_ref, sc_ref, risk_ref,
                         wmain_s, wlr_s, wcv_s, wfu_s, wout_s,
                         s_ref, ubuf_ref, qs_ref, kd_ref, klt_ref, dec_ref, mid_ref, v_ref, og_ref, mix_ref,
                         *stable_refs, tt, stable):
    t = pl.program_id(1)
    nt = pl.num_programs(1)
    C = GLA_CHUNK
    n_chunks = tt // C

    @pl.when((pl.program_id(0) == 0) & (t == 0))
    def _():
        _cast_mixer_weights(w_int_ref, w_fu_ref, w_out_ref, wmain_s, wlr_s, wcv_s, wfu_s, wout_s)

    @pl.when(t == 0)
    def _():
        s_ref[...] = jnp.zeros_like(s_ref)
        ubuf_ref[0:CONV_PAD, :] = jnp.zeros((CONV_PAD, CONV_WIDTH), F32)

    x = x_ref[...]
    q, k, v, og, la, u = _project(x, g_mix_ref[...], wmain_s, wlr_s, wfu_s, b_f_ref[...], wcv_s)
    ubuf_ref[CONV_PAD:CONV_PAD + tt, :] = u
    v_ref[...] = v.astype(BF16)
    og_ref[...] = _silu(og)

    w_conv = w_conv_ref[...]
    b_conv = b_conv_ref[...]
    g_ln = g_ln_ref[...]
    b_ln = b_ln_ref[...]
    row = lax.broadcasted_iota(jnp.int32, (C, C), 0)
    col = lax.broadcasted_iota(jnp.int32, (C, C), 1)
    tri = (col <= row).astype(F32)
    risk = jnp.zeros((1, QK_COLS), F32)
    for c in range(n_chunks):
        rows = slice(c * C, (c + 1) * C)
        win = ubuf_ref[c * C:c * C + C + CONV_PAD, :]
        cact = _conv_ln_act(_causal_conv(win, w_conv, b_conv, C), g_ln, b_ln)
        mix_ref[rows, V_COLS:] = cact.astype(BF16)
        bits = pltpu.bitcast(cact[C - SUBLANES:C, 0:QK_COLS], jnp.uint32)
        zero = pltpu.bitcast(lax.shift_right_logical(lax.shift_right_logical(bits, jnp.uint32(16)), jnp.uint32(16)),
                             F32)[0:1, :]
        qc = q[rows, :] + zero
        b = _dot_hi(tri, la[rows, :])
        b_last = b[C - 1:C, :]
        if stable:
            q_st, k_st, b_st = stable_refs
            q_st[rows, :] = qc
            k_st[rows, :] = k[rows, :]
            b_st[rows, :] = b
            b_mid = jnp.zeros_like(b_last)
        else:
            b_mid = b[C // 2 - 1:C // 2, :]
            risk = jnp.maximum(risk, jnp.maximum(-b_mid, b_mid - b_last))
        qs_ref[c] = _stack_heads(qc * jnp.exp(b - b_mid)).astype(BF16)
        kd_ref[rows, :] = (k[rows, :] * jnp.exp(jnp.minimum(b_mid - b, DECAY_SAFE))).astype(BF16)
        kl = k[rows, :] * jnp.exp(b_last - b)
        klt = jnp.concatenate([kl, jnp.broadcast_to(jnp.exp(b_last), (C // 2, QK_COLS)),
                               jnp.broadcast_to(jnp.exp(b_mid), (C // 2, QK_COLS))], axis=0).T
        klt_ref[c] = klt.astype(BF16)
        dec_ref[c] = jnp.broadcast_to(klt[:, C:C + 1], (QK_COLS, GLA_DV))
        mid_ref[c] = jnp.broadcast_to(klt[:, 3 * C // 2:3 * C // 2 + 1], (QK_COLS, GLA_DV))

    r4 = lax.broadcasted_iota(jnp.int32, (GLA_HEADS * C, C), 0)
    c4 = lax.broadcasted_iota(jnp.int32, (GLA_HEADS * C, C), 1)
    causal4 = c4 <= (r4 % C)
    g_gla = g_gla_ref[...]
    s = s_ref[...]
    for c in range(n_chunks):
        rows = slice(c * C, (c + 1) * C)
        qs = qs_ref[c]
        vc = v_ref[rows, :]
        if stable:
            scores = _direct_scores(*stable_refs, c * C, C).astype(BF16)
        else:
            scores = jnp.where(causal4, _dot_t(qs, kd_ref[rows, :]), 0.0).astype(BF16)
        o_inter = jnp.dot(qs, (mid_ref[c] * s).astype(BF16), preferred_element_type=F32)
        upd = jnp.dot(klt_ref[c][:, :C], vc, preferred_element_type=F32)
        s = dec_ref[c] * s + _head_diag(upd)
        o_parts = []
        for h in range(GLA_HEADS):
            vh = vc[:, h * GLA_DV:(h + 1) * GLA_DV]
            o_parts.append(jnp.dot(scores[h * C:(h + 1) * C, :], vh, preferred_element_type=F32)
                           + o_inter[h * C:(h + 1) * C, :])
        o = jnp.concatenate(o_parts, axis=1)
        gated = []
        for h in range(GLA_HEADS):
            sl = slice(h * GLA_DV, (h + 1) * GLA_DV)
            gated.append(_rmsnorm(o[:, sl], g_gla) * og_ref[rows, sl])
        mix_ref[rows, 0:V_COLS] = jnp.concatenate(gated, axis=1).astype(BF16)
    s_ref[...] = s

    h_ref[...] = x + jnp.dot(mix_ref[...], wout_s[...], preferred_element_type=F32)
    tail = ubuf_ref[tt:tt + CONV_PAD, :]
    ubuf_ref[0:CONV_PAD, :] = tail

    risk_ref[0] = jnp.broadcast_to(jnp.max(risk, axis=-1, keepdims=True), (SUBLANES, LANES))

    @pl.when(t == nt - 1)
    def _():
        sg_ref[0] = s
        sc_ref[0] = tail[CONV_OFF:, :]


def _direct_scores(q_st, k_st, b_st, r0, C):
    kc = k_st[r0:r0 + C, :]
    bc = b_st[r0:r0 + C, :]
    srow = lax.broadcasted_iota(jnp.int32, (C, QK_COLS), 0)
    lane = lax.broadcasted_iota(jnp.int32, (C, LANES), 1)
    head_of = lax.broadcasted_iota(jnp.int32, (QK_COLS, LANES), 0) // GLA_DK
    head_sum = (head_of == lax.broadcasted_iota(jnp.int32, (QK_COLS, LANES), 1)).astype(F32)

    def one_query(t, acc):
        d = b_st[pl.ds(r0 + t, 1), :] - bc
        w = jnp.where(srow <= t, jnp.exp(jnp.minimum(d, 0.0)), 0.0) * kc * q_st[pl.ds(r0 + t, 1), :]
        per_head = _dot_hi(w, head_sum)
        return tuple(jnp.where(lane == t, per_head[:, h:h + 1], acc[h]) for h in range(GLA_HEADS))

    acc = lax.fori_loop(0, C, one_query, tuple(jnp.zeros((C, LANES), F32) for _ in range(GLA_HEADS)))
    return jnp.concatenate([a.T[0:C, :] for a in acc], axis=0)


def _mixer_prompt(x, row_off, nb, seq, l, P, *, tt=512, stable=False):
    nt = seq // tt
    n_chunks = tt // GLA_CHUNK
    blk_off = row_off // tt
    weights, w_specs = _mixer_weight_args(l, P)
    in_specs = [pl.BlockSpec((tt, D_MODEL), lambda b, t: (blk_off + b * nt + t, 0))] + w_specs
    out_shape = (jax.ShapeDtypeStruct((nb * seq, D_MODEL), F32),
                 jax.ShapeDtypeStruct((nb, QK_COLS, GLA_DV), F32),
                 jax.ShapeDtypeStruct((nb, CONV_K - 1, CONV_WIDTH), F32),
                 jax.ShapeDtypeStruct((nb * nt, SUBLANES, LANES), F32))
    out_specs = (pl.BlockSpec((tt, D_MODEL), lambda b, t: (b * nt + t, 0)),
                 pl.BlockSpec((1, QK_COLS, GLA_DV), lambda b, t: (b, 0, 0)),
                 pl.BlockSpec((1, CONV_K - 1, CONV_WIDTH), lambda b, t: (b, 0, 0)),
                 pl.BlockSpec((1, SUBLANES, LANES), lambda b, t: (b * nt + t, 0, 0)))
    scratch = _MIXER_WEIGHT_SCRATCH + [
        pltpu.VMEM((QK_COLS, GLA_DV), F32),
        pltpu.VMEM((CONV_PAD + tt, CONV_WIDTH), F32),
        pltpu.VMEM((n_chunks, GLA_HEADS * GLA_CHUNK, QK_COLS), BF16), pltpu.VMEM((tt, QK_COLS), BF16),
        pltpu.VMEM((n_chunks, QK_COLS, 2 * GLA_CHUNK), BF16), pltpu.VMEM((n_chunks, QK_COLS, GLA_DV), F32),
        pltpu.VMEM((n_chunks, QK_COLS, GLA_DV), F32),
        pltpu.VMEM((tt, V_COLS), BF16), pltpu.VMEM((tt, V_COLS), F32),
        pltpu.VMEM((tt, D_MODEL), BF16)]
    if stable:
        scratch += [pltpu.VMEM((tt, QK_COLS), F32)] * 3
    return pl.pallas_call(
        functools.partial(_mixer_prompt_kernel, tt=tt, stable=stable),
        grid=(nb, nt), in_specs=in_specs, out_specs=out_specs, out_shape=out_shape,
        scratch_shapes=scratch,
        compiler_params=pltpu.CompilerParams(dimension_semantics=("arbitrary", "arbitrary"),
                                             vmem_limit_bytes=VMEM_LIMIT),
        name="mixer_prompt_stable" if stable else "mixer_prompt",
    )(x, *weights)


def _guarded(mixer):
    *outs, risk = mixer(stable=False)
    return lax.cond(jnp.max(risk) > DECAY_SAFE, lambda: tuple(mixer(stable=True)[:-1]), lambda: tuple(outs))


def _mixer_sample_kernel(x_ref, s_in_ref, c_in_ref, g_mix_ref, w_int_ref, w_fu_ref, b_f_ref,
                         g_gla_ref, w_conv_ref, b_conv_ref, g_ln_ref, b_ln_ref, w_out_ref,
                         h_ref, sg_ref, sc_ref, risk_ref,
                         wmain_s, wlr_s, wcv_s, wfu_s, wout_s,
                         u4_ref, oi_ref, cacc4_ref, *, sb, seq, stable):
    R = sb * seq
    n_slabs = CONV_WIDTH // LANES

    @pl.when(pl.program_id(0) == 0)
    def _():
        _cast_mixer_weights(w_int_ref, w_fu_ref, w_out_ref, wmain_s, wlr_s, wcv_s, wfu_s, wout_s)

    x = x_ref[...]
    q, k, v, og, la, u = _project(x, g_mix_ref[...], wmain_s, wlr_s, wfu_s, b_f_ref[...], wcv_s)

    for kk in range(n_slabs):
        u4_ref[kk] = u[:, kk * LANES:(kk + 1) * LANES]
    full = [c_in_ref[j] for j in range(CONV_K - 1)]
    for t in range(seq):
        full.append(jnp.concatenate([u4_ref.at[kk][pl.ds(t, sb, stride=seq), :] for kk in range(n_slabs)], axis=1))
    w_conv = w_conv_ref[...]
    for t in range(seq):
        acc = jnp.broadcast_to(b_conv_ref[...], (sb, CONV_WIDTH))
        for j in range(CONV_K):
            acc = acc + w_conv[j:j + 1, :] * full[t + j]
        for kk in range(n_slabs):
            cacc4_ref.at[kk][pl.ds(t, sb, stride=seq), :] = acc[:, kk * LANES:(kk + 1) * LANES]
    for j in range(CONV_K - 1):
        sc_ref[j] = full[seq + j]

    row = lax.broadcasted_iota(jnp.int32, (R, R), 0)
    col = lax.broadcasted_iota(jnp.int32, (R, R), 1)
    same = (row // seq) == (col // seq)
    b = _dot_hi((same & (col <= row)).astype(F32), la)
    b_tot = _dot_hi(same.astype(F32), la)
    qd = q * jnp.exp(b)
    kl = k * jnp.exp(b_tot - b)
    qs = _stack_heads(qd)
    risk_ref[0] = jnp.broadcast_to(jnp.max(jnp.max(-b_tot, axis=-1, keepdims=True), axis=0, keepdims=True),
                                   (SUBLANES, LANES))
    if stable:
        head_of = lax.broadcasted_iota(jnp.int32, (QK_COLS, LANES), 0) // GLA_DK
        head_sum = (head_of == lax.broadcasted_iota(jnp.int32, (QK_COLS, LANES), 1)).astype(F32)
        t_in_seq = lax.broadcasted_iota(jnp.int32, (R, QK_COLS), 0) % seq
        parts = [jnp.zeros((R, R), F32) for _ in range(GLA_HEADS)]
        for d in range(seq):
            k_d = jnp.concatenate([jnp.zeros((d, QK_COLS), F32), k[:R - d, :]], axis=0) if d else k
            b_d = jnp.concatenate([jnp.zeros((d, QK_COLS), F32), b[:R - d, :]], axis=0) if d else b
            w = jnp.where(t_in_seq >= d, jnp.exp(jnp.minimum(b - b_d, 0.0)), 0.0) * k_d * q
            per_head = _dot_hi(w, head_sum)
            for h in range(GLA_HEADS):
                parts[h] = jnp.where(col == row - d, per_head[:, h:h + 1], parts[h])
        scores = jnp.concatenate(parts, axis=0)
    else:
        kd = k * jnp.exp(jnp.minimum(-b, DECAY_SAFE))
        r4 = lax.broadcasted_iota(jnp.int32, (GLA_HEADS * R, R), 0) % R
        c4 = lax.broadcasted_iota(jnp.int32, (GLA_HEADS * R, R), 1)
        mask4 = ((r4 // seq) == (c4 // seq)) & (c4 <= r4)
        scores = jnp.where(mask4, _dot_t(qs, kd), 0.0)

    klt = kl.T
    dect = jnp.exp(b_tot).T
    lane_h = lax.broadcasted_iota(jnp.int32, (GLA_DK, R), 1)
    lane_r = lax.broadcasted_iota(jnp.int32, (QK_COLS, R), 1)
    upd = []
    for h in range(GLA_HEADS):
        klt_h = klt[h * GLA_DK:(h + 1) * GLA_DK, :]
        lhs = jnp.concatenate([jnp.where((lane_h >= i * seq) & (lane_h < (i + 1) * seq), klt_h, 0.0)
                               for i in range(sb)], axis=0)
        upd.append(_dot(lhs, v[:, h * GLA_DV:(h + 1) * GLA_DV]))
    for i in range(sb):
        s_old = s_in_ref[i]
        qsel = jnp.concatenate([qs[h * R + i * seq:h * R + (i + 1) * seq, :] for h in range(GLA_HEADS)], axis=0)
        oi = _dot(qsel, s_old)
        for h in range(GLA_HEADS):
            oi_ref[h, i * seq:(i + 1) * seq, :] = oi[h * seq:(h + 1) * seq, :]
        smask = (lane_r >= i * seq) & (lane_r < (i + 1) * seq)
        dec = jnp.sum(jnp.where(smask, dect, 0.0), axis=1, keepdims=True) * (1.0 / seq)
        u_new = jnp.concatenate([upd[h][i * GLA_DK:(i + 1) * GLA_DK, :] for h in range(GLA_HEADS)], axis=0)
        sg_ref[i] = dec * s_old + u_new

    o_parts = []
    for h in range(GLA_HEADS):
        vh = v[:, h * GLA_DV:(h + 1) * GLA_DV]
        o_parts.append(_dot(scores[h * R:(h + 1) * R, :], vh) + oi_ref[h])
    o = jnp.concatenate(o_parts, axis=1)
    cacc = jnp.concatenate([cacc4_ref[kk] for kk in range(n_slabs)], axis=1)
    mix = jnp.concatenate([_gated_head_norm(o, og, g_gla_ref[...]),
                           _conv_ln_act(cacc, g_ln_ref[...], b_ln_ref[...])], axis=1)
    h_ref[...] = x + jnp.dot(mix.astype(BF16), wout_s[...], preferred_element_type=F32)


def _mixer_sample(x, row_off, nb, seq, l, s_in, c_in_t, P, *, sb=16, stable=False):
    R = sb * seq
    blk_off = row_off // R
    n_slabs = CONV_WIDTH // LANES
    weights, w_specs = _mixer_weight_args(l, P)
    in_specs = [pl.BlockSpec((R, D_MODEL), lambda i: (blk_off + i, 0)),
                pl.BlockSpec((None, sb, QK_COLS, GLA_DV), lambda i: (l, i, 0, 0)),
                pl.BlockSpec((None, CONV_K - 1, sb, CONV_WIDTH), lambda i: (l, 0, i, 0))] + w_specs
    out_shape = (jax.ShapeDtypeStruct((nb * seq, D_MODEL), F32),
                 jax.ShapeDtypeStruct((nb, QK_COLS, GLA_DV), F32),
                 jax.ShapeDtypeStruct((CONV_K - 1, nb, CONV_WIDTH), F32),
                 jax.ShapeDtypeStruct((nb // sb, SUBLANES, LANES), F32))
    out_specs = (pl.BlockSpec((R, D_MODEL), lambda i: (i, 0)),
                 pl.BlockSpec((sb, QK_COLS, GLA_DV), lambda i: (i, 0, 0)),
                 pl.BlockSpec((CONV_K - 1, sb, CONV_WIDTH), lambda i: (0, i, 0)),
                 pl.BlockSpec((1, SUBLANES, LANES), lambda i: (i, 0, 0)))
    scratch = _MIXER_WEIGHT_SCRATCH + [
        pltpu.VMEM((n_slabs, R, LANES), F32),
        pltpu.VMEM((GLA_HEADS, R, GLA_DV), F32),
        pltpu.VMEM((n_slabs, R, LANES), F32)]
    return pl.pallas_call(
        functools.partial(_mixer_sample_kernel, sb=sb, seq=seq, stable=stable),
        grid=(nb // sb,), in_specs=in_specs, out_specs=out_specs, out_shape=out_shape,
        scratch_shapes=scratch,
        compiler_params=pltpu.CompilerParams(dimension_semantics=("arbitrary",),
                                             vmem_limit_bytes=VMEM_LIMIT),
        name="mixer_sample_stable" if stable else "mixer_sample",
    )(x, s_in, c_in_t, *weights)


def _pair_specs(n_first_tiles, width):
    return [pl.BlockSpec((TOK_TILE, width), lambda t, *_: (jnp.minimum(t, n_first_tiles - 1), 0)),
            pl.BlockSpec((TOK_TILE, width), lambda t, *_: (jnp.maximum(t - n_first_tiles, 0), 0))]


def _pick(t, n_first_tiles, a_ref, b_ref):
    return jnp.where(t < n_first_tiles, a_ref[...], b_ref[...])


def _route(logits):
    lane = lax.broadcasted_iota(jnp.int32, logits.shape, 1)
    lane_f = lane.astype(F32)
    neg = jnp.float32(-jnp.inf)
    big = jnp.float32(1e9)
    is_grp = (lane >= N_EXPERTS) & (lane < N_EXPERTS + N_GROUPS)
    gl = jnp.where(is_grp, logits, neg)
    gmax = jnp.max(gl, axis=-1, keepdims=True)
    gidx = jnp.min(jnp.where(is_grp & (gl == gmax), lane_f - N_EXPERTS, big), axis=-1, keepdims=True)
    gsum = jnp.sum(jnp.where(is_grp, jnp.exp(gl - gmax), 0.0), axis=-1, keepdims=True)
    g_w = 1.0 / gsum
    grp_of_lane = jnp.floor(lane_f * (1.0 / EXPERTS_PER_GROUP))
    in_grp = (lane < N_EXPERTS) & (grp_of_lane == gidx)
    ml = jnp.where(in_grp, logits, neg)
    v1 = jnp.max(ml, axis=-1, keepdims=True)
    i1 = jnp.min(jnp.where(in_grp & (ml == v1), lane_f, big), axis=-1, keepdims=True)
    ml2 = jnp.where(lane_f == i1, neg, ml)
    v2 = jnp.max(ml2, axis=-1, keepdims=True)
    i2 = jnp.min(jnp.where(in_grp & (ml2 == v2), lane_f, big), axis=-1, keepdims=True)
    e2 = jnp.exp(v2 - v1)
    w1 = g_w / (1.0 + e2)
    w2 = g_w * e2 / (1.0 + e2)
    return i1, i2, w1, w2


def _route_kernel(hp_ref, hs_ref, g_ffn_ref, w_rt_ref, b_rt_ref, pos_ref, cnt_ref, *, npt):
    T = TOK_TILE
    xn = _rmsnorm(_pick(pl.program_id(0), npt, hp_ref, hs_ref), g_ffn_ref[...])
    x_hi = xn.astype(BF16)
    x_lo = (xn - x_hi.astype(F32)).astype(BF16)
    w_split = w_rt_ref[...]
    both = jnp.dot(x_hi, w_split, preferred_element_type=F32)
    logits = (both[:, :LANES] + both[:, LANES:] + jnp.dot(x_lo, w_split[:, :LANES], preferred_element_type=F32)
              + b_rt_ref[...])
    i1, i2, w1, w2 = _route(logits)
    lane = _iota_f32((T, LANES), 1)
    a0 = (lane == i1).astype(F32)
    a1 = (lane == i2).astype(F32)
    a = a0 + a1
    cnt = jnp.sum(a, axis=0, keepdims=True)
    earlier = (_iota_f32((T, T), 1) < _iota_f32((T, T), 0)).astype(BF16)
    rank = jnp.dot(earlier, a.astype(BF16), preferred_element_type=F32)
    cnt_pad = jnp.ceil(cnt * (1.0 / CHUNK)) * CHUNK
    below = (_iota_f32((LANES, LANES), 0) < _iota_f32((LANES, LANES), 1)).astype(F32)
    first = _dot_hi(jnp.broadcast_to(cnt_pad, (SUBLANES, LANES)), below)[0:1, :]
    base = first + rank
    pos0 = jnp.sum(a0 * base, axis=1, keepdims=True)
    pos1 = jnp.sum(a1 * base, axis=1, keepdims=True)
    pos_ref[...] = jnp.where(lane == 0.0, pos0, jnp.where(lane == 1.0, pos1, jnp.where(
        lane == 2.0, w1, jnp.where(lane == 3.0, w2, jnp.where(lane == 4.0, i1, 0.0)))))
    cnt_ref[0] = jnp.broadcast_to(cnt, (SUBLANES, LANES))


def _route_call(hp, hs, l, P, w_rt, b_rt):
    npt = hp.shape[0] // TOK_TILE
    nt = npt + hs.shape[0] // TOK_TILE
    return pl.pallas_call(
        functools.partial(_route_kernel, npt=npt), grid=(nt,),
        in_specs=_pair_specs(npt, D_MODEL) + [_layer_spec(P["g_ffn"], l), _const_spec(w_rt.shape),
                                              _const_spec(b_rt.shape)],
        out_specs=(pl.BlockSpec((TOK_TILE, LANES), lambda t: (t, 0)),
                   pl.BlockSpec((1, SUBLANES, LANES), lambda t: (t, 0, 0))),
        out_shape=(jax.ShapeDtypeStruct((nt * TOK_TILE, LANES), F32),
                   jax.ShapeDtypeStruct((nt, SUBLANES, LANES), F32)),
        compiler_params=pltpu.CompilerParams(dimension_semantics=("arbitrary",), vmem_limit_bytes=VMEM_LIMIT),
        name="moe_route",
    )(hp, hs, P["g_ffn"], w_rt, b_rt)


def _chunk_plan(cnt, n_row_tiles):
    n16 = (cnt + (CHUNK - 1)) // CHUNK
    lofs16 = jnp.cumsum(n16, axis=1) - n16
    tile_pref16 = jnp.cumsum(n16, axis=0) - n16
    tot16 = jnp.sum(n16, axis=0)
    per_tile = EXP_TILE // CHUNK
    seg16 = ((tot16 + per_tile - 1) // per_tile) * per_tile
    seg_end16 = jnp.cumsum(seg16)
    dst16 = (seg_end16 - seg16)[None, :] + tile_pref16
    n_tot = jnp.sum(n16, axis=1)
    gap16 = seg_end16 - seg16 + tot16
    gapn16 = seg16 - tot16
    tile_start16 = jnp.arange(n_row_tiles, dtype=jnp.int32) * per_tile
    n_valid = seg_end16[-1] // per_tile
    misc = n_valid.reshape(1)
    exp_of_tile = jnp.minimum(jnp.sum(seg_end16[None, :] <= tile_start16[:, None], axis=1), N_EXPERTS - 1)
    i32 = lambda a: a.astype(jnp.int32).reshape(-1)
    return (i32(dst16), i32(n16), i32(lofs16), i32(n_tot), i32(gap16), i32(gapn16), i32(misc), i32(exp_of_tile))


def _chunk_copy(src, dst, src_chunk, dst_chunk, sem, n_chunks=1):
    rows = n_chunks * CHUNK
    return pltpu.make_async_copy(src.at[pl.ds(pl.multiple_of(src_chunk * CHUNK, CHUNK), rows), :],
                                 dst.at[pl.ds(pl.multiple_of(dst_chunk * CHUNK, CHUNK), rows), :], sem)


def _slab_copies(src, dst, sem, tile, src_ofs_ref, dst_ofs_ref, n16_ref):
    for e in range(N_EXPERTS):
        k = tile * N_EXPERTS + e
        n = n16_ref[k]

        @pl.when(n > 0)
        def _(k=k, n=n):
            _chunk_copy(src, dst, src_ofs_ref[k], dst_ofs_ref[k], sem, n).start()


def _wait_slabs(src, dst, sem, n_chunks):
    @pl.when(n_chunks > 0)
    def _():
        _chunk_copy(src, dst, 0, 0, sem, n_chunks).wait()


def _tile_copy(src, dst, dst_tile, sem):
    return pltpu.make_async_copy(src, dst.at[pl.ds(pl.multiple_of(dst_tile * EXP_TILE, EXP_TILE), EXP_TILE), :], sem)


def _dispatch_kernel(dst16_ref, n16_ref, lofs16_ref, ntot_ref, gap16_ref, gapn16_ref, misc_ref,
                     hp_ref, hs_ref, pos_ref, g_ffn_ref, xs_hbm, cbuf, zbuf, sem, *, n_row_tiles, npt):
    t = pl.program_id(0)
    nt = pl.num_programs(0)
    slot = t % 2
    T = TOK_TILE
    n_tail = n_row_tiles - misc_ref[0]

    @pl.when(t == 0)
    def _():
        zbuf[...] = jnp.zeros_like(zbuf)
        for e in range(N_EXPERTS):
            g = gapn16_ref[e]

            @pl.when(g > 0)
            def _(e=e, g=g):
                _chunk_copy(zbuf, xs_hbm, 0, gap16_ref[e], sem.at[2], g).start()

        def fill_tile(i, carry):
            _tile_copy(zbuf, xs_hbm, misc_ref[0] + i, sem.at[2]).start()
            return carry
        lax.fori_loop(0, n_tail, fill_tile, 0)

    xn = _rmsnorm(_pick(t, npt, hp_ref, hs_ref), g_ffn_ref[...]).astype(BF16)
    pos = pos_ref[...]
    pos_t = pos.T
    rows = _iota_f32((CBUF_ROWS, T), 0)
    onehot = jnp.where((rows == pos_t[0:1, :]) | (rows == pos_t[1:2, :]), 1.0, 0.0).astype(BF16)
    lane = lax.broadcasted_iota(jnp.int32, (T, LANES), 1)
    extra = jnp.zeros((T, LANES), F32)
    for s in range(TOP_K):
        c = pos[:, 2 + s:3 + s]
        hi = c.astype(BF16).astype(F32)
        mid = (c - hi).astype(BF16).astype(F32)
        lo = c - hi - mid
        for j, piece in enumerate((hi, mid, lo)):
            extra = jnp.where(lane == 3 * s + j, piece, extra)
    extra = jnp.where(lane == 3 * TOP_K, pos[:, 4:5], extra)
    cbuf[slot] = jnp.dot(onehot, jnp.concatenate([xn, extra.astype(BF16)], axis=1),
                         preferred_element_type=F32).astype(BF16)

    src = cbuf.at[slot]
    _slab_copies(src, xs_hbm, sem.at[slot], t, lofs16_ref, dst16_ref, n16_ref)

    @pl.when(t > 0)
    def _():
        _wait_slabs(cbuf.at[1 - slot], xs_hbm, sem.at[1 - slot], ntot_ref[t - 1])

    @pl.when(t == nt - 1)
    def _():
        _wait_slabs(src, xs_hbm, sem.at[slot], ntot_ref[t])
        for e in range(N_EXPERTS):
            _wait_slabs(zbuf, xs_hbm, sem.at[2], gapn16_ref[e])

        def wait_tile(_, carry):
            _tile_copy(zbuf, xs_hbm, 0, sem.at[2]).wait()
            return carry
        lax.fori_loop(0, n_tail, wait_tile, 0)


def _dispatch_call(plan, hp, hs, pos, l, P, n_sorted):
    npt = hp.shape[0] // TOK_TILE
    nt = npt + hs.shape[0] // TOK_TILE
    g_ffn = P["g_ffn"]
    grid_spec = pltpu.PrefetchScalarGridSpec(
        num_scalar_prefetch=7, grid=(nt,),
        in_specs=_pair_specs(npt, D_MODEL) + [
            pl.BlockSpec((TOK_TILE, LANES), lambda t, *_: (t, 0)),
            pl.BlockSpec((None,) + g_ffn.shape[1:], lambda t, *_: (l, 0, 0))],
        out_specs=pl.BlockSpec(memory_space=pl.ANY),
        scratch_shapes=[pltpu.VMEM((2, CBUF_ROWS, XS_COLS), BF16), pltpu.VMEM((EXP_TILE, XS_COLS), BF16),
                        pltpu.SemaphoreType.DMA((3,))])
    return pl.pallas_call(
        functools.partial(_dispatch_kernel, n_row_tiles=n_sorted // EXP_TILE, npt=npt), grid_spec=grid_spec,
        out_shape=jax.ShapeDtypeStruct((n_sorted, XS_COLS), BF16),
        compiler_params=pltpu.CompilerParams(dimension_semantics=("arbitrary",), vmem_limit_bytes=VMEM_LIMIT),
        name="moe_dispatch",
    )(*plan[:7], hp, hs, pos, g_ffn)


def _expert_kernel(eot_ref, misc_ref, xs_ref, wg_ref, wu_ref, wd_ref, y_ref, wg_s, wu_s, wd_s):
    i = pl.program_id(0)
    valid = i < misc_ref[0]

    @pl.when(jnp.logical_not(valid))
    def _():
        y_ref[...] = jnp.zeros_like(y_ref)

    @pl.when(valid & ((i == 0) | (eot_ref[i] != eot_ref[jnp.maximum(i - 1, 0)])))
    def _():
        wg_s[...] = wg_ref[...].astype(BF16)
        wu_s[...] = wu_ref[...].astype(BF16)
        wd_s[...] = wd_ref[...].astype(BF16)

    @pl.when(valid)
    def _():
        xs = xs_ref[...]
        x = xs[:, :D_MODEL]
        ex = xs[:, D_MODEL:].astype(F32)
        lane = lax.broadcasted_iota(jnp.int32, ex.shape, 1)
        id0 = jnp.sum(jnp.where(lane == 3 * TOP_K, ex, 0.0), axis=-1, keepdims=True)
        first = id0 == eot_ref[i].astype(F32)
        mine = (first & (lane < 3)) | (jnp.logical_not(first) & (lane >= 3) & (lane < 3 * TOP_K))
        c = jnp.sum(jnp.where(mine, ex, 0.0), axis=-1, keepdims=True)
        hg = _silu(jnp.dot(x, wg_s[...], preferred_element_type=F32)) * jnp.dot(x, wu_s[...],
                                                                               preferred_element_type=F32)
        y_ref[...] = jnp.dot((hg * c).astype(BF16), wd_s[...], preferred_element_type=F32).astype(BF16)


def _expert_call(plan, xs, l, wg, wu, wd):
    misc, exp_of_tile = plan[6:]
    n_row_tiles = xs.shape[0] // EXP_TILE

    def last_valid(i, nv):
        return jnp.maximum(jnp.minimum(i, nv[0] - 1), 0)

    def row_map(i, eot, nv):
        return (last_valid(i, nv), 0)

    def w_map(i, eot, nv):
        return (l * N_EXPERTS + eot[last_valid(i, nv)], 0, 0)

    grid_spec = pltpu.PrefetchScalarGridSpec(
        num_scalar_prefetch=2, grid=(n_row_tiles,),
        in_specs=[pl.BlockSpec((EXP_TILE, XS_COLS), row_map),
                  pl.BlockSpec((None, D_MODEL, EXPERT_FF), w_map),
                  pl.BlockSpec((None, D_MODEL, EXPERT_FF), w_map),
                  pl.BlockSpec((None, EXPERT_FF, D_MODEL), w_map)],
        out_specs=pl.BlockSpec((EXP_TILE, D_MODEL), lambda i, eot, nv: (i, 0)),
        scratch_shapes=[pltpu.VMEM((D_MODEL, EXPERT_FF), BF16), pltpu.VMEM((D_MODEL, EXPERT_FF), BF16),
                        pltpu.VMEM((EXPERT_FF, D_MODEL), BF16)])
    return pl.pallas_call(
        _expert_kernel, grid_spec=grid_spec,
        out_shape=jax.ShapeDtypeStruct((xs.shape[0], D_MODEL), BF16),
        compiler_params=pltpu.CompilerParams(dimension_semantics=("arbitrary",), vmem_limit_bytes=VMEM_LIMIT),
        name="moe_experts",
    )(exp_of_tile, misc, xs, wg, wu, wd)


def _combine_kernel(dst16_ref, n16_ref, lofs16_ref, ntot_ref, hp_ref, hs_ref, pp_ref, ps_ref, pos_ref, y_hbm,
                    g_ple_ref, w_pg_ref, w_pp_ref, g_fin_ref, *rest, final, npt, n_state):
    if final:
        st_in, (op_ref, os_ref, *st_out, ybuf, wpg_s, wpp_s, sem) = rest[:n_state], rest[n_state:]
        depth = n_state // len(st_out)
        for k, out in enumerate(st_out):
            for l in range(depth):
                out[l] = st_in[k * depth + l][...]
    else:
        o_ref, ybuf, wpg_s, wpp_s, sem = rest
    t = pl.program_id(0)
    nt = pl.num_programs(0)
    slot = t % 2

    def fetch(tile, sl):
        _slab_copies(y_hbm, ybuf.at[sl], sem.at[sl], tile, dst16_ref, lofs16_ref, n16_ref)

    @pl.when(t == 0)
    def _():
        ybuf[...] = jnp.zeros_like(ybuf)
        fetch(t, slot)
        wpg_s[...] = w_pg_ref[...].astype(BF16)
        wpp_s[...] = w_pp_ref[...].astype(BF16)

    @pl.when(t + 1 < nt)
    def _():
        fetch(t + 1, 1 - slot)

    _wait_slabs(y_hbm, ybuf.at[slot], sem.at[slot], ntot_ref[t])

    pos = pos_ref[...]
    cols = _iota_f32((TOK_TILE, CBUF_ROWS), 1)
    pick = jnp.where((cols == pos[:, 0:1]) | (cols == pos[:, 1:2]), 1.0, 0.0).astype(BF16)
    h2 = _pick(t, npt, hp_ref, hs_ref) + jnp.dot(pick, ybuf[slot], preferred_element_type=F32)
    xn2 = _rmsnorm(h2, g_ple_ref[...])
    gate = _sigmoid(_dot(xn2, wpg_s[...]))
    p = jnp.where(t < npt, pp_ref[...], ps_ref[...])
    h3 = h2 + gate * _dot(p, wpp_s[...])
    if final:
        h3 = _rmsnorm(h3, g_fin_ref[...])

        @pl.when(t < npt)
        def _():
            op_ref[...] = h3

        @pl.when(t >= npt)
        def _():
            os_ref[...] = h3
    else:
        o_ref[...] = h3


def _state_specs(arrs, axis, nt):
    shape = arrs[0].shape
    unit = SUBLANES if axis == len(shape) - 2 else 1
    per = -(-shape[axis] // (nt * unit)) * unit
    assert shape[axis] % per == 0, (shape, axis, nt)
    n_blocks = shape[axis] // per
    block = shape[:axis] + (per,) + shape[axis + 1:]

    def in_map(t, *_):
        return (0,) * axis + (jnp.minimum(t, n_blocks - 1),) + (0,) * (len(shape) - axis - 1)

    return [pl.BlockSpec(block, in_map)] * len(arrs), pl.BlockSpec((len(arrs),) + block, lambda t, *_: (0,) + in_map(t))


def _combine_call(plan, hp, hs, pp, ps, pos, y, l, P, g_final, *, states=None):
    final = states is not None
    st_in = [a for arrs, _ in states for a in arrs] if final else []
    npt = hp.shape[0] // TOK_TILE
    nst = hs.shape[0] // TOK_TILE
    nt = npt + nst

    def lmap(t, *_):
        return (l, 0, 0)

    in_specs = _pair_specs(npt, D_MODEL) + [
        pl.BlockSpec((None, TOK_TILE, PLE_DIM), lambda t, *_: (l, jnp.minimum(t, npt - 1), 0)),
        pl.BlockSpec((None, TOK_TILE, PLE_DIM), lambda t, *_: (l, jnp.maximum(t - npt, 0), 0)),
        pl.BlockSpec((TOK_TILE, LANES), lambda t, *_: (t, 0)),
        pl.BlockSpec(memory_space=pl.ANY),
        pl.BlockSpec((None,) + P["g_ple"].shape[1:], lmap),
        pl.BlockSpec((None,) + P["w_ple_gate"].shape[1:], lmap),
        pl.BlockSpec((None,) + P["w_ple_proj"].shape[1:], lmap),
        pl.BlockSpec(g_final.shape, lambda t, *_: (0, 0))]
    if final:
        st_specs = [_state_specs(arrs, axis, nt) for arrs, axis in states]
        in_specs += [s for ins, _ in st_specs for s in ins]
        out_specs = (pl.BlockSpec((TOK_TILE, D_MODEL), lambda t, *_: (jnp.minimum(t, npt - 1), 0)),
                     pl.BlockSpec((TOK_TILE, D_MODEL), lambda t, *_: (jnp.maximum(t - npt, 0), 0)),
                     ) + tuple(out for _, out in st_specs)
        out_shape = (jax.ShapeDtypeStruct(hp.shape, F32), jax.ShapeDtypeStruct(hs.shape, F32),
                     ) + tuple(jax.ShapeDtypeStruct((len(arrs),) + arrs[0].shape, F32) for arrs, _ in states)
    else:
        out_specs = pl.BlockSpec((TOK_TILE, D_MODEL), lambda t, *_: (t, 0))
        out_shape = jax.ShapeDtypeStruct((nt * TOK_TILE, D_MODEL), F32)
    grid_spec = pltpu.PrefetchScalarGridSpec(
        num_scalar_prefetch=4, grid=(nt,), in_specs=in_specs, out_specs=out_specs,
        scratch_shapes=[pltpu.VMEM((2, CBUF_ROWS, D_MODEL), BF16), pltpu.VMEM((D_MODEL, D_MODEL), BF16),
                        pltpu.VMEM((PLE_DIM, D_MODEL), BF16), pltpu.SemaphoreType.DMA((2,))])
    return pl.pallas_call(
        functools.partial(_combine_kernel, final=final, npt=npt, n_state=len(st_in)),
        grid_spec=grid_spec, out_shape=out_shape,
        compiler_params=pltpu.CompilerParams(dimension_semantics=("arbitrary",), vmem_limit_bytes=VMEM_LIMIT),
        name="moe_combine_final" if final else "moe_combine",
    )(*plan[:4], hp, hs, pp, ps, pos, y, P["g_ple"], P["w_ple_gate"], P["w_ple_proj"], g_final, *st_in)


def _ffn(hp, hs, pp, ps, l, P, w_rt, b_rt, g_final, *, states=None):
    n = hp.shape[0] + hs.shape[0]
    nt = n // TOK_TILE
    bound = TOP_K * n + nt * N_EXPERTS * (CHUNK - 1) + N_EXPERTS * (EXP_TILE - 1)
    n_sorted = -(-bound // EXP_TILE) * EXP_TILE
    pos, cnt = _route_call(hp, hs, l, P, w_rt, b_rt)
    plan = _chunk_plan(cnt[:, 0, :N_EXPERTS].astype(jnp.int32), n_sorted // EXP_TILE)
    xs = _dispatch_call(plan, hp, hs, pos, l, P, n_sorted)
    y = _expert_call(plan, xs, l, P["wg"], P["wu"], P["wd"])
    return _combine_call(plan, hp, hs, pp, ps, pos, y, l, P, g_final, states=states)


def _router_weights(l, w_grp_router, b_grp_router, w_exp_router, b_exp_router):
    w_er = jnp.transpose(w_exp_router[l], (1, 0, 2)).reshape(D_MODEL, N_EXPERTS)
    w_rt = jnp.zeros((D_MODEL, LANES), F32).at[:, :N_EXPERTS].set(w_er)
    w_rt = w_rt.at[:, N_EXPERTS:N_EXPERTS + N_GROUPS].set(w_grp_router[l])
    b_rt = jnp.zeros((1, LANES), F32).at[0, :N_EXPERTS].set(b_exp_router[l].reshape(-1))
    b_rt = b_rt.at[0, N_EXPERTS:N_EXPERTS + N_GROUPS].set(b_grp_router[l])
    w_hi = w_rt.astype(BF16)
    w_lo = (w_rt - w_hi.astype(F32)).astype(BF16)
    return jnp.concatenate([w_hi, w_lo], axis=1), b_rt


def kernel(x_prompt, x_sample, state_gla, state_conv, p_prompt, p_sample, g_mix, w_in, w_forget_up, b_forget,
           g_gla_out, w_conv, b_conv, g_conv_ln, b_conv_ln, w_out, g_ffn, w_grp_router, b_grp_router,
           w_exp_router, b_exp_router, w_exp_gate, w_exp_up, w_exp_down, g_ple, w_ple_gate, w_ple_proj, g_final):
    depth = w_in.shape[0]
    nbp, seq_p, _ = x_prompt.shape
    nbs, seq_s, _ = x_sample.shape
    n_p = nbp * seq_p
    n_s = nbs * seq_s

    def rows(v):
        return v.reshape(depth, 1, -1)

    P = {
        "g_mix": rows(g_mix), "w_in_t": jnp.swapaxes(w_in, 1, 2),
        "w_forget_up": w_forget_up, "b_forget": rows(b_forget), "g_gla_out": rows(g_gla_out),
        "w_conv": w_conv, "b_conv": rows(b_conv), "g_conv_ln": rows(g_conv_ln), "b_conv_ln": rows(b_conv_ln),
        "w_out": w_out, "g_ffn": rows(g_ffn), "g_ple": rows(g_ple),
        "w_ple_gate": w_ple_gate, "w_ple_proj": w_ple_proj,
        "wg": w_exp_gate.reshape(depth * N_EXPERTS, D_MODEL, EXPERT_FF),
        "wu": w_exp_up.reshape(depth * N_EXPERTS, D_MODEL, EXPERT_FF),
        "wd": w_exp_down.reshape(depth * N_EXPERTS, EXPERT_FF, D_MODEL),
    }
    g_fin = g_final.reshape(1, -1)
    xp = x_prompt.reshape(n_p, D_MODEL)
    xs = x_sample.reshape(n_s, D_MODEL)
    pp = p_prompt.reshape(depth, n_p, PLE_DIM)
    ps = p_sample.reshape(depth, n_s, PLE_DIM)
    s_in = state_gla.reshape(depth, nbs, QK_COLS, GLA_DV)
    c_in_t = jnp.swapaxes(state_conv, 1, 2)

    h = None
    sg_p, sg_s, sc_p, sc_s = [], [], [], []
    for l in range(depth):
        src_p, src_s, off_s = (xp, xs, 0) if l == 0 else (h, h, n_p)
        hp, sgp, scp = _guarded(functools.partial(_mixer_prompt, src_p, 0, nbp, seq_p, l, P))
        hs, sgs, scs = _guarded(functools.partial(_mixer_sample, src_s, off_s, nbs, seq_s, l, s_in, c_in_t, P))
        sg_p.append(sgp)
        sg_s.append(sgs)
        sc_p.append(scp)
        sc_s.append(scs)
        w_rt, b_rt = _router_weights(l, w_grp_router, b_grp_router, w_exp_router, b_exp_router)
        h = _ffn(hp, hs, pp, ps, l, P, w_rt, b_rt, g_fin, states=((sg_p, 0), (sg_s, 0), (sc_s, 1)) if l == depth - 1 else None)

    y_prompt, y_sample, sg_p_all, sg_s_all, sc_s_all = h
    return (y_prompt.reshape(nbp, seq_p, D_MODEL), y_sample.reshape(nbs, seq_s, D_MODEL),
            sg_p_all.reshape(depth, nbp, GLA_HEADS, GLA_DK, GLA_DV), sg_s_all.reshape(depth, nbs, GLA_HEADS, GLA_DK, GLA_DV),
            jnp.stack(sc_p), jnp.swapaxes(sc_s_all, 1, 2))
```

```python
import functools

import jax
import jax.numpy as jnp
from jax import lax
from jax.experimental import pallas as pl
from jax.experimental.pallas import tpu as pltpu

D_MODEL = 1024
GLA_HEADS = 4
GLA_DK = 64
GLA_DV = 128
QK_COLS = GLA_HEADS * GLA_DK
V_COLS = GLA_HEADS * GLA_DV
CONV_WIDTH = 512
CONV_K = 31
GLA_LOWRANK = 16
GLA_TAU = 16.0
GLA_CHUNK = 64
PLE_DIM = 256
N_GROUPS = 4
EXPERTS_PER_GROUP = 8
N_EXPERTS = N_GROUPS * EXPERTS_PER_GROUP
EXPERT_FF = 256
TOP_K = 2
EPS = 1e-6
N_MAIN = 2 * QK_COLS + 2 * V_COLS

LANES = 128
SUBLANES = 8
CONV_PAD = 32
CONV_OFF = CONV_PAD - (CONV_K - 1)
VMEM_LIMIT = 56 * 1024 * 1024
TOK_TILE = 512
CHUNK = 16
EXP_TILE = 512
CBUF_ROWS = -(-(TOP_K * TOK_TILE + N_EXPERTS * (CHUNK - 1)) // LANES) * LANES
XS_COLS = D_MODEL + LANES
DECAY_SAFE = 80.0

F32 = jnp.float32
BF16 = jnp.bfloat16
HI = lax.Precision.HIGHEST


def _sigmoid(x):
    return 1.0 / (1.0 + jnp.exp(-x))


def _silu(x):
    return x * _sigmoid(x)


def _log_sigmoid(x):
    return jnp.minimum(x, 0.0) - jnp.log(1.0 + jnp.exp(-jnp.abs(x)))


def _rmsnorm(x, g):
    return x * lax.rsqrt(jnp.mean(x * x, axis=-1, keepdims=True) + EPS) * g


def _dot(a, b):
    return jnp.dot(a.astype(BF16), b.astype(BF16), preferred_element_type=F32)


def _dot_t(a, b):
    return lax.dot_general(a.astype(BF16), b.astype(BF16), (((1,), (1,)), ((), ())),
                           preferred_element_type=F32)


def _dot_hi(a, b):
    return jnp.dot(a, b, preferred_element_type=F32, precision=HI)


def _iota_f32(shape, dim):
    return lax.broadcasted_iota(jnp.int32, shape, dim).astype(F32)


def _const_spec(shape):
    nd = len(shape)
    return pl.BlockSpec(shape, lambda *_: (0,) * nd)


def _layer_spec(arr, l):
    nd = arr.ndim - 1
    return pl.BlockSpec((None,) + arr.shape[1:], lambda *_: (l,) + (0,) * nd, pipeline_mode=pl.Buffered(1))


def _cast_mixer_weights(w_int_ref, w_fu_ref, w_out_ref, wmain_s, wlr_s, wcv_s, wfu_s, wout_s):
    blk = 4 * LANES
    for r in range(0, N_MAIN, blk):
        wmain_s[:, r:r + blk] = w_int_ref[r:r + blk, :].T.astype(BF16)
    lane = lax.broadcasted_iota(jnp.int32, (D_MODEL, LANES), 1)
    wlr_s[...] = jnp.where(lane < GLA_LOWRANK, w_int_ref[N_MAIN:N_MAIN + LANES, :].T, 0.0).astype(BF16)
    cv0 = N_MAIN + GLA_LOWRANK
    for r in range(0, 2 * CONV_WIDTH, blk):
        wcv_s[:, r:r + blk] = w_int_ref[cv0 + r:cv0 + r + blk, :].T.astype(BF16)
    wfu_s[...] = jnp.zeros_like(wfu_s)
    wfu_s[0:GLA_LOWRANK, :] = w_fu_ref[...].astype(BF16)
    wout_s[...] = w_out_ref[...].astype(BF16)


def _project(x, g_mix, wmain_s, wlr_s, wfu_s, b_f, wcv_s):
    xn = _rmsnorm(x, g_mix).astype(BF16)
    cv = jnp.dot(xn, wcv_s[...], preferred_element_type=F32)
    u = cv[:, :CONV_WIDTH] * _sigmoid(cv[:, CONV_WIDTH:])
    lr = jnp.dot(xn, wlr_s[...], preferred_element_type=F32)
    zf = _dot(lr, wfu_s[...]) + b_f
    la = _log_sigmoid(zf) * (1.0 / GLA_TAU)
    zqk = jnp.dot(xn, wmain_s[:, :2 * QK_COLS], preferred_element_type=F32)
    q = zqk[:, :QK_COLS] * (GLA_DK ** -0.5)
    k = zqk[:, QK_COLS:]
    zvo = jnp.dot(xn, wmain_s[:, 2 * QK_COLS:], preferred_element_type=F32)
    v = zvo[:, :V_COLS]
    og = zvo[:, V_COLS:]
    return q, k, v, og, la, u


def _stack_heads(qd):
    lane = lax.broadcasted_iota(jnp.int32, qd.shape, 1)
    return jnp.concatenate(
        [jnp.where((lane >= h * GLA_DK) & (lane < (h + 1) * GLA_DK), qd, 0.0) for h in range(GLA_HEADS)],
        axis=0)


def _gated_head_norm(o, og, g_gla):
    outs = []
    for h in range(GLA_HEADS):
        sl = slice(h * GLA_DV, (h + 1) * GLA_DV)
        outs.append(_rmsnorm(o[:, sl], g_gla) * _silu(og[:, sl]))
    return jnp.concatenate(outs, axis=1)


def _causal_conv(win, w_conv, b_conv, n):
    acc = jnp.broadcast_to(b_conv, (n, CONV_WIDTH))
    for s in range(SUBLANES):
        taps = [j for j in range(CONV_K) if (CONV_OFF + j) % SUBLANES == s]
        if not taps:
            continue
        rows = n if s == 0 else n + SUBLANES
        part = None
        for j in taps:
            a = (CONV_OFF + j) - s
            term = w_conv[j:j + 1, :] * win[a:a + rows, :]
            part = term if part is None else part + term
        acc = acc + part[s:s + n, :]
    return acc


def _conv_ln_act(acc, g_ln, b_ln):
    mu = jnp.mean(acc, axis=-1, keepdims=True)
    xc = acc - mu
    y = xc * lax.rsqrt(jnp.mean(xc * xc, axis=-1, keepdims=True) + EPS) * g_ln + b_ln
    return _silu(y)


def _head_diag(upd):
    return jnp.concatenate([upd[h * GLA_DK:(h + 1) * GLA_DK, h * GLA_DV:(h + 1) * GLA_DV]
                            for h in range(GLA_HEADS)], axis=0)


def _mixer_weight_args(l, P):
    names = ("g_mix", "w_in_t", "w_forget_up", "b_forget", "g_gla_out", "w_conv", "b_conv", "g_conv_ln",
             "b_conv_ln", "w_out")
    arrs = [P[n] for n in names]
    return arrs, [_layer_spec(a, l) for a in arrs]


_MIXER_WEIGHT_SCRATCH = [pltpu.VMEM((D_MODEL, N_MAIN), BF16), pltpu.VMEM((D_MODEL, LANES), BF16),
                         pltpu.VMEM((D_MODEL, 2 * CONV_WIDTH), BF16), pltpu.VMEM((LANES, QK_COLS), BF16),
                         pltpu.VMEM((D_MODEL, D_MODEL), BF16)]


def _mixer_prompt_kernel(x_ref, g_mix_ref, w_int_ref, w_fu_ref, b_f_ref, g_gla_ref, w_conv_ref,
                         b_conv_ref, g_ln_ref, b_ln_ref, w_out_ref,
                         h_ref, sg_ref, sc_ref, risk_ref,
                         wmain_s, wlr_s, wcv_s, wfu_s, wout_s,
                         s_ref, ubuf_ref, qs_ref, kd_ref, klt_ref, dec_ref, mid_ref, v_ref, og_ref, mix_ref,
                         *stable_refs, tt, stable):
    t = pl.program_id(1)
    nt = pl.num_programs(1)
    C = GLA_CHUNK
    n_chunks = tt // C

    @pl.when((pl.program_id(0) == 0) & (t == 0))
    def _():
        _cast_mixer_weights(w_int_ref, w_fu_ref, w_out_ref, wmain_s, wlr_s, wcv_s, wfu_s, wout_s)

    @pl.when(t == 0)
    def _():
        s_ref[...] = jnp.zeros_like(s_ref)
        ubuf_ref[0:CONV_PAD, :] = jnp.zeros((CONV_PAD, CONV_WIDTH), F32)

    x = x_ref[...]
    q, k, v, og, la, u = _project(x, g_mix_ref[...], wmain_s, wlr_s, wfu_s, b_f_ref[...], wcv_s)
    ubuf_ref[CONV_PAD:CONV_PAD + tt, :] = u
    v_ref[...] = v.astype(BF16)
    og_ref[...] = _silu(og)

    w_conv = w_conv_ref[...]
    b_conv = b_conv_ref[...]
    g_ln = g_ln_ref[...]
    b_ln = b_ln_ref[...]
    row = lax.broadcasted_iota(jnp.int32, (C, C), 0)
    col = lax.broadcasted_iota(jnp.int32, (C, C), 1)
    tri = (col <= row).astype(F32)
    risk = jnp.zeros((1, QK_COLS), F32)
    for c in range(n_chunks):
        rows = slice(c * C, (c + 1) * C)
        win = ubuf_ref[c * C:c * C + C + CONV_PAD, :]
        cact = _conv_ln_act(_causal_conv(win, w_conv, b_conv, C), g_ln, b_ln)
        mix_ref[rows, V_COLS:] = cact.astype(BF16)
        bits = pltpu.bitcast(cact[C - SUBLANES:C, 0:QK_COLS], jnp.uint32)
        zero = pltpu.bitcast(lax.shift_right_logical(lax.shift_right_logical(bits, jnp.uint32(16)), jnp.uint32(16)),
                             F32)[0:1, :]
        qc = q[rows, :] + zero
        b = _dot_hi(tri, la[rows, :])
        b_last = b[C - 1:C, :]
        if stable:
            q_st, k_st, b_st = stable_refs
            q_st[rows, :] = qc
            k_st[rows, :] = k[rows, :]
            b_st[rows, :] = b
            b_mid = jnp.zeros_like(b_last)
        else:
            b_mid = b[C // 2 - 1:C // 2, :]
            risk = jnp.maximum(risk, jnp.maximum(-b_mid, b_mid - b_last))
        qs_ref[c] = _stack_heads(qc * jnp.exp(b - b_mid)).astype(BF16)
        kd_ref[rows, :] = (k[rows, :] * jnp.exp(jnp.minimum(b_mid - b, DECAY_SAFE))).astype(BF16)
        kl = k[rows, :] * jnp.exp(b_last - b)
        klt = jnp.concatenate([kl, jnp.broadcast_to(jnp.exp(b_last), (C // 2, QK_COLS)),
                               jnp.broadcast_to(jnp.exp(b_mid), (C // 2, QK_COLS))], axis=0).T
        klt_ref[c] = klt.astype(BF16)
        dec_ref[c] = jnp.broadcast_to(klt[:, C:C + 1], (QK_COLS, GLA_DV))
        mid_ref[c] = jnp.broadcast_to(klt[:, 3 * C // 2:3 * C // 2 + 1], (QK_COLS, GLA_DV))

    r4 = lax.broadcasted_iota(jnp.int32, (GLA_HEADS * C, C), 0)
    c4 = lax.broadcasted_iota(jnp.int32, (GLA_HEADS * C, C), 1)
    causal4 = c4 <= (r4 % C)
    g_gla = g_gla_ref[...]
    s = s_ref[...]
    for c in range(n_chunks):
        rows = slice(c * C, (c + 1) * C)
        qs = qs_ref[c]
        vc = v_ref[rows, :]
        if stable:
            scores = _direct_scores(*stable_refs, c * C, C).astype(BF16)
        else:
            scores = jnp.where(causal4, _dot_t(qs, kd_ref[rows, :]), 0.0).astype(BF16)
        o_inter = jnp.dot(qs, (mid_ref[c] * s).astype(BF16), preferred_element_type=F32)
        upd = jnp.dot(klt_ref[c][:, :C], vc, preferred_element_type=F32)
        s = dec_ref[c] * s + _head_diag(upd)
        o_parts = []
        for h in range(GLA_HEADS):
            vh = vc[:, h * GLA_DV:(h + 1) * GLA_DV]
            o_parts.append(jnp.dot(scores[h * C:(h + 1) * C, :], vh, preferred_element_type=F32)
                           + o_inter[h * C:(h + 1) * C, :])
        o = jnp.concatenate(o_parts, axis=1)
        gated = []
        for h in range(GLA_HEADS):
            sl = slice(h * GLA_DV, (h + 1) * GLA_DV)
            gated.append(_rmsnorm(o[:, sl], g_gla) * og_ref[rows, sl])
        mix_ref[rows, 0:V_COLS] = jnp.concatenate(gated, axis=1).astype(BF16)
    s_ref[...] = s

    h_ref[...] = x + jnp.dot(mix_ref[...], wout_s[...], preferred_element_type=F32)
    tail = ubuf_ref[tt:tt + CONV_PAD, :]
    ubuf_ref[0:CONV_PAD, :] = tail

    risk_ref[0] = jnp.broadcast_to(jnp.max(risk, axis=-1, keepdims=True), (SUBLANES, LANES))

    @pl.when(t == nt - 1)
    def _():
        sg_ref[0] = s
        sc_ref[0] = tail[CONV_OFF:, :]


def _direct_scores(q_st, k_st, b_st, r0, C):
    kc = k_st[r0:r0 + C, :]
    bc = b_st[r0:r0 + C, :]
    srow = lax.broadcasted_iota(jnp.int32, (C, QK_COLS), 0)
    lane = lax.broadcasted_iota(jnp.int32, (C, LANES), 1)
    head_of = lax.broadcasted_iota(jnp.int32, (QK_COLS, LANES), 0) // GLA_DK
    head_sum = (head_of == lax.broadcasted_iota(jnp.int32, (QK_COLS, LANES), 1)).astype(F32)

    def one_query(t, acc):
        d = b_st[pl.ds(r0 + t, 1), :] - bc
        w = jnp.where(srow <= t, jnp.exp(jnp.minimum(d, 0.0)), 0.0) * kc * q_st[pl.ds(r0 + t, 1), :]
        per_head = _dot_hi(w, head_sum)
        return tuple(jnp.where(lane == t, per_head[:, h:h + 1], acc[h]) for h in range(GLA_HEADS))

    acc = lax.fori_loop(0, C, one_query, tuple(jnp.zeros((C, LANES), F32) for _ in range(GLA_HEADS)))
    return jnp.concatenate([a.T[0:C, :] for a in acc], axis=0)


def _mixer_prompt(x, row_off, nb, seq, l, P, *, tt=512, stable=False):
    nt = seq // tt
    n_chunks = tt // GLA_CHUNK
    blk_off = row_off // tt
    weights, w_specs = _mixer_weight_args(l, P)
    in_specs = [pl.BlockSpec((tt, D_MODEL), lambda b, t: (blk_off + b * nt + t, 0))] + w_specs
    out_shape = (jax.ShapeDtypeStruct((nb * seq, D_MODEL), F32),
                 jax.ShapeDtypeStruct((nb, QK_COLS, GLA_DV), F32),
                 jax.ShapeDtypeStruct((nb, CONV_K - 1, CONV_WIDTH), F32),
                 jax.ShapeDtypeStruct((nb * nt, SUBLANES, LANES), F32))
    out_specs = (pl.BlockSpec((tt, D_MODEL), lambda b, t: (b * nt + t, 0)),
                 pl.BlockSpec((1, QK_COLS, GLA_DV), lambda b, t: (b, 0, 0)),
                 pl.BlockSpec((1, CONV_K - 1, CONV_WIDTH), lambda b, t: (b, 0, 0)),
                 pl.BlockSpec((1, SUBLANES, LANES), lambda b, t: (b * nt + t, 0, 0)))
    scratch = _MIXER_WEIGHT_SCRATCH + [
        pltpu.VMEM((QK_COLS, GLA_DV), F32),
        pltpu.VMEM((CONV_PAD + tt, CONV_WIDTH), F32),
        pltpu.VMEM((n_chunks, GLA_HEADS * GLA_CHUNK, QK_COLS), BF16), pltpu.VMEM((tt, QK_COLS), BF16),
        pltpu.VMEM((n_chunks, QK_COLS, 2 * GLA_CHUNK), BF16), pltpu.VMEM((n_chunks, QK_COLS, GLA_DV), F32),
        pltpu.VMEM((n_chunks, QK_COLS, GLA_DV), F32),
        pltpu.VMEM((tt, V_COLS), BF16), pltpu.VMEM((tt, V_COLS), F32),
        pltpu.VMEM((tt, D_MODEL), BF16)]
    if stable:
        scratch += [pltpu.VMEM((tt, QK_COLS), F32)] * 3
    return pl.pallas_call(
        functools.partial(_mixer_prompt_kernel, tt=tt, stable=stable),
        grid=(nb, nt), in_specs=in_specs, out_specs=out_specs, out_shape=out_shape,
        scratch_shapes=scratch,
        compiler_params=pltpu.CompilerParams(dimension_semantics=("arbitrary", "arbitrary"),
                                             vmem_limit_bytes=VMEM_LIMIT),
        name="mixer_prompt_stable" if stable else "mixer_prompt",
    )(x, *weights)


def _guarded(mixer):
    *outs, risk = mixer(stable=False)
    return lax.cond(jnp.max(risk) > DECAY_SAFE, lambda: tuple(mixer(stable=True)[:-1]), lambda: tuple(outs))


def _mixer_sample_kernel(x_ref, s_in_ref, c_in_ref, g_mix_ref, w_int_ref, w_fu_ref, b_f_ref,
                         g_gla_ref, w_conv_ref, b_conv_ref, g_ln_ref, b_ln_ref, w_out_ref,
                         h_ref, sg_ref, sc_ref, risk_ref,
                         wmain_s, wlr_s, wcv_s, wfu_s, wout_s,
                         u4_ref, oi_ref, cacc4_ref, *, sb, seq, stable):
    R = sb * seq
    n_slabs = CONV_WIDTH // LANES

    @pl.when(pl.program_id(0) == 0)
    def _():
        _cast_mixer_weights(w_int_ref, w_fu_ref, w_out_ref, wmain_s, wlr_s, wcv_s, wfu_s, wout_s)

    x = x_ref[...]
    q, k, v, og, la, u = _project(x, g_mix_ref[...], wmain_s, wlr_s, wfu_s, b_f_ref[...], wcv_s)

    for kk in range(n_slabs):
        u4_ref[kk] = u[:, kk * LANES:(kk + 1) * LANES]
    full = [c_in_ref[j] for j in range(CONV_K - 1)]
    for t in range(seq):
        full.append(jnp.concatenate([u4_ref.at[kk][pl.ds(t, sb, stride=seq), :] for kk in range(n_slabs)], axis=1))
    w_conv = w_conv_ref[...]
    for t in range(seq):
        acc = jnp.broadcast_to(b_conv_ref[...], (sb, CONV_WIDTH))
        for j in range(CONV_K):
            acc = acc + w_conv[j:j + 1, :] * full[t + j]
        for kk in range(n_slabs):
            cacc4_ref.at[kk][pl.ds(t, sb, stride=seq), :] = acc[:, kk * LANES:(kk + 1) * LANES]
    for j in range(CONV_K - 1):
        sc_ref[j] = full[seq + j]

    row = lax.broadcasted_iota(jnp.int32, (R, R), 0)
    col = lax.broadcasted_iota(jnp.int32, (R, R), 1)
    same = (row // seq) == (col // seq)
    b = _dot_hi((same & (col <= row)).astype(F32), la)
    b_tot = _dot_hi(same.astype(F32), la)
    qd = q * jnp.exp(b)
    kl = k * jnp.exp(b_tot - b)
    qs = _stack_heads(qd)
    risk_ref[0] = jnp.broadcast_to(jnp.max(jnp.max(-b_tot, axis=-1, keepdims=True), axis=0, keepdims=True),
                                   (SUBLANES, LANES))
    if stable:
        head_of = lax.broadcasted_iota(jnp.int32, (QK_COLS, LANES), 0) // GLA_DK
        head_sum = (head_of == lax.broadcasted_iota(jnp.int32, (QK_COLS, LANES), 1)).astype(F32)
        t_in_seq = lax.broadcasted_iota(jnp.int32, (R, QK_COLS), 0) % seq
        parts = [jnp.zeros((R, R), F32) for _ in range(GLA_HEADS)]
        for d in range(seq):
            k_d = jnp.concatenate([jnp.zeros((d, QK_COLS), F32), k[:R - d, :]], axis=0) if d else k
            b_d = jnp.concatenate([jnp.zeros((d, QK_COLS), F32), b[:R - d, :]], axis=0) if d else b
            w = jnp.where(t_in_seq >= d, jnp.exp(jnp.minimum(b - b_d, 0.0)), 0.0) * k_d * q
            per_head = _dot_hi(w, head_sum)
            for h in range(GLA_HEADS):
                parts[h] = jnp.where(col == row - d, per_head[:, h:h + 1], parts[h])
        scores = jnp.concatenate(parts, axis=0)
    else:
        kd = k * jnp.exp(jnp.minimum(-b, DECAY_SAFE))
        r4 = lax.broadcasted_iota(jnp.int32, (GLA_HEADS * R, R), 0) % R
        c4 = lax.broadcasted_iota(jnp.int32, (GLA_HEADS * R, R), 1)
        mask4 = ((r4 // seq) == (c4 // seq)) & (c4 <= r4)
        scores = jnp.where(mask4, _dot_t(qs, kd), 0.0)

    klt = kl.T
    dect = jnp.exp(b_tot).T
    lane_h = lax.broadcasted_iota(jnp.int32, (GLA_DK, R), 1)
    lane_r = lax.broadcasted_iota(jnp.int32, (QK_COLS, R), 1)
    upd = []
    for h in range(GLA_HEADS):
        klt_h = klt[h * GLA_DK:(h + 1) * GLA_DK, :]
        lhs = jnp.concatenate([jnp.where((lane_h >= i * seq) & (lane_h < (i + 1) * seq), klt_h, 0.0)
                               for i in range(sb)], axis=0)
        upd.append(_dot(lhs, v[:, h * GLA_DV:(h + 1) * GLA_DV]))
    for i in range(sb):
        s_old = s_in_ref[i]
        qsel = jnp.concatenate([qs[h * R + i * seq:h * R + (i + 1) * seq, :] for h in range(GLA_HEADS)], axis=0)
        oi = _dot(qsel, s_old)
        for h in range(GLA_HEADS):
            oi_ref[h, i * seq:(i + 1) * seq, :] = oi[h * seq:(h + 1) * seq, :]
        smask = (lane_r >= i * seq) & (lane_r < (i + 1) * seq)
        dec = jnp.sum(jnp.where(smask, dect, 0.0), axis=1, keepdims=True) * (1.0 / seq)
        u_new = jnp.concatenate([upd[h][i * GLA_DK:(i + 1) * GLA_DK, :] for h in range(GLA_HEADS)], axis=0)
        sg_ref[i] = dec * s_old + u_new

    o_parts = []
    for h in range(GLA_HEADS):
        vh = v[:, h * GLA_DV:(h + 1) * GLA_DV]
        o_parts.append(_dot(scores[h * R:(h + 1) * R, :], vh) + oi_ref[h])
    o = jnp.concatenate(o_parts, axis=1)
    cacc = jnp.concatenate([cacc4_ref[kk] for kk in range(n_slabs)], axis=1)
    mix = jnp.concatenate([_gated_head_norm(o, og, g_gla_ref[...]),
                           _conv_ln_act(cacc, g_ln_ref[...], b_ln_ref[...])], axis=1)
    h_ref[...] = x + jnp.dot(mix.astype(BF16), wout_s[...], preferred_element_type=F32)


def _mixer_sample(x, row_off, nb, seq, l, s_in, c_in_t, P, *, sb=16, stable=False):
    R = sb * seq
    blk_off = row_off // R
    n_slabs = CONV_WIDTH // LANES
    weights, w_specs = _mixer_weight_args(l, P)
    in_specs = [pl.BlockSpec((R, D_MODEL), lambda i: (blk_off + i, 0)),
                pl.BlockSpec((None, sb, QK_COLS, GLA_DV), lambda i: (l, i, 0, 0)),
                pl.BlockSpec((None, CONV_K - 1, sb, CONV_WIDTH), lambda i: (l, 0, i, 0))] + w_specs
    out_shape = (jax.ShapeDtypeStruct((nb * seq, D_MODEL), F32),
                 jax.ShapeDtypeStruct((nb, QK_COLS, GLA_DV), F32),
                 jax.ShapeDtypeStruct((CONV_K - 1, nb, CONV_WIDTH), F32),
                 jax.ShapeDtypeStruct((nb // sb, SUBLANES, LANES), F32))
    out_specs = (pl.BlockSpec((R, D_MODEL), lambda i: (i, 0)),
                 pl.BlockSpec((sb, QK_COLS, GLA_DV), lambda i: (i, 0, 0)),
                 pl.BlockSpec((CONV_K - 1, sb, CONV_WIDTH), lambda i: (0, i, 0)),
                 pl.BlockSpec((1, SUBLANES, LANES), lambda i: (i, 0, 0)))
    scratch = _MIXER_WEIGHT_SCRATCH + [
        pltpu.VMEM((n_slabs, R, LANES), F32),
        pltpu.VMEM((GLA_HEADS, R, GLA_DV), F32),
        pltpu.VMEM((n_slabs, R, LANES), F32)]
    return pl.pallas_call(
        functools.partial(_mixer_sample_kernel, sb=sb, seq=seq, stable=stable),
        grid=(nb // sb,), in_specs=in_specs, out_specs=out_specs, out_shape=out_shape,
        scratch_shapes=scratch,
        compiler_params=pltpu.CompilerParams(dimension_semantics=("arbitrary",),
                                             vmem_limit_bytes=VMEM_LIMIT),
        name="mixer_sample_stable" if stable else "mixer_sample",
    )(x, s_in, c_in_t, *weights)


def _pair_specs(n_first_tiles, width):
    return [pl.BlockSpec((TOK_TILE, width), lambda t, *_: (jnp.minimum(t, n_first_tiles - 1), 0)),
            pl.BlockSpec((TOK_TILE, width), lambda t, *_: (jnp.maximum(t - n_first_tiles, 0), 0))]


def _pick(t, n_first_tiles, a_ref, b_ref):
    return jnp.where(t < n_first_tiles, a_ref[...], b_ref[...])


def _route(logits):
    lane = lax.broadcasted_iota(jnp.int32, logits.shape, 1)
    lane_f = lane.astype(F32)
    neg = jnp.float32(-jnp.inf)
    big = jnp.float32(1e9)
    is_grp = (lane >= N_EXPERTS) & (lane < N_EXPERTS + N_GROUPS)
    gl = jnp.where(is_grp, logits, neg)
    gmax = jnp.max(gl, axis=-1, keepdims=True)
    gidx = jnp.min(jnp.where(is_grp & (gl == gmax), lane_f - N_EXPERTS, big), axis=-1, keepdims=True)
    gsum = jnp.sum(jnp.where(is_grp, jnp.exp(gl - gmax), 0.0), axis=-1, keepdims=True)
    g_w = 1.0 / gsum
    grp_of_lane = jnp.floor(lane_f * (1.0 / EXPERTS_PER_GROUP))
    in_grp = (lane < N_EXPERTS) & (grp_of_lane == gidx)
    ml = jnp.where(in_grp, logits, neg)
    v1 = jnp.max(ml, axis=-1, keepdims=True)
    i1 = jnp.min(jnp.where(in_grp & (ml == v1), lane_f, big), axis=-1, keepdims=True)
    ml2 = jnp.where(lane_f == i1, neg, ml)
    v2 = jnp.max(ml2, axis=-1, keepdims=True)
    i2 = jnp.min(jnp.where(in_grp & (ml2 == v2), lane_f, big), axis=-1, keepdims=True)
    e2 = jnp.exp(v2 - v1)
    w1 = g_w / (1.0 + e2)
    w2 = g_w * e2 / (1.0 + e2)
    return i1, i2, w1, w2


def _route_kernel(hp_ref, hs_ref, g_ffn_ref, w_rt_ref, b_rt_ref, pos_ref, cnt_ref, *, npt):
    T = TOK_TILE
    xn = _rmsnorm(_pick(pl.program_id(0), npt, hp_ref, hs_ref), g_ffn_ref[...])
    x_hi = xn.astype(BF16)
    x_lo = (xn - x_hi.astype(F32)).astype(BF16)
    w_split = w_rt_ref[...]
    both = jnp.dot(x_hi, w_split, preferred_element_type=F32)
    logits = (both[:, :LANES] + both[:, LANES:] + jnp.dot(x_lo, w_split[:, :LANES], preferred_element_type=F32)
              + b_rt_ref[...])
    i1, i2, w1, w2 = _route(logits)
    lane = _iota_f32((T, LANES), 1)
    a0 = (lane == i1).astype(F32)
    a1 = (lane == i2).astype(F32)
    a = a0 + a1
    cnt = jnp.sum(a, axis=0, keepdims=True)
    earlier = (_iota_f32((T, T), 1) < _iota_f32((T, T), 0)).astype(BF16)
    rank = jnp.dot(earlier, a.astype(BF16), preferred_element_type=F32)
    cnt_pad = jnp.ceil(cnt * (1.0 / CHUNK)) * CHUNK
    below = (_iota_f32((LANES, LANES), 0) < _iota_f32((LANES, LANES), 1)).astype(F32)
    first = _dot_hi(jnp.broadcast_to(cnt_pad, (SUBLANES, LANES)), below)[0:1, :]
    base = first + rank
    pos0 = jnp.sum(a0 * base, axis=1, keepdims=True)
    pos1 = jnp.sum(a1 * base, axis=1, keepdims=True)
    pos_ref[...] = jnp.where(lane == 0.0, pos0, jnp.where(lane == 1.0, pos1, jnp.where(
        lane == 2.0, w1, jnp.where(lane == 3.0, w2, jnp.where(lane == 4.0, i1, 0.0)))))
    cnt_ref[0] = jnp.broadcast_to(cnt, (SUBLANES, LANES))


def _route_call(hp, hs, l, P):
    npt = hp.shape[0] // TOK_TILE
    nt = npt + hs.shape[0] // TOK_TILE
    return pl.pallas_call(
        functools.partial(_route_kernel, npt=npt), grid=(nt,),
        in_specs=_pair_specs(npt, D_MODEL) + [_layer_spec(P[n], l) for n in ("g_ffn", "w_rt", "b_rt")],
        out_specs=(pl.BlockSpec((TOK_TILE, LANES), lambda t: (t, 0)),
                   pl.BlockSpec((1, SUBLANES, LANES), lambda t: (t, 0, 0))),
        out_shape=(jax.ShapeDtypeStruct((nt * TOK_TILE, LANES), F32),
                   jax.ShapeDtypeStruct((nt, SUBLANES, LANES), F32)),
        compiler_params=pltpu.CompilerParams(dimension_semantics=("arbitrary",), vmem_limit_bytes=VMEM_LIMIT),
        name="moe_route",
    )(hp, hs, P["g_ffn"], P["w_rt"], P["b_rt"])


def _chunk_plan(cnt, n_row_tiles):
    n16 = (cnt + (CHUNK - 1)) // CHUNK
    lofs16 = jnp.cumsum(n16, axis=1) - n16
    tile_pref16 = jnp.cumsum(n16, axis=0) - n16
    tot16 = jnp.sum(n16, axis=0)
    per_tile = EXP_TILE // CHUNK
    seg16 = ((tot16 + per_tile - 1) // per_tile) * per_tile
    seg_end16 = jnp.cumsum(seg16)
    dst16 = (seg_end16 - seg16)[None, :] + tile_pref16
    n_tot = jnp.sum(n16, axis=1)
    gap16 = seg_end16 - seg16 + tot16
    gapn16 = seg16 - tot16
    tile_start16 = jnp.arange(n_row_tiles, dtype=jnp.int32) * per_tile
    n_valid = seg_end16[-1] // per_tile
    misc = n_valid.reshape(1)
    exp_of_tile = jnp.minimum(jnp.sum(seg_end16[None, :] <= tile_start16[:, None], axis=1), N_EXPERTS - 1)
    i32 = lambda a: a.astype(jnp.int32).reshape(-1)
    return (i32(dst16), i32(n16), i32(lofs16), i32(n_tot), i32(gap16), i32(gapn16), i32(misc), i32(exp_of_tile))


def _chunk_copy(src, dst, src_chunk, dst_chunk, sem, n_chunks=1):
    rows = n_chunks * CHUNK
    return pltpu.make_async_copy(src.at[pl.ds(pl.multiple_of(src_chunk * CHUNK, CHUNK), rows), :],
                                 dst.at[pl.ds(pl.multiple_of(dst_chunk * CHUNK, CHUNK), rows), :], sem)


def _slab_copies(src, dst, sem, tile, src_ofs_ref, dst_ofs_ref, n16_ref):
    for e in range(N_EXPERTS):
        k = tile * N_EXPERTS + e
        n = n16_ref[k]

        @pl.when(n > 0)
        def _(k=k, n=n):
            _chunk_copy(src, dst, src_ofs_ref[k], dst_ofs_ref[k], sem, n).start()


def _wait_slabs(src, dst, sem, n_chunks):
    @pl.when(n_chunks > 0)
    def _():
        _chunk_copy(src, dst, 0, 0, sem, n_chunks).wait()


def _tile_copy(src, dst, dst_tile, sem):
    return pltpu.make_async_copy(src, dst.at[pl.ds(pl.multiple_of(dst_tile * EXP_TILE, EXP_TILE), EXP_TILE), :], sem)


def _dispatch_kernel(dst16_ref, n16_ref, lofs16_ref, ntot_ref, gap16_ref, gapn16_ref, misc_ref,
                     hp_ref, hs_ref, pos_ref, g_ffn_ref, xs_hbm, cbuf, zbuf, sem, *, n_row_tiles, npt):
    t = pl.program_id(0)
    nt = pl.num_programs(0)
    slot = t % 2
    T = TOK_TILE
    n_tail = n_row_tiles - misc_ref[0]

    @pl.when(t == 0)
    def _():
        zbuf[...] = jnp.zeros_like(zbuf)
        for e in range(N_EXPERTS):
            g = gapn16_ref[e]

            @pl.when(g > 0)
            def _(e=e, g=g):
                _chunk_copy(zbuf, xs_hbm, 0, gap16_ref[e], sem.at[2], g).start()

        def fill_tile(i, carry):
            _tile_copy(zbuf, xs_hbm, misc_ref[0] + i, sem.at[2]).start()
            return carry
        lax.fori_loop(0, n_tail, fill_tile, 0)

    xn = _rmsnorm(_pick(t, npt, hp_ref, hs_ref), g_ffn_ref[...]).astype(BF16)
    pos = pos_ref[...]
    pos_t = pos.T
    rows = _iota_f32((CBUF_ROWS, T), 0)
    onehot = jnp.where((rows == pos_t[0:1, :]) | (rows == pos_t[1:2, :]), 1.0, 0.0).astype(BF16)
    lane = lax.broadcasted_iota(jnp.int32, (T, LANES), 1)
    extra = jnp.zeros((T, LANES), F32)
    for s in range(TOP_K):
        c = pos[:, 2 + s:3 + s]
        hi = c.astype(BF16).astype(F32)
        mid = (c - hi).astype(BF16).astype(F32)
        lo = c - hi - mid
        for j, piece in enumerate((hi, mid, lo)):
            extra = jnp.where(lane == 3 * s + j, piece, extra)
    extra = jnp.where(lane == 3 * TOP_K, pos[:, 4:5], extra)
    cbuf[slot] = jnp.dot(onehot, jnp.concatenate([xn, extra.astype(BF16)], axis=1),
                         preferred_element_type=F32).astype(BF16)

    src = cbuf.at[slot]
    _slab_copies(src, xs_hbm, sem.at[slot], t, lofs16_ref, dst16_ref, n16_ref)

    @pl.when(t > 0)
    def _():
        _wait_slabs(cbuf.at[1 - slot], xs_hbm, sem.at[1 - slot], ntot_ref[t - 1])

    @pl.when(t == nt - 1)
    def _():
        _wait_slabs(src, xs_hbm, sem.at[slot], ntot_ref[t])
        for e in range(N_EXPERTS):
            _wait_slabs(zbuf, xs_hbm, sem.at[2], gapn16_ref[e])

        def wait_tile(_, carry):
            _tile_copy(zbuf, xs_hbm, 0, sem.at[2]).wait()
            return carry
        lax.fori_loop(0, n_tail, wait_tile, 0)


def _dispatch_call(plan, hp, hs, pos, l, P, n_sorted):
    npt = hp.shape[0] // TOK_TILE
    nt = npt + hs.shape[0] // TOK_TILE
    g_ffn = P["g_ffn"]
    grid_spec = pltpu.PrefetchScalarGridSpec(
        num_scalar_prefetch=7, grid=(nt,),
        in_specs=_pair_specs(npt, D_MODEL) + [
            pl.BlockSpec((TOK_TILE, LANES), lambda t, *_: (t, 0)),
            pl.BlockSpec((None,) + g_ffn.shape[1:], lambda t, *_: (l, 0, 0))],
        out_specs=pl.BlockSpec(memory_space=pl.ANY),
        scratch_shapes=[pltpu.VMEM((2, CBUF_ROWS, XS_COLS), BF16), pltpu.VMEM((EXP_TILE, XS_COLS), BF16),
                        pltpu.SemaphoreType.DMA((3,))])
    return pl.pallas_call(
        functools.partial(_dispatch_kernel, n_row_tiles=n_sorted // EXP_TILE, npt=npt), grid_spec=grid_spec,
        out_shape=jax.ShapeDtypeStruct((n_sorted, XS_COLS), BF16),
        compiler_params=pltpu.CompilerParams(dimension_semantics=("arbitrary",), vmem_limit_bytes=VMEM_LIMIT),
        name="moe_dispatch",
    )(*plan[:7], hp, hs, pos, g_ffn)


def _expert_kernel(eot_ref, misc_ref, xs_ref, wg_ref, wu_ref, wd_ref, y_ref, wg_s, wu_s, wd_s):
    i = pl.program_id(0)
    valid = i < misc_ref[0]

    @pl.when(jnp.logical_not(valid))
    def _():
        y_ref[...] = jnp.zeros_like(y_ref)

    @pl.when(valid & ((i == 0) | (eot_ref[i] != eot_ref[jnp.maximum(i - 1, 0)])))
    def _():
        wg_s[...] = wg_ref[...].astype(BF16)
        wu_s[...] = wu_ref[...].astype(BF16)
        wd_s[...] = wd_ref[...].astype(BF16)

    @pl.when(valid)
    def _():
        xs = xs_ref[...]
        x = xs[:, :D_MODEL]
        ex = xs[:, D_MODEL:].astype(F32)
        lane = lax.broadcasted_iota(jnp.int32, ex.shape, 1)
        id0 = jnp.sum(jnp.where(lane == 3 * TOP_K, ex, 0.0), axis=-1, keepdims=True)
        first = id0 == eot_ref[i].astype(F32)
        mine = (first & (lane < 3)) | (jnp.logical_not(first) & (lane >= 3) & (lane < 3 * TOP_K))
        c = jnp.sum(jnp.where(mine, ex, 0.0), axis=-1, keepdims=True)
        hg = _silu(jnp.dot(x, wg_s[...], preferred_element_type=F32)) * jnp.dot(x, wu_s[...],
                                                                               preferred_element_type=F32)
        y_ref[...] = jnp.dot((hg * c).astype(BF16), wd_s[...], preferred_element_type=F32).astype(BF16)


def _expert_call(plan, xs, l, wg, wu, wd):
    misc, exp_of_tile = plan[6:]
    n_row_tiles = xs.shape[0] // EXP_TILE

    def last_valid(i, nv):
        return jnp.maximum(jnp.minimum(i, nv[0] - 1), 0)

    def row_map(i, eot, nv):
        return (last_valid(i, nv), 0)

    def w_map(i, eot, nv):
        return (l * N_EXPERTS + eot[last_valid(i, nv)], 0, 0)

    grid_spec = pltpu.PrefetchScalarGridSpec(
        num_scalar_prefetch=2, grid=(n_row_tiles,),
        in_specs=[pl.BlockSpec((EXP_TILE, XS_COLS), row_map),
                  pl.BlockSpec((None, D_MODEL, EXPERT_FF), w_map),
                  pl.BlockSpec((None, D_MODEL, EXPERT_FF), w_map),
                  pl.BlockSpec((None, EXPERT_FF, D_MODEL), w_map)],
        out_specs=pl.BlockSpec((EXP_TILE, D_MODEL), lambda i, eot, nv: (i, 0)),
        scratch_shapes=[pltpu.VMEM((D_MODEL, EXPERT_FF), BF16), pltpu.VMEM((D_MODEL, EXPERT_FF), BF16),
                        pltpu.VMEM((EXPERT_FF, D_MODEL), BF16)])
    return pl.pallas_call(
        _expert_kernel, grid_spec=grid_spec,
        out_shape=jax.ShapeDtypeStruct((xs.shape[0], D_MODEL), BF16),
        compiler_params=pltpu.CompilerParams(dimension_semantics=("arbitrary",), vmem_limit_bytes=VMEM_LIMIT),
        name="moe_experts",
    )(exp_of_tile, misc, xs, wg, wu, wd)


def _combine_kernel(dst16_ref, n16_ref, lofs16_ref, ntot_ref, hp_ref, hs_ref, pp_ref, ps_ref, pos_ref, y_hbm,
                    g_ple_ref, w_pg_ref, w_pp_ref, g_fin_ref, *rest, final, npt, n_state):
    if final:
        st_in, (op_ref, os_ref, *st_out, ybuf, wpg_s, wpp_s, sem) = rest[:n_state], rest[n_state:]
        depth = n_state // len(st_out)
        for k, out in enumerate(st_out):
            for l in range(depth):
                out[l] = st_in[k * depth + l][...]
    else:
        o_ref, ybuf, wpg_s, wpp_s, sem = rest
    t = pl.program_id(0)
    nt = pl.num_programs(0)
    slot = t % 2

    def fetch(tile, sl):
        _slab_copies(y_hbm, ybuf.at[sl], sem.at[sl], tile, dst16_ref, lofs16_ref, n16_ref)

    @pl.when(t == 0)
    def _():
        ybuf[...] = jnp.zeros_like(ybuf)
        fetch(t, slot)
        wpg_s[...] = w_pg_ref[...].astype(BF16)
        wpp_s[...] = w_pp_ref[...].astype(BF16)

    @pl.when(t + 1 < nt)
    def _():
        fetch(t + 1, 1 - slot)

    _wait_slabs(y_hbm, ybuf.at[slot], sem.at[slot], ntot_ref[t])

    pos = pos_ref[...]
    cols = _iota_f32((TOK_TILE, CBUF_ROWS), 1)
    pick = jnp.where((cols == pos[:, 0:1]) | (cols == pos[:, 1:2]), 1.0, 0.0).astype(BF16)
    h2 = _pick(t, npt, hp_ref, hs_ref) + jnp.dot(pick, ybuf[slot], preferred_element_type=F32)
    xn2 = _rmsnorm(h2, g_ple_ref[...])
    gate = _sigmoid(_dot(xn2, wpg_s[...]))
    p = jnp.where(t < npt, pp_ref[...], ps_ref[...])
    h3 = h2 + gate * _dot(p, wpp_s[...])
    if final:
        h3 = _rmsnorm(h3, g_fin_ref[...])

        @pl.when(t < npt)
        def _():
            op_ref[...] = h3

        @pl.when(t >= npt)
        def _():
            os_ref[...] = h3
    else:
        o_ref[...] = h3


def _state_specs(arrs, axis, nt):
    shape = arrs[0].shape
    unit = SUBLANES if axis == len(shape) - 2 else 1
    per = -(-shape[axis] // (nt * unit)) * unit
    assert shape[axis] % per == 0, (shape, axis, nt)
    n_blocks = shape[axis] // per
    block = shape[:axis] + (per,) + shape[axis + 1:]

    def in_map(t, *_):
        return (0,) * axis + (jnp.minimum(t, n_blocks - 1),) + (0,) * (len(shape) - axis - 1)

    return [pl.BlockSpec(block, in_map)] * len(arrs), pl.BlockSpec((len(arrs),) + block, lambda t, *_: (0,) + in_map(t))


def _combine_call(plan, hp, hs, pp, ps, pos, y, l, P, g_final, *, states=None):
    final = states is not None
    st_in = [a for arrs, _ in states for a in arrs] if final else []
    npt = hp.shape[0] // TOK_TILE
    nst = hs.shape[0] // TOK_TILE
    nt = npt + nst

    def lmap(t, *_):
        return (l, 0, 0)

    in_specs = _pair_specs(npt, D_MODEL) + [
        pl.BlockSpec((None, TOK_TILE, PLE_DIM), lambda t, *_: (l, jnp.minimum(t, npt - 1), 0)),
        pl.BlockSpec((None, TOK_TILE, PLE_DIM), lambda t, *_: (l, jnp.maximum(t - npt, 0), 0)),
        pl.BlockSpec((TOK_TILE, LANES), lambda t, *_: (t, 0)),
        pl.BlockSpec(memory_space=pl.ANY),
        pl.BlockSpec((None,) + P["g_ple"].shape[1:], lmap),
        pl.BlockSpec((None,) + P["w_ple_gate"].shape[1:], lmap),
        pl.BlockSpec((None,) + P["w_ple_proj"].shape[1:], lmap),
        pl.BlockSpec(g_final.shape, lambda t, *_: (0, 0))]
    if final:
        st_specs = [_state_specs(arrs, axis, nt) for arrs, axis in states]
        in_specs += [s for ins, _ in st_specs for s in ins]
        out_specs = (pl.BlockSpec((TOK_TILE, D_MODEL), lambda t, *_: (jnp.minimum(t, npt - 1), 0)),
                     pl.BlockSpec((TOK_TILE, D_MODEL), lambda t, *_: (jnp.maximum(t - npt, 0), 0)),
                     ) + tuple(out for _, out in st_specs)
        out_shape = (jax.ShapeDtypeStruct(hp.shape, F32), jax.ShapeDtypeStruct(hs.shape, F32),
                     ) + tuple(jax.ShapeDtypeStruct((len(arrs),) + arrs[0].shape, F32) for arrs, _ in states)
    else:
        out_specs = pl.BlockSpec((TOK_TILE, D_MODEL), lambda t, *_: (t, 0))
        out_shape = jax.ShapeDtypeStruct((nt * TOK_TILE, D_MODEL), F32)
    grid_spec = pltpu.PrefetchScalarGridSpec(
        num_scalar_prefetch=4, grid=(nt,), in_specs=in_specs, out_specs=out_specs,
        scratch_shapes=[pltpu.VMEM((2, CBUF_ROWS, D_MODEL), BF16), pltpu.VMEM((D_MODEL, D_MODEL), BF16),
                        pltpu.VMEM((PLE_DIM, D_MODEL), BF16), pltpu.SemaphoreType.DMA((2,))])
    return pl.pallas_call(
        functools.partial(_combine_kernel, final=final, npt=npt, n_state=len(st_in)),
        grid_spec=grid_spec, out_shape=out_shape,
        compiler_params=pltpu.CompilerParams(dimension_semantics=("arbitrary",), vmem_limit_bytes=VMEM_LIMIT),
        name="moe_combine_final" if final else "moe_combine",
    )(*plan[:4], hp, hs, pp, ps, pos, y, P["g_ple"], P["w_ple_gate"], P["w_ple_proj"], g_final, *st_in)


def _ffn(hp, hs, pp, ps, l, P, g_final, *, states=None):
    n = hp.shape[0] + hs.shape[0]
    nt = n // TOK_TILE
    bound = TOP_K * n + nt * N_EXPERTS * (CHUNK - 1) + N_EXPERTS * (EXP_TILE - 1)
    n_sorted = -(-bound // EXP_TILE) * EXP_TILE
    pos, cnt = _route_call(hp, hs, l, P)
    plan = _chunk_plan(cnt[:, 0, :N_EXPERTS].astype(jnp.int32), n_sorted // EXP_TILE)
    xs = _dispatch_call(plan, hp, hs, pos, l, P, n_sorted)
    y = _expert_call(plan, xs, l, P["wg"], P["wu"], P["wd"])
    return _combine_call(plan, hp, hs, pp, ps, pos, y, l, P, g_final, states=states)


def _router_weights(w_grp_router, b_grp_router, w_exp_router, b_exp_router):
    depth = w_grp_router.shape[0]
    n_pad = LANES - N_EXPERTS - N_GROUPS
    w_er = jnp.transpose(w_exp_router, (0, 2, 1, 3)).reshape(depth, D_MODEL, N_EXPERTS)
    w_rt = jnp.concatenate([w_er, w_grp_router, jnp.zeros((depth, D_MODEL, n_pad), F32)], axis=2)
    b_rt = jnp.concatenate([b_exp_router.reshape(depth, N_EXPERTS), b_grp_router, jnp.zeros((depth, n_pad), F32)], axis=1)
    w_hi = w_rt.astype(BF16)
    w_lo = (w_rt - w_hi.astype(F32)).astype(BF16)
    return jnp.concatenate([w_hi, w_lo], axis=2), b_rt.reshape(depth, 1, LANES)


def kernel(x_prompt, x_sample, state_gla, state_conv, p_prompt, p_sample, g_mix, w_in, w_forget_up, b_forget,
           g_gla_out, w_conv, b_conv, g_conv_ln, b_conv_ln, w_out, g_ffn, w_grp_router, b_grp_router,
           w_exp_router, b_exp_router, w_exp_gate, w_exp_up, w_exp_down, g_ple, w_ple_gate, w_ple_proj, g_final):
    depth = w_in.shape[0]
    nbp, seq_p, _ = x_prompt.shape
    nbs, seq_s, _ = x_sample.shape
    n_p = nbp * seq_p
    n_s = nbs * seq_s

    def rows(v):
        return v.reshape(depth, 1, -1)

    P = {
        "g_mix": rows(g_mix), "w_in_t": jnp.swapaxes(w_in, 1, 2),
        "w_forget_up": w_forget_up, "b_forget": rows(b_forget), "g_gla_out": rows(g_gla_out),
        "w_conv": w_conv, "b_conv": rows(b_conv), "g_conv_ln": rows(g_conv_ln), "b_conv_ln": rows(b_conv_ln),
        "w_out": w_out, "g_ffn": rows(g_ffn), "g_ple": rows(g_ple),
        "w_ple_gate": w_ple_gate, "w_ple_proj": w_ple_proj,
        "wg": w_exp_gate.reshape(depth * N_EXPERTS, D_MODEL, EXPERT_FF),
        "wu": w_exp_up.reshape(depth * N_EXPERTS, D_MODEL, EXPERT_FF),
        "wd": w_exp_down.reshape(depth * N_EXPERTS, EXPERT_FF, D_MODEL),
    }
    P["w_rt"], P["b_rt"] = _router_weights(w_grp_router, b_grp_router, w_exp_router, b_exp_router)
    g_fin = g_final.reshape(1, -1)
    xp = x_prompt.reshape(n_p, D_MODEL)
    xs = x_sample.reshape(n_s, D_MODEL)
    pp = p_prompt.reshape(depth, n_p, PLE_DIM)
    ps = p_sample.reshape(depth, n_s, PLE_DIM)
    s_in = state_gla.reshape(depth, nbs, QK_COLS, GLA_DV)
    c_in_t = jnp.swapaxes(state_conv, 1, 2)

    h = None
    sg_p, sg_s, sc_p, sc_s = [], [], [], []
    for l in range(depth):
        src_p, src_s, off_s = (xp, xs, 0) if l == 0 else (h, h, n_p)
        hp, sgp, scp = _guarded(functools.partial(_mixer_prompt, src_p, 0, nbp, seq_p, l, P))
        hs, sgs, scs = _guarded(functools.partial(_mixer_sample, src_s, off_s, nbs, seq_s, l, s_in, c_in_t, P))
        sg_p.append(sgp)
        sg_s.append(sgs)
        sc_p.append(scp)
        sc_s.append(scs)
        h = _ffn(hp, hs, pp, ps, l, P, g_fin, states=((sg_p, 0), (sg_s, 0), (sc_s, 1)) if l == depth - 1 else None)

    y_prompt, y_sample, sg_p_all, sg_s_all, sc_s_all = h
    return (y_prompt.reshape(nbp, seq_p, D_MODEL), y_sample.reshape(nbs, seq_s, D_MODEL),
            sg_p_all.reshape(depth, nbp, GLA_HEADS, GLA_DK, GLA_DV), sg_s_all.reshape(depth, nbs, GLA_HEADS, GLA_DK, GLA_DV),
            jnp.stack(sc_p), jnp.swapaxes(sc_s_all, 1, 2))
```

```python
import functools

import jax
import jax.numpy as jnp
from jax import lax
from jax.experimental import pallas as pl
from jax.experimental.pallas import tpu as pltpu

D_MODEL = 1024
GLA_HEADS = 4
GLA_DK = 64
GLA_DV = 128
QK_COLS = GLA_HEADS * GLA_DK
V_COLS = GLA_HEADS * GLA_DV
CONV_WIDTH = 512
CONV_K = 31
GLA_LOWRANK = 16
GLA_TAU = 16.0
GLA_CHUNK = 64
PLE_DIM = 256
N_GROUPS = 4
EXPERTS_PER_GROUP = 8
N_EXPERTS = N_GROUPS * EXPERTS_PER_GROUP
EXPERT_FF = 256
TOP_K = 2
EPS = 1e-6
N_MAIN = 2 * QK_COLS + 2 * V_COLS

LANES = 128
SUBLANES = 8
CONV_PAD = 32
CONV_OFF = CONV_PAD - (CONV_K - 1)
VMEM_LIMIT = 56 * 1024 * 1024
TOK_TILE = 512
CHUNK = 16
EXP_TILE = 512
CBUF_ROWS = -(-(TOP_K * TOK_TILE + N_EXPERTS * (CHUNK - 1)) // LANES) * LANES
XS_COLS = D_MODEL + LANES
ROUTE_ROWS = 48
DECAY_SAFE = 80.0

F32 = jnp.float32
BF16 = jnp.bfloat16
HI = lax.Precision.HIGHEST


def _sigmoid(x):
    return 1.0 / (1.0 + jnp.exp(-x))


def _silu(x):
    return x * _sigmoid(x)


def _log_sigmoid(x):
    return jnp.minimum(x, 0.0) - jnp.log(1.0 + jnp.exp(-jnp.abs(x)))


def _rmsnorm(x, g):
    return x * lax.rsqrt(jnp.mean(x * x, axis=-1, keepdims=True) + EPS) * g


def _dot(a, b):
    return jnp.dot(a.astype(BF16), b.astype(BF16), preferred_element_type=F32)


def _dot_t(a, b):
    return lax.dot_general(a.astype(BF16), b.astype(BF16), (((1,), (1,)), ((), ())),
                           preferred_element_type=F32)


def _dot_hi(a, b):
    return jnp.dot(a, b, preferred_element_type=F32, precision=HI)


def _iota_f32(shape, dim):
    return lax.broadcasted_iota(jnp.int32, shape, dim).astype(F32)


def _const_spec(shape):
    nd = len(shape)
    return pl.BlockSpec(shape, lambda *_: (0,) * nd)


def _layer_spec(arr, l):
    nd = arr.ndim - 1
    return pl.BlockSpec((None,) + arr.shape[1:], lambda *_: (l,) + (0,) * nd, pipeline_mode=pl.Buffered(1))


def _cast_mixer_weights(w_int_ref, w_fu_ref, w_out_ref, wmain_s, wlr_s, wcv_s, wfu_s, wout_s):
    blk = 4 * LANES
    for r in range(0, N_MAIN, blk):
        wmain_s[:, r:r + blk] = w_int_ref[r:r + blk, :].T.astype(BF16)
    lane = lax.broadcasted_iota(jnp.int32, (D_MODEL, LANES), 1)
    wlr_s[...] = jnp.where(lane < GLA_LOWRANK, w_int_ref[N_MAIN:N_MAIN + LANES, :].T, 0.0).astype(BF16)
    cv0 = N_MAIN + GLA_LOWRANK
    for r in range(0, 2 * CONV_WIDTH, blk):
        wcv_s[:, r:r + blk] = w_int_ref[cv0 + r:cv0 + r + blk, :].T.astype(BF16)
    wfu_s[...] = jnp.zeros_like(wfu_s)
    wfu_s[0:GLA_LOWRANK, :] = w_fu_ref[...].astype(BF16)
    wout_s[...] = w_out_ref[...].astype(BF16)


def _project(x, g_mix, wmain_s, wlr_s, wfu_s, b_f, wcv_s):
    xn = _rmsnorm(x, g_mix).astype(BF16)
    cv = jnp.dot(xn, wcv_s[...], preferred_element_type=F32)
    u = cv[:, :CONV_WIDTH] * _sigmoid(cv[:, CONV_WIDTH:])
    lr = jnp.dot(xn, wlr_s[...], preferred_element_type=F32)
    zf = _dot(lr, wfu_s[...]) + b_f
    la = _log_sigmoid(zf) * (1.0 / GLA_TAU)
    zqk = jnp.dot(xn, wmain_s[:, :2 * QK_COLS], preferred_element_type=F32)
    q = zqk[:, :QK_COLS] * (GLA_DK ** -0.5)
    k = zqk[:, QK_COLS:]
    zvo = jnp.dot(xn, wmain_s[:, 2 * QK_COLS:], preferred_element_type=F32)
    v = zvo[:, :V_COLS]
    og = zvo[:, V_COLS:]
    return q, k, v, og, la, u


def _stack_heads(qd):
    lane = lax.broadcasted_iota(jnp.int32, qd.shape, 1)
    return jnp.concatenate(
        [jnp.where((lane >= h * GLA_DK) & (lane < (h + 1) * GLA_DK), qd, 0.0) for h in range(GLA_HEADS)],
        axis=0)


def _gated_head_norm(o, og, g_gla):
    outs = []
    for h in range(GLA_HEADS):
        sl = slice(h * GLA_DV, (h + 1) * GLA_DV)
        outs.append(_rmsnorm(o[:, sl], g_gla) * _silu(og[:, sl]))
    return jnp.concatenate(outs, axis=1)


def _causal_conv(win, w_conv, b_conv, n):
    acc = jnp.broadcast_to(b_conv, (n, CONV_WIDTH))
    for s in range(SUBLANES):
        taps = [j for j in range(CONV_K) if (CONV_OFF + j) % SUBLANES == s]
        if not taps:
            continue
        rows = n if s == 0 else n + SUBLANES
        part = None
        for j in taps:
            a = (CONV_OFF + j) - s
            term = w_conv[j:j + 1, :] * win[a:a + rows, :]
            part = term if part is None else part + term
        acc = acc + part[s:s + n, :]
    return acc


def _conv_ln_act(acc, g_ln, b_ln):
    mu = jnp.mean(acc, axis=-1, keepdims=True)
    xc = acc - mu
    y = xc * lax.rsqrt(jnp.mean(xc * xc, axis=-1, keepdims=True) + EPS) * g_ln + b_ln
    return _silu(y)


def _head_diag(upd):
    return jnp.concatenate([upd[h * GLA_DK:(h + 1) * GLA_DK, h * GLA_DV:(h + 1) * GLA_DV]
                            for h in range(GLA_HEADS)], axis=0)


def _mixer_weight_args(l, P):
    names = ("g_mix", "w_in_t", "w_forget_up", "b_forget", "g_gla_out", "w_conv", "b_conv", "g_conv_ln",
             "b_conv_ln", "w_out")
    arrs = [P[n] for n in names]
    return arrs, [_layer_spec(a, l) for a in arrs]


_MIXER_WEIGHT_SCRATCH = [pltpu.VMEM((D_MODEL, N_MAIN), BF16), pltpu.VMEM((D_MODEL, LANES), BF16),
                         pltpu.VMEM((D_MODEL, 2 * CONV_WIDTH), BF16), pltpu.VMEM((LANES, QK_COLS), BF16),
                         pltpu.VMEM((D_MODEL, D_MODEL), BF16)]


def _mixer_prompt_kernel(x_ref, g_mix_ref, w_int_ref, w_fu_ref, b_f_ref, g_gla_ref, w_conv_ref,
                         b_conv_ref, g_ln_ref, b_ln_ref, w_out_ref,
                         h_ref, sg_ref, sc_ref, risk_ref,
                         wmain_s, wlr_s, wcv_s, wfu_s, wout_s,
                         s_ref, ubuf_ref, qs_ref, kd_ref, klt_ref, dec_ref, mid_ref, v_ref, og_ref, mix_ref,
                         *stable_refs, tt, stable):
    t = pl.program_id(1)
    nt = pl.num_programs(1)
    C = GLA_CHUNK
    n_chunks = tt // C

    @pl.when((pl.program_id(0) == 0) & (t == 0))
    def _():
        _cast_mixer_weights(w_int_ref, w_fu_ref, w_out_ref, wmain_s, wlr_s, wcv_s, wfu_s, wout_s)

    @pl.when(t == 0)
    def _():
        s_ref[...] = jnp.zeros_like(s_ref)
        ubuf_ref[0:CONV_PAD, :] = jnp.zeros((CONV_PAD, CONV_WIDTH), F32)

    x = x_ref[...]
    q, k, v, og, la, u = _project(x, g_mix_ref[...], wmain_s, wlr_s, wfu_s, b_f_ref[...], wcv_s)
    ubuf_ref[CONV_PAD:CONV_PAD + tt, :] = u
    v_ref[...] = v.astype(BF16)
    og_ref[...] = _silu(og)

    w_conv = w_conv_ref[...]
    b_conv = b_conv_ref[...]
    g_ln = g_ln_ref[...]
    b_ln = b_ln_ref[...]
    row = lax.broadcasted_iota(jnp.int32, (C, C), 0)
    col = lax.broadcasted_iota(jnp.int32, (C, C), 1)
    tri = (col <= row).astype(F32)
    risk = jnp.zeros((1, QK_COLS), F32)
    for c in range(n_chunks):
        rows = slice(c * C, (c + 1) * C)
        win = ubuf_ref[c * C:c * C + C + CONV_PAD, :]
        cact = _conv_ln_act(_causal_conv(win, w_conv, b_conv, C), g_ln, b_ln)
        mix_ref[rows, V_COLS:] = cact.astype(BF16)
        bits = pltpu.bitcast(cact[C - SUBLANES:C, 0:QK_COLS], jnp.uint32)
        zero = pltpu.bitcast(lax.shift_right_logical(lax.shift_right_logical(bits, jnp.uint32(16)), jnp.uint32(16)),
                             F32)[0:1, :]
        qc = q[rows, :] + zero
        b = _dot_hi(tri, la[rows, :])
        b_last = b[C - 1:C, :]
        if stable:
            q_st, k_st, b_st = stable_refs
            q_st[rows, :] = qc
            k_st[rows, :] = k[rows, :]
            b_st[rows, :] = b
            b_mid = jnp.zeros_like(b_last)
        else:
            b_mid = b[C // 2 - 1:C // 2, :]
            risk = jnp.maximum(risk, jnp.maximum(-b_mid, b_mid - b_last))
        qs_ref[c] = _stack_heads(qc * jnp.exp(b - b_mid)).astype(BF16)
        kd_ref[rows, :] = (k[rows, :] * jnp.exp(jnp.minimum(b_mid - b, DECAY_SAFE))).astype(BF16)
        kl = k[rows, :] * jnp.exp(b_last - b)
        klt = jnp.concatenate([kl, jnp.broadcast_to(jnp.exp(b_last), (C // 2, QK_COLS)),
                               jnp.broadcast_to(jnp.exp(b_mid), (C // 2, QK_COLS))], axis=0).T
        klt_ref[c] = klt.astype(BF16)
        dec_ref[c] = jnp.broadcast_to(klt[:, C:C + 1], (QK_COLS, GLA_DV))
        mid_ref[c] = jnp.broadcast_to(klt[:, 3 * C // 2:3 * C // 2 + 1], (QK_COLS, GLA_DV))

    r4 = lax.broadcasted_iota(jnp.int32, (GLA_HEADS * C, C), 0)
    c4 = lax.broadcasted_iota(jnp.int32, (GLA_HEADS * C, C), 1)
    causal4 = c4 <= (r4 % C)
    g_gla = g_gla_ref[...]
    s = s_ref[...]
    for c in range(n_chunks):
        rows = slice(c * C, (c + 1) * C)
        qs = qs_ref[c]
        vc = v_ref[rows, :]
        if stable:
            scores = _direct_scores(*stable_refs, c * C, C).astype(BF16)
        else:
            scores = jnp.where(causal4, _dot_t(qs, kd_ref[rows, :]), 0.0).astype(BF16)
        o_inter = jnp.dot(qs, (mid_ref[c] * s).astype(BF16), preferred_element_type=F32)
        upd = jnp.dot(klt_ref[c][:, :C], vc, preferred_element_type=F32)
        s = dec_ref[c] * s + _head_diag(upd)
        o_parts = []
        for h in range(GLA_HEADS):
            vh = vc[:, h * GLA_DV:(h + 1) * GLA_DV]
            o_parts.append(jnp.dot(scores[h * C:(h + 1) * C, :], vh, preferred_element_type=F32)
                           + o_inter[h * C:(h + 1) * C, :])
        o = jnp.concatenate(o_parts, axis=1)
        gated = []
        for h in range(GLA_HEADS):
            sl = slice(h * GLA_DV, (h + 1) * GLA_DV)
            gated.append(_rmsnorm(o[:, sl], g_gla) * og_ref[rows, sl])
        mix_ref[rows, 0:V_COLS] = jnp.concatenate(gated, axis=1).astype(BF16)
    s_ref[...] = s

    h_ref[...] = x + jnp.dot(mix_ref[...], wout_s[...], preferred_element_type=F32)
    tail = ubuf_ref[tt:tt + CONV_PAD, :]
    ubuf_ref[0:CONV_PAD, :] = tail

    risk_ref[0] = jnp.broadcast_to(jnp.max(risk, axis=-1, keepdims=True), (SUBLANES, LANES))

    @pl.when(t == nt - 1)
    def _():
        sg_ref[0] = s
        sc_ref[0] = tail[CONV_OFF:, :]


def _direct_scores(q_st, k_st, b_st, r0, C):
    kc = k_st[r0:r0 + C, :]
    bc = b_st[r0:r0 + C, :]
    srow = lax.broadcasted_iota(jnp.int32, (C, QK_COLS), 0)
    lane = lax.broadcasted_iota(jnp.int32, (C, LANES), 1)
    head_of = lax.broadcasted_iota(jnp.int32, (QK_COLS, LANES), 0) // GLA_DK
    head_sum = (head_of == lax.broadcasted_iota(jnp.int32, (QK_COLS, LANES), 1)).astype(F32)

    def one_query(t, acc):
        d = b_st[pl.ds(r0 + t, 1), :] - bc
        w = jnp.where(srow <= t, jnp.exp(jnp.minimum(d, 0.0)), 0.0) * kc * q_st[pl.ds(r0 + t, 1), :]
        per_head = _dot_hi(w, head_sum)
        return tuple(jnp.where(lane == t, per_head[:, h:h + 1], acc[h]) for h in range(GLA_HEADS))

    acc = lax.fori_loop(0, C, one_query, tuple(jnp.zeros((C, LANES), F32) for _ in range(GLA_HEADS)))
    return jnp.concatenate([a.T[0:C, :] for a in acc], axis=0)


def _mixer_prompt(x, row_off, nb, seq, l, P, *, tt=512, stable=False):
    nt = seq // tt
    n_chunks = tt // GLA_CHUNK
    blk_off = row_off // tt
    weights, w_specs = _mixer_weight_args(l, P)
    in_specs = [pl.BlockSpec((tt, D_MODEL), lambda b, t: (blk_off + b * nt + t, 0))] + w_specs
    out_shape = (jax.ShapeDtypeStruct((nb * seq, D_MODEL), F32),
                 jax.ShapeDtypeStruct((nb, QK_COLS, GLA_DV), F32),
                 jax.ShapeDtypeStruct((nb, CONV_K - 1, CONV_WIDTH), F32),
                 jax.ShapeDtypeStruct((nb * nt, SUBLANES, LANES), F32))
    out_specs = (pl.BlockSpec((tt, D_MODEL), lambda b, t: (b * nt + t, 0)),
                 pl.BlockSpec((1, QK_COLS, GLA_DV), lambda b, t: (b, 0, 0)),
                 pl.BlockSpec((1, CONV_K - 1, CONV_WIDTH), lambda b, t: (b, 0, 0)),
                 pl.BlockSpec((1, SUBLANES, LANES), lambda b, t: (b * nt + t, 0, 0)))
    scratch = _MIXER_WEIGHT_SCRATCH + [
        pltpu.VMEM((QK_COLS, GLA_DV), F32),
        pltpu.VMEM((CONV_PAD + tt, CONV_WIDTH), F32),
        pltpu.VMEM((n_chunks, GLA_HEADS * GLA_CHUNK, QK_COLS), BF16), pltpu.VMEM((tt, QK_COLS), BF16),
        pltpu.VMEM((n_chunks, QK_COLS, 2 * GLA_CHUNK), BF16), pltpu.VMEM((n_chunks, QK_COLS, GLA_DV), F32),
        pltpu.VMEM((n_chunks, QK_COLS, GLA_DV), F32),
        pltpu.VMEM((tt, V_COLS), BF16), pltpu.VMEM((tt, V_COLS), F32),
        pltpu.VMEM((tt, D_MODEL), BF16)]
    if stable:
        scratch += [pltpu.VMEM((tt, QK_COLS), F32)] * 3
    return pl.pallas_call(
        functools.partial(_mixer_prompt_kernel, tt=tt, stable=stable),
        grid=(nb, nt), in_specs=in_specs, out_specs=out_specs, out_shape=out_shape,
        scratch_shapes=scratch,
        compiler_params=pltpu.CompilerParams(dimension_semantics=("arbitrary", "arbitrary"),
                                             vmem_limit_bytes=VMEM_LIMIT),
        name="mixer_prompt_stable" if stable else "mixer_prompt",
    )(x, *weights)


def _guarded(mixer):
    *outs, risk = mixer(stable=False)
    return lax.cond(jnp.max(risk) > DECAY_SAFE, lambda: tuple(mixer(stable=True)[:-1]), lambda: tuple(outs))


def _mixer_sample_kernel(x_ref, s_in_ref, c_in_ref, g_mix_ref, w_int_ref, w_fu_ref, b_f_ref,
                         g_gla_ref, w_conv_ref, b_conv_ref, g_ln_ref, b_ln_ref, w_out_ref,
                         h_ref, sg_ref, sc_ref, risk_ref,
                         wmain_s, wlr_s, wcv_s, wfu_s, wout_s,
                         u4_ref, oi_ref, cacc4_ref, *, sb, seq, stable):
    R = sb * seq
    n_slabs = CONV_WIDTH // LANES

    @pl.when(pl.program_id(0) == 0)
    def _():
        _cast_mixer_weights(w_int_ref, w_fu_ref, w_out_ref, wmain_s, wlr_s, wcv_s, wfu_s, wout_s)

    x = x_ref[...]
    q, k, v, og, la, u = _project(x, g_mix_ref[...], wmain_s, wlr_s, wfu_s, b_f_ref[...], wcv_s)

    for kk in range(n_slabs):
        u4_ref[kk] = u[:, kk * LANES:(kk + 1) * LANES]
    full = [c_in_ref[j] for j in range(CONV_K - 1)]
    for t in range(seq):
        full.append(jnp.concatenate([u4_ref.at[kk][pl.ds(t, sb, stride=seq), :] for kk in range(n_slabs)], axis=1))
    w_conv = w_conv_ref[...]
    for t in range(seq):
        acc = jnp.broadcast_to(b_conv_ref[...], (sb, CONV_WIDTH))
        for j in range(CONV_K):
            acc = acc + w_conv[j:j + 1, :] * full[t + j]
        for kk in range(n_slabs):
            cacc4_ref.at[kk][pl.ds(t, sb, stride=seq), :] = acc[:, kk * LANES:(kk + 1) * LANES]
    for j in range(CONV_K - 1):
        sc_ref[j] = full[seq + j]

    row = lax.broadcasted_iota(jnp.int32, (R, R), 0)
    col = lax.broadcasted_iota(jnp.int32, (R, R), 1)
    same = (row // seq) == (col // seq)
    b = _dot_hi((same & (col <= row)).astype(F32), la)
    b_tot = _dot_hi(same.astype(F32), la)
    qd = q * jnp.exp(b)
    kl = k * jnp.exp(b_tot - b)
    qs = _stack_heads(qd)
    risk_ref[0] = jnp.broadcast_to(jnp.max(jnp.max(-b_tot, axis=-1, keepdims=True), axis=0, keepdims=True),
                                   (SUBLANES, LANES))
    if stable:
        head_of = lax.broadcasted_iota(jnp.int32, (QK_COLS, LANES), 0) // GLA_DK
        head_sum = (head_of == lax.broadcasted_iota(jnp.int32, (QK_COLS, LANES), 1)).astype(F32)
        t_in_seq = lax.broadcasted_iota(jnp.int32, (R, QK_COLS), 0) % seq
        parts = [jnp.zeros((R, R), F32) for _ in range(GLA_HEADS)]
        for d in range(seq):
            k_d = jnp.concatenate([jnp.zeros((d, QK_COLS), F32), k[:R - d, :]], axis=0) if d else k
            b_d = jnp.concatenate([jnp.zeros((d, QK_COLS), F32), b[:R - d, :]], axis=0) if d else b
            w = jnp.where(t_in_seq >= d, jnp.exp(jnp.minimum(b - b_d, 0.0)), 0.0) * k_d * q
            per_head = _dot_hi(w, head_sum)
            for h in range(GLA_HEADS):
                parts[h] = jnp.where(col == row - d, per_head[:, h:h + 1], parts[h])
        scores = jnp.concatenate(parts, axis=0)
    else:
        kd = k * jnp.exp(jnp.minimum(-b, DECAY_SAFE))
        r4 = lax.broadcasted_iota(jnp.int32, (GLA_HEADS * R, R), 0) % R
        c4 = lax.broadcasted_iota(jnp.int32, (GLA_HEADS * R, R), 1)
        mask4 = ((r4 // seq) == (c4 // seq)) & (c4 <= r4)
        scores = jnp.where(mask4, _dot_t(qs, kd), 0.0)

    klt = kl.T
    dect = jnp.exp(b_tot).T
    lane_h = lax.broadcasted_iota(jnp.int32, (GLA_DK, R), 1)
    lane_r = lax.broadcasted_iota(jnp.int32, (QK_COLS, R), 1)
    upd = []
    for h in range(GLA_HEADS):
        klt_h = klt[h * GLA_DK:(h + 1) * GLA_DK, :]
        lhs = jnp.concatenate([jnp.where((lane_h >= i * seq) & (lane_h < (i + 1) * seq), klt_h, 0.0)
                               for i in range(sb)], axis=0)
        upd.append(_dot(lhs, v[:, h * GLA_DV:(h + 1) * GLA_DV]))
    for i in range(sb):
        s_old = s_in_ref[i]
        qsel = jnp.concatenate([qs[h * R + i * seq:h * R + (i + 1) * seq, :] for h in range(GLA_HEADS)], axis=0)
        oi = _dot(qsel, s_old)
        for h in range(GLA_HEADS):
            oi_ref[h, i * seq:(i + 1) * seq, :] = oi[h * seq:(h + 1) * seq, :]
        smask = (lane_r >= i * seq) & (lane_r < (i + 1) * seq)
        dec = jnp.sum(jnp.where(smask, dect, 0.0), axis=1, keepdims=True) * (1.0 / seq)
        u_new = jnp.concatenate([upd[h][i * GLA_DK:(i + 1) * GLA_DK, :] for h in range(GLA_HEADS)], axis=0)
        sg_ref[i] = dec * s_old + u_new

    o_parts = []
    for h in range(GLA_HEADS):
        vh = v[:, h * GLA_DV:(h + 1) * GLA_DV]
        o_parts.append(_dot(scores[h * R:(h + 1) * R, :], vh) + oi_ref[h])
    o = jnp.concatenate(o_parts, axis=1)
    cacc = jnp.concatenate([cacc4_ref[kk] for kk in range(n_slabs)], axis=1)
    mix = jnp.concatenate([_gated_head_norm(o, og, g_gla_ref[...]),
                           _conv_ln_act(cacc, g_ln_ref[...], b_ln_ref[...])], axis=1)
    h_ref[...] = x + jnp.dot(mix.astype(BF16), wout_s[...], preferred_element_type=F32)


def _mixer_sample(x, row_off, nb, seq, l, s_in, c_in_t, P, *, sb=16, stable=False):
    R = sb * seq
    blk_off = row_off // R
    n_slabs = CONV_WIDTH // LANES
    weights, w_specs = _mixer_weight_args(l, P)
    in_specs = [pl.BlockSpec((R, D_MODEL), lambda i: (blk_off + i, 0)),
                pl.BlockSpec((None, sb, QK_COLS, GLA_DV), lambda i: (l, i, 0, 0)),
                pl.BlockSpec((None, CONV_K - 1, sb, CONV_WIDTH), lambda i: (l, 0, i, 0))] + w_specs
    out_shape = (jax.ShapeDtypeStruct((nb * seq, D_MODEL), F32),
                 jax.ShapeDtypeStruct((nb, QK_COLS, GLA_DV), F32),
                 jax.ShapeDtypeStruct((CONV_K - 1, nb, CONV_WIDTH), F32),
                 jax.ShapeDtypeStruct((nb // sb, SUBLANES, LANES), F32))
    out_specs = (pl.BlockSpec((R, D_MODEL), lambda i: (i, 0)),
                 pl.BlockSpec((sb, QK_COLS, GLA_DV), lambda i: (i, 0, 0)),
                 pl.BlockSpec((CONV_K - 1, sb, CONV_WIDTH), lambda i: (0, i, 0)),
                 pl.BlockSpec((1, SUBLANES, LANES), lambda i: (i, 0, 0)))
    scratch = _MIXER_WEIGHT_SCRATCH + [
        pltpu.VMEM((n_slabs, R, LANES), F32),
        pltpu.VMEM((GLA_HEADS, R, GLA_DV), F32),
        pltpu.VMEM((n_slabs, R, LANES), F32)]
    return pl.pallas_call(
        functools.partial(_mixer_sample_kernel, sb=sb, seq=seq, stable=stable),
        grid=(nb // sb,), in_specs=in_specs, out_specs=out_specs, out_shape=out_shape,
        scratch_shapes=scratch,
        compiler_params=pltpu.CompilerParams(dimension_semantics=("arbitrary",),
                                             vmem_limit_bytes=VMEM_LIMIT),
        name="mixer_sample_stable" if stable else "mixer_sample",
    )(x, s_in, c_in_t, *weights)


def _pair_specs(n_first_tiles, width):
    return [pl.BlockSpec((TOK_TILE, width), lambda t, *_: (jnp.minimum(t, n_first_tiles - 1), 0)),
            pl.BlockSpec((TOK_TILE, width), lambda t, *_: (jnp.maximum(t - n_first_tiles, 0), 0))]


def _pick(t, n_first_tiles, a_ref, b_ref):
    return jnp.where(t < n_first_tiles, a_ref[...], b_ref[...])


def _route(logits):
    row = lax.broadcasted_iota(jnp.int32, logits.shape, 0)
    row_f = row.astype(F32)
    neg = jnp.float32(-jnp.inf)
    big = jnp.float32(1e9)
    is_grp = (row >= N_EXPERTS) & (row < N_EXPERTS + N_GROUPS)
    gl = jnp.where(is_grp, logits, neg)
    gmax = jnp.max(gl, axis=0, keepdims=True)
    gidx = jnp.min(jnp.where(is_grp & (gl == gmax), row_f - N_EXPERTS, big), axis=0, keepdims=True)
    gsum = jnp.sum(jnp.where(is_grp, jnp.exp(gl - gmax), 0.0), axis=0, keepdims=True)
    g_w = 1.0 / gsum
    grp_of_row = jnp.floor(row_f * (1.0 / EXPERTS_PER_GROUP))
    in_grp = (row < N_EXPERTS) & (grp_of_row == gidx)
    ml = jnp.where(in_grp, logits, neg)
    v1 = jnp.max(ml, axis=0, keepdims=True)
    i1 = jnp.min(jnp.where(in_grp & (ml == v1), row_f, big), axis=0, keepdims=True)
    ml2 = jnp.where(row_f == i1, neg, ml)
    v2 = jnp.max(ml2, axis=0, keepdims=True)
    i2 = jnp.min(jnp.where(in_grp & (ml2 == v2), row_f, big), axis=0, keepdims=True)
    e2 = jnp.exp(v2 - v1)
    w1 = g_w / (1.0 + e2)
    w2 = g_w * e2 / (1.0 + e2)
    return i1, i2, w1, w2


def _route_kernel(hp_ref, hs_ref, w_rt_ref, b_rt_ref, pos_ref, cnt_ref, earlier_s, *, npt):
    T = TOK_TILE
    R = ROUTE_ROWS

    @pl.when(pl.program_id(0) == 0)
    def _():
        earlier_s[...] = (_iota_f32((T, T), 0) < _iota_f32((T, T), 1)).astype(BF16)

    x = _pick(pl.program_id(0), npt, hp_ref, hs_ref)
    scale = lax.rsqrt(jnp.mean(x * x, axis=-1, keepdims=True) + EPS)
    x_hi = x.astype(BF16)
    x_lo = (x - x_hi.astype(F32)).astype(BF16)
    w_split = w_rt_ref[...]
    both = jnp.dot(x_hi, w_split, preferred_element_type=F32)
    logits = ((both[:, :LANES] + both[:, LANES:] + jnp.dot(x_lo, w_split[:, :LANES], preferred_element_type=F32))
              * scale + b_rt_ref[...])
    i1, i2, w1, w2 = _route(logits.T[:R])
    row = _iota_f32((R, T), 0)
    a0 = (row == i1).astype(F32)
    a1 = (row == i2).astype(F32)
    a = a0 + a1
    cnt = jnp.sum(a, axis=1, keepdims=True)
    rank = jnp.dot(a.astype(BF16), earlier_s[...], preferred_element_type=F32)
    cnt_pad = jnp.ceil(cnt * (1.0 / CHUNK)) * CHUNK
    no_rows = jnp.zeros((LANES - R, LANES), F32)
    before = (_iota_f32((LANES, LANES), 1) < _iota_f32((LANES, LANES), 0)).astype(F32)
    first = _dot_hi(before, jnp.concatenate([jnp.broadcast_to(cnt_pad, (R, LANES)), no_rows], axis=0))[:R, 0:1]
    base = first + rank
    pos0 = jnp.sum(a0 * base, axis=0, keepdims=True)
    pos1 = jnp.sum(a1 * base, axis=0, keepdims=True)
    r8 = _iota_f32((SUBLANES, T), 0)
    res = jnp.where(r8 == 0.0, pos0, jnp.where(r8 == 1.0, pos1, jnp.where(
        r8 == 2.0, w1, jnp.where(r8 == 3.0, w2, jnp.where(r8 == 4.0, i1, 0.0)))))
    pos_ref[...] = jnp.concatenate([res, jnp.zeros((LANES - SUBLANES, T), F32)], axis=0).T
    cnt_ref[0] = jnp.concatenate([jnp.broadcast_to(cnt, (R, LANES)), no_rows], axis=0).T[:SUBLANES]


def _route_call(hp, hs, l, P):
    npt = hp.shape[0] // TOK_TILE
    nt = npt + hs.shape[0] // TOK_TILE
    return pl.pallas_call(
        functools.partial(_route_kernel, npt=npt), grid=(nt,),
        in_specs=_pair_specs(npt, D_MODEL) + [_layer_spec(P[n], l) for n in ("w_rt", "b_rt")],
        scratch_shapes=[pltpu.VMEM((TOK_TILE, TOK_TILE), BF16)],
        out_specs=(pl.BlockSpec((TOK_TILE, LANES), lambda t: (t, 0)),
                   pl.BlockSpec((1, SUBLANES, LANES), lambda t: (t, 0, 0))),
        out_shape=(jax.ShapeDtypeStruct((nt * TOK_TILE, LANES), F32),
                   jax.ShapeDtypeStruct((nt, SUBLANES, LANES), F32)),
        compiler_params=pltpu.CompilerParams(dimension_semantics=("arbitrary",), vmem_limit_bytes=VMEM_LIMIT),
        name="moe_route",
    )(hp, hs, P["w_rt"], P["b_rt"])


def _chunk_plan(cnt, n_row_tiles):
    n16 = (cnt + (CHUNK - 1)) // CHUNK
    lofs16 = jnp.cumsum(n16, axis=1) - n16
    tile_pref16 = jnp.cumsum(n16, axis=0) - n16
    tot16 = jnp.sum(n16, axis=0)
    per_tile = EXP_TILE // CHUNK
    seg16 = ((tot16 + per_tile - 1) // per_tile) * per_tile
    seg_end16 = jnp.cumsum(seg16)
    dst16 = (seg_end16 - seg16)[None, :] + tile_pref16
    n_tot = jnp.sum(n16, axis=1)
    gap16 = seg_end16 - seg16 + tot16
    gapn16 = seg16 - tot16
    tile_start16 = jnp.arange(n_row_tiles, dtype=jnp.int32) * per_tile
    n_valid = seg_end16[-1] // per_tile
    misc = n_valid.reshape(1)
    exp_of_tile = jnp.minimum(jnp.sum(seg_end16[None, :] <= tile_start16[:, None], axis=1), N_EXPERTS - 1)
    i32 = lambda a: a.astype(jnp.int32).reshape(-1)
    return (i32(dst16), i32(n16), i32(lofs16), i32(n_tot), i32(gap16), i32(gapn16), i32(misc), i32(exp_of_tile))


def _chunk_copy(src, dst, src_chunk, dst_chunk, sem, n_chunks=1):
    rows = n_chunks * CHUNK
    return pltpu.make_async_copy(src.at[pl.ds(pl.multiple_of(src_chunk * CHUNK, CHUNK), rows), :],
                                 dst.at[pl.ds(pl.multiple_of(dst_chunk * CHUNK, CHUNK), rows), :], sem)


def _slab_copies(src, dst, sem, tile, src_ofs_ref, dst_ofs_ref, n16_ref):
    for e in range(N_EXPERTS):
        k = tile * N_EXPERTS + e
        n = n16_ref[k]

        @pl.when(n > 0)
        def _(k=k, n=n):
            _chunk_copy(src, dst, src_ofs_ref[k], dst_ofs_ref[k], sem, n).start()


def _wait_slabs(src, dst, sem, n_chunks):
    @pl.when(n_chunks > 0)
    def _():
        _chunk_copy(src, dst, 0, 0, sem, n_chunks).wait()


def _tile_copy(src, dst, dst_tile, sem):
    return pltpu.make_async_copy(src, dst.at[pl.ds(pl.multiple_of(dst_tile * EXP_TILE, EXP_TILE), EXP_TILE), :], sem)


def _dispatch_kernel(dst16_ref, n16_ref, lofs16_ref, ntot_ref, gap16_ref, gapn16_ref, misc_ref,
                     hp_ref, hs_ref, pos_ref, g_ffn_ref, xs_hbm, cbuf, zbuf, sem, *, n_row_tiles, npt):
    t = pl.program_id(0)
    nt = pl.num_programs(0)
    slot = t % 2
    T = TOK_TILE
    n_tail = n_row_tiles - misc_ref[0]

    @pl.when(t == 0)
    def _():
        zbuf[...] = jnp.zeros_like(zbuf)
        for e in range(N_EXPERTS):
            g = gapn16_ref[e]

            @pl.when(g > 0)
            def _(e=e, g=g):
                _chunk_copy(zbuf, xs_hbm, 0, gap16_ref[e], sem.at[2], g).start()

        def fill_tile(i, carry):
            _tile_copy(zbuf, xs_hbm, misc_ref[0] + i, sem.at[2]).start()
            return carry
        lax.fori_loop(0, n_tail, fill_tile, 0)

    xn = _rmsnorm(_pick(t, npt, hp_ref, hs_ref), g_ffn_ref[...]).astype(BF16)
    pos = pos_ref[...]
    pos_t = pos.T
    rows = _iota_f32((CBUF_ROWS, T), 0)
    onehot = jnp.where((rows == pos_t[0:1, :]) | (rows == pos_t[1:2, :]), 1.0, 0.0).astype(BF16)
    lane = lax.broadcasted_iota(jnp.int32, (T, LANES), 1)
    extra = jnp.zeros((T, LANES), F32)
    for s in range(TOP_K):
        c = pos[:, 2 + s:3 + s]
        hi = c.astype(BF16).astype(F32)
        mid = (c - hi).astype(BF16).astype(F32)
        lo = c - hi - mid
        for j, piece in enumerate((hi, mid, lo)):
            extra = jnp.where(lane == 3 * s + j, piece, extra)
    extra = jnp.where(lane == 3 * TOP_K, pos[:, 4:5], extra)
    cbuf[slot] = jnp.dot(onehot, jnp.concatenate([xn, extra.astype(BF16)], axis=1),
                         preferred_element_type=F32).astype(BF16)

    src = cbuf.at[slot]
    _slab_copies(src, xs_hbm, sem.at[slot], t, lofs16_ref, dst16_ref, n16_ref)

    @pl.when(t > 0)
    def _():
        _wait_slabs(cbuf.at[1 - slot], xs_hbm, sem.at[1 - slot], ntot_ref[t - 1])

    @pl.when(t == nt - 1)
    def _():
        _wait_slabs(src, xs_hbm, sem.at[slot], ntot_ref[t])
        for e in range(N_EXPERTS):
            _wait_slabs(zbuf, xs_hbm, sem.at[2], gapn16_ref[e])

        def wait_tile(_, carry):
            _tile_copy(zbuf, xs_hbm, 0, sem.at[2]).wait()
            return carry
        lax.fori_loop(0, n_tail, wait_tile, 0)


def _dispatch_call(plan, hp, hs, pos, l, P, n_sorted):
    npt = hp.shape[0] // TOK_TILE
    nt = npt + hs.shape[0] // TOK_TILE
    g_ffn = P["g_ffn"]
    grid_spec = pltpu.PrefetchScalarGridSpec(
        num_scalar_prefetch=7, grid=(nt,),
        in_specs=_pair_specs(npt, D_MODEL) + [
            pl.BlockSpec((TOK_TILE, LANES), lambda t, *_: (t, 0)),
            pl.BlockSpec((None,) + g_ffn.shape[1:], lambda t, *_: (l, 0, 0))],
        out_specs=pl.BlockSpec(memory_space=pl.ANY),
        scratch_shapes=[pltpu.VMEM((2, CBUF_ROWS, XS_COLS), BF16), pltpu.VMEM((EXP_TILE, XS_COLS), BF16),
                        pltpu.SemaphoreType.DMA((3,))])
    return pl.pallas_call(
        functools.partial(_dispatch_kernel, n_row_tiles=n_sorted // EXP_TILE, npt=npt), grid_spec=grid_spec,
        out_shape=jax.ShapeDtypeStruct((n_sorted, XS_COLS), BF16),
        compiler_params=pltpu.CompilerParams(dimension_semantics=("arbitrary",), vmem_limit_bytes=VMEM_LIMIT),
        name="moe_dispatch",
    )(*plan[:7], hp, hs, pos, g_ffn)


def _expert_kernel(eot_ref, misc_ref, xs_ref, wg_ref, wu_ref, wd_ref, y_ref, wg_s, wu_s, wd_s):
    i = pl.program_id(0)
    valid = i < misc_ref[0]

    @pl.when(jnp.logical_not(valid))
    def _():
        y_ref[...] = jnp.zeros_like(y_ref)

    @pl.when(valid & ((i == 0) | (eot_ref[i] != eot_ref[jnp.maximum(i - 1, 0)])))
    def _():
        wg_s[...] = wg_ref[...].astype(BF16)
        wu_s[...] = wu_ref[...].astype(BF16)
        wd_s[...] = wd_ref[...].astype(BF16)

    @pl.when(valid)
    def _():
        xs = xs_ref[...]
        x = xs[:, :D_MODEL]
        ex = xs[:, D_MODEL:].astype(F32)
        lane = lax.broadcasted_iota(jnp.int32, ex.shape, 1)
        id0 = jnp.sum(jnp.where(lane == 3 * TOP_K, ex, 0.0), axis=-1, keepdims=True)
        first = id0 == eot_ref[i].astype(F32)
        mine = (first & (lane < 3)) | (jnp.logical_not(first) & (lane >= 3) & (lane < 3 * TOP_K))
        c = jnp.sum(jnp.where(mine, ex, 0.0), axis=-1, keepdims=True)
        hg = _silu(jnp.dot(x, wg_s[...], preferred_element_type=F32)) * jnp.dot(x, wu_s[...],
                                                                               preferred_element_type=F32)
        y_ref[...] = jnp.dot((hg * c).astype(BF16), wd_s[...], preferred_element_type=F32).astype(BF16)


def _expert_call(plan, xs, l, wg, wu, wd):
    misc, exp_of_tile = plan[6:]
    n_row_tiles = xs.shape[0] // EXP_TILE

    def last_valid(i, nv):
        return jnp.maximum(jnp.minimum(i, nv[0] - 1), 0)

    def row_map(i, eot, nv):
        return (last_valid(i, nv), 0)

    def w_map(i, eot, nv):
        return (l * N_EXPERTS + eot[last_valid(i, nv)], 0, 0)

    grid_spec = pltpu.PrefetchScalarGridSpec(
        num_scalar_prefetch=2, grid=(n_row_tiles,),
        in_specs=[pl.BlockSpec((EXP_TILE, XS_COLS), row_map),
                  pl.BlockSpec((None, D_MODEL, EXPERT_FF), w_map),
                  pl.BlockSpec((None, D_MODEL, EXPERT_FF), w_map),
                  pl.BlockSpec((None, EXPERT_FF, D_MODEL), w_map)],
        out_specs=pl.BlockSpec((EXP_TILE, D_MODEL), lambda i, eot, nv: (i, 0)),
        scratch_shapes=[pltpu.VMEM((D_MODEL, EXPERT_FF), BF16), pltpu.VMEM((D_MODEL, EXPERT_FF), BF16),
                        pltpu.VMEM((EXPERT_FF, D_MODEL), BF16)])
    return pl.pallas_call(
        _expert_kernel, grid_spec=grid_spec,
        out_shape=jax.ShapeDtypeStruct((xs.shape[0], D_MODEL), BF16),
        compiler_params=pltpu.CompilerParams(dimension_semantics=("arbitrary",), vmem_limit_bytes=VMEM_LIMIT),
        name="moe_experts",
    )(exp_of_tile, misc, xs, wg, wu, wd)


def _combine_kernel(dst16_ref, n16_ref, lofs16_ref, ntot_ref, hp_ref, hs_ref, pp_ref, ps_ref, pos_ref, y_hbm,
                    g_ple_ref, w_pg_ref, w_pp_ref, g_fin_ref, *rest, final, npt, n_state):
    if final:
        st_in, (op_ref, os_ref, *st_out, ybuf, wpg_s, wpp_s, sem) = rest[:n_state], rest[n_state:]
        depth = n_state // len(st_out)
        for k, out in enumerate(st_out):
            for l in range(depth):
                out[l] = st_in[k * depth + l][...]
    else:
        o_ref, ybuf, wpg_s, wpp_s, sem = rest
    t = pl.program_id(0)
    nt = pl.num_programs(0)
    slot = t % 2

    def fetch(tile, sl):
        _slab_copies(y_hbm, ybuf.at[sl], sem.at[sl], tile, dst16_ref, lofs16_ref, n16_ref)

    @pl.when(t == 0)
    def _():
        ybuf[...] = jnp.zeros_like(ybuf)
        fetch(t, slot)
        wpg_s[...] = w_pg_ref[...].astype(BF16)
        wpp_s[...] = w_pp_ref[...].astype(BF16)

    @pl.when(t + 1 < nt)
    def _():
        fetch(t + 1, 1 - slot)

    _wait_slabs(y_hbm, ybuf.at[slot], sem.at[slot], ntot_ref[t])

    pos = pos_ref[...]
    cols = _iota_f32((TOK_TILE, CBUF_ROWS), 1)
    pick = jnp.where((cols == pos[:, 0:1]) | (cols == pos[:, 1:2]), 1.0, 0.0).astype(BF16)
    h2 = _pick(t, npt, hp_ref, hs_ref) + jnp.dot(pick, ybuf[slot], preferred_element_type=F32)
    xn2 = _rmsnorm(h2, g_ple_ref[...])
    gate = _sigmoid(_dot(xn2, wpg_s[...]))
    p = jnp.where(t < npt, pp_ref[...], ps_ref[...])
    h3 = h2 + gate * _dot(p, wpp_s[...])
    if final:
        h3 = _rmsnorm(h3, g_fin_ref[...])

        @pl.when(t < npt)
        def _():
            op_ref[...] = h3

        @pl.when(t >= npt)
        def _():
            os_ref[...] = h3
    else:
        o_ref[...] = h3


def _state_specs(arrs, axis, nt):
    shape = arrs[0].shape
    unit = SUBLANES if axis == len(shape) - 2 else 1
    per = -(-shape[axis] // (nt * unit)) * unit
    assert shape[axis] % per == 0, (shape, axis, nt)
    n_blocks = shape[axis] // per
    block = shape[:axis] + (per,) + shape[axis + 1:]

    def in_map(t, *_):
        return (0,) * axis + (jnp.minimum(t, n_blocks - 1),) + (0,) * (len(shape) - axis - 1)

    return [pl.BlockSpec(block, in_map)] * len(arrs), pl.BlockSpec((len(arrs),) + block, lambda t, *_: (0,) + in_map(t))


def _combine_call(plan, hp, hs, pp, ps, pos, y, l, P, g_final, *, states=None):
    final = states is not None
    st_in = [a for arrs, _ in states for a in arrs] if final else []
    npt = hp.shape[0] // TOK_TILE
    nst = hs.shape[0] // TOK_TILE
    nt = npt + nst

    def lmap(t, *_):
        return (l, 0, 0)

    in_specs = _pair_specs(npt, D_MODEL) + [
        pl.BlockSpec((None, TOK_TILE, PLE_DIM), lambda t, *_: (l, jnp.minimum(t, npt - 1), 0)),
        pl.BlockSpec((None, TOK_TILE, PLE_DIM), lambda t, *_: (l, jnp.maximum(t - npt, 0), 0)),
        pl.BlockSpec((TOK_TILE, LANES), lambda t, *_: (t, 0)),
        pl.BlockSpec(memory_space=pl.ANY),
        pl.BlockSpec((None,) + P["g_ple"].shape[1:], lmap),
        pl.BlockSpec((None,) + P["w_ple_gate"].shape[1:], lmap),
        pl.BlockSpec((None,) + P["w_ple_proj"].shape[1:], lmap),
        pl.BlockSpec(g_final.shape, lambda t, *_: (0, 0))]
    if final:
        st_specs = [_state_specs(arrs, axis, nt) for arrs, axis in states]
        in_specs += [s for ins, _ in st_specs for s in ins]
        out_specs = (pl.BlockSpec((TOK_TILE, D_MODEL), lambda t, *_: (jnp.minimum(t, npt - 1), 0)),
                     pl.BlockSpec((TOK_TILE, D_MODEL), lambda t, *_: (jnp.maximum(t - npt, 0), 0)),
                     ) + tuple(out for _, out in st_specs)
        out_shape = (jax.ShapeDtypeStruct(hp.shape, F32), jax.ShapeDtypeStruct(hs.shape, F32),
                     ) + tuple(jax.ShapeDtypeStruct((len(arrs),) + arrs[0].shape, F32) for arrs, _ in states)
    else:
        out_specs = pl.BlockSpec((TOK_TILE, D_MODEL), lambda t, *_: (t, 0))
        out_shape = jax.ShapeDtypeStruct((nt * TOK_TILE, D_MODEL), F32)
    grid_spec = pltpu.PrefetchScalarGridSpec(
        num_scalar_prefetch=4, grid=(nt,), in_specs=in_specs, out_specs=out_specs,
        scratch_shapes=[pltpu.VMEM((2, CBUF_ROWS, D_MODEL), BF16), pltpu.VMEM((D_MODEL, D_MODEL), BF16),
                        pltpu.VMEM((PLE_DIM, D_MODEL), BF16), pltpu.SemaphoreType.DMA((2,))])
    return pl.pallas_call(
        functools.partial(_combine_kernel, final=final, npt=npt, n_state=len(st_in)),
        grid_spec=grid_spec, out_shape=out_shape,
        compiler_params=pltpu.CompilerParams(dimension_semantics=("arbitrary",), vmem_limit_bytes=VMEM_LIMIT),
        name="moe_combine_final" if final else "moe_combine",
    )(*plan[:4], hp, hs, pp, ps, pos, y, P["g_ple"], P["w_ple_gate"], P["w_ple_proj"], g_final, *st_in)


def _ffn(hp, hs, pp, ps, l, P, g_final, *, states=None):
    n = hp.shape[0] + hs.shape[0]
    nt = n // TOK_TILE
    bound = TOP_K * n + nt * N_EXPERTS * (CHUNK - 1) + N_EXPERTS * (EXP_TILE - 1)
    n_sorted = -(-bound // EXP_TILE) * EXP_TILE
    pos, cnt = _route_call(hp, hs, l, P)
    plan = _chunk_plan(cnt[:, 0, :N_EXPERTS].astype(jnp.int32), n_sorted // EXP_TILE)
    xs = _dispatch_call(plan, hp, hs, pos, l, P, n_sorted)
    y = _expert_call(plan, xs, l, P["wg"], P["wu"], P["wd"])
    return _combine_call(plan, hp, hs, pp, ps, pos, y, l, P, g_final, states=states)


def _router_weights(g_ffn, w_grp_router, b_grp_router, w_exp_router, b_exp_router):
    depth = w_grp_router.shape[0]
    n_pad = LANES - N_EXPERTS - N_GROUPS
    w_er = jnp.transpose(w_exp_router, (0, 2, 1, 3)).reshape(depth, D_MODEL, N_EXPERTS)
    w_rt = jnp.concatenate([w_er, w_grp_router, jnp.zeros((depth, D_MODEL, n_pad), F32)], axis=2) * g_ffn[:, :, None]
    b_rt = jnp.concatenate([b_exp_router.reshape(depth, N_EXPERTS), b_grp_router, jnp.zeros((depth, n_pad), F32)], axis=1)
    w_hi = w_rt.astype(BF16)
    w_lo = (w_rt - w_hi.astype(F32)).astype(BF16)
    return jnp.concatenate([w_hi, w_lo], axis=2), b_rt.reshape(depth, 1, LANES)


def kernel(x_prompt, x_sample, state_gla, state_conv, p_prompt, p_sample, g_mix, w_in, w_forget_up, b_forget,
           g_gla_out, w_conv, b_conv, g_conv_ln, b_conv_ln, w_out, g_ffn, w_grp_router, b_grp_router,
           w_exp_router, b_exp_router, w_exp_gate, w_exp_up, w_exp_down, g_ple, w_ple_gate, w_ple_proj, g_final):
    depth = w_in.shape[0]
    nbp, seq_p, _ = x_prompt.shape
    nbs, seq_s, _ = x_sample.shape
    n_p = nbp * seq_p
    n_s = nbs * seq_s

    def rows(v):
        return v.reshape(depth, 1, -1)

    P = {
        "g_mix": rows(g_mix), "w_in_t": jnp.swapaxes(w_in, 1, 2),
        "w_forget_up": w_forget_up, "b_forget": rows(b_forget), "g_gla_out": rows(g_gla_out),
        "w_conv": w_conv, "b_conv": rows(b_conv), "g_conv_ln": rows(g_conv_ln), "b_conv_ln": rows(b_conv_ln),
        "w_out": w_out, "g_ffn": rows(g_ffn), "g_ple": rows(g_ple),
        "w_ple_gate": w_ple_gate, "w_ple_proj": w_ple_proj,
        "wg": w_exp_gate.reshape(depth * N_EXPERTS, D_MODEL, EXPERT_FF),
        "wu": w_exp_up.reshape(depth * N_EXPERTS, D_MODEL, EXPERT_FF),
        "wd": w_exp_down.reshape(depth * N_EXPERTS, EXPERT_FF, D_MODEL),
    }
    P["w_rt"], P["b_rt"] = _router_weights(g_ffn, w_grp_router, b_grp_router, w_exp_router, b_exp_router)
    g_fin = g_final.reshape(1, -1)
    xp = x_prompt.reshape(n_p, D_MODEL)
    xs = x_sample.reshape(n_s, D_MODEL)
    pp = p_prompt.reshape(depth, n_p, PLE_DIM)
    ps = p_sample.reshape(depth, n_s, PLE_DIM)
    s_in = state_gla.reshape(depth, nbs, QK_COLS, GLA_DV)
    c_in_t = jnp.swapaxes(state_conv, 1, 2)

    h = None
    sg_p, sg_s, sc_p, sc_s = [], [], [], []
    for l in range(depth):
        src_p, src_s, off_s = (xp, xs, 0) if l == 0 else (h, h, n_p)
        hp, sgp, scp = _guarded(functools.partial(_mixer_prompt, src_p, 0, nbp, seq_p, l, P))
        hs, sgs, scs = _guarded(functools.partial(_mixer_sample, src_s, off_s, nbs, seq_s, l, s_in, c_in_t, P))
        sg_p.append(sgp)
        sg_s.append(sgs)
        sc_p.append(scp)
        sc_s.append(scs)
        h = _ffn(hp, hs, pp, ps, l, P, g_fin, states=((sg_p, 0), (sg_s, 0), (sc_s, 1)) if l == depth - 1 else None)

    y_prompt, y_sample, sg_p_all, sg_s_all, sc_s_all = h
    return (y_prompt.reshape(nbp, seq_p, D_MODEL), y_sample.reshape(nbs, seq_s, D_MODEL),
            sg_p_all.reshape(depth, nbp, GLA_HEADS, GLA_DK, GLA_DV), sg_s_all.reshape(depth, nbs, GLA_HEADS, GLA_DK, GLA_DV),
            jnp.stack(sc_p), jnp.swapaxes(sc_s_all, 1, 2))
```

```python
import functools

import jax
import jax.numpy as jnp
from jax import lax
from jax.experimental import pallas as pl
from jax.experimental.pallas import tpu as pltpu

D_MODEL = 1024
GLA_HEADS = 4
GLA_DK = 64
GLA_DV = 128
QK_COLS = GLA_HEADS * GLA_DK
V_COLS = GLA_HEADS * GLA_DV
CONV_WIDTH = 512
CONV_K = 31
GLA_LOWRANK = 16
GLA_TAU = 16.0
GLA_CHUNK = 64
PLE_DIM = 256
N_GROUPS = 4
EXPERTS_PER_GROUP = 8
N_EXPERTS = N_GROUPS * EXPERTS_PER_GROUP
EXPERT_FF = 256
TOP_K = 2
EPS = 1e-6
N_MAIN = 2 * QK_COLS + 2 * V_COLS

LANES = 128
SUBLANES = 8
CONV_PAD = 32
CONV_OFF = CONV_PAD - (CONV_K - 1)
VMEM_LIMIT = 56 * 1024 * 1024
TOK_TILE = 512
CHUNK = 16
EXP_TILE = 512
CBUF_ROWS = -(-(TOP_K * TOK_TILE + N_EXPERTS * CHUNK) // LANES) * LANES
SORT_ROWS = 512
XS_COLS = D_MODEL + LANES
ROUTE_ROWS = 48
DECAY_SAFE = 80.0

F32 = jnp.float32
BF16 = jnp.bfloat16
HI = lax.Precision.HIGHEST


def _sigmoid(x):
    return 1.0 / (1.0 + jnp.exp(-x))


def _silu(x):
    return x * _sigmoid(x)


def _log_sigmoid(x):
    return jnp.minimum(x, 0.0) - jnp.log(1.0 + jnp.exp(-jnp.abs(x)))


def _rmsnorm(x, g):
    return x * lax.rsqrt(jnp.mean(x * x, axis=-1, keepdims=True) + EPS) * g


def _dot(a, b):
    return jnp.dot(a.astype(BF16), b.astype(BF16), preferred_element_type=F32)


def _dot_t(a, b):
    return lax.dot_general(a.astype(BF16), b.astype(BF16), (((1,), (1,)), ((), ())),
                           preferred_element_type=F32)


def _dot_hi(a, b):
    return jnp.dot(a, b, preferred_element_type=F32, precision=HI)


def _iota_f32(shape, dim):
    return lax.broadcasted_iota(jnp.int32, shape, dim).astype(F32)


def _const_spec(shape):
    nd = len(shape)
    return pl.BlockSpec(shape, lambda *_: (0,) * nd)


def _layer_spec(arr, l):
    nd = arr.ndim - 1
    return pl.BlockSpec((None,) + arr.shape[1:], lambda *_: (l,) + (0,) * nd, pipeline_mode=pl.Buffered(1))


def _cast_mixer_weights(w_int_ref, w_fu_ref, w_out_ref, wmain_s, wlr_s, wcv_s, wfu_s, wout_s):
    blk = 4 * LANES
    for r in range(0, N_MAIN, blk):
        wmain_s[:, r:r + blk] = w_int_ref[r:r + blk, :].T.astype(BF16)
    lane = lax.broadcasted_iota(jnp.int32, (D_MODEL, LANES), 1)
    wlr_s[...] = jnp.where(lane < GLA_LOWRANK, w_int_ref[N_MAIN:N_MAIN + LANES, :].T, 0.0).astype(BF16)
    cv0 = N_MAIN + GLA_LOWRANK
    for r in range(0, 2 * CONV_WIDTH, blk):
        wcv_s[:, r:r + blk] = w_int_ref[cv0 + r:cv0 + r + blk, :].T.astype(BF16)
    wfu_s[...] = jnp.zeros_like(wfu_s)
    wfu_s[0:GLA_LOWRANK, :] = w_fu_ref[...].astype(BF16)
    wout_s[...] = w_out_ref[...].astype(BF16)


def _project(x, g_mix, wmain_s, wlr_s, wfu_s, b_f, wcv_s):
    xn = _rmsnorm(x, g_mix).astype(BF16)
    cv = jnp.dot(xn, wcv_s[...], preferred_element_type=F32)
    u = cv[:, :CONV_WIDTH] * _sigmoid(cv[:, CONV_WIDTH:])
    lr = jnp.dot(xn, wlr_s[...], preferred_element_type=F32)
    zf = _dot(lr, wfu_s[...]) + b_f
    la = _log_sigmoid(zf) * (1.0 / GLA_TAU)
    zqk = jnp.dot(xn, wmain_s[:, :2 * QK_COLS], preferred_element_type=F32)
    q = zqk[:, :QK_COLS] * (GLA_DK ** -0.5)
    k = zqk[:, QK_COLS:]
    zvo = jnp.dot(xn, wmain_s[:, 2 * QK_COLS:], preferred_element_type=F32)
    v = zvo[:, :V_COLS]
    og = zvo[:, V_COLS:]
    return q, k, v, og, la, u


def _stack_heads(qd):
    lane = lax.broadcasted_iota(jnp.int32, qd.shape, 1)
    return jnp.concatenate(
        [jnp.where((lane >= h * GLA_DK) & (lane < (h + 1) * GLA_DK), qd, 0.0) for h in range(GLA_HEADS)],
        axis=0)


def _gated_head_norm(o, og, g_gla):
    outs = []
    for h in range(GLA_HEADS):
        sl = slice(h * GLA_DV, (h + 1) * GLA_DV)
        outs.append(_rmsnorm(o[:, sl], g_gla) * _silu(og[:, sl]))
    return jnp.concatenate(outs, axis=1)


def _causal_conv(win, w_conv, b_conv, n):
    acc = jnp.broadcast_to(b_conv, (n, CONV_WIDTH))
    for s in range(SUBLANES):
        taps = [j for j in range(CONV_K) if (CONV_OFF + j) % SUBLANES == s]
        if not taps:
            continue
        rows = n if s == 0 else n + SUBLANES
        part = None
        for j in taps:
            a = (CONV_OFF + j) - s
            term = w_conv[j:j + 1, :] * win[a:a + rows, :]
            part = term if part is None else part + term
        acc = acc + part[s:s + n, :]
    return acc


def _conv_ln_act(acc, g_ln, b_ln):
    mu = jnp.mean(acc, axis=-1, keepdims=True)
    xc = acc - mu
    y = xc * lax.rsqrt(jnp.mean(xc * xc, axis=-1, keepdims=True) + EPS) * g_ln + b_ln
    return _silu(y)


def _head_diag(upd):
    return jnp.concatenate([upd[h * GLA_DK:(h + 1) * GLA_DK, h * GLA_DV:(h + 1) * GLA_DV]
                            for h in range(GLA_HEADS)], axis=0)


def _mixer_weight_args(l, P):
    names = ("g_mix", "w_in_t", "w_forget_up", "b_forget", "g_gla_out", "w_conv", "b_conv", "g_conv_ln",
             "b_conv_ln", "w_out")
    arrs = [P[n] for n in names]
    return arrs, [_layer_spec(a, l) for a in arrs]


_MIXER_WEIGHT_SCRATCH = [pltpu.VMEM((D_MODEL, N_MAIN), BF16), pltpu.VMEM((D_MODEL, LANES), BF16),
                         pltpu.VMEM((D_MODEL, 2 * CONV_WIDTH), BF16), pltpu.VMEM((LANES, QK_COLS), BF16),
                         pltpu.VMEM((D_MODEL, D_MODEL), BF16)]


def _mixer_prompt_kernel(x_ref, g_mix_ref, w_int_ref, w_fu_ref, b_f_ref, g_gla_ref, w_conv_ref,
                         b_conv_ref, g_ln_ref, b_ln_ref, w_out_ref,
                         h_ref, sg_ref, sc_ref, risk_ref,
                         wmain_s, wlr_s, wcv_s, wfu_s, wout_s,
                         s_ref, ubuf_ref, qs_ref, kd_ref, klt_ref, dec_ref, mid_ref, v_ref, og_ref, mix_ref,
                         *stable_refs, tt, stable):
    t = pl.program_id(1)
    nt = pl.num_programs(1)
    C = GLA_CHUNK
    n_chunks = tt // C

    @pl.when((pl.program_id(0) == 0) & (t == 0))
    def _():
        _cast_mixer_weights(w_int_ref, w_fu_ref, w_out_ref, wmain_s, wlr_s, wcv_s, wfu_s, wout_s)

    @pl.when(t == 0)
    def _():
        s_ref[...] = jnp.zeros_like(s_ref)
        ubuf_ref[0:CONV_PAD, :] = jnp.zeros((CONV_PAD, CONV_WIDTH), F32)

    x = x_ref[...]
    q, k, v, og, la, u = _project(x, g_mix_ref[...], wmain_s, wlr_s, wfu_s, b_f_ref[...], wcv_s)
    ubuf_ref[CONV_PAD:CONV_PAD + tt, :] = u
    v_ref[...] = v.astype(BF16)
    og_ref[...] = _silu(og)

    w_conv = w_conv_ref[...]
    b_conv = b_conv_ref[...]
    g_ln = g_ln_ref[...]
    b_ln = b_ln_ref[...]
    row = lax.broadcasted_iota(jnp.int32, (C, C), 0)
    col = lax.broadcasted_iota(jnp.int32, (C, C), 1)
    tri = (col <= row).astype(F32)
    risk = jnp.zeros((1, QK_COLS), F32)
    for c in range(n_chunks):
        rows = slice(c * C, (c + 1) * C)
        win = ubuf_ref[c * C:c * C + C + CONV_PAD, :]
        cact = _conv_ln_act(_causal_conv(win, w_conv, b_conv, C), g_ln, b_ln)
        mix_ref[rows, V_COLS:] = cact.astype(BF16)
        bits = pltpu.bitcast(cact[C - SUBLANES:C, 0:QK_COLS], jnp.uint32)
        zero = pltpu.bitcast(lax.shift_right_logical(lax.shift_right_logical(bits, jnp.uint32(16)), jnp.uint32(16)),
                             F32)[0:1, :]
        qc = q[rows, :] + zero
        b = _dot_hi(tri, la[rows, :])
        b_last = b[C - 1:C, :]
        if stable:
            q_st, k_st, b_st = stable_refs
            q_st[rows, :] = qc
            k_st[rows, :] = k[rows, :]
            b_st[rows, :] = b
            b_mid = jnp.zeros_like(b_last)
        else:
            b_mid = b[C // 2 - 1:C // 2, :]
            risk = jnp.maximum(risk, jnp.maximum(-b_mid, b_mid - b_last))
        qs_ref[c] = _stack_heads(qc * jnp.exp(b - b_mid)).astype(BF16)
        kd_ref[rows, :] = (k[rows, :] * jnp.exp(jnp.minimum(b_mid - b, DECAY_SAFE))).astype(BF16)
        kl = k[rows, :] * jnp.exp(b_last - b)
        klt = jnp.concatenate([kl, jnp.broadcast_to(jnp.exp(b_last), (C // 2, QK_COLS)),
                               jnp.broadcast_to(jnp.exp(b_mid), (C // 2, QK_COLS))], axis=0).T
        klt_ref[c] = klt.astype(BF16)
        dec_ref[c] = jnp.broadcast_to(klt[:, C:C + 1], (QK_COLS, GLA_DV))
        mid_ref[c] = jnp.broadcast_to(klt[:, 3 * C // 2:3 * C // 2 + 1], (QK_COLS, GLA_DV))

    r4 = lax.broadcasted_iota(jnp.int32, (GLA_HEADS * C, C), 0)
    c4 = lax.broadcasted_iota(jnp.int32, (GLA_HEADS * C, C), 1)
    causal4 = c4 <= (r4 % C)
    g_gla = g_gla_ref[...]
    s = s_ref[...]
    for c in range(n_chunks):
        rows = slice(c * C, (c + 1) * C)
        qs = qs_ref[c]
        vc = v_ref[rows, :]
        if stable:
            scores = _direct_scores(*stable_refs, c * C, C).astype(BF16)
        else:
            scores = jnp.where(causal4, _dot_t(qs, kd_ref[rows, :]), 0.0).astype(BF16)
        o_inter = jnp.dot(qs, (mid_ref[c] * s).astype(BF16), preferred_element_type=F32)
        upd = jnp.dot(klt_ref[c][:, :C], vc, preferred_element_type=F32)
        s = dec_ref[c] * s + _head_diag(upd)
        o_parts = []
        for h in range(GLA_HEADS):
            vh = vc[:, h * GLA_DV:(h + 1) * GLA_DV]
            o_parts.append(jnp.dot(scores[h * C:(h + 1) * C, :], vh, preferred_element_type=F32)
                           + o_inter[h * C:(h + 1) * C, :])
        o = jnp.concatenate(o_parts, axis=1)
        gated = []
        for h in range(GLA_HEADS):
            sl = slice(h * GLA_DV, (h + 1) * GLA_DV)
            gated.append(_rmsnorm(o[:, sl], g_gla) * og_ref[rows, sl])
        mix_ref[rows, 0:V_COLS] = jnp.concatenate(gated, axis=1).astype(BF16)
    s_ref[...] = s

    h_ref[...] = x + jnp.dot(mix_ref[...], wout_s[...], preferred_element_type=F32)
    tail = ubuf_ref[tt:tt + CONV_PAD, :]
    ubuf_ref[0:CONV_PAD, :] = tail

    risk_ref[0] = jnp.broadcast_to(jnp.max(risk, axis=-1, keepdims=True), (SUBLANES, LANES))

    @pl.when(t == nt - 1)
    def _():
        sg_ref[0] = s
        sc_ref[0] = tail[CONV_OFF:, :]


def _direct_scores(q_st, k_st, b_st, r0, C):
    kc = k_st[r0:r0 + C, :]
    bc = b_st[r0:r0 + C, :]
    srow = lax.broadcasted_iota(jnp.int32, (C, QK_COLS), 0)
    lane = lax.broadcasted_iota(jnp.int32, (C, LANES), 1)
    head_of = lax.broadcasted_iota(jnp.int32, (QK_COLS, LANES), 0) // GLA_DK
    head_sum = (head_of == lax.broadcasted_iota(jnp.int32, (QK_COLS, LANES), 1)).astype(F32)

    def one_query(t, acc):
        d = b_st[pl.ds(r0 + t, 1), :] - bc
        w = jnp.where(srow <= t, jnp.exp(jnp.minimum(d, 0.0)), 0.0) * kc * q_st[pl.ds(r0 + t, 1), :]
        per_head = _dot_hi(w, head_sum)
        return tuple(jnp.where(lane == t, per_head[:, h:h + 1], acc[h]) for h in range(GLA_HEADS))

    acc = lax.fori_loop(0, C, one_query, tuple(jnp.zeros((C, LANES), F32) for _ in range(GLA_HEADS)))
    return jnp.concatenate([a.T[0:C, :] for a in acc], axis=0)


def _mixer_prompt(x, row_off, nb, seq, l, P, *, tt=512, stable=False):
    nt = seq // tt
    n_chunks = tt // GLA_CHUNK
    blk_off = row_off // tt
    weights, w_specs = _mixer_weight_args(l, P)
    in_specs = [pl.BlockSpec((tt, D_MODEL), lambda b, t: (blk_off + b * nt + t, 0))] + w_specs
    out_shape = (jax.ShapeDtypeStruct((nb * seq, D_MODEL), F32),
                 jax.ShapeDtypeStruct((nb, QK_COLS, GLA_DV), F32),
                 jax.ShapeDtypeStruct((nb, CONV_K - 1, CONV_WIDTH), F32),
                 jax.ShapeDtypeStruct((nb * nt, SUBLANES, LANES), F32))
    out_specs = (pl.BlockSpec((tt, D_MODEL), lambda b, t: (b * nt + t, 0)),
                 pl.BlockSpec((1, QK_COLS, GLA_DV), lambda b, t: (b, 0, 0)),
                 pl.BlockSpec((1, CONV_K - 1, CONV_WIDTH), lambda b, t: (b, 0, 0)),
                 pl.BlockSpec((1, SUBLANES, LANES), lambda b, t: (b * nt + t, 0, 0)))
    scratch = _MIXER_WEIGHT_SCRATCH + [
        pltpu.VMEM((QK_COLS, GLA_DV), F32),
        pltpu.VMEM((CONV_PAD + tt, CONV_WIDTH), F32),
        pltpu.VMEM((n_chunks, GLA_HEADS * GLA_CHUNK, QK_COLS), BF16), pltpu.VMEM((tt, QK_COLS), BF16),
        pltpu.VMEM((n_chunks, QK_COLS, 2 * GLA_CHUNK), BF16), pltpu.VMEM((n_chunks, QK_COLS, GLA_DV), F32),
        pltpu.VMEM((n_chunks, QK_COLS, GLA_DV), F32),
        pltpu.VMEM((tt, V_COLS), BF16), pltpu.VMEM((tt, V_COLS), F32),
        pltpu.VMEM((tt, D_MODEL), BF16)]
    if stable:
        scratch += [pltpu.VMEM((tt, QK_COLS), F32)] * 3
    return pl.pallas_call(
        functools.partial(_mixer_prompt_kernel, tt=tt, stable=stable),
        grid=(nb, nt), in_specs=in_specs, out_specs=out_specs, out_shape=out_shape,
        scratch_shapes=scratch,
        compiler_params=pltpu.CompilerParams(dimension_semantics=("arbitrary", "arbitrary"),
                                             vmem_limit_bytes=VMEM_LIMIT),
        name="mixer_prompt_stable" if stable else "mixer_prompt",
    )(x, *weights)


def _guarded(mixer):
    *outs, risk = mixer(stable=False)
    return lax.cond(jnp.max(risk) > DECAY_SAFE, lambda: tuple(mixer(stable=True)[:-1]), lambda: tuple(outs))


def _mixer_sample_kernel(x_ref, s_in_ref, c_in_ref, g_mix_ref, w_int_ref, w_fu_ref, b_f_ref,
                         g_gla_ref, w_conv_ref, b_conv_ref, g_ln_ref, b_ln_ref, w_out_ref,
                         h_ref, sg_ref, sc_ref, risk_ref,
                         wmain_s, wlr_s, wcv_s, wfu_s, wout_s,
                         u4_ref, oi_ref, cacc4_ref, *, sb, seq, stable):
    R = sb * seq
    n_slabs = CONV_WIDTH // LANES

    @pl.when(pl.program_id(0) == 0)
    def _():
        _cast_mixer_weights(w_int_ref, w_fu_ref, w_out_ref, wmain_s, wlr_s, wcv_s, wfu_s, wout_s)

    x = x_ref[...]
    q, k, v, og, la, u = _project(x, g_mix_ref[...], wmain_s, wlr_s, wfu_s, b_f_ref[...], wcv_s)

    for kk in range(n_slabs):
        u4_ref[kk] = u[:, kk * LANES:(kk + 1) * LANES]
    full = [c_in_ref[j] for j in range(CONV_K - 1)]
    for t in range(seq):
        full.append(jnp.concatenate([u4_ref.at[kk][pl.ds(t, sb, stride=seq), :] for kk in range(n_slabs)], axis=1))
    w_conv = w_conv_ref[...]
    for t in range(seq):
        acc = jnp.broadcast_to(b_conv_ref[...], (sb, CONV_WIDTH))
        for j in range(CONV_K):
            acc = acc + w_conv[j:j + 1, :] * full[t + j]
        for kk in range(n_slabs):
            cacc4_ref.at[kk][pl.ds(t, sb, stride=seq), :] = acc[:, kk * LANES:(kk + 1) * LANES]
    for j in range(CONV_K - 1):
        sc_ref[j] = full[seq + j]

    row = lax.broadcasted_iota(jnp.int32, (R, R), 0)
    col = lax.broadcasted_iota(jnp.int32, (R, R), 1)
    same = (row // seq) == (col // seq)
    b = _dot_hi((same & (col <= row)).astype(F32), la)
    b_tot = _dot_hi(same.astype(F32), la)
    qd = q * jnp.exp(b)
    kl = k * jnp.exp(b_tot - b)
    qs = _stack_heads(qd)
    risk_ref[0] = jnp.broadcast_to(jnp.max(jnp.max(-b_tot, axis=-1, keepdims=True), axis=0, keepdims=True),
                                   (SUBLANES, LANES))
    if stable:
        head_of = lax.broadcasted_iota(jnp.int32, (QK_COLS, LANES), 0) // GLA_DK
        head_sum = (head_of == lax.broadcasted_iota(jnp.int32, (QK_COLS, LANES), 1)).astype(F32)
        t_in_seq = lax.broadcasted_iota(jnp.int32, (R, QK_COLS), 0) % seq
        parts = [jnp.zeros((R, R), F32) for _ in range(GLA_HEADS)]
        for d in range(seq):
            k_d = jnp.concatenate([jnp.zeros((d, QK_COLS), F32), k[:R - d, :]], axis=0) if d else k
            b_d = jnp.concatenate([jnp.zeros((d, QK_COLS), F32), b[:R - d, :]], axis=0) if d else b
            w = jnp.where(t_in_seq >= d, jnp.exp(jnp.minimum(b - b_d, 0.0)), 0.0) * k_d * q
            per_head = _dot_hi(w, head_sum)
            for h in range(GLA_HEADS):
                parts[h] = jnp.where(col == row - d, per_head[:, h:h + 1], parts[h])
        scores = jnp.concatenate(parts, axis=0)
    else:
        kd = k * jnp.exp(jnp.minimum(-b, DECAY_SAFE))
        r4 = lax.broadcasted_iota(jnp.int32, (GLA_HEADS * R, R), 0) % R
        c4 = lax.broadcasted_iota(jnp.int32, (GLA_HEADS * R, R), 1)
        mask4 = ((r4 // seq) == (c4 // seq)) & (c4 <= r4)
        scores = jnp.where(mask4, _dot_t(qs, kd), 0.0)

    klt = kl.T
    dect = jnp.exp(b_tot).T
    lane_h = lax.broadcasted_iota(jnp.int32, (GLA_DK, R), 1)
    lane_r = lax.broadcasted_iota(jnp.int32, (QK_COLS, R), 1)
    upd = []
    for h in range(GLA_HEADS):
        klt_h = klt[h * GLA_DK:(h + 1) * GLA_DK, :]
        lhs = jnp.concatenate([jnp.where((lane_h >= i * seq) & (lane_h < (i + 1) * seq), klt_h, 0.0)
                               for i in range(sb)], axis=0)
        upd.append(_dot(lhs, v[:, h * GLA_DV:(h + 1) * GLA_DV]))
    for i in range(sb):
        s_old = s_in_ref[i]
        qsel = jnp.concatenate([qs[h * R + i * seq:h * R + (i + 1) * seq, :] for h in range(GLA_HEADS)], axis=0)
        oi = _dot(qsel, s_old)
        for h in range(GLA_HEADS):
            oi_ref[h, i * seq:(i + 1) * seq, :] = oi[h * seq:(h + 1) * seq, :]
        smask = (lane_r >= i * seq) & (lane_r < (i + 1) * seq)
        dec = jnp.sum(jnp.where(smask, dect, 0.0), axis=1, keepdims=True) * (1.0 / seq)
        u_new = jnp.concatenate([upd[h][i * GLA_DK:(i + 1) * GLA_DK, :] for h in range(GLA_HEADS)], axis=0)
        sg_ref[i] = dec * s_old + u_new

    o_parts = []
    for h in range(GLA_HEADS):
        vh = v[:, h * GLA_DV:(h + 1) * GLA_DV]
        o_parts.append(_dot(scores[h * R:(h + 1) * R, :], vh) + oi_ref[h])
    o = jnp.concatenate(o_parts, axis=1)
    cacc = jnp.concatenate([cacc4_ref[kk] for kk in range(n_slabs)], axis=1)
    mix = jnp.concatenate([_gated_head_norm(o, og, g_gla_ref[...]),
                           _conv_ln_act(cacc, g_ln_ref[...], b_ln_ref[...])], axis=1)
    h_ref[...] = x + jnp.dot(mix.astype(BF16), wout_s[...], preferred_element_type=F32)


def _mixer_sample(x, row_off, nb, seq, l, s_in, c_in_t, P, *, sb=16, stable=False):
    R = sb * seq
    blk_off = row_off // R
    n_slabs = CONV_WIDTH // LANES
    weights, w_specs = _mixer_weight_args(l, P)
    in_specs = [pl.BlockSpec((R, D_MODEL), lambda i: (blk_off + i, 0)),
                pl.BlockSpec((None, sb, QK_COLS, GLA_DV), lambda i: (l, i, 0, 0)),
                pl.BlockSpec((None, CONV_K - 1, sb, CONV_WIDTH), lambda i: (l, 0, i, 0))] + w_specs
    out_shape = (jax.ShapeDtypeStruct((nb * seq, D_MODEL), F32),
                 jax.ShapeDtypeStruct((nb, QK_COLS, GLA_DV), F32),
                 jax.ShapeDtypeStruct((CONV_K - 1, nb, CONV_WIDTH), F32),
                 jax.ShapeDtypeStruct((nb // sb, SUBLANES, LANES), F32))
    out_specs = (pl.BlockSpec((R, D_MODEL), lambda i: (i, 0)),
                 pl.BlockSpec((sb, QK_COLS, GLA_DV), lambda i: (i, 0, 0)),
                 pl.BlockSpec((CONV_K - 1, sb, CONV_WIDTH), lambda i: (0, i, 0)),
                 pl.BlockSpec((1, SUBLANES, LANES), lambda i: (i, 0, 0)))
    scratch = _MIXER_WEIGHT_SCRATCH + [
        pltpu.VMEM((n_slabs, R, LANES), F32),
        pltpu.VMEM((GLA_HEADS, R, GLA_DV), F32),
        pltpu.VMEM((n_slabs, R, LANES), F32)]
    return pl.pallas_call(
        functools.partial(_mixer_sample_kernel, sb=sb, seq=seq, stable=stable),
        grid=(nb // sb,), in_specs=in_specs, out_specs=out_specs, out_shape=out_shape,
        scratch_shapes=scratch,
        compiler_params=pltpu.CompilerParams(dimension_semantics=("arbitrary",),
                                             vmem_limit_bytes=VMEM_LIMIT),
        name="mixer_sample_stable" if stable else "mixer_sample",
    )(x, s_in, c_in_t, *weights)


def _pair_specs(n_first_tiles, width):
    return [pl.BlockSpec((TOK_TILE, width), lambda t, *_: (jnp.minimum(t, n_first_tiles - 1), 0)),
            pl.BlockSpec((TOK_TILE, width), lambda t, *_: (jnp.maximum(t - n_first_tiles, 0), 0))]


def _pick(t, n_first_tiles, a_ref, b_ref):
    return jnp.where(t < n_first_tiles, a_ref[...], b_ref[...])


def _route(logits):
    row = lax.broadcasted_iota(jnp.int32, logits.shape, 0)
    row_f = row.astype(F32)
    neg = jnp.float32(-jnp.inf)
    big = jnp.float32(1e9)
    is_grp = (row >= N_EXPERTS) & (row < N_EXPERTS + N_GROUPS)
    gl = jnp.where(is_grp, logits, neg)
    gmax = jnp.max(gl, axis=0, keepdims=True)
    gidx = jnp.min(jnp.where(is_grp & (gl == gmax), row_f - N_EXPERTS, big), axis=0, keepdims=True)
    gsum = jnp.sum(jnp.where(is_grp, jnp.exp(gl - gmax), 0.0), axis=0, keepdims=True)
    g_w = 1.0 / gsum
    grp_of_row = jnp.floor(row_f * (1.0 / EXPERTS_PER_GROUP))
    in_grp = (row < N_EXPERTS) & (grp_of_row == gidx)
    ml = jnp.where(in_grp, logits, neg)
    v1 = jnp.max(ml, axis=0, keepdims=True)
    i1 = jnp.min(jnp.where(in_grp & (ml == v1), row_f, big), axis=0, keepdims=True)
    ml2 = jnp.where(row_f == i1, neg, ml)
    v2 = jnp.max(ml2, axis=0, keepdims=True)
    i2 = jnp.min(jnp.where(in_grp & (ml2 == v2), row_f, big), axis=0, keepdims=True)
    e2 = jnp.exp(v2 - v1)
    w1 = g_w / (1.0 + e2)
    w2 = g_w * e2 / (1.0 + e2)
    return i1, i2, w1, w2


def _route_kernel(hp_ref, hs_ref, w_rt_ref, b_rt_ref, pos_ref, cnt_ref, earlier_s, *, npt):
    T = TOK_TILE
    R = ROUTE_ROWS

    @pl.when(pl.program_id(0) == 0)
    def _():
        earlier_s[...] = (_iota_f32((T, T), 0) < _iota_f32((T, T), 1)).astype(BF16)

    x = _pick(pl.program_id(0), npt, hp_ref, hs_ref)
    scale = lax.rsqrt(jnp.mean(x * x, axis=-1, keepdims=True) + EPS)
    x_hi = x.astype(BF16)
    x_lo = (x - x_hi.astype(F32)).astype(BF16)
    w_split = w_rt_ref[...]
    both = jnp.dot(x_hi, w_split, preferred_element_type=F32)
    logits = ((both[:, :LANES] + both[:, LANES:] + jnp.dot(x_lo, w_split[:, :LANES], preferred_element_type=F32))
              * scale + b_rt_ref[...])
    i1, i2, w1, w2 = _route(logits.T[:R])
    row = _iota_f32((R, T), 0)
    a0 = (row == i1).astype(F32)
    a1 = (row == i2).astype(F32)
    a = a0 + a1
    cnt = jnp.sum(a, axis=1, keepdims=True)
    rank = jnp.dot(a.astype(BF16), earlier_s[...], preferred_element_type=F32)
    cnt_pad = jnp.maximum(jnp.ceil(cnt * (1.0 / CHUNK)), 1.0) * CHUNK
    no_rows = jnp.zeros((LANES - R, LANES), F32)
    before = (_iota_f32((LANES, LANES), 1) < _iota_f32((LANES, LANES), 0)).astype(F32)
    first = _dot_hi(before, jnp.concatenate([jnp.broadcast_to(cnt_pad, (R, LANES)), no_rows], axis=0))[:R, 0:1]
    base = first + rank
    pos0 = jnp.sum(a0 * base, axis=0, keepdims=True)
    pos1 = jnp.sum(a1 * base, axis=0, keepdims=True)
    r8 = _iota_f32((SUBLANES, T), 0)
    res = jnp.where(r8 == 0.0, pos0, jnp.where(r8 == 1.0, pos1, jnp.where(
        r8 == 2.0, w1, jnp.where(r8 == 3.0, w2, jnp.where(r8 == 4.0, i1, 0.0)))))
    pos_ref[...] = jnp.concatenate([res, jnp.zeros((LANES - SUBLANES, T), F32)], axis=0).T
    cnt_ref[0] = jnp.concatenate([jnp.broadcast_to(cnt, (R, LANES)), no_rows], axis=0).T[:SUBLANES]


def _route_call(hp, hs, l, P):
    npt = hp.shape[0] // TOK_TILE
    nt = npt + hs.shape[0] // TOK_TILE
    return pl.pallas_call(
        functools.partial(_route_kernel, npt=npt), grid=(nt,),
        in_specs=_pair_specs(npt, D_MODEL) + [_layer_spec(P[n], l) for n in ("w_rt", "b_rt")],
        scratch_shapes=[pltpu.VMEM((TOK_TILE, TOK_TILE), BF16)],
        out_specs=(pl.BlockSpec((TOK_TILE, LANES), lambda t: (t, 0)),
                   pl.BlockSpec((1, SUBLANES, LANES), lambda t: (t, 0, 0))),
        out_shape=(jax.ShapeDtypeStruct((nt * TOK_TILE, LANES), F32),
                   jax.ShapeDtypeStruct((nt, SUBLANES, LANES), F32)),
        compiler_params=pltpu.CompilerParams(dimension_semantics=("arbitrary",), vmem_limit_bytes=VMEM_LIMIT),
        name="moe_route",
    )(hp, hs, P["w_rt"], P["b_rt"])


def _chunk_plan(cnt, n_row_tiles):
    n16 = jnp.maximum((cnt + (CHUNK - 1)) // CHUNK, 1)
    lofs16 = jnp.cumsum(n16, axis=1) - n16
    tile_pref16 = jnp.cumsum(n16, axis=0) - n16
    tot16 = jnp.sum(n16, axis=0)
    per_tile = EXP_TILE // CHUNK
    seg16 = ((tot16 + per_tile - 1) // per_tile) * per_tile
    seg_end16 = jnp.cumsum(seg16)
    dst16 = (seg_end16 - seg16)[None, :] + tile_pref16
    n_tot = jnp.sum(n16, axis=1)
    gap16 = seg_end16 - seg16 + tot16
    gapn16 = seg16 - tot16
    tile_start16 = jnp.arange(n_row_tiles, dtype=jnp.int32) * per_tile
    n_valid = seg_end16[-1] // per_tile
    misc = n_valid.reshape(1)
    exp_of_tile = jnp.minimum(jnp.sum(seg_end16[None, :] <= tile_start16[:, None], axis=1), N_EXPERTS - 1)
    i32 = lambda a: a.astype(jnp.int32).reshape(-1)
    return (i32(dst16), i32(n16), i32(lofs16), i32(n_tot), i32(gap16), i32(gapn16), i32(misc), i32(exp_of_tile))


def _chunk_copy(src, dst, src_chunk, dst_chunk, sem, n_chunks=1):
    rows = n_chunks * CHUNK
    return pltpu.make_async_copy(src.at[pl.ds(pl.multiple_of(src_chunk * CHUNK, CHUNK), rows), :],
                                 dst.at[pl.ds(pl.multiple_of(dst_chunk * CHUNK, CHUNK), rows), :], sem)


def _slab_copies(src, dst, sem, tile, src_ofs_ref, dst_ofs_ref, n16_ref):
    for e in range(N_EXPERTS):
        k = tile * N_EXPERTS + e
        _chunk_copy(src, dst, src_ofs_ref[k], dst_ofs_ref[k], sem, n16_ref[k]).start()


def _wait_slabs(src, dst, sem, n_chunks):
    _chunk_copy(src, dst, 0, 0, sem, n_chunks).wait()


def _wait_fill(src, dst, sem, n_chunks):
    @pl.when(n_chunks > 0)
    def _():
        _wait_slabs(src, dst, sem, n_chunks)


def _tile_copy(src, dst, dst_tile, sem):
    return pltpu.make_async_copy(src, dst.at[pl.ds(pl.multiple_of(dst_tile * EXP_TILE, EXP_TILE), EXP_TILE), :], sem)


def _dispatch_kernel(dst16_ref, n16_ref, lofs16_ref, ntot_ref, gap16_ref, gapn16_ref, misc_ref,
                     hp_ref, hs_ref, pos_ref, g_ffn_ref, xs_hbm, cbuf, zbuf, sem, *, n_row_tiles, npt):
    t = pl.program_id(0)
    nt = pl.num_programs(0)
    slot = t % 2
    T = TOK_TILE
    n_tail = n_row_tiles - misc_ref[0]

    @pl.when(t == 0)
    def _():
        zbuf[...] = jnp.zeros_like(zbuf)
        for e in range(N_EXPERTS):
            g = gapn16_ref[e]

            @pl.when(g > 0)
            def _(e=e, g=g):
                _chunk_copy(zbuf, xs_hbm, 0, gap16_ref[e], sem.at[2], g).start()

        def fill_tile(i, carry):
            _tile_copy(zbuf, xs_hbm, misc_ref[0] + i, sem.at[2]).start()
            return carry
        lax.fori_loop(0, n_tail, fill_tile, 0)

    xn = _rmsnorm(_pick(t, npt, hp_ref, hs_ref), g_ffn_ref[...]).astype(BF16)
    pos = pos_ref[...]
    pos_t = pos.T
    lane = lax.broadcasted_iota(jnp.int32, (T, LANES), 1)
    extra = jnp.zeros((T, LANES), F32)
    for s in range(TOP_K):
        c = pos[:, 2 + s:3 + s]
        hi = c.astype(BF16).astype(F32)
        mid = (c - hi).astype(BF16).astype(F32)
        lo = c - hi - mid
        for j, piece in enumerate((hi, mid, lo)):
            extra = jnp.where(lane == 3 * s + j, piece, extra)
    extra = jnp.where(lane == 3 * TOP_K, pos[:, 4:5], extra)
    moved = jnp.concatenate([xn, extra.astype(BF16)], axis=1)
    for r0 in range(0, CBUF_ROWS, SORT_ROWS):
        rows = _iota_f32((SORT_ROWS, T), 0) + float(r0)
        onehot = jnp.where((rows == pos_t[0:1, :]) | (rows == pos_t[1:2, :]), 1.0, 0.0).astype(BF16)
        cbuf[slot, r0:r0 + SORT_ROWS] = jnp.dot(onehot, moved, preferred_element_type=F32).astype(BF16)

    src = cbuf.at[slot]
    _slab_copies(src, xs_hbm, sem.at[slot], t, lofs16_ref, dst16_ref, n16_ref)

    @pl.when(t > 0)
    def _():
        _wait_slabs(cbuf.at[1 - slot], xs_hbm, sem.at[1 - slot], ntot_ref[t - 1])

    @pl.when(t == nt - 1)
    def _():
        _wait_slabs(src, xs_hbm, sem.at[slot], ntot_ref[t])
        for e in range(N_EXPERTS):
            _wait_fill(zbuf, xs_hbm, sem.at[2], gapn16_ref[e])

        def wait_tile(_, carry):
            _tile_copy(zbuf, xs_hbm, 0, sem.at[2]).wait()
            return carry
        lax.fori_loop(0, n_tail, wait_tile, 0)


def _dispatch_call(plan, hp, hs, pos, l, P, n_sorted):
    npt = hp.shape[0] // TOK_TILE
    nt = npt + hs.shape[0] // TOK_TILE
    g_ffn = P["g_ffn"]
    grid_spec = pltpu.PrefetchScalarGridSpec(
        num_scalar_prefetch=7, grid=(nt,),
        in_specs=_pair_specs(npt, D_MODEL) + [
            pl.BlockSpec((TOK_TILE, LANES), lambda t, *_: (t, 0)),
            pl.BlockSpec((None,) + g_ffn.shape[1:], lambda t, *_: (l, 0, 0))],
        out_specs=pl.BlockSpec(memory_space=pl.ANY),
        scratch_shapes=[pltpu.VMEM((2, CBUF_ROWS, XS_COLS), BF16), pltpu.VMEM((EXP_TILE, XS_COLS), BF16),
                        pltpu.SemaphoreType.DMA((3,))])
    return pl.pallas_call(
        functools.partial(_dispatch_kernel, n_row_tiles=n_sorted // EXP_TILE, npt=npt), grid_spec=grid_spec,
        out_shape=jax.ShapeDtypeStruct((n_sorted, XS_COLS), BF16),
        compiler_params=pltpu.CompilerParams(dimension_semantics=("arbitrary",), vmem_limit_bytes=VMEM_LIMIT),
        name="moe_dispatch",
    )(*plan[:7], hp, hs, pos, g_ffn)


def _expert_kernel(eot_ref, misc_ref, xs_ref, wg_ref, wu_ref, wd_ref, y_ref, wg_s, wu_s, wd_s):
    i = pl.program_id(0)
    valid = i < misc_ref[0]

    @pl.when(jnp.logical_not(valid))
    def _():
        y_ref[...] = jnp.zeros_like(y_ref)

    @pl.when(valid & ((i == 0) | (eot_ref[i] != eot_ref[jnp.maximum(i - 1, 0)])))
    def _():
        wg_s[...] = wg_ref[...].astype(BF16)
        wu_s[...] = wu_ref[...].astype(BF16)
        wd_s[...] = wd_ref[...].astype(BF16)

    @pl.when(valid)
    def _():
        xs = xs_ref[...]
        x = xs[:, :D_MODEL]
        ex = xs[:, D_MODEL:].astype(F32)
        lane = lax.broadcasted_iota(jnp.int32, ex.shape, 1)
        id0 = jnp.sum(jnp.where(lane == 3 * TOP_K, ex, 0.0), axis=-1, keepdims=True)
        first = id0 == eot_ref[i].astype(F32)
        mine = (first & (lane < 3)) | (jnp.logical_not(first) & (lane >= 3) & (lane < 3 * TOP_K))
        c = jnp.sum(jnp.where(mine, ex, 0.0), axis=-1, keepdims=True)
        hg = _silu(jnp.dot(x, wg_s[...], preferred_element_type=F32)) * jnp.dot(x, wu_s[...],
                                                                               preferred_element_type=F32)
        y_ref[...] = jnp.dot((hg * c).astype(BF16), wd_s[...], preferred_element_type=F32).astype(BF16)


def _expert_call(plan, xs, l, wg, wu, wd):
    misc, exp_of_tile = plan[6:]
    n_row_tiles = xs.shape[0] // EXP_TILE

    def last_valid(i, nv):
        return jnp.maximum(jnp.minimum(i, nv[0] - 1), 0)

    def row_map(i, eot, nv):
        return (last_valid(i, nv), 0)

    def w_map(i, eot, nv):
        return (l * N_EXPERTS + eot[last_valid(i, nv)], 0, 0)

    grid_spec = pltpu.PrefetchScalarGridSpec(
        num_scalar_prefetch=2, grid=(n_row_tiles,),
        in_specs=[pl.BlockSpec((EXP_TILE, XS_COLS), row_map),
                  pl.BlockSpec((None, D_MODEL, EXPERT_FF), w_map),
                  pl.BlockSpec((None, D_MODEL, EXPERT_FF), w_map),
                  pl.BlockSpec((None, EXPERT_FF, D_MODEL), w_map)],
        out_specs=pl.BlockSpec((EXP_TILE, D_MODEL), lambda i, eot, nv: (i, 0)),
        scratch_shapes=[pltpu.VMEM((D_MODEL, EXPERT_FF), BF16), pltpu.VMEM((D_MODEL, EXPERT_FF), BF16),
                        pltpu.VMEM((EXPERT_FF, D_MODEL), BF16)])
    return pl.pallas_call(
        _expert_kernel, grid_spec=grid_spec,
        out_shape=jax.ShapeDtypeStruct((xs.shape[0], D_MODEL), BF16),
        compiler_params=pltpu.CompilerParams(dimension_semantics=("arbitrary",), vmem_limit_bytes=VMEM_LIMIT),
        name="moe_experts",
    )(exp_of_tile, misc, xs, wg, wu, wd)


def _combine_kernel(dst16_ref, n16_ref, lofs16_ref, ntot_ref, hp_ref, hs_ref, pp_ref, ps_ref, pos_ref, y_hbm,
                    g_ple_ref, w_pg_ref, w_pp_ref, g_fin_ref, *rest, final, npt, n_state):
    if final:
        st_in, (op_ref, os_ref, *st_out, ybuf, wpg_s, wpp_s, sem) = rest[:n_state], rest[n_state:]
        depth = n_state // len(st_out)
        for k, out in enumerate(st_out):
            for l in range(depth):
                out[l] = st_in[k * depth + l][...]
    else:
        o_ref, ybuf, wpg_s, wpp_s, sem = rest
    t = pl.program_id(0)
    nt = pl.num_programs(0)
    slot = t % 2

    def fetch(tile, sl):
        _slab_copies(y_hbm, ybuf.at[sl], sem.at[sl], tile, dst16_ref, lofs16_ref, n16_ref)

    @pl.when(t == 0)
    def _():
        ybuf[...] = jnp.zeros_like(ybuf)
        fetch(t, slot)
        wpg_s[...] = w_pg_ref[...].astype(BF16)
        wpp_s[...] = w_pp_ref[...].astype(BF16)

    @pl.when(t + 1 < nt)
    def _():
        fetch(t + 1, 1 - slot)

    _wait_slabs(y_hbm, ybuf.at[slot], sem.at[slot], ntot_ref[t])

    pos = pos_ref[...]
    cols = _iota_f32((TOK_TILE, CBUF_ROWS), 1)
    pick = jnp.where((cols == pos[:, 0:1]) | (cols == pos[:, 1:2]), 1.0, 0.0).astype(BF16)
    h2 = _pick(t, npt, hp_ref, hs_ref) + jnp.dot(pick, ybuf[slot], preferred_element_type=F32)
    xn2 = _rmsnorm(h2, g_ple_ref[...])
    gate = _sigmoid(_dot(xn2, wpg_s[...]))
    p = jnp.where(t < npt, pp_ref[...], ps_ref[...])
    h3 = h2 + gate * _dot(p, wpp_s[...])
    if final:
        h3 = _rmsnorm(h3, g_fin_ref[...])

        @pl.when(t < npt)
        def _():
            op_ref[...] = h3

        @pl.when(t >= npt)
        def _():
            os_ref[...] = h3
    else:
        o_ref[...] = h3


def _state_specs(arrs, axis, nt):
    shape = arrs[0].shape
    unit = SUBLANES if axis == len(shape) - 2 else 1
    per = -(-shape[axis] // (nt * unit)) * unit
    assert shape[axis] % per == 0, (shape, axis, nt)
    n_blocks = shape[axis] // per
    block = shape[:axis] + (per,) + shape[axis + 1:]

    def in_map(t, *_):
        return (0,) * axis + (jnp.minimum(t, n_blocks - 1),) + (0,) * (len(shape) - axis - 1)

    return [pl.BlockSpec(block, in_map)] * len(arrs), pl.BlockSpec((len(arrs),) + block, lambda t, *_: (0,) + in_map(t))


def _combine_call(plan, hp, hs, pp, ps, pos, y, l, P, g_final, *, states=None):
    final = states is not None
    st_in = [a for arrs, _ in states for a in arrs] if final else []
    npt = hp.shape[0] // TOK_TILE
    nst = hs.shape[0] // TOK_TILE
    nt = npt + nst

    def lmap(t, *_):
        return (l, 0, 0)

    in_specs = _pair_specs(npt, D_MODEL) + [
        pl.BlockSpec((None, TOK_TILE, PLE_DIM), lambda t, *_: (l, jnp.minimum(t, npt - 1), 0)),
        pl.BlockSpec((None, TOK_TILE, PLE_DIM), lambda t, *_: (l, jnp.maximum(t - npt, 0), 0)),
        pl.BlockSpec((TOK_TILE, LANES), lambda t, *_: (t, 0)),
        pl.BlockSpec(memory_space=pl.ANY),
        pl.BlockSpec((None,) + P["g_ple"].shape[1:], lmap),
        pl.BlockSpec((None,) + P["w_ple_gate"].shape[1:], lmap),
        pl.BlockSpec((None,) + P["w_ple_proj"].shape[1:], lmap),
        pl.BlockSpec(g_final.shape, lambda t, *_: (0, 0))]
    if final:
        st_specs = [_state_specs(arrs, axis, nt) for arrs, axis in states]
        in_specs += [s for ins, _ in st_specs for s in ins]
        out_specs = (pl.BlockSpec((TOK_TILE, D_MODEL), lambda t, *_: (jnp.minimum(t, npt - 1), 0)),
                     pl.BlockSpec((TOK_TILE, D_MODEL), lambda t, *_: (jnp.maximum(t - npt, 0), 0)),
                     ) + tuple(out for _, out in st_specs)
        out_shape = (jax.ShapeDtypeStruct(hp.shape, F32), jax.ShapeDtypeStruct(hs.shape, F32),
                     ) + tuple(jax.ShapeDtypeStruct((len(arrs),) + arrs[0].shape, F32) for arrs, _ in states)
    else:
        out_specs = pl.BlockSpec((TOK_TILE, D_MODEL), lambda t, *_: (t, 0))
        out_shape = jax.ShapeDtypeStruct((nt * TOK_TILE, D_MODEL), F32)
    grid_spec = pltpu.PrefetchScalarGridSpec(
        num_scalar_prefetch=4, grid=(nt,), in_specs=in_specs, out_specs=out_specs,
        scratch_shapes=[pltpu.VMEM((2, CBUF_ROWS, D_MODEL), BF16), pltpu.VMEM((D_MODEL, D_MODEL), BF16),
                        pltpu.VMEM((PLE_DIM, D_MODEL), BF16), pltpu.SemaphoreType.DMA((2,))])
    return pl.pallas_call(
        functools.partial(_combine_kernel, final=final, npt=npt, n_state=len(st_in)),
        grid_spec=grid_spec, out_shape=out_shape,
        compiler_params=pltpu.CompilerParams(dimension_semantics=("arbitrary",), vmem_limit_bytes=VMEM_LIMIT),
        name="moe_combine_final" if final else "moe_combine",
    )(*plan[:4], hp, hs, pp, ps, pos, y, P["g_ple"], P["w_ple_gate"], P["w_ple_proj"], g_final, *st_in)


def _ffn(hp, hs, pp, ps, l, P, g_final, *, states=None):
    n = hp.shape[0] + hs.shape[0]
    nt = n // TOK_TILE
    bound = TOP_K * n + nt * N_EXPERTS * CHUNK + N_EXPERTS * (EXP_TILE - 1)
    n_sorted = -(-bound // EXP_TILE) * EXP_TILE
    pos, cnt = _route_call(hp, hs, l, P)
    plan = _chunk_plan(cnt[:, 0, :N_EXPERTS].astype(jnp.int32), n_sorted // EXP_TILE)
    xs = _dispatch_call(plan, hp, hs, pos, l, P, n_sorted)
    y = _expert_call(plan, xs, l, P["wg"], P["wu"], P["wd"])
    return _combine_call(plan, hp, hs, pp, ps, pos, y, l, P, g_final, states=states)


def _router_weights(g_ffn, w_grp_router, b_grp_router, w_exp_router, b_exp_router):
    depth = w_grp_router.shape[0]
    n_pad = LANES - N_EXPERTS - N_GROUPS
    w_er = jnp.transpose(w_exp_router, (0, 2, 1, 3)).reshape(depth, D_MODEL, N_EXPERTS)
    w_rt = jnp.concatenate([w_er, w_grp_router, jnp.zeros((depth, D_MODEL, n_pad), F32)], axis=2) * g_ffn[:, :, None]
    b_rt = jnp.concatenate([b_exp_router.reshape(depth, N_EXPERTS), b_grp_router, jnp.zeros((depth, n_pad), F32)], axis=1)
    w_hi = w_rt.astype(BF16)
    w_lo = (w_rt - w_hi.astype(F32)).astype(BF16)
    return jnp.concatenate([w_hi, w_lo], axis=2), b_rt.reshape(depth, 1, LANES)


def kernel(x_prompt, x_sample, state_gla, state_conv, p_prompt, p_sample, g_mix, w_in, w_forget_up, b_forget,
           g_gla_out, w_conv, b_conv, g_conv_ln, b_conv_ln, w_out, g_ffn, w_grp_router, b_grp_router,
           w_exp_router, b_exp_router, w_exp_gate, w_exp_up, w_exp_down, g_ple, w_ple_gate, w_ple_proj, g_final):
    depth = w_in.shape[0]
    nbp, seq_p, _ = x_prompt.shape
    nbs, seq_s, _ = x_sample.shape
    n_p = nbp * seq_p
    n_s = nbs * seq_s

    def rows(v):
        return v.reshape(depth, 1, -1)

    P = {
        "g_mix": rows(g_mix), "w_in_t": jnp.swapaxes(w_in, 1, 2),
        "w_forget_up": w_forget_up, "b_forget": rows(b_forget), "g_gla_out": rows(g_gla_out),
        "w_conv": w_conv, "b_conv": rows(b_conv), "g_conv_ln": rows(g_conv_ln), "b_conv_ln": rows(b_conv_ln),
        "w_out": w_out, "g_ffn": rows(g_ffn), "g_ple": rows(g_ple),
        "w_ple_gate": w_ple_gate, "w_ple_proj": w_ple_proj,
        "wg": w_exp_gate.reshape(depth * N_EXPERTS, D_MODEL, EXPERT_FF),
        "wu": w_exp_up.reshape(depth * N_EXPERTS, D_MODEL, EXPERT_FF),
        "wd": w_exp_down.reshape(depth * N_EXPERTS, EXPERT_FF, D_MODEL),
    }
    P["w_rt"], P["b_rt"] = _router_weights(g_ffn, w_grp_router, b_grp_router, w_exp_router, b_exp_router)
    g_fin = g_final.reshape(1, -1)
    xp = x_prompt.reshape(n_p, D_MODEL)
    xs = x_sample.reshape(n_s, D_MODEL)
    pp = p_prompt.reshape(depth, n_p, PLE_DIM)
    ps = p_sample.reshape(depth, n_s, PLE_DIM)
    s_in = state_gla.reshape(depth, nbs, QK_COLS, GLA_DV)
    c_in_t = jnp.swapaxes(state_conv, 1, 2)

    h = None
    sg_p, sg_s, sc_p, sc_s = [], [], [], []
    for l in range(depth):
        src_p, src_s, off_s = (xp, xs, 0) if l == 0 else (h, h, n_p)
        hp, sgp, scp = _guarded(functools.partial(_mixer_prompt, src_p, 0, nbp, seq_p, l, P))
        hs, sgs, scs = _guarded(functools.partial(_mixer_sample, src_s, off_s, nbs, seq_s, l, s_in, c_in_t, P))
        sg_p.append(sgp)
        sg_s.append(sgs)
        sc_p.append(scp)
        sc_s.append(scs)
        h = _ffn(hp, hs, pp, ps, l, P, g_fin, states=((sg_p, 0), (sg_s, 0), (sc_s, 1)) if l == depth - 1 else None)

    y_prompt, y_sample, sg_p_all, sg_s_all, sc_s_all = h
    return (y_prompt.reshape(nbp, seq_p, D_MODEL), y_sample.reshape(nbs, seq_s, D_MODEL),
            sg_p_all.reshape(depth, nbp, GLA_HEADS, GLA_DK, GLA_DV), sg_s_all.reshape(depth, nbs, GLA_HEADS, GLA_DK, GLA_DV),
            jnp.stack(sc_p), jnp.swapaxes(sc_s_all, 1, 2))
```

```python
import functools

import jax
import jax.numpy as jnp
from jax import lax
from jax.experimental import pallas as pl
from jax.experimental.pallas import tpu as pltpu

D_MODEL = 1024
GLA_HEADS = 4
GLA_DK = 64
GLA_DV = 128
QK_COLS = GLA_HEADS * GLA_DK
V_COLS = GLA_HEADS * GLA_DV
CONV_WIDTH = 512
CONV_K = 31
GLA_LOWRANK = 16
GLA_TAU = 16.0
GLA_CHUNK = 64
PLE_DIM = 256
N_GROUPS = 4
EXPERTS_PER_GROUP = 8
N_EXPERTS = N_GROUPS * EXPERTS_PER_GROUP
EXPERT_FF = 256
TOP_K = 2
EPS = 1e-6
N_MAIN = 2 * QK_COLS + 2 * V_COLS

LANES = 128
SUBLANES = 8
CONV_PAD = 32
CONV_OFF = CONV_PAD - (CONV_K - 1)
VMEM_LIMIT = 56 * 1024 * 1024
TOK_TILE = 512
CHUNK = 16
EXP_TILE = 512
CBUF_ROWS = -(-(TOP_K * TOK_TILE + N_EXPERTS * CHUNK) // LANES) * LANES
SORT_ROWS = 512
XS_COLS = D_MODEL + LANES
ROUTE_ROWS = 48
DECAY_SAFE = 80.0

F32 = jnp.float32
BF16 = jnp.bfloat16
HI = lax.Precision.HIGHEST


def _sigmoid(x):
    return 1.0 / (1.0 + jnp.exp(-x))


def _silu(x):
    return x * _sigmoid(x)


def _log_sigmoid(x):
    return jnp.minimum(x, 0.0) - jnp.log(1.0 + jnp.exp(-jnp.abs(x)))


def _rmsnorm(x, g):
    return x * lax.rsqrt(jnp.mean(x * x, axis=-1, keepdims=True) + EPS) * g


def _dot(a, b):
    return jnp.dot(a.astype(BF16), b.astype(BF16), preferred_element_type=F32)


def _dot_t(a, b):
    return lax.dot_general(a.astype(BF16), b.astype(BF16), (((1,), (1,)), ((), ())),
                           preferred_element_type=F32)


def _dot_hi(a, b):
    return jnp.dot(a, b, preferred_element_type=F32, precision=HI)


def _iota_f32(shape, dim):
    return lax.broadcasted_iota(jnp.int32, shape, dim).astype(F32)


def _const_spec(shape):
    nd = len(shape)
    return pl.BlockSpec(shape, lambda *_: (0,) * nd)


def _layer_spec(arr, l):
    nd = arr.ndim - 1
    return pl.BlockSpec((None,) + arr.shape[1:], lambda *_: (l,) + (0,) * nd, pipeline_mode=pl.Buffered(1))


def _cast_mixer_weights(w_int_ref, w_fu_ref, w_out_ref, wmain_s, wlr_s, wcv_s, wfu_s, wout_s):
    blk = 4 * LANES
    for r in range(0, N_MAIN, blk):
        wmain_s[:, r:r + blk] = w_int_ref[r:r + blk, :].T.astype(BF16)
    lane = lax.broadcasted_iota(jnp.int32, (D_MODEL, LANES), 1)
    wlr_s[...] = jnp.where(lane < GLA_LOWRANK, w_int_ref[N_MAIN:N_MAIN + LANES, :].T, 0.0).astype(BF16)
    cv0 = N_MAIN + GLA_LOWRANK
    for r in range(0, 2 * CONV_WIDTH, blk):
        wcv_s[:, r:r + blk] = w_int_ref[cv0 + r:cv0 + r + blk, :].T.astype(BF16)
    wfu_s[...] = jnp.zeros_like(wfu_s)
    wfu_s[0:GLA_LOWRANK, :] = w_fu_ref[...].astype(BF16)
    wout_s[...] = w_out_ref[...].astype(BF16)


def _project(x, g_mix, wmain_s, wlr_s, wfu_s, b_f, wcv_s):
    xn = _rmsnorm(x, g_mix).astype(BF16)
    cv = jnp.dot(xn, wcv_s[...], preferred_element_type=F32)
    u = cv[:, :CONV_WIDTH] * _sigmoid(cv[:, CONV_WIDTH:])
    lr = jnp.dot(xn, wlr_s[...], preferred_element_type=F32)
    zf = _dot(lr, wfu_s[...]) + b_f
    la = _log_sigmoid(zf) * (1.0 / GLA_TAU)
    zqk = jnp.dot(xn, wmain_s[:, :2 * QK_COLS], preferred_element_type=F32)
    q = zqk[:, :QK_COLS] * (GLA_DK ** -0.5)
    k = zqk[:, QK_COLS:]
    zvo = jnp.dot(xn, wmain_s[:, 2 * QK_COLS:], preferred_element_type=F32)
    v = zvo[:, :V_COLS]
    og = zvo[:, V_COLS:]
    return q, k, v, og, la, u


def _stack_heads(qd):
    lane = lax.broadcasted_iota(jnp.int32, qd.shape, 1)
    return jnp.concatenate(
        [jnp.where((lane >= h * GLA_DK) & (lane < (h + 1) * GLA_DK), qd, 0.0) for h in range(GLA_HEADS)],
        axis=0)


def _gated_head_norm(o, og, g_gla):
    outs = []
    for h in range(GLA_HEADS):
        sl = slice(h * GLA_DV, (h + 1) * GLA_DV)
        outs.append(_rmsnorm(o[:, sl], g_gla) * _silu(og[:, sl]))
    return jnp.concatenate(outs, axis=1)


def _causal_conv(win, w_conv, b_conv, n):
    acc = jnp.broadcast_to(b_conv, (n, CONV_WIDTH))
    for s in range(SUBLANES):
        taps = [j for j in range(CONV_K) if (CONV_OFF + j) % SUBLANES == s]
        if not taps:
            continue
        rows = n if s == 0 else n + SUBLANES
        part = None
        for j in taps:
            a = (CONV_OFF + j) - s
            term = w_conv[j:j + 1, :] * win[a:a + rows, :]
            part = term if part is None else part + term
        acc = acc + part[s:s + n, :]
    return acc


def _conv_ln_act(acc, g_ln, b_ln):
    mu = jnp.mean(acc, axis=-1, keepdims=True)
    xc = acc - mu
    y = xc * lax.rsqrt(jnp.mean(xc * xc, axis=-1, keepdims=True) + EPS) * g_ln + b_ln
    return _silu(y)


def _head_diag(upd):
    return jnp.concatenate([upd[h * GLA_DK:(h + 1) * GLA_DK, h * GLA_DV:(h + 1) * GLA_DV]
                            for h in range(GLA_HEADS)], axis=0)


def _mixer_weight_args(l, P):
    names = ("g_mix", "w_in_t", "w_forget_up", "b_forget", "g_gla_out", "w_conv", "b_conv", "g_conv_ln",
             "b_conv_ln", "w_out")
    arrs = [P[n] for n in names]
    return arrs, [_layer_spec(a, l) for a in arrs]


_MIXER_WEIGHT_SCRATCH = [pltpu.VMEM((D_MODEL, N_MAIN), BF16), pltpu.VMEM((D_MODEL, LANES), BF16),
                         pltpu.VMEM((D_MODEL, 2 * CONV_WIDTH), BF16), pltpu.VMEM((LANES, QK_COLS), BF16),
                         pltpu.VMEM((D_MODEL, D_MODEL), BF16)]


def _mixer_prompt_kernel(x_ref, g_mix_ref, w_int_ref, w_fu_ref, b_f_ref, g_gla_ref, w_conv_ref,
                         b_conv_ref, g_ln_ref, b_ln_ref, w_out_ref,
                         h_ref, sg_ref, sc_ref, risk_ref,
                         wmain_s, wlr_s, wcv_s, wfu_s, wout_s,
                         s_ref, ubuf_ref, qs_ref, kd_ref, klt_ref, dec_ref, mid_ref, v_ref, og_ref, mix_ref,
                         *stable_refs, tt, stable):
    t = pl.program_id(1)
    nt = pl.num_programs(1)
    C = GLA_CHUNK
    n_chunks = tt // C

    @pl.when((pl.program_id(0) == 0) & (t == 0))
    def _():
        _cast_mixer_weights(w_int_ref, w_fu_ref, w_out_ref, wmain_s, wlr_s, wcv_s, wfu_s, wout_s)

    @pl.when(t == 0)
    def _():
        s_ref[...] = jnp.zeros_like(s_ref)
        ubuf_ref[0:CONV_PAD, :] = jnp.zeros((CONV_PAD, CONV_WIDTH), F32)

    x = x_ref[...]
    q, k, v, og, la, u = _project(x, g_mix_ref[...], wmain_s, wlr_s, wfu_s, b_f_ref[...], wcv_s)
    ubuf_ref[CONV_PAD:CONV_PAD + tt, :] = u
    v_ref[...] = v.astype(BF16)
    og_ref[...] = _silu(og)

    w_conv = w_conv_ref[...]
    b_conv = b_conv_ref[...]
    g_ln = g_ln_ref[...]
    b_ln = b_ln_ref[...]
    row = lax.broadcasted_iota(jnp.int32, (C, C), 0)
    col = lax.broadcasted_iota(jnp.int32, (C, C), 1)
    tri = (col <= row).astype(F32)
    risk = jnp.zeros((1, QK_COLS), F32)
    for c in range(n_chunks):
        rows = slice(c * C, (c + 1) * C)
        win = ubuf_ref[c * C:c * C + C + CONV_PAD, :]
        cact = _conv_ln_act(_causal_conv(win, w_conv, b_conv, C), g_ln, b_ln)
        mix_ref[rows, V_COLS:] = cact.astype(BF16)
        bits = pltpu.bitcast(cact[C - SUBLANES:C, 0:QK_COLS], jnp.uint32)
        zero = pltpu.bitcast(lax.shift_right_logical(lax.shift_right_logical(bits, jnp.uint32(16)), jnp.uint32(16)),
                             F32)[0:1, :]
        qc = q[rows, :] + zero
        b = _dot_hi(tri, la[rows, :])
        b_last = b[C - 1:C, :]
        if stable:
            q_st, k_st, b_st = stable_refs
            q_st[rows, :] = qc
            k_st[rows, :] = k[rows, :]
            b_st[rows, :] = b
            b_mid = jnp.zeros_like(b_last)
        else:
            b_mid = b[C // 2 - 1:C // 2, :]
            risk = jnp.maximum(risk, jnp.maximum(-b_mid, b_mid - b_last))
        qs_ref[c] = _stack_heads(qc * jnp.exp(b - b_mid)).astype(BF16)
        kd_ref[rows, :] = (k[rows, :] * jnp.exp(jnp.minimum(b_mid - b, DECAY_SAFE))).astype(BF16)
        kl = k[rows, :] * jnp.exp(b_last - b)
        klt = jnp.concatenate([kl, jnp.broadcast_to(jnp.exp(b_last), (C // 2, QK_COLS)),
                               jnp.broadcast_to(jnp.exp(b_mid), (C // 2, QK_COLS))], axis=0).T
        klt_ref[c] = klt.astype(BF16)
        dec_ref[c] = jnp.broadcast_to(klt[:, C:C + 1], (QK_COLS, GLA_DV))
        mid_ref[c] = jnp.broadcast_to(klt[:, 3 * C // 2:3 * C // 2 + 1], (QK_COLS, GLA_DV))

    r4 = lax.broadcasted_iota(jnp.int32, (GLA_HEADS * C, C), 0)
    c4 = lax.broadcasted_iota(jnp.int32, (GLA_HEADS * C, C), 1)
    causal4 = c4 <= (r4 % C)
    g_gla = g_gla_ref[...]
    s = s_ref[...]
    for c in range(n_chunks):
        rows = slice(c * C, (c + 1) * C)
        qs = qs_ref[c]
        vc = v_ref[rows, :]
        if stable:
            scores = _direct_scores(*stable_refs, c * C, C).astype(BF16)
        else:
            scores = jnp.where(causal4, _dot_t(qs, kd_ref[rows, :]), 0.0).astype(BF16)
        o_inter = jnp.dot(qs, (mid_ref[c] * s).astype(BF16), preferred_element_type=F32)
        upd = jnp.dot(klt_ref[c][:, :C], vc, preferred_element_type=F32)
        s = dec_ref[c] * s + _head_diag(upd)
        o_parts = []
        for h in range(GLA_HEADS):
            vh = vc[:, h * GLA_DV:(h + 1) * GLA_DV]
            o_parts.append(jnp.dot(scores[h * C:(h + 1) * C, :], vh, preferred_element_type=F32)
                           + o_inter[h * C:(h + 1) * C, :])
        o = jnp.concatenate(o_parts, axis=1)
        gated = []
        for h in range(GLA_HEADS):
            sl = slice(h * GLA_DV, (h + 1) * GLA_DV)
            gated.append(_rmsnorm(o[:, sl], g_gla) * og_ref[rows, sl])
        mix_ref[rows, 0:V_COLS] = jnp.concatenate(gated, axis=1).astype(BF16)
    s_ref[...] = s

    h_ref[...] = x + jnp.dot(mix_ref[...], wout_s[...], preferred_element_type=F32)
    tail = ubuf_ref[tt:tt + CONV_PAD, :]
    ubuf_ref[0:CONV_PAD, :] = tail

    risk_ref[0] = jnp.broadcast_to(jnp.max(risk, axis=-1, keepdims=True), (SUBLANES, LANES))

    @pl.when(t == nt - 1)
    def _():
        sg_ref[0] = s
        sc_ref[0] = tail[CONV_OFF:, :]


def _direct_scores(q_st, k_st, b_st, r0, C):
    kc = k_st[r0:r0 + C, :]
    bc = b_st[r0:r0 + C, :]
    srow = lax.broadcasted_iota(jnp.int32, (C, QK_COLS), 0)
    lane = lax.broadcasted_iota(jnp.int32, (C, LANES), 1)
    head_of = lax.broadcasted_iota(jnp.int32, (QK_COLS, LANES), 0) // GLA_DK
    head_sum = (head_of == lax.broadcasted_iota(jnp.int32, (QK_COLS, LANES), 1)).astype(F32)

    def one_query(t, acc):
        d = b_st[pl.ds(r0 + t, 1), :] - bc
        w = jnp.where(srow <= t, jnp.exp(jnp.minimum(d, 0.0)), 0.0) * kc * q_st[pl.ds(r0 + t, 1), :]
        per_head = _dot_hi(w, head_sum)
        return tuple(jnp.where(lane == t, per_head[:, h:h + 1], acc[h]) for h in range(GLA_HEADS))

    acc = lax.fori_loop(0, C, one_query, tuple(jnp.zeros((C, LANES), F32) for _ in range(GLA_HEADS)))
    return jnp.concatenate([a.T[0:C, :] for a in acc], axis=0)


def _mixer_prompt(x, row_off, nb, seq, l, P, *, tt=512, stable=False):
    nt = seq // tt
    n_chunks = tt // GLA_CHUNK
    blk_off = row_off // tt
    weights, w_specs = _mixer_weight_args(l, P)
    in_specs = [pl.BlockSpec((tt, D_MODEL), lambda b, t: (blk_off + b * nt + t, 0))] + w_specs
    out_shape = (jax.ShapeDtypeStruct((nb * seq, D_MODEL), F32),
                 jax.ShapeDtypeStruct((nb, QK_COLS, GLA_DV), F32),
                 jax.ShapeDtypeStruct((nb, CONV_K - 1, CONV_WIDTH), F32),
                 jax.ShapeDtypeStruct((nb * nt, SUBLANES, LANES), F32))
    out_specs = (pl.BlockSpec((tt, D_MODEL), lambda b, t: (b * nt + t, 0)),
                 pl.BlockSpec((1, QK_COLS, GLA_DV), lambda b, t: (b, 0, 0)),
                 pl.BlockSpec((1, CONV_K - 1, CONV_WIDTH), lambda b, t: (b, 0, 0)),
                 pl.BlockSpec((1, SUBLANES, LANES), lambda b, t: (b * nt + t, 0, 0)))
    scratch = _MIXER_WEIGHT_SCRATCH + [
        pltpu.VMEM((QK_COLS, GLA_DV), F32),
        pltpu.VMEM((CONV_PAD + tt, CONV_WIDTH), F32),
        pltpu.VMEM((n_chunks, GLA_HEADS * GLA_CHUNK, QK_COLS), BF16), pltpu.VMEM((tt, QK_COLS), BF16),
        pltpu.VMEM((n_chunks, QK_COLS, 2 * GLA_CHUNK), BF16), pltpu.VMEM((n_chunks, QK_COLS, GLA_DV), F32),
        pltpu.VMEM((n_chunks, QK_COLS, GLA_DV), F32),
        pltpu.VMEM((tt, V_COLS), BF16), pltpu.VMEM((tt, V_COLS), F32),
        pltpu.VMEM((tt, D_MODEL), BF16)]
    if stable:
        scratch += [pltpu.VMEM((tt, QK_COLS), F32)] * 3
    return pl.pallas_call(
        functools.partial(_mixer_prompt_kernel, tt=tt, stable=stable),
        grid=(nb, nt), in_specs=in_specs, out_specs=out_specs, out_shape=out_shape,
        scratch_shapes=scratch,
        compiler_params=pltpu.CompilerParams(dimension_semantics=("arbitrary", "arbitrary"),
                                             vmem_limit_bytes=VMEM_LIMIT),
        name="mixer_prompt_stable" if stable else "mixer_prompt",
    )(x, *weights)


def _guarded(*mixers):
    fast = [mixer(stable=False) for mixer in mixers]
    risk = functools.reduce(jnp.maximum, [jnp.max(outs[-1]) for outs in fast])
    return lax.cond(risk > DECAY_SAFE, lambda: tuple(tuple(mixer(stable=True)[:-1]) for mixer in mixers),
                    lambda: tuple(tuple(outs[:-1]) for outs in fast))


def _mixer_sample_kernel(x_ref, s_in_ref, c_in_ref, g_mix_ref, w_int_ref, w_fu_ref, b_f_ref,
                         g_gla_ref, w_conv_ref, b_conv_ref, g_ln_ref, b_ln_ref, w_out_ref,
                         h_ref, sg_ref, sc_ref, risk_ref,
                         wmain_s, wlr_s, wcv_s, wfu_s, wout_s,
                         u4_ref, oi_ref, cacc4_ref, *, sb, seq, stable):
    R = sb * seq
    n_slabs = CONV_WIDTH // LANES

    @pl.when(pl.program_id(0) == 0)
    def _():
        _cast_mixer_weights(w_int_ref, w_fu_ref, w_out_ref, wmain_s, wlr_s, wcv_s, wfu_s, wout_s)

    x = x_ref[...]
    q, k, v, og, la, u = _project(x, g_mix_ref[...], wmain_s, wlr_s, wfu_s, b_f_ref[...], wcv_s)

    for kk in range(n_slabs):
        u4_ref[kk] = u[:, kk * LANES:(kk + 1) * LANES]
    full = [c_in_ref[j] for j in range(CONV_K - 1)]
    for t in range(seq):
        full.append(jnp.concatenate([u4_ref.at[kk][pl.ds(t, sb, stride=seq), :] for kk in range(n_slabs)], axis=1))
    w_conv = w_conv_ref[...]
    for t in range(seq):
        acc = jnp.broadcast_to(b_conv_ref[...], (sb, CONV_WIDTH))
        for j in range(CONV_K):
            acc = acc + w_conv[j:j + 1, :] * full[t + j]
        for kk in range(n_slabs):
            cacc4_ref.at[kk][pl.ds(t, sb, stride=seq), :] = acc[:, kk * LANES:(kk + 1) * LANES]
    for j in range(CONV_K - 1):
        sc_ref[j] = full[seq + j]

    row = lax.broadcasted_iota(jnp.int32, (R, R), 0)
    col = lax.broadcasted_iota(jnp.int32, (R, R), 1)
    same = (row // seq) == (col // seq)
    b = _dot_hi((same & (col <= row)).astype(F32), la)
    b_tot = _dot_hi(same.astype(F32), la)
    qd = q * jnp.exp(b)
    kl = k * jnp.exp(b_tot - b)
    qs = _stack_heads(qd)
    risk_ref[0] = jnp.broadcast_to(jnp.max(jnp.max(-b_tot, axis=-1, keepdims=True), axis=0, keepdims=True),
                                   (SUBLANES, LANES))
    if stable:
        head_of = lax.broadcasted_iota(jnp.int32, (QK_COLS, LANES), 0) // GLA_DK
        head_sum = (head_of == lax.broadcasted_iota(jnp.int32, (QK_COLS, LANES), 1)).astype(F32)
        t_in_seq = lax.broadcasted_iota(jnp.int32, (R, QK_COLS), 0) % seq
        parts = [jnp.zeros((R, R), F32) for _ in range(GLA_HEADS)]
        for d in range(seq):
            k_d = jnp.concatenate([jnp.zeros((d, QK_COLS), F32), k[:R - d, :]], axis=0) if d else k
            b_d = jnp.concatenate([jnp.zeros((d, QK_COLS), F32), b[:R - d, :]], axis=0) if d else b
            w = jnp.where(t_in_seq >= d, jnp.exp(jnp.minimum(b - b_d, 0.0)), 0.0) * k_d * q
            per_head = _dot_hi(w, head_sum)
            for h in range(GLA_HEADS):
                parts[h] = jnp.where(col == row - d, per_head[:, h:h + 1], parts[h])
        scores = jnp.concatenate(parts, axis=0)
    else:
        kd = k * jnp.exp(jnp.minimum(-b, DECAY_SAFE))
        r4 = lax.broadcasted_iota(jnp.int32, (GLA_HEADS * R, R), 0) % R
        c4 = lax.broadcasted_iota(jnp.int32, (GLA_HEADS * R, R), 1)
        mask4 = ((r4 // seq) == (c4 // seq)) & (c4 <= r4)
        scores = jnp.where(mask4, _dot_t(qs, kd), 0.0)

    klt = kl.T
    dect = jnp.exp(b_tot).T
    lane_h = lax.broadcasted_iota(jnp.int32, (GLA_DK, R), 1)
    lane_r = lax.broadcasted_iota(jnp.int32, (QK_COLS, R), 1)
    upd = []
    for h in range(GLA_HEADS):
        klt_h = klt[h * GLA_DK:(h + 1) * GLA_DK, :]
        lhs = jnp.concatenate([jnp.where((lane_h >= i * seq) & (lane_h < (i + 1) * seq), klt_h, 0.0)
                               for i in range(sb)], axis=0)
        upd.append(_dot(lhs, v[:, h * GLA_DV:(h + 1) * GLA_DV]))
    for i in range(sb):
        s_old = s_in_ref[i]
        qsel = jnp.concatenate([qs[h * R + i * seq:h * R + (i + 1) * seq, :] for h in range(GLA_HEADS)], axis=0)
        oi = _dot(qsel, s_old)
        for h in range(GLA_HEADS):
            oi_ref[h, i * seq:(i + 1) * seq, :] = oi[h * seq:(h + 1) * seq, :]
        smask = (lane_r >= i * seq) & (lane_r < (i + 1) * seq)
        dec = jnp.sum(jnp.where(smask, dect, 0.0), axis=1, keepdims=True) * (1.0 / seq)
        u_new = jnp.concatenate([upd[h][i * GLA_DK:(i + 1) * GLA_DK, :] for h in range(GLA_HEADS)], axis=0)
        sg_ref[i] = dec * s_old + u_new

    o_parts = []
    for h in range(GLA_HEADS):
        vh = v[:, h * GLA_DV:(h + 1) * GLA_DV]
        o_parts.append(_dot(scores[h * R:(h + 1) * R, :], vh) + oi_ref[h])
    o = jnp.concatenate(o_parts, axis=1)
    cacc = jnp.concatenate([cacc4_ref[kk] for kk in range(n_slabs)], axis=1)
    mix = jnp.concatenate([_gated_head_norm(o, og, g_gla_ref[...]),
                           _conv_ln_act(cacc, g_ln_ref[...], b_ln_ref[...])], axis=1)
    h_ref[...] = x + jnp.dot(mix.astype(BF16), wout_s[...], preferred_element_type=F32)


def _mixer_sample(x, row_off, nb, seq, l, s_in, c_in_t, P, *, sb=16, stable=False):
    R = sb * seq
    blk_off = row_off // R
    n_slabs = CONV_WIDTH // LANES
    weights, w_specs = _mixer_weight_args(l, P)
    in_specs = [pl.BlockSpec((R, D_MODEL), lambda i: (blk_off + i, 0)),
                pl.BlockSpec((None, sb, QK_COLS, GLA_DV), lambda i: (l, i, 0, 0)),
                pl.BlockSpec((None, CONV_K - 1, sb, CONV_WIDTH), lambda i: (l, 0, i, 0))] + w_specs
    out_shape = (jax.ShapeDtypeStruct((nb * seq, D_MODEL), F32),
                 jax.ShapeDtypeStruct((nb, QK_COLS, GLA_DV), F32),
                 jax.ShapeDtypeStruct((CONV_K - 1, nb, CONV_WIDTH), F32),
                 jax.ShapeDtypeStruct((nb // sb, SUBLANES, LANES), F32))
    out_specs = (pl.BlockSpec((R, D_MODEL), lambda i: (i, 0)),
                 pl.BlockSpec((sb, QK_COLS, GLA_DV), lambda i: (i, 0, 0)),
                 pl.BlockSpec((CONV_K - 1, sb, CONV_WIDTH), lambda i: (0, i, 0)),
                 pl.BlockSpec((1, SUBLANES, LANES), lambda i: (i, 0, 0)))
    scratch = _MIXER_WEIGHT_SCRATCH + [
        pltpu.VMEM((n_slabs, R, LANES), F32),
        pltpu.VMEM((GLA_HEADS, R, GLA_DV), F32),
        pltpu.VMEM((n_slabs, R, LANES), F32)]
    return pl.pallas_call(
        functools.partial(_mixer_sample_kernel, sb=sb, seq=seq, stable=stable),
        grid=(nb // sb,), in_specs=in_specs, out_specs=out_specs, out_shape=out_shape,
        scratch_shapes=scratch,
        compiler_params=pltpu.CompilerParams(dimension_semantics=("arbitrary",),
                                             vmem_limit_bytes=VMEM_LIMIT),
        name="mixer_sample_stable" if stable else "mixer_sample",
    )(x, s_in, c_in_t, *weights)


def _pair_specs(n_first_tiles, width):
    return [pl.BlockSpec((TOK_TILE, width), lambda t, *_: (jnp.minimum(t, n_first_tiles - 1), 0)),
            pl.BlockSpec((TOK_TILE, width), lambda t, *_: (jnp.maximum(t - n_first_tiles, 0), 0))]


def _pick(t, n_first_tiles, a_ref, b_ref):
    return jnp.where(t < n_first_tiles, a_ref[...], b_ref[...])


def _route(logits):
    row = lax.broadcasted_iota(jnp.int32, logits.shape, 0)
    row_f = row.astype(F32)
    neg = jnp.float32(-jnp.inf)
    big = jnp.float32(1e9)
    is_grp = (row >= N_EXPERTS) & (row < N_EXPERTS + N_GROUPS)
    gl = jnp.where(is_grp, logits, neg)
    gmax = jnp.max(gl, axis=0, keepdims=True)
    gidx = jnp.min(jnp.where(is_grp & (gl == gmax), row_f - N_EXPERTS, big), axis=0, keepdims=True)
    gsum = jnp.sum(jnp.where(is_grp, jnp.exp(gl - gmax), 0.0), axis=0, keepdims=True)
    g_w = 1.0 / gsum
    grp_of_row = jnp.floor(row_f * (1.0 / EXPERTS_PER_GROUP))
    in_grp = (row < N_EXPERTS) & (grp_of_row == gidx)
    ml = jnp.where(in_grp, logits, neg)
    v1 = jnp.max(ml, axis=0, keepdims=True)
    i1 = jnp.min(jnp.where(in_grp & (ml == v1), row_f, big), axis=0, keepdims=True)
    ml2 = jnp.where(row_f == i1, neg, ml)
    v2 = jnp.max(ml2, axis=0, keepdims=True)
    i2 = jnp.min(jnp.where(in_grp & (ml2 == v2), row_f, big), axis=0, keepdims=True)
    e2 = jnp.exp(v2 - v1)
    w1 = g_w / (1.0 + e2)
    w2 = g_w * e2 / (1.0 + e2)
    return i1, i2, w1, w2


def _route_kernel(hp_ref, hs_ref, w_rt_ref, b_rt_ref, pos_ref, cnt_ref, earlier_s, *, npt):
    T = TOK_TILE
    R = ROUTE_ROWS

    @pl.when(pl.program_id(0) == 0)
    def _():
        earlier_s[...] = (_iota_f32((T, T), 0) < _iota_f32((T, T), 1)).astype(BF16)

    x = _pick(pl.program_id(0), npt, hp_ref, hs_ref)
    scale = lax.rsqrt(jnp.mean(x * x, axis=-1, keepdims=True) + EPS)
    x_hi = x.astype(BF16)
    x_lo = (x - x_hi.astype(F32)).astype(BF16)
    w_split = w_rt_ref[...]
    both = jnp.dot(x_hi, w_split, preferred_element_type=F32)
    logits = ((both[:, :LANES] + both[:, LANES:] + jnp.dot(x_lo, w_split[:, :LANES], preferred_element_type=F32))
              * scale + b_rt_ref[...])
    i1, i2, w1, w2 = _route(logits.T[:R])
    row = _iota_f32((R, T), 0)
    a0 = (row == i1).astype(F32)
    a1 = (row == i2).astype(F32)
    a = a0 + a1
    cnt = jnp.sum(a, axis=1, keepdims=True)
    rank = jnp.dot(a.astype(BF16), earlier_s[...], preferred_element_type=F32)
    cnt_pad = jnp.maximum(jnp.ceil(cnt * (1.0 / CHUNK)), 1.0) * CHUNK
    no_rows = jnp.zeros((LANES - R, LANES), F32)
    before = (_iota_f32((LANES, LANES), 1) < _iota_f32((LANES, LANES), 0)).astype(F32)
    first = _dot_hi(before, jnp.concatenate([jnp.broadcast_to(cnt_pad, (R, LANES)), no_rows], axis=0))[:R, 0:1]
    base = first + rank
    pos0 = jnp.sum(a0 * base, axis=0, keepdims=True)
    pos1 = jnp.sum(a1 * base, axis=0, keepdims=True)
    r8 = _iota_f32((SUBLANES, T), 0)
    res = jnp.where(r8 == 0.0, pos0, jnp.where(r8 == 1.0, pos1, jnp.where(
        r8 == 2.0, w1, jnp.where(r8 == 3.0, w2, jnp.where(r8 == 4.0, i1, 0.0)))))
    pos_ref[...] = jnp.concatenate([res, jnp.zeros((LANES - SUBLANES, T), F32)], axis=0).T
    cnt_ref[0] = jnp.concatenate([jnp.broadcast_to(cnt, (R, LANES)), no_rows], axis=0).T[:SUBLANES]


def _route_call(hp, hs, l, P):
    npt = hp.shape[0] // TOK_TILE
    nt = npt + hs.shape[0] // TOK_TILE
    return pl.pallas_call(
        functools.partial(_route_kernel, npt=npt), grid=(nt,),
        in_specs=_pair_specs(npt, D_MODEL) + [_layer_spec(P[n], l) for n in ("w_rt", "b_rt")],
        scratch_shapes=[pltpu.VMEM((TOK_TILE, TOK_TILE), BF16)],
        out_specs=(pl.BlockSpec((TOK_TILE, LANES), lambda t: (t, 0)),
                   pl.BlockSpec((1, SUBLANES, LANES), lambda t: (t, 0, 0))),
        out_shape=(jax.ShapeDtypeStruct((nt * TOK_TILE, LANES), F32),
                   jax.ShapeDtypeStruct((nt, SUBLANES, LANES), F32)),
        compiler_params=pltpu.CompilerParams(dimension_semantics=("arbitrary",), vmem_limit_bytes=VMEM_LIMIT),
        name="moe_route",
    )(hp, hs, P["w_rt"], P["b_rt"])


def _chunk_plan(cnt, n_row_tiles):
    n16 = jnp.maximum((cnt + (CHUNK - 1)) // CHUNK, 1)
    lofs16 = jnp.cumsum(n16, axis=1) - n16
    tile_pref16 = jnp.cumsum(n16, axis=0) - n16
    tot16 = jnp.sum(n16, axis=0)
    per_tile = EXP_TILE // CHUNK
    seg16 = ((tot16 + per_tile - 1) // per_tile) * per_tile
    seg_end16 = jnp.cumsum(seg16)
    dst16 = (seg_end16 - seg16)[None, :] + tile_pref16
    n_tot = jnp.sum(n16, axis=1)
    gap16 = seg_end16 - seg16 + tot16
    gapn16 = seg16 - tot16
    tile_start16 = jnp.arange(n_row_tiles, dtype=jnp.int32) * per_tile
    n_valid = seg_end16[-1] // per_tile
    misc = n_valid.reshape(1)
    exp_of_tile = jnp.minimum(jnp.sum(seg_end16[None, :] <= tile_start16[:, None], axis=1), N_EXPERTS - 1)
    i32 = lambda a: a.astype(jnp.int32).reshape(-1)
    return (i32(dst16), i32(n16), i32(lofs16), i32(n_tot), i32(gap16), i32(gapn16), i32(misc), i32(exp_of_tile))


def _chunk_copy(src, dst, src_chunk, dst_chunk, sem, n_chunks=1):
    rows = n_chunks * CHUNK
    return pltpu.make_async_copy(src.at[pl.ds(pl.multiple_of(src_chunk * CHUNK, CHUNK), rows), :],
                                 dst.at[pl.ds(pl.multiple_of(dst_chunk * CHUNK, CHUNK), rows), :], sem)


def _slab_copies(src, dst, sem, tile, src_ofs_ref, dst_ofs_ref, n16_ref):
    for e in range(N_EXPERTS):
        k = tile * N_EXPERTS + e
        _chunk_copy(src, dst, src_ofs_ref[k], dst_ofs_ref[k], sem, n16_ref[k]).start()


def _wait_slabs(src, dst, sem, n_chunks):
    _chunk_copy(src, dst, 0, 0, sem, n_chunks).wait()


def _wait_fill(src, dst, sem, n_chunks):
    @pl.when(n_chunks > 0)
    def _():
        _wait_slabs(src, dst, sem, n_chunks)


def _tile_copy(src, dst, dst_tile, sem):
    return pltpu.make_async_copy(src, dst.at[pl.ds(pl.multiple_of(dst_tile * EXP_TILE, EXP_TILE), EXP_TILE), :], sem)


def _dispatch_kernel(dst16_ref, n16_ref, lofs16_ref, ntot_ref, gap16_ref, gapn16_ref, misc_ref,
                     hp_ref, hs_ref, pos_ref, g_ffn_ref, xs_hbm, cbuf, zbuf, sem, *, n_row_tiles, npt):
    t = pl.program_id(0)
    nt = pl.num_programs(0)
    slot = t % 2
    T = TOK_TILE
    n_tail = n_row_tiles - misc_ref[0]

    @pl.when(t == 0)
    def _():
        zbuf[...] = jnp.zeros_like(zbuf)
        for e in range(N_EXPERTS):
            g = gapn16_ref[e]

            @pl.when(g > 0)
            def _(e=e, g=g):
                _chunk_copy(zbuf, xs_hbm, 0, gap16_ref[e], sem.at[2], g).start()

        def fill_tile(i, carry):
            _tile_copy(zbuf, xs_hbm, misc_ref[0] + i, sem.at[2]).start()
            return carry
        lax.fori_loop(0, n_tail, fill_tile, 0)

    xn = _rmsnorm(_pick(t, npt, hp_ref, hs_ref), g_ffn_ref[...]).astype(BF16)
    pos = pos_ref[...]
    pos_t = pos.T
    lane = lax.broadcasted_iota(jnp.int32, (T, LANES), 1)
    extra = jnp.zeros((T, LANES), F32)
    for s in range(TOP_K):
        c = pos[:, 2 + s:3 + s]
        hi = c.astype(BF16).astype(F32)
        mid = (c - hi).astype(BF16).astype(F32)
        lo = c - hi - mid
        for j, piece in enumerate((hi, mid, lo)):
            extra = jnp.where(lane == 3 * s + j, piece, extra)
    extra = jnp.where(lane == 3 * TOP_K, pos[:, 4:5], extra)
    moved = jnp.concatenate([xn, extra.astype(BF16)], axis=1)
    for r0 in range(0, CBUF_ROWS, SORT_ROWS):
        rows = _iota_f32((SORT_ROWS, T), 0) + float(r0)
        onehot = jnp.where((rows == pos_t[0:1, :]) | (rows == pos_t[1:2, :]), 1.0, 0.0).astype(BF16)
        cbuf[slot, r0:r0 + SORT_ROWS] = jnp.dot(onehot, moved, preferred_element_type=F32).astype(BF16)

    src = cbuf.at[slot]
    _slab_copies(src, xs_hbm, sem.at[slot], t, lofs16_ref, dst16_ref, n16_ref)

    @pl.when(t > 0)
    def _():
        _wait_slabs(cbuf.at[1 - slot], xs_hbm, sem.at[1 - slot], ntot_ref[t - 1])

    @pl.when(t == nt - 1)
    def _():
        _wait_slabs(src, xs_hbm, sem.at[slot], ntot_ref[t])
        for e in range(N_EXPERTS):
            _wait_fill(zbuf, xs_hbm, sem.at[2], gapn16_ref[e])

        def wait_tile(_, carry):
            _tile_copy(zbuf, xs_hbm, 0, sem.at[2]).wait()
            return carry
        lax.fori_loop(0, n_tail, wait_tile, 0)


def _dispatch_call(plan, hp, hs, pos, l, P, n_sorted):
    npt = hp.shape[0] // TOK_TILE
    nt = npt + hs.shape[0] // TOK_TILE
    g_ffn = P["g_ffn"]
    grid_spec = pltpu.PrefetchScalarGridSpec(
        num_scalar_prefetch=7, grid=(nt,),
        in_specs=_pair_specs(npt, D_MODEL) + [
            pl.BlockSpec((TOK_TILE, LANES), lambda t, *_: (t, 0)),
            pl.BlockSpec((None,) + g_ffn.shape[1:], lambda t, *_: (l, 0, 0))],
        out_specs=pl.BlockSpec(memory_space=pl.ANY),
        scratch_shapes=[pltpu.VMEM((2, CBUF_ROWS, XS_COLS), BF16), pltpu.VMEM((EXP_TILE, XS_COLS), BF16),
                        pltpu.SemaphoreType.DMA((3,))])
    return pl.pallas_call(
        functools.partial(_dispatch_kernel, n_row_tiles=n_sorted // EXP_TILE, npt=npt), grid_spec=grid_spec,
        out_shape=jax.ShapeDtypeStruct((n_sorted, XS_COLS), BF16),
        compiler_params=pltpu.CompilerParams(dimension_semantics=("arbitrary",), vmem_limit_bytes=VMEM_LIMIT),
        name="moe_dispatch",
    )(*plan[:7], hp, hs, pos, g_ffn)


def _expert_kernel(eot_ref, misc_ref, xs_ref, wg_ref, wu_ref, wd_ref, y_ref, wg_s, wu_s, wd_s):
    i = pl.program_id(0)
    valid = i < misc_ref[0]

    @pl.when(jnp.logical_not(valid))
    def _():
        y_ref[...] = jnp.zeros_like(y_ref)

    @pl.when(valid & ((i == 0) | (eot_ref[i] != eot_ref[jnp.maximum(i - 1, 0)])))
    def _():
        wg_s[...] = wg_ref[...].astype(BF16)
        wu_s[...] = wu_ref[...].astype(BF16)
        wd_s[...] = wd_ref[...].astype(BF16)

    @pl.when(valid)
    def _():
        xs = xs_ref[...]
        x = xs[:, :D_MODEL]
        ex = xs[:, D_MODEL:].astype(F32)
        lane = lax.broadcasted_iota(jnp.int32, ex.shape, 1)
        id0 = jnp.sum(jnp.where(lane == 3 * TOP_K, ex, 0.0), axis=-1, keepdims=True)
        first = id0 == eot_ref[i].astype(F32)
        mine = (first & (lane < 3)) | (jnp.logical_not(first) & (lane >= 3) & (lane < 3 * TOP_K))
        c = jnp.sum(jnp.where(mine, ex, 0.0), axis=-1, keepdims=True)
        hg = _silu(jnp.dot(x, wg_s[...], preferred_element_type=F32)) * jnp.dot(x, wu_s[...],
                                                                               preferred_element_type=F32)
        y_ref[...] = jnp.dot((hg * c).astype(BF16), wd_s[...], preferred_element_type=F32).astype(BF16)


def _expert_call(plan, xs, l, wg, wu, wd):
    misc, exp_of_tile = plan[6:]
    n_row_tiles = xs.shape[0] // EXP_TILE

    def last_valid(i, nv):
        return jnp.maximum(jnp.minimum(i, nv[0] - 1), 0)

    def row_map(i, eot, nv):
        return (last_valid(i, nv), 0)

    def w_map(i, eot, nv):
        return (l * N_EXPERTS + eot[last_valid(i, nv)], 0, 0)

    grid_spec = pltpu.PrefetchScalarGridSpec(
        num_scalar_prefetch=2, grid=(n_row_tiles,),
        in_specs=[pl.BlockSpec((EXP_TILE, XS_COLS), row_map),
                  pl.BlockSpec((None, D_MODEL, EXPERT_FF), w_map),
                  pl.BlockSpec((None, D_MODEL, EXPERT_FF), w_map),
                  pl.BlockSpec((None, EXPERT_FF, D_MODEL), w_map)],
        out_specs=pl.BlockSpec((EXP_TILE, D_MODEL), lambda i, eot, nv: (i, 0)),
        scratch_shapes=[pltpu.VMEM((D_MODEL, EXPERT_FF), BF16), pltpu.VMEM((D_MODEL, EXPERT_FF), BF16),
                        pltpu.VMEM((EXPERT_FF, D_MODEL), BF16)])
    return pl.pallas_call(
        _expert_kernel, grid_spec=grid_spec,
        out_shape=jax.ShapeDtypeStruct((xs.shape[0], D_MODEL), BF16),
        compiler_params=pltpu.CompilerParams(dimension_semantics=("arbitrary",), vmem_limit_bytes=VMEM_LIMIT),
        name="moe_experts",
    )(exp_of_tile, misc, xs, wg, wu, wd)


def _combine_kernel(dst16_ref, n16_ref, lofs16_ref, ntot_ref, hp_ref, hs_ref, pp_ref, ps_ref, pos_ref, y_hbm,
                    g_ple_ref, w_pg_ref, w_pp_ref, g_fin_ref, *rest, final, npt, n_state):
    if final:
        st_in, (op_ref, os_ref, *st_out, ybuf, wpg_s, wpp_s, sem) = rest[:n_state], rest[n_state:]
        depth = n_state // len(st_out)
        for k, out in enumerate(st_out):
            for l in range(depth):
                out[l] = st_in[k * depth + l][...]
    else:
        o_ref, ybuf, wpg_s, wpp_s, sem = rest
    t = pl.program_id(0)
    nt = pl.num_programs(0)
    slot = t % 2

    def fetch(tile, sl):
        _slab_copies(y_hbm, ybuf.at[sl], sem.at[sl], tile, dst16_ref, lofs16_ref, n16_ref)

    @pl.when(t == 0)
    def _():
        ybuf[...] = jnp.zeros_like(ybuf)
        fetch(t, slot)
        wpg_s[...] = w_pg_ref[...].astype(BF16)
        wpp_s[...] = w_pp_ref[...].astype(BF16)

    @pl.when(t + 1 < nt)
    def _():
        fetch(t + 1, 1 - slot)

    _wait_slabs(y_hbm, ybuf.at[slot], sem.at[slot], ntot_ref[t])

    pos = pos_ref[...]
    cols = _iota_f32((TOK_TILE, CBUF_ROWS), 1)
    pick = jnp.where((cols == pos[:, 0:1]) | (cols == pos[:, 1:2]), 1.0, 0.0).astype(BF16)
    h2 = _pick(t, npt, hp_ref, hs_ref) + jnp.dot(pick, ybuf[slot], preferred_element_type=F32)
    xn2 = _rmsnorm(h2, g_ple_ref[...])
    gate = _sigmoid(_dot(xn2, wpg_s[...]))
    p = jnp.where(t < npt, pp_ref[...], ps_ref[...])
    h3 = h2 + gate * _dot(p, wpp_s[...])
    if final:
        h3 = _rmsnorm(h3, g_fin_ref[...])

        @pl.when(t < npt)
        def _():
            op_ref[...] = h3

        @pl.when(t >= npt)
        def _():
            os_ref[...] = h3
    else:
        o_ref[...] = h3


def _state_specs(arrs, axis, nt):
    shape = arrs[0].shape
    unit = SUBLANES if axis == len(shape) - 2 else 1
    per = -(-shape[axis] // (nt * unit)) * unit
    assert shape[axis] % per == 0, (shape, axis, nt)
    n_blocks = shape[axis] // per
    block = shape[:axis] + (per,) + shape[axis + 1:]

    def in_map(t, *_):
        return (0,) * axis + (jnp.minimum(t, n_blocks - 1),) + (0,) * (len(shape) - axis - 1)

    return [pl.BlockSpec(block, in_map)] * len(arrs), pl.BlockSpec((len(arrs),) + block, lambda t, *_: (0,) + in_map(t))


def _combine_call(plan, hp, hs, pp, ps, pos, y, l, P, g_final, *, states=None):
    final = states is not None
    st_in = [a for arrs, _ in states for a in arrs] if final else []
    npt = hp.shape[0] // TOK_TILE
    nst = hs.shape[0] // TOK_TILE
    nt = npt + nst

    def lmap(t, *_):
        return (l, 0, 0)

    in_specs = _pair_specs(npt, D_MODEL) + [
        pl.BlockSpec((None, TOK_TILE, PLE_DIM), lambda t, *_: (l, jnp.minimum(t, npt - 1), 0)),
        pl.BlockSpec((None, TOK_TILE, PLE_DIM), lambda t, *_: (l, jnp.maximum(t - npt, 0), 0)),
        pl.BlockSpec((TOK_TILE, LANES), lambda t, *_: (t, 0)),
        pl.BlockSpec(memory_space=pl.ANY),
        pl.BlockSpec((None,) + P["g_ple"].shape[1:], lmap),
        pl.BlockSpec((None,) + P["w_ple_gate"].shape[1:], lmap),
        pl.BlockSpec((None,) + P["w_ple_proj"].shape[1:], lmap),
        pl.BlockSpec(g_final.shape, lambda t, *_: (0, 0))]
    if final:
        st_specs = [_state_specs(arrs, axis, nt) for arrs, axis in states]
        in_specs += [s for ins, _ in st_specs for s in ins]
        out_specs = (pl.BlockSpec((TOK_TILE, D_MODEL), lambda t, *_: (jnp.minimum(t, npt - 1), 0)),
                     pl.BlockSpec((TOK_TILE, D_MODEL), lambda t, *_: (jnp.maximum(t - npt, 0), 0)),
                     ) + tuple(out for _, out in st_specs)
        out_shape = (jax.ShapeDtypeStruct(hp.shape, F32), jax.ShapeDtypeStruct(hs.shape, F32),
                     ) + tuple(jax.ShapeDtypeStruct((len(arrs),) + arrs[0].shape, F32) for arrs, _ in states)
    else:
        out_specs = pl.BlockSpec((TOK_TILE, D_MODEL), lambda t, *_: (t, 0))
        out_shape = jax.ShapeDtypeStruct((nt * TOK_TILE, D_MODEL), F32)
    grid_spec = pltpu.PrefetchScalarGridSpec(
        num_scalar_prefetch=4, grid=(nt,), in_specs=in_specs, out_specs=out_specs,
        scratch_shapes=[pltpu.VMEM((2, CBUF_ROWS, D_MODEL), BF16), pltpu.VMEM((D_MODEL, D_MODEL), BF16),
                        pltpu.VMEM((PLE_DIM, D_MODEL), BF16), pltpu.SemaphoreType.DMA((2,))])
    return pl.pallas_call(
        functools.partial(_combine_kernel, final=final, npt=npt, n_state=len(st_in)),
        grid_spec=grid_spec, out_shape=out_shape,
        compiler_params=pltpu.CompilerParams(dimension_semantics=("arbitrary",), vmem_limit_bytes=VMEM_LIMIT),
        name="moe_combine_final" if final else "moe_combine",
    )(*plan[:4], hp, hs, pp, ps, pos, y, P["g_ple"], P["w_ple_gate"], P["w_ple_proj"], g_final, *st_in)


def _ffn(hp, hs, pp, ps, l, P, g_final, *, states=None):
    n = hp.shape[0] + hs.shape[0]
    nt = n // TOK_TILE
    bound = TOP_K * n + nt * N_EXPERTS * CHUNK + N_EXPERTS * (EXP_TILE - 1)
    n_sorted = -(-bound // EXP_TILE) * EXP_TILE
    pos, cnt = _route_call(hp, hs, l, P)
    plan = _chunk_plan(cnt[:, 0, :N_EXPERTS].astype(jnp.int32), n_sorted // EXP_TILE)
    xs = _dispatch_call(plan, hp, hs, pos, l, P, n_sorted)
    y = _expert_call(plan, xs, l, P["wg"], P["wu"], P["wd"])
    return _combine_call(plan, hp, hs, pp, ps, pos, y, l, P, g_final, states=states)


def _router_weights(g_ffn, w_grp_router, b_grp_router, w_exp_router, b_exp_router):
    depth = w_grp_router.shape[0]
    n_pad = LANES - N_EXPERTS - N_GROUPS
    w_er = jnp.transpose(w_exp_router, (0, 2, 1, 3)).reshape(depth, D_MODEL, N_EXPERTS)
    w_rt = jnp.concatenate([w_er, w_grp_router, jnp.zeros((depth, D_MODEL, n_pad), F32)], axis=2) * g_ffn[:, :, None]
    b_rt = jnp.concatenate([b_exp_router.reshape(depth, N_EXPERTS), b_grp_router, jnp.zeros((depth, n_pad), F32)], axis=1)
    w_hi = w_rt.astype(BF16)
    w_lo = (w_rt - w_hi.astype(F32)).astype(BF16)
    return jnp.concatenate([w_hi, w_lo], axis=2), b_rt.reshape(depth, 1, LANES)


def kernel(x_prompt, x_sample, state_gla, state_conv, p_prompt, p_sample, g_mix, w_in, w_forget_up, b_forget,
           g_gla_out, w_conv, b_conv, g_conv_ln, b_conv_ln, w_out, g_ffn, w_grp_router, b_grp_router,
           w_exp_router, b_exp_router, w_exp_gate, w_exp_up, w_exp_down, g_ple, w_ple_gate, w_ple_proj, g_final):
    depth = w_in.shape[0]
    nbp, seq_p, _ = x_prompt.shape
    nbs, seq_s, _ = x_sample.shape
    n_p = nbp * seq_p
    n_s = nbs * seq_s

    def rows(v):
        return v.reshape(depth, 1, -1)

    P = {
        "g_mix": rows(g_mix), "w_in_t": jnp.swapaxes(w_in, 1, 2),
        "w_forget_up": w_forget_up, "b_forget": rows(b_forget), "g_gla_out": rows(g_gla_out),
        "w_conv": w_conv, "b_conv": rows(b_conv), "g_conv_ln": rows(g_conv_ln), "b_conv_ln": rows(b_conv_ln),
        "w_out": w_out, "g_ffn": rows(g_ffn), "g_ple": rows(g_ple),
        "w_ple_gate": w_ple_gate, "w_ple_proj": w_ple_proj,
        "wg": w_exp_gate.reshape(depth * N_EXPERTS, D_MODEL, EXPERT_FF),
        "wu": w_exp_up.reshape(depth * N_EXPERTS, D_MODEL, EXPERT_FF),
        "wd": w_exp_down.reshape(depth * N_EXPERTS, EXPERT_FF, D_MODEL),
    }
    P["w_rt"], P["b_rt"] = _router_weights(g_ffn, w_grp_router, b_grp_router, w_exp_router, b_exp_router)
    g_fin = g_final.reshape(1, -1)
    xp = x_prompt.reshape(n_p, D_MODEL)
    xs = x_sample.reshape(n_s, D_MODEL)
    pp = p_prompt.reshape(depth, n_p, PLE_DIM)
    ps = p_sample.reshape(depth, n_s, PLE_DIM)
    s_in = state_gla.reshape(depth, nbs, QK_COLS, GLA_DV)
    c_in_t = jnp.swapaxes(state_conv, 1, 2)

    h = None
    sg_p, sg_s, sc_p, sc_s = [], [], [], []
    for l in range(depth):
        src_p, src_s, off_s = (xp, xs, 0) if l == 0 else (h, h, n_p)
        (hp, sgp, scp), (hs, sgs, scs) = _guarded(
            functools.partial(_mixer_prompt, src_p, 0, nbp, seq_p, l, P),
            functools.partial(_mixer_sample, src_s, off_s, nbs, seq_s, l, s_in, c_in_t, P))
        sg_p.append(sgp)
        sg_s.append(sgs)
        sc_p.append(scp)
        sc_s.append(scs)
        h = _ffn(hp, hs, pp, ps, l, P, g_fin, states=((sg_p, 0), (sg_s, 0), (sc_s, 1)) if l == depth - 1 else None)

    y_prompt, y_sample, sg_p_all, sg_s_all, sc_s_all = h
    return (y_prompt.reshape(nbp, seq_p, D_MODEL), y_sample.reshape(nbs, seq_s, D_MODEL),
            sg_p_all.reshape(depth, nbp, GLA_HEADS, GLA_DK, GLA_DV), sg_s_all.reshape(depth, nbs, GLA_HEADS, GLA_DK, GLA_DV),
            jnp.stack(sc_p), jnp.swapaxes(sc_s_all, 1, 2))
```

```python
import functools

import jax
import jax.numpy as jnp
from jax import lax
from jax.experimental import pallas as pl
from jax.experimental.pallas import tpu as pltpu

D_MODEL = 1024
GLA_HEADS = 4
GLA_DK = 64
GLA_DV = 128
QK_COLS = GLA_HEADS * GLA_DK
V_COLS = GLA_HEADS * GLA_DV
CONV_WIDTH = 512
CONV_K = 31
GLA_LOWRANK = 16
GLA_TAU = 16.0
GLA_CHUNK = 64
PLE_DIM = 256
N_GROUPS = 4
EXPERTS_PER_GROUP = 8
N_EXPERTS = N_GROUPS * EXPERTS_PER_GROUP
EXPERT_FF = 256
TOP_K = 2
EPS = 1e-6
N_MAIN = 2 * QK_COLS + 2 * V_COLS

LANES = 128
SUBLANES = 8
CONV_PAD = 32
CONV_OFF = CONV_PAD - (CONV_K - 1)
VMEM_LIMIT = 56 * 1024 * 1024
TOK_TILE = 512
CHUNK = 16
EXP_TILE = 512
CBUF_ROWS = -(-(TOP_K * TOK_TILE + N_EXPERTS * CHUNK) // LANES) * LANES
XS_SLOTS = 3
SORT_ROWS = 512
XS_COLS = D_MODEL + LANES
ROUTE_ROWS = 48
DECAY_SAFE = 80.0

F32 = jnp.float32
BF16 = jnp.bfloat16
HI = lax.Precision.HIGHEST


def _sigmoid(x):
    return 1.0 / (1.0 + jnp.exp(-x))


def _silu(x):
    return x * _sigmoid(x)


def _log_sigmoid(x):
    return jnp.minimum(x, 0.0) - jnp.log(1.0 + jnp.exp(-jnp.abs(x)))


def _rmsnorm(x, g):
    return x * lax.rsqrt(jnp.mean(x * x, axis=-1, keepdims=True) + EPS) * g


def _dot(a, b):
    return jnp.dot(a.astype(BF16), b.astype(BF16), preferred_element_type=F32)


def _dot_t(a, b):
    return lax.dot_general(a.astype(BF16), b.astype(BF16), (((1,), (1,)), ((), ())),
                           preferred_element_type=F32)


def _dot_hi(a, b):
    return jnp.dot(a, b, preferred_element_type=F32, precision=HI)


def _iota_f32(shape, dim):
    return lax.broadcasted_iota(jnp.int32, shape, dim).astype(F32)


def _const_spec(shape):
    nd = len(shape)
    return pl.BlockSpec(shape, lambda *_: (0,) * nd)


def _layer_spec(arr, l):
    nd = arr.ndim - 1
    return pl.BlockSpec((None,) + arr.shape[1:], lambda *_: (l,) + (0,) * nd, pipeline_mode=pl.Buffered(1))


def _cast_mixer_weights(w_int_ref, w_fu_ref, w_out_ref, wmain_s, wlr_s, wcv_s, wfu_s, wout_s):
    blk = 4 * LANES
    for r in range(0, N_MAIN, blk):
        wmain_s[:, r:r + blk] = w_int_ref[r:r + blk, :].T.astype(BF16)
    lane = lax.broadcasted_iota(jnp.int32, (D_MODEL, LANES), 1)
    wlr_s[...] = jnp.where(lane < GLA_LOWRANK, w_int_ref[N_MAIN:N_MAIN + LANES, :].T, 0.0).astype(BF16)
    cv0 = N_MAIN + GLA_LOWRANK
    for r in range(0, 2 * CONV_WIDTH, blk):
        wcv_s[:, r:r + blk] = w_int_ref[cv0 + r:cv0 + r + blk, :].T.astype(BF16)
    wfu_s[...] = jnp.zeros_like(wfu_s)
    wfu_s[0:GLA_LOWRANK, :] = w_fu_ref[...].astype(BF16)
    wout_s[...] = w_out_ref[...].astype(BF16)


def _project(x, g_mix, wmain_s, wlr_s, wfu_s, b_f, wcv_s):
    xn = _rmsnorm(x, g_mix).astype(BF16)
    cv = jnp.dot(xn, wcv_s[...], preferred_element_type=F32)
    u = cv[:, :CONV_WIDTH] * _sigmoid(cv[:, CONV_WIDTH:])
    lr = jnp.dot(xn, wlr_s[...], preferred_element_type=F32)
    zf = _dot(lr, wfu_s[...]) + b_f
    la = _log_sigmoid(zf) * (1.0 / GLA_TAU)
    zqk = jnp.dot(xn, wmain_s[:, :2 * QK_COLS], preferred_element_type=F32)
    q = zqk[:, :QK_COLS] * (GLA_DK ** -0.5)
    k = zqk[:, QK_COLS:]
    zvo = jnp.dot(xn, wmain_s[:, 2 * QK_COLS:], preferred_element_type=F32)
    v = zvo[:, :V_COLS]
    og = zvo[:, V_COLS:]
    return q, k, v, og, la, u


def _stack_heads(qd):
    lane = lax.broadcasted_iota(jnp.int32, qd.shape, 1)
    return jnp.concatenate(
        [jnp.where((lane >= h * GLA_DK) & (lane < (h + 1) * GLA_DK), qd, 0.0) for h in range(GLA_HEADS)],
        axis=0)


def _gated_head_norm(o, og, g_gla):
    outs = []
    for h in range(GLA_HEADS):
        sl = slice(h * GLA_DV, (h + 1) * GLA_DV)
        outs.append(_rmsnorm(o[:, sl], g_gla) * _silu(og[:, sl]))
    return jnp.concatenate(outs, axis=1)


def _causal_conv(win, w_conv, b_conv, n):
    acc = jnp.broadcast_to(b_conv, (n, CONV_WIDTH))
    for s in range(SUBLANES):
        taps = [j for j in range(CONV_K) if (CONV_OFF + j) % SUBLANES == s]
        if not taps:
            continue
        rows = n if s == 0 else n + SUBLANES
        part = None
        for j in taps:
            a = (CONV_OFF + j) - s
            term = w_conv[j:j + 1, :] * win[a:a + rows, :]
            part = term if part is None else part + term
        acc = acc + part[s:s + n, :]
    return acc


def _conv_ln_act(acc, g_ln, b_ln):
    mu = jnp.mean(acc, axis=-1, keepdims=True)
    xc = acc - mu
    y = xc * lax.rsqrt(jnp.mean(xc * xc, axis=-1, keepdims=True) + EPS) * g_ln + b_ln
    return _silu(y)


def _head_diag(upd):
    return jnp.concatenate([upd[h * GLA_DK:(h + 1) * GLA_DK, h * GLA_DV:(h + 1) * GLA_DV]
                            for h in range(GLA_HEADS)], axis=0)


def _mixer_weight_args(l, P):
    names = ("g_mix", "w_in_t", "w_forget_up", "b_forget", "g_gla_out", "w_conv", "b_conv", "g_conv_ln",
             "b_conv_ln", "w_out")
    arrs = [P[n] for n in names]
    return arrs, [_layer_spec(a, l) for a in arrs]


_MIXER_WEIGHT_SCRATCH = [pltpu.VMEM((D_MODEL, N_MAIN), BF16), pltpu.VMEM((D_MODEL, LANES), BF16),
                         pltpu.VMEM((D_MODEL, 2 * CONV_WIDTH), BF16), pltpu.VMEM((LANES, QK_COLS), BF16),
                         pltpu.VMEM((D_MODEL, D_MODEL), BF16)]


def _mixer_prompt_kernel(x_ref, g_mix_ref, w_int_ref, w_fu_ref, b_f_ref, g_gla_ref, w_conv_ref,
                         b_conv_ref, g_ln_ref, b_ln_ref, w_out_ref,
                         h_ref, sg_ref, sc_ref, risk_ref,
                         wmain_s, wlr_s, wcv_s, wfu_s, wout_s,
                         s_ref, ubuf_ref, qs_ref, kd_ref, klt_ref, dec_ref, mid_ref, v_ref, og_ref, mix_ref,
                         *stable_refs, tt, stable):
    t = pl.program_id(1)
    nt = pl.num_programs(1)
    C = GLA_CHUNK
    n_chunks = tt // C

    @pl.when((pl.program_id(0) == 0) & (t == 0))
    def _():
        _cast_mixer_weights(w_int_ref, w_fu_ref, w_out_ref, wmain_s, wlr_s, wcv_s, wfu_s, wout_s)

    @pl.when(t == 0)
    def _():
        s_ref[...] = jnp.zeros_like(s_ref)
        ubuf_ref[0:CONV_PAD, :] = jnp.zeros((CONV_PAD, CONV_WIDTH), F32)

    x = x_ref[...]
    q, k, v, og, la, u = _project(x, g_mix_ref[...], wmain_s, wlr_s, wfu_s, b_f_ref[...], wcv_s)
    ubuf_ref[CONV_PAD:CONV_PAD + tt, :] = u
    v_ref[...] = v.astype(BF16)
    og_ref[...] = _silu(og)

    w_conv = w_conv_ref[...]
    b_conv = b_conv_ref[...]
    g_ln = g_ln_ref[...]
    b_ln = b_ln_ref[...]
    row = lax.broadcasted_iota(jnp.int32, (C, C), 0)
    col = lax.broadcasted_iota(jnp.int32, (C, C), 1)
    tri = (col <= row).astype(F32)
    risk = jnp.zeros((1, QK_COLS), F32)
    for c in range(n_chunks):
        rows = slice(c * C, (c + 1) * C)
        win = ubuf_ref[c * C:c * C + C + CONV_PAD, :]
        cact = _conv_ln_act(_causal_conv(win, w_conv, b_conv, C), g_ln, b_ln)
        mix_ref[rows, V_COLS:] = cact.astype(BF16)
        bits = pltpu.bitcast(cact[C - SUBLANES:C, 0:QK_COLS], jnp.uint32)
        zero = pltpu.bitcast(lax.shift_right_logical(lax.shift_right_logical(bits, jnp.uint32(16)), jnp.uint32(16)),
                             F32)[0:1, :]
        qc = q[rows, :] + zero
        b = _dot_hi(tri, la[rows, :])
        b_last = b[C - 1:C, :]
        if stable:
            q_st, k_st, b_st = stable_refs
            q_st[rows, :] = qc
            k_st[rows, :] = k[rows, :]
            b_st[rows, :] = b
            b_mid = jnp.zeros_like(b_last)
        else:
            b_mid = b[C // 2 - 1:C // 2, :]
            risk = jnp.maximum(risk, jnp.maximum(-b_mid, b_mid - b_last))
        qs_ref[c] = _stack_heads(qc * jnp.exp(b - b_mid)).astype(BF16)
        kd_ref[rows, :] = (k[rows, :] * jnp.exp(jnp.minimum(b_mid - b, DECAY_SAFE))).astype(BF16)
        kl = k[rows, :] * jnp.exp(b_last - b)
        klt = jnp.concatenate([kl, jnp.broadcast_to(jnp.exp(b_last), (C // 2, QK_COLS)),
                               jnp.broadcast_to(jnp.exp(b_mid), (C // 2, QK_COLS))], axis=0).T
        klt_ref[c] = klt.astype(BF16)
        dec_ref[c] = jnp.broadcast_to(klt[:, C:C + 1], (QK_COLS, GLA_DV))
        mid_ref[c] = jnp.broadcast_to(klt[:, 3 * C // 2:3 * C // 2 + 1], (QK_COLS, GLA_DV))

    r4 = lax.broadcasted_iota(jnp.int32, (GLA_HEADS * C, C), 0)
    c4 = lax.broadcasted_iota(jnp.int32, (GLA_HEADS * C, C), 1)
    causal4 = c4 <= (r4 % C)
    g_gla = g_gla_ref[...]
    s = s_ref[...]
    for c in range(n_chunks):
        rows = slice(c * C, (c + 1) * C)
        qs = qs_ref[c]
        vc = v_ref[rows, :]
        if stable:
            scores = _direct_scores(*stable_refs, c * C, C).astype(BF16)
        else:
            scores = jnp.where(causal4, _dot_t(qs, kd_ref[rows, :]), 0.0).astype(BF16)
        o_inter = jnp.dot(qs, (mid_ref[c] * s).astype(BF16), preferred_element_type=F32)
        upd = jnp.dot(klt_ref[c][:, :C], vc, preferred_element_type=F32)
        s = dec_ref[c] * s + _head_diag(upd)
        o_parts = []
        for h in range(GLA_HEADS):
            vh = vc[:, h * GLA_DV:(h + 1) * GLA_DV]
            o_parts.append(jnp.dot(scores[h * C:(h + 1) * C, :], vh, preferred_element_type=F32)
                           + o_inter[h * C:(h + 1) * C, :])
        o = jnp.concatenate(o_parts, axis=1)
        gated = []
        for h in range(GLA_HEADS):
            sl = slice(h * GLA_DV, (h + 1) * GLA_DV)
            gated.append(_rmsnorm(o[:, sl], g_gla) * og_ref[rows, sl])
        mix_ref[rows, 0:V_COLS] = jnp.concatenate(gated, axis=1).astype(BF16)
    s_ref[...] = s

    h_ref[...] = x + jnp.dot(mix_ref[...], wout_s[...], preferred_element_type=F32)
    tail = ubuf_ref[tt:tt + CONV_PAD, :]
    ubuf_ref[0:CONV_PAD, :] = tail

    risk_ref[0] = jnp.broadcast_to(jnp.max(risk, axis=-1, keepdims=True), (SUBLANES, LANES))

    @pl.when(t == nt - 1)
    def _():
        sg_ref[0] = s
        sc_ref[0] = tail[CONV_OFF:, :]


def _direct_scores(q_st, k_st, b_st, r0, C):
    kc = k_st[r0:r0 + C, :]
    bc = b_st[r0:r0 + C, :]
    srow = lax.broadcasted_iota(jnp.int32, (C, QK_COLS), 0)
    lane = lax.broadcasted_iota(jnp.int32, (C, LANES), 1)
    head_of = lax.broadcasted_iota(jnp.int32, (QK_COLS, LANES), 0) // GLA_DK
    head_sum = (head_of == lax.broadcasted_iota(jnp.int32, (QK_COLS, LANES), 1)).astype(F32)

    def one_query(t, acc):
        d = b_st[pl.ds(r0 + t, 1), :] - bc
        w = jnp.where(srow <= t, jnp.exp(jnp.minimum(d, 0.0)), 0.0) * kc * q_st[pl.ds(r0 + t, 1), :]
        per_head = _dot_hi(w, head_sum)
        return tuple(jnp.where(lane == t, per_head[:, h:h + 1], acc[h]) for h in range(GLA_HEADS))

    acc = lax.fori_loop(0, C, one_query, tuple(jnp.zeros((C, LANES), F32) for _ in range(GLA_HEADS)))
    return jnp.concatenate([a.T[0:C, :] for a in acc], axis=0)


def _mixer_prompt(x, row_off, nb, seq, l, P, *, tt=512, stable=False):
    nt = seq // tt
    n_chunks = tt // GLA_CHUNK
    blk_off = row_off // tt
    weights, w_specs = _mixer_weight_args(l, P)
    in_specs = [pl.BlockSpec((tt, D_MODEL), lambda b, t: (blk_off + b * nt + t, 0))] + w_specs
    out_shape = (jax.ShapeDtypeStruct((nb * seq, D_MODEL), F32),
                 jax.ShapeDtypeStruct((nb, QK_COLS, GLA_DV), F32),
                 jax.ShapeDtypeStruct((nb, CONV_K - 1, CONV_WIDTH), F32),
                 jax.ShapeDtypeStruct((nb * nt, SUBLANES, LANES), F32))
    out_specs = (pl.BlockSpec((tt, D_MODEL), lambda b, t: (b * nt + t, 0)),
                 pl.BlockSpec((1, QK_COLS, GLA_DV), lambda b, t: (b, 0, 0)),
                 pl.BlockSpec((1, CONV_K - 1, CONV_WIDTH), lambda b, t: (b, 0, 0)),
                 pl.BlockSpec((1, SUBLANES, LANES), lambda b, t: (b * nt + t, 0, 0)))
    scratch = _MIXER_WEIGHT_SCRATCH + [
        pltpu.VMEM((QK_COLS, GLA_DV), F32),
        pltpu.VMEM((CONV_PAD + tt, CONV_WIDTH), F32),
        pltpu.VMEM((n_chunks, GLA_HEADS * GLA_CHUNK, QK_COLS), BF16), pltpu.VMEM((tt, QK_COLS), BF16),
        pltpu.VMEM((n_chunks, QK_COLS, 2 * GLA_CHUNK), BF16), pltpu.VMEM((n_chunks, QK_COLS, GLA_DV), F32),
        pltpu.VMEM((n_chunks, QK_COLS, GLA_DV), F32),
        pltpu.VMEM((tt, V_COLS), BF16), pltpu.VMEM((tt, V_COLS), F32),
        pltpu.VMEM((tt, D_MODEL), BF16)]
    if stable:
        scratch += [pltpu.VMEM((tt, QK_COLS), F32)] * 3
    return pl.pallas_call(
        functools.partial(_mixer_prompt_kernel, tt=tt, stable=stable),
        grid=(nb, nt), in_specs=in_specs, out_specs=out_specs, out_shape=out_shape,
        scratch_shapes=scratch,
        compiler_params=pltpu.CompilerParams(dimension_semantics=("arbitrary", "arbitrary"),
                                             vmem_limit_bytes=VMEM_LIMIT),
        name="mixer_prompt_stable" if stable else "mixer_prompt",
    )(x, *weights)


def _guarded(*mixers):
    fast = [mixer(stable=False) for mixer in mixers]
    risk = functools.reduce(jnp.maximum, [jnp.max(outs[-1]) for outs in fast])
    return lax.cond(risk > DECAY_SAFE, lambda: tuple(tuple(mixer(stable=True)[:-1]) for mixer in mixers),
                    lambda: tuple(tuple(outs[:-1]) for outs in fast))


def _mixer_sample_kernel(x_ref, s_in_ref, c_in_ref, g_mix_ref, w_int_ref, w_fu_ref, b_f_ref,
                         g_gla_ref, w_conv_ref, b_conv_ref, g_ln_ref, b_ln_ref, w_out_ref,
                         h_ref, sg_ref, sc_ref, risk_ref,
                         wmain_s, wlr_s, wcv_s, wfu_s, wout_s,
                         u4_ref, oi_ref, cacc4_ref, *, sb, seq, stable):
    R = sb * seq
    n_slabs = CONV_WIDTH // LANES

    @pl.when(pl.program_id(0) == 0)
    def _():
        _cast_mixer_weights(w_int_ref, w_fu_ref, w_out_ref, wmain_s, wlr_s, wcv_s, wfu_s, wout_s)

    x = x_ref[...]
    q, k, v, og, la, u = _project(x, g_mix_ref[...], wmain_s, wlr_s, wfu_s, b_f_ref[...], wcv_s)

    for kk in range(n_slabs):
        u4_ref[kk] = u[:, kk * LANES:(kk + 1) * LANES]
    full = [c_in_ref[j] for j in range(CONV_K - 1)]
    for t in range(seq):
        full.append(jnp.concatenate([u4_ref.at[kk][pl.ds(t, sb, stride=seq), :] for kk in range(n_slabs)], axis=1))
    w_conv = w_conv_ref[...]
    for t in range(seq):
        acc = jnp.broadcast_to(b_conv_ref[...], (sb, CONV_WIDTH))
        for j in range(CONV_K):
            acc = acc + w_conv[j:j + 1, :] * full[t + j]
        for kk in range(n_slabs):
            cacc4_ref.at[kk][pl.ds(t, sb, stride=seq), :] = acc[:, kk * LANES:(kk + 1) * LANES]
    for j in range(CONV_K - 1):
        sc_ref[j] = full[seq + j]

    row = lax.broadcasted_iota(jnp.int32, (R, R), 0)
    col = lax.broadcasted_iota(jnp.int32, (R, R), 1)
    same = (row // seq) == (col // seq)
    b = _dot_hi((same & (col <= row)).astype(F32), la)
    b_tot = _dot_hi(same.astype(F32), la)
    qd = q * jnp.exp(b)
    kl = k * jnp.exp(b_tot - b)
    qs = _stack_heads(qd)
    risk_ref[0] = jnp.broadcast_to(jnp.max(jnp.max(-b_tot, axis=-1, keepdims=True), axis=0, keepdims=True),
                                   (SUBLANES, LANES))
    if stable:
        head_of = lax.broadcasted_iota(jnp.int32, (QK_COLS, LANES), 0) // GLA_DK
        head_sum = (head_of == lax.broadcasted_iota(jnp.int32, (QK_COLS, LANES), 1)).astype(F32)
        t_in_seq = lax.broadcasted_iota(jnp.int32, (R, QK_COLS), 0) % seq
        parts = [jnp.zeros((R, R), F32) for _ in range(GLA_HEADS)]
        for d in range(seq):
            k_d = jnp.concatenate([jnp.zeros((d, QK_COLS), F32), k[:R - d, :]], axis=0) if d else k
            b_d = jnp.concatenate([jnp.zeros((d, QK_COLS), F32), b[:R - d, :]], axis=0) if d else b
            w = jnp.where(t_in_seq >= d, jnp.exp(jnp.minimum(b - b_d, 0.0)), 0.0) * k_d * q
            per_head = _dot_hi(w, head_sum)
            for h in range(GLA_HEADS):
                parts[h] = jnp.where(col == row - d, per_head[:, h:h + 1], parts[h])
        scores = jnp.concatenate(parts, axis=0)
    else:
        kd = k * jnp.exp(jnp.minimum(-b, DECAY_SAFE))
        r4 = lax.broadcasted_iota(jnp.int32, (GLA_HEADS * R, R), 0) % R
        c4 = lax.broadcasted_iota(jnp.int32, (GLA_HEADS * R, R), 1)
        mask4 = ((r4 // seq) == (c4 // seq)) & (c4 <= r4)
        scores = jnp.where(mask4, _dot_t(qs, kd), 0.0)

    klt = kl.T
    dect = jnp.exp(b_tot).T
    lane_h = lax.broadcasted_iota(jnp.int32, (GLA_DK, R), 1)
    lane_r = lax.broadcasted_iota(jnp.int32, (QK_COLS, R), 1)
    upd = []
    for h in range(GLA_HEADS):
        klt_h = klt[h * GLA_DK:(h + 1) * GLA_DK, :]
        lhs = jnp.concatenate([jnp.where((lane_h >= i * seq) & (lane_h < (i + 1) * seq), klt_h, 0.0)
                               for i in range(sb)], axis=0)
        upd.append(_dot(lhs, v[:, h * GLA_DV:(h + 1) * GLA_DV]))
    for i in range(sb):
        s_old = s_in_ref[i]
        qsel = jnp.concatenate([qs[h * R + i * seq:h * R + (i + 1) * seq, :] for h in range(GLA_HEADS)], axis=0)
        oi = _dot(qsel, s_old)
        for h in range(GLA_HEADS):
            oi_ref[h, i * seq:(i + 1) * seq, :] = oi[h * seq:(h + 1) * seq, :]
        smask = (lane_r >= i * seq) & (lane_r < (i + 1) * seq)
        dec = jnp.sum(jnp.where(smask, dect, 0.0), axis=1, keepdims=True) * (1.0 / seq)
        u_new = jnp.concatenate([upd[h][i * GLA_DK:(i + 1) * GLA_DK, :] for h in range(GLA_HEADS)], axis=0)
        sg_ref[i] = dec * s_old + u_new

    o_parts = []
    for h in range(GLA_HEADS):
        vh = v[:, h * GLA_DV:(h + 1) * GLA_DV]
        o_parts.append(_dot(scores[h * R:(h + 1) * R, :], vh) + oi_ref[h])
    o = jnp.concatenate(o_parts, axis=1)
    cacc = jnp.concatenate([cacc4_ref[kk] for kk in range(n_slabs)], axis=1)
    mix = jnp.concatenate([_gated_head_norm(o, og, g_gla_ref[...]),
                           _conv_ln_act(cacc, g_ln_ref[...], b_ln_ref[...])], axis=1)
    h_ref[...] = x + jnp.dot(mix.astype(BF16), wout_s[...], preferred_element_type=F32)


def _mixer_sample(x, row_off, nb, seq, l, s_in, c_in_t, P, *, sb=16, stable=False):
    R = sb * seq
    blk_off = row_off // R
    n_slabs = CONV_WIDTH // LANES
    weights, w_specs = _mixer_weight_args(l, P)
    in_specs = [pl.BlockSpec((R, D_MODEL), lambda i: (blk_off + i, 0)),
                pl.BlockSpec((None, sb, QK_COLS, GLA_DV), lambda i: (l, i, 0, 0)),
                pl.BlockSpec((None, CONV_K - 1, sb, CONV_WIDTH), lambda i: (l, 0, i, 0))] + w_specs
    out_shape = (jax.ShapeDtypeStruct((nb * seq, D_MODEL), F32),
                 jax.ShapeDtypeStruct((nb, QK_COLS, GLA_DV), F32),
                 jax.ShapeDtypeStruct((CONV_K - 1, nb, CONV_WIDTH), F32),
                 jax.ShapeDtypeStruct((nb // sb, SUBLANES, LANES), F32))
    out_specs = (pl.BlockSpec((R, D_MODEL), lambda i: (i, 0)),
                 pl.BlockSpec((sb, QK_COLS, GLA_DV), lambda i: (i, 0, 0)),
                 pl.BlockSpec((CONV_K - 1, sb, CONV_WIDTH), lambda i: (0, i, 0)),
                 pl.BlockSpec((1, SUBLANES, LANES), lambda i: (i, 0, 0)))
    scratch = _MIXER_WEIGHT_SCRATCH + [
        pltpu.VMEM((n_slabs, R, LANES), F32),
        pltpu.VMEM((GLA_HEADS, R, GLA_DV), F32),
        pltpu.VMEM((n_slabs, R, LANES), F32)]
    return pl.pallas_call(
        functools.partial(_mixer_sample_kernel, sb=sb, seq=seq, stable=stable),
        grid=(nb // sb,), in_specs=in_specs, out_specs=out_specs, out_shape=out_shape,
        scratch_shapes=scratch,
        compiler_params=pltpu.CompilerParams(dimension_semantics=("arbitrary",),
                                             vmem_limit_bytes=VMEM_LIMIT),
        name="mixer_sample_stable" if stable else "mixer_sample",
    )(x, s_in, c_in_t, *weights)


def _pair_specs(n_first_tiles, width):
    return [pl.BlockSpec((TOK_TILE, width), lambda t, *_: (jnp.minimum(t, n_first_tiles - 1), 0)),
            pl.BlockSpec((TOK_TILE, width), lambda t, *_: (jnp.maximum(t - n_first_tiles, 0), 0))]


def _pick(t, n_first_tiles, a_ref, b_ref):
    return jnp.where(t < n_first_tiles, a_ref[...], b_ref[...])


def _route(logits):
    row = lax.broadcasted_iota(jnp.int32, logits.shape, 0)
    row_f = row.astype(F32)
    neg = jnp.float32(-jnp.inf)
    big = jnp.float32(1e9)
    is_grp = (row >= N_EXPERTS) & (row < N_EXPERTS + N_GROUPS)
    gl = jnp.where(is_grp, logits, neg)
    gmax = jnp.max(gl, axis=0, keepdims=True)
    gidx = jnp.min(jnp.where(is_grp & (gl == gmax), row_f - N_EXPERTS, big), axis=0, keepdims=True)
    gsum = jnp.sum(jnp.where(is_grp, jnp.exp(gl - gmax), 0.0), axis=0, keepdims=True)
    g_w = 1.0 / gsum
    grp_of_row = jnp.floor(row_f * (1.0 / EXPERTS_PER_GROUP))
    in_grp = (row < N_EXPERTS) & (grp_of_row == gidx)
    ml = jnp.where(in_grp, logits, neg)
    v1 = jnp.max(ml, axis=0, keepdims=True)
    i1 = jnp.min(jnp.where(in_grp & (ml == v1), row_f, big), axis=0, keepdims=True)
    ml2 = jnp.where(row_f == i1, neg, ml)
    v2 = jnp.max(ml2, axis=0, keepdims=True)
    i2 = jnp.min(jnp.where(in_grp & (ml2 == v2), row_f, big), axis=0, keepdims=True)
    e2 = jnp.exp(v2 - v1)
    w1 = g_w / (1.0 + e2)
    w2 = g_w * e2 / (1.0 + e2)
    return i1, i2, w1, w2


def _route_kernel(hp_ref, hs_ref, w_rt_ref, b_rt_ref, pos_ref, cnt_ref, earlier_s, *, npt):
    T = TOK_TILE
    R = ROUTE_ROWS

    @pl.when(pl.program_id(0) == 0)
    def _():
        earlier_s[...] = (_iota_f32((T, T), 0) < _iota_f32((T, T), 1)).astype(BF16)

    x = _pick(pl.program_id(0), npt, hp_ref, hs_ref)
    scale = lax.rsqrt(jnp.mean(x * x, axis=-1, keepdims=True) + EPS)
    x_hi = x.astype(BF16)
    x_lo = (x - x_hi.astype(F32)).astype(BF16)
    w_split = w_rt_ref[...]
    both = jnp.dot(x_hi, w_split, preferred_element_type=F32)
    logits = ((both[:, :LANES] + both[:, LANES:] + jnp.dot(x_lo, w_split[:, :LANES], preferred_element_type=F32))
              * scale + b_rt_ref[...])
    i1, i2, w1, w2 = _route(logits.T[:R])
    row = _iota_f32((R, T), 0)
    a0 = (row == i1).astype(F32)
    a1 = (row == i2).astype(F32)
    a = a0 + a1
    cnt = jnp.sum(a, axis=1, keepdims=True)
    rank = jnp.dot(a.astype(BF16), earlier_s[...], preferred_element_type=F32)
    cnt_pad = jnp.maximum(jnp.ceil(cnt * (1.0 / CHUNK)), 1.0) * CHUNK
    no_rows = jnp.zeros((LANES - R, LANES), F32)
    before = (_iota_f32((LANES, LANES), 1) < _iota_f32((LANES, LANES), 0)).astype(F32)
    first = _dot_hi(before, jnp.concatenate([jnp.broadcast_to(cnt_pad, (R, LANES)), no_rows], axis=0))[:R, 0:1]
    base = first + rank
    pos0 = jnp.sum(a0 * base, axis=0, keepdims=True)
    pos1 = jnp.sum(a1 * base, axis=0, keepdims=True)
    r8 = _iota_f32((SUBLANES, T), 0)
    res = jnp.where(r8 == 0.0, pos0, jnp.where(r8 == 1.0, pos1, jnp.where(
        r8 == 2.0, w1, jnp.where(r8 == 3.0, w2, jnp.where(r8 == 4.0, i1, 0.0)))))
    pos_ref[...] = jnp.concatenate([res, jnp.zeros((LANES - SUBLANES, T), F32)], axis=0).T
    cnt_ref[0] = jnp.concatenate([jnp.broadcast_to(cnt, (R, LANES)), no_rows], axis=0).T[:SUBLANES]


def _route_call(hp, hs, l, P):
    npt = hp.shape[0] // TOK_TILE
    nt = npt + hs.shape[0] // TOK_TILE
    return pl.pallas_call(
        functools.partial(_route_kernel, npt=npt), grid=(nt,),
        in_specs=_pair_specs(npt, D_MODEL) + [_layer_spec(P[n], l) for n in ("w_rt", "b_rt")],
        scratch_shapes=[pltpu.VMEM((TOK_TILE, TOK_TILE), BF16)],
        out_specs=(pl.BlockSpec((TOK_TILE, LANES), lambda t: (t, 0)),
                   pl.BlockSpec((1, SUBLANES, LANES), lambda t: (t, 0, 0))),
        out_shape=(jax.ShapeDtypeStruct((nt * TOK_TILE, LANES), F32),
                   jax.ShapeDtypeStruct((nt, SUBLANES, LANES), F32)),
        compiler_params=pltpu.CompilerParams(dimension_semantics=("arbitrary",), vmem_limit_bytes=VMEM_LIMIT),
        name="moe_route",
    )(hp, hs, P["w_rt"], P["b_rt"])


def _chunk_plan(cnt, n_row_tiles):
    n16 = jnp.maximum((cnt + (CHUNK - 1)) // CHUNK, 1)
    lofs16 = jnp.cumsum(n16, axis=1) - n16
    tile_pref16 = jnp.cumsum(n16, axis=0) - n16
    tot16 = jnp.sum(n16, axis=0)
    per_tile = EXP_TILE // CHUNK
    seg16 = ((tot16 + per_tile - 1) // per_tile) * per_tile
    seg_end16 = jnp.cumsum(seg16)
    dst16 = (seg_end16 - seg16)[None, :] + tile_pref16
    n_tot = jnp.sum(n16, axis=1)
    gap16 = seg_end16 - seg16 + tot16
    gapn16 = seg16 - tot16
    tile_start16 = jnp.arange(n_row_tiles, dtype=jnp.int32) * per_tile
    n_valid = seg_end16[-1] // per_tile
    misc = n_valid.reshape(1)
    exp_of_tile = jnp.minimum(jnp.sum(seg_end16[None, :] <= tile_start16[:, None], axis=1), N_EXPERTS - 1)
    i32 = lambda a: a.astype(jnp.int32).reshape(-1)
    return (i32(dst16), i32(n16), i32(lofs16), i32(n_tot), i32(gap16), i32(gapn16), i32(misc), i32(exp_of_tile))


def _chunk_copy(src, dst, src_chunk, dst_chunk, sem, n_chunks=1):
    rows = n_chunks * CHUNK
    return pltpu.make_async_copy(src.at[pl.ds(pl.multiple_of(src_chunk * CHUNK, CHUNK), rows), :],
                                 dst.at[pl.ds(pl.multiple_of(dst_chunk * CHUNK, CHUNK), rows), :], sem)


def _slab_copies(src, dst, sem, tile, src_ofs_ref, dst_ofs_ref, n16_ref):
    for e in range(N_EXPERTS):
        k = tile * N_EXPERTS + e
        _chunk_copy(src, dst, src_ofs_ref[k], dst_ofs_ref[k], sem, n16_ref[k]).start()


def _wait_slabs(src, dst, sem, n_chunks):
    _chunk_copy(src, dst, 0, 0, sem, n_chunks).wait()


def _wait_fill(src, dst, sem, n_chunks):
    @pl.when(n_chunks > 0)
    def _():
        _wait_slabs(src, dst, sem, n_chunks)


def _tile_copy(src, dst, dst_tile, sem):
    return pltpu.make_async_copy(src, dst.at[pl.ds(pl.multiple_of(dst_tile * EXP_TILE, EXP_TILE), EXP_TILE), :], sem)


def _dispatch_kernel(dst16_ref, n16_ref, lofs16_ref, ntot_ref, gap16_ref, gapn16_ref, misc_ref,
                     hp_ref, hs_ref, pos_ref, g_ffn_ref, xs_hbm, cbuf, zbuf, sem, *, n_row_tiles, npt):
    t = pl.program_id(0)
    nt = pl.num_programs(0)
    slot = t % 2
    T = TOK_TILE
    n_tail = n_row_tiles - misc_ref[0]

    @pl.when(t == 0)
    def _():
        zbuf[...] = jnp.zeros_like(zbuf)
        for e in range(N_EXPERTS):
            g = gapn16_ref[e]

            @pl.when(g > 0)
            def _(e=e, g=g):
                _chunk_copy(zbuf, xs_hbm, 0, gap16_ref[e], sem.at[2], g).start()

        def fill_tile(i, carry):
            _tile_copy(zbuf, xs_hbm, misc_ref[0] + i, sem.at[2]).start()
            return carry
        lax.fori_loop(0, n_tail, fill_tile, 0)

    xn = _rmsnorm(_pick(t, npt, hp_ref, hs_ref), g_ffn_ref[...]).astype(BF16)
    pos = pos_ref[...]
    pos_t = pos.T
    lane = lax.broadcasted_iota(jnp.int32, (T, LANES), 1)
    extra = jnp.zeros((T, LANES), F32)
    for s in range(TOP_K):
        c = pos[:, 2 + s:3 + s]
        hi = c.astype(BF16).astype(F32)
        mid = (c - hi).astype(BF16).astype(F32)
        lo = c - hi - mid
        for j, piece in enumerate((hi, mid, lo)):
            extra = jnp.where(lane == 3 * s + j, piece, extra)
    extra = jnp.where(lane == 3 * TOP_K, pos[:, 4:5], extra)
    moved = jnp.concatenate([xn, extra.astype(BF16)], axis=1)
    for r0 in range(0, CBUF_ROWS, SORT_ROWS):
        rows = _iota_f32((SORT_ROWS, T), 0) + float(r0)
        onehot = jnp.where((rows == pos_t[0:1, :]) | (rows == pos_t[1:2, :]), 1.0, 0.0).astype(BF16)
        cbuf[slot, r0:r0 + SORT_ROWS] = jnp.dot(onehot, moved, preferred_element_type=F32).astype(BF16)

    src = cbuf.at[slot]
    _slab_copies(src, xs_hbm, sem.at[slot], t, lofs16_ref, dst16_ref, n16_ref)

    @pl.when(t > 0)
    def _():
        _wait_slabs(cbuf.at[1 - slot], xs_hbm, sem.at[1 - slot], ntot_ref[t - 1])

    @pl.when(t == nt - 1)
    def _():
        _wait_slabs(src, xs_hbm, sem.at[slot], ntot_ref[t])
        for e in range(N_EXPERTS):
            _wait_fill(zbuf, xs_hbm, sem.at[2], gapn16_ref[e])

        def wait_tile(_, carry):
            _tile_copy(zbuf, xs_hbm, 0, sem.at[2]).wait()
            return carry
        lax.fori_loop(0, n_tail, wait_tile, 0)


def _dispatch_call(plan, hp, hs, pos, l, P, n_sorted):
    npt = hp.shape[0] // TOK_TILE
    nt = npt + hs.shape[0] // TOK_TILE
    g_ffn = P["g_ffn"]
    grid_spec = pltpu.PrefetchScalarGridSpec(
        num_scalar_prefetch=7, grid=(nt,),
        in_specs=_pair_specs(npt, D_MODEL) + [
            pl.BlockSpec((TOK_TILE, LANES), lambda t, *_: (t, 0)),
            pl.BlockSpec((None,) + g_ffn.shape[1:], lambda t, *_: (l, 0, 0))],
        out_specs=pl.BlockSpec(memory_space=pl.ANY),
        scratch_shapes=[pltpu.VMEM((2, CBUF_ROWS, XS_COLS), BF16), pltpu.VMEM((EXP_TILE, XS_COLS), BF16),
                        pltpu.SemaphoreType.DMA((3,))])
    return pl.pallas_call(
        functools.partial(_dispatch_kernel, n_row_tiles=n_sorted // EXP_TILE, npt=npt), grid_spec=grid_spec,
        out_shape=jax.ShapeDtypeStruct((n_sorted, XS_COLS), BF16),
        compiler_params=pltpu.CompilerParams(dimension_semantics=("arbitrary",), vmem_limit_bytes=VMEM_LIMIT),
        name="moe_dispatch",
    )(*plan[:7], hp, hs, pos, g_ffn)


def _expert_kernel(eot_ref, misc_ref, xs_hbm, wg_ref, wu_ref, wd_ref, y_ref, wg_s, wu_s, wd_s, xbuf, xsem):
    i = pl.program_id(0)
    n_valid = misc_ref[0]
    valid = i < n_valid

    def row_tile(tile):
        return pltpu.make_async_copy(xs_hbm.at[pl.ds(pl.multiple_of(tile * EXP_TILE, EXP_TILE), EXP_TILE), :],
                                     xbuf.at[tile % XS_SLOTS], xsem.at[tile % XS_SLOTS])

    @pl.when(i == 0)
    def _():
        for ahead in range(XS_SLOTS - 1):
            @pl.when(ahead < n_valid)
            def _(ahead=ahead):
                row_tile(ahead).start()

    @pl.when(i + (XS_SLOTS - 1) < n_valid)
    def _():
        row_tile(i + (XS_SLOTS - 1)).start()

    @pl.when(jnp.logical_not(valid))
    def _():
        y_ref[...] = jnp.zeros_like(y_ref)

    @pl.when(valid & ((i == 0) | (eot_ref[i] != eot_ref[jnp.maximum(i - 1, 0)])))
    def _():
        wg_s[...] = wg_ref[...].astype(BF16)
        wu_s[...] = wu_ref[...].astype(BF16)
        wd_s[...] = wd_ref[...].astype(BF16)

    @pl.when(valid)
    def _():
        row_tile(i).wait()
        xs = xbuf[i % XS_SLOTS]
        x = xs[:, :D_MODEL]
        ex = xs[:, D_MODEL:].astype(F32)
        lane = lax.broadcasted_iota(jnp.int32, ex.shape, 1)
        id0 = jnp.sum(jnp.where(lane == 3 * TOP_K, ex, 0.0), axis=-1, keepdims=True)
        first = id0 == eot_ref[i].astype(F32)
        mine = (first & (lane < 3)) | (jnp.logical_not(first) & (lane >= 3) & (lane < 3 * TOP_K))
        c = jnp.sum(jnp.where(mine, ex, 0.0), axis=-1, keepdims=True)
        hg = _silu(jnp.dot(x, wg_s[...], preferred_element_type=F32)) * jnp.dot(x, wu_s[...],
                                                                               preferred_element_type=F32)
        y_ref[...] = jnp.dot((hg * c).astype(BF16), wd_s[...], preferred_element_type=F32).astype(BF16)


def _expert_call(plan, xs, l, wg, wu, wd):
    misc, exp_of_tile = plan[6:]
    n_row_tiles = xs.shape[0] // EXP_TILE

    def last_valid(i, nv):
        return jnp.maximum(jnp.minimum(i, nv[0] - 1), 0)

    def w_map(i, eot, nv):
        return (l * N_EXPERTS + eot[last_valid(i, nv)], 0, 0)

    grid_spec = pltpu.PrefetchScalarGridSpec(
        num_scalar_prefetch=2, grid=(n_row_tiles,),
        in_specs=[pl.BlockSpec(memory_space=pl.ANY),
                  pl.BlockSpec((None, D_MODEL, EXPERT_FF), w_map),
                  pl.BlockSpec((None, D_MODEL, EXPERT_FF), w_map),
                  pl.BlockSpec((None, EXPERT_FF, D_MODEL), w_map)],
        out_specs=pl.BlockSpec((EXP_TILE, D_MODEL), lambda i, eot, nv: (i, 0)),
        scratch_shapes=[pltpu.VMEM((D_MODEL, EXPERT_FF), BF16), pltpu.VMEM((D_MODEL, EXPERT_FF), BF16),
                        pltpu.VMEM((EXPERT_FF, D_MODEL), BF16), pltpu.VMEM((XS_SLOTS, EXP_TILE, XS_COLS), BF16),
                        pltpu.SemaphoreType.DMA((XS_SLOTS,))])
    return pl.pallas_call(
        _expert_kernel, grid_spec=grid_spec,
        out_shape=jax.ShapeDtypeStruct((xs.shape[0], D_MODEL), BF16),
        compiler_params=pltpu.CompilerParams(dimension_semantics=("arbitrary",), vmem_limit_bytes=VMEM_LIMIT),
        name="moe_experts",
    )(exp_of_tile, misc, xs, wg, wu, wd)


def _combine_kernel(dst16_ref, n16_ref, lofs16_ref, ntot_ref, hp_ref, hs_ref, pp_ref, ps_ref, pos_ref, y_hbm,
                    g_ple_ref, w_pg_ref, w_pp_ref, g_fin_ref, *rest, final, npt, n_state):
    if final:
        st_in, (op_ref, os_ref, *st_out, ybuf, wpg_s, wpp_s, sem) = rest[:n_state], rest[n_state:]
        depth = n_state // len(st_out)
        for k, out in enumerate(st_out):
            for l in range(depth):
                out[l] = st_in[k * depth + l][...]
    else:
        o_ref, ybuf, wpg_s, wpp_s, sem = rest
    t = pl.program_id(0)
    nt = pl.num_programs(0)
    slot = t % 2

    def fetch(tile, sl):
        _slab_copies(y_hbm, ybuf.at[sl], sem.at[sl], tile, dst16_ref, lofs16_ref, n16_ref)

    @pl.when(t == 0)
    def _():
        ybuf[...] = jnp.zeros_like(ybuf)
        fetch(t, slot)
        wpg_s[...] = w_pg_ref[...].astype(BF16)
        wpp_s[...] = w_pp_ref[...].astype(BF16)

    @pl.when(t + 1 < nt)
    def _():
        fetch(t + 1, 1 - slot)

    _wait_slabs(y_hbm, ybuf.at[slot], sem.at[slot], ntot_ref[t])

    pos = pos_ref[...]
    cols = _iota_f32((TOK_TILE, CBUF_ROWS), 1)
    pick = jnp.where((cols == pos[:, 0:1]) | (cols == pos[:, 1:2]), 1.0, 0.0).astype(BF16)
    h2 = _pick(t, npt, hp_ref, hs_ref) + jnp.dot(pick, ybuf[slot], preferred_element_type=F32)
    xn2 = _rmsnorm(h2, g_ple_ref[...])
    gate = _sigmoid(_dot(xn2, wpg_s[...]))
    p = jnp.where(t < npt, pp_ref[...], ps_ref[...])
    h3 = h2 + gate * _dot(p, wpp_s[...])
    if final:
        h3 = _rmsnorm(h3, g_fin_ref[...])

        @pl.when(t < npt)
        def _():
            op_ref[...] = h3

        @pl.when(t >= npt)
        def _():
            os_ref[...] = h3
    else:
        o_ref[...] = h3


def _state_specs(arrs, axis, nt):
    shape = arrs[0].shape
    unit = SUBLANES if axis == len(shape) - 2 else 1
    per = -(-shape[axis] // (nt * unit)) * unit
    assert shape[axis] % per == 0, (shape, axis, nt)
    n_blocks = shape[axis] // per
    block = shape[:axis] + (per,) + shape[axis + 1:]

    def in_map(t, *_):
        return (0,) * axis + (jnp.minimum(t, n_blocks - 1),) + (0,) * (len(shape) - axis - 1)

    return [pl.BlockSpec(block, in_map)] * len(arrs), pl.BlockSpec((len(arrs),) + block, lambda t, *_: (0,) + in_map(t))


def _combine_call(plan, hp, hs, pp, ps, pos, y, l, P, g_final, *, states=None):
    final = states is not None
    st_in = [a for arrs, _ in states for a in arrs] if final else []
    npt = hp.shape[0] // TOK_TILE
    nst = hs.shape[0] // TOK_TILE
    nt = npt + nst

    def lmap(t, *_):
        return (l, 0, 0)

    in_specs = _pair_specs(npt, D_MODEL) + [
        pl.BlockSpec((None, TOK_TILE, PLE_DIM), lambda t, *_: (l, jnp.minimum(t, npt - 1), 0)),
        pl.BlockSpec((None, TOK_TILE, PLE_DIM), lambda t, *_: (l, jnp.maximum(t - npt, 0), 0)),
        pl.BlockSpec((TOK_TILE, LANES), lambda t, *_: (t, 0)),
        pl.BlockSpec(memory_space=pl.ANY),
        pl.BlockSpec((None,) + P["g_ple"].shape[1:], lmap),
        pl.BlockSpec((None,) + P["w_ple_gate"].shape[1:], lmap),
        pl.BlockSpec((None,) + P["w_ple_proj"].shape[1:], lmap),
        pl.BlockSpec(g_final.shape, lambda t, *_: (0, 0))]
    if final:
        st_specs = [_state_specs(arrs, axis, nt) for arrs, axis in states]
        in_specs += [s for ins, _ in st_specs for s in ins]
        out_specs = (pl.BlockSpec((TOK_TILE, D_MODEL), lambda t, *_: (jnp.minimum(t, npt - 1), 0)),
                     pl.BlockSpec((TOK_TILE, D_MODEL), lambda t, *_: (jnp.maximum(t - npt, 0), 0)),
                     ) + tuple(out for _, out in st_specs)
        out_shape = (jax.ShapeDtypeStruct(hp.shape, F32), jax.ShapeDtypeStruct(hs.shape, F32),
                     ) + tuple(jax.ShapeDtypeStruct((len(arrs),) + arrs[0].shape, F32) for arrs, _ in states)
    else:
        out_specs = pl.BlockSpec((TOK_TILE, D_MODEL), lambda t, *_: (t, 0))
        out_shape = jax.ShapeDtypeStruct((nt * TOK_TILE, D_MODEL), F32)
    grid_spec = pltpu.PrefetchScalarGridSpec(
        num_scalar_prefetch=4, grid=(nt,), in_specs=in_specs, out_specs=out_specs,
        scratch_shapes=[pltpu.VMEM((2, CBUF_ROWS, D_MODEL), BF16), pltpu.VMEM((D_MODEL, D_MODEL), BF16),
                        pltpu.VMEM((PLE_DIM, D_MODEL), BF16), pltpu.SemaphoreType.DMA((2,))])
    return pl.pallas_call(
        functools.partial(_combine_kernel, final=final, npt=npt, n_state=len(st_in)),
        grid_spec=grid_spec, out_shape=out_shape,
        compiler_params=pltpu.CompilerParams(dimension_semantics=("arbitrary",), vmem_limit_bytes=VMEM_LIMIT),
        name="moe_combine_final" if final else "moe_combine",
    )(*plan[:4], hp, hs, pp, ps, pos, y, P["g_ple"], P["w_ple_gate"], P["w_ple_proj"], g_final, *st_in)


def _ffn(hp, hs, pp, ps, l, P, g_final, *, states=None):
    n = hp.shape[0] + hs.shape[0]
    nt = n // TOK_TILE
    bound = TOP_K * n + nt * N_EXPERTS * CHUNK + N_EXPERTS * (EXP_TILE - 1)
    n_sorted = -(-bound // EXP_TILE) * EXP_TILE
    pos, cnt = _route_call(hp, hs, l, P)
    plan = _chunk_plan(cnt[:, 0, :N_EXPERTS].astype(jnp.int32), n_sorted // EXP_TILE)
    xs = _dispatch_call(plan, hp, hs, pos, l, P, n_sorted)
    y = _expert_call(plan, xs, l, P["wg"], P["wu"], P["wd"])
    return _combine_call(plan, hp, hs, pp, ps, pos, y, l, P, g_final, states=states)


def _router_weights(g_ffn, w_grp_router, b_grp_router, w_exp_router, b_exp_router):
    depth = w_grp_router.shape[0]
    n_pad = LANES - N_EXPERTS - N_GROUPS
    w_er = jnp.transpose(w_exp_router, (0, 2, 1, 3)).reshape(depth, D_MODEL, N_EXPERTS)
    w_rt = jnp.concatenate([w_er, w_grp_router, jnp.zeros((depth, D_MODEL, n_pad), F32)], axis=2) * g_ffn[:, :, None]
    b_rt = jnp.concatenate([b_exp_router.reshape(depth, N_EXPERTS), b_grp_router, jnp.zeros((depth, n_pad), F32)], axis=1)
    w_hi = w_rt.astype(BF16)
    w_lo = (w_rt - w_hi.astype(F32)).astype(BF16)
    return jnp.concatenate([w_hi, w_lo], axis=2), b_rt.reshape(depth, 1, LANES)


def kernel(x_prompt, x_sample, state_gla, state_conv, p_prompt, p_sample, g_mix, w_in, w_forget_up, b_forget,
           g_gla_out, w_conv, b_conv, g_conv_ln, b_conv_ln, w_out, g_ffn, w_grp_router, b_grp_router,
           w_exp_router, b_exp_router, w_exp_gate, w_exp_up, w_exp_down, g_ple, w_ple_gate, w_ple_proj, g_final):
    depth = w_in.shape[0]
    nbp, seq_p, _ = x_prompt.shape
    nbs, seq_s, _ = x_sample.shape
    n_p = nbp * seq_p
    n_s = nbs * seq_s

    def rows(v):
        return v.reshape(depth, 1, -1)

    P = {
        "g_mix": rows(g_mix), "w_in_t": jnp.swapaxes(w_in, 1, 2),
        "w_forget_up": w_forget_up, "b_forget": rows(b_forget), "g_gla_out": rows(g_gla_out),
        "w_conv": w_conv, "b_conv": rows(b_conv), "g_conv_ln": rows(g_conv_ln), "b_conv_ln": rows(b_conv_ln),
        "w_out": w_out, "g_ffn": rows(g_ffn), "g_ple": rows(g_ple),
        "w_ple_gate": w_ple_gate, "w_ple_proj": w_ple_proj,
        "wg": w_exp_gate.reshape(depth * N_EXPERTS, D_MODEL, EXPERT_FF),
        "wu": w_exp_up.reshape(depth * N_EXPERTS, D_MODEL, EXPERT_FF),
        "wd": w_exp_down.reshape(depth * N_EXPERTS, EXPERT_FF, D_MODEL),
    }
    P["w_rt"], P["b_rt"] = _router_weights(g_ffn, w_grp_router, b_grp_router, w_exp_router, b_exp_router)
    g_fin = g_final.reshape(1, -1)
    xp = x_prompt.reshape(n_p, D_MODEL)
    xs = x_sample.reshape(n_s, D_MODEL)
    pp = p_prompt.reshape(depth, n_p, PLE_DIM)
    ps = p_sample.reshape(depth, n_s, PLE_DIM)
    s_in = state_gla.reshape(depth, nbs, QK_COLS, GLA_DV)
    c_in_t = jnp.swapaxes(state_conv, 1, 2)

    h = None
    sg_p, sg_s, sc_p, sc_s = [], [], [], []
    for l in range(depth):
        src_p, src_s, off_s = (xp, xs, 0) if l == 0 else (h, h, n_p)
        (hp, sgp, scp), (hs, sgs, scs) = _guarded(
            functools.partial(_mixer_prompt, src_p, 0, nbp, seq_p, l, P),
            functools.partial(_mixer_sample, src_s, off_s, nbs, seq_s, l, s_in, c_in_t, P))
        sg_p.append(sgp)
        sg_s.append(sgs)
        sc_p.append(scp)
        sc_s.append(scs)
        h = _ffn(hp, hs, pp, ps, l, P, g_fin, states=((sg_p, 0), (sg_s, 0), (sc_s, 1)) if l == depth - 1 else None)

    y_prompt, y_sample, sg_p_all, sg_s_all, sc_s_all = h
    return (y_prompt.reshape(nbp, seq_p, D_MODEL), y_sample.reshape(nbs, seq_s, D_MODEL),
            sg_p_all.reshape(depth, nbp, GLA_HEADS, GLA_DK, GLA_DV), sg_s_all.reshape(depth, nbs, GLA_HEADS, GLA_DK, GLA_DV),
            jnp.stack(sc_p), jnp.swapaxes(sc_s_all, 1, 2))
```

```python
import functools

import jax
import jax.numpy as jnp
from jax import lax
from jax.experimental import pallas as pl
from jax.experimental.pallas import tpu as pltpu

D_MODEL = 1024
GLA_HEADS = 4
GLA_DK = 64
GLA_DV = 128
QK_COLS = GLA_HEADS * GLA_DK
V_COLS = GLA_HEADS * GLA_DV
CONV_WIDTH = 512
CONV_K = 31
GLA_LOWRANK = 16
GLA_TAU = 16.0
GLA_CHUNK = 64
PLE_DIM = 256
N_GROUPS = 4
EXPERTS_PER_GROUP = 8
N_EXPERTS = N_GROUPS * EXPERTS_PER_GROUP
EXPERT_FF = 256
TOP_K = 2
EPS = 1e-6
N_MAIN = 2 * QK_COLS + 2 * V_COLS

LANES = 128
SUBLANES = 8
CONV_PAD = 32
CONV_OFF = CONV_PAD - (CONV_K - 1)
VMEM_LIMIT = 56 * 1024 * 1024
TOK_TILE = 512
CHUNK = 16
EXP_TILE = 512
CBUF_ROWS = -(-(TOP_K * TOK_TILE + N_EXPERTS * CHUNK) // LANES) * LANES
XS_SLOTS = 3
SORT_ROWS = 512
XS_COLS = D_MODEL + LANES
ROUTE_ROWS = 48
DECAY_SAFE = 80.0

F32 = jnp.float32
BF16 = jnp.bfloat16
HI = lax.Precision.HIGHEST


def _sigmoid(x):
    return 1.0 / (1.0 + jnp.exp(-x))


def _silu(x):
    return x * _sigmoid(x)


def _log_sigmoid(x):
    return jnp.minimum(x, 0.0) - jnp.log(1.0 + jnp.exp(-jnp.abs(x)))


def _rmsnorm(x, g):
    return x * lax.rsqrt(jnp.mean(x * x, axis=-1, keepdims=True) + EPS) * g


def _dot(a, b):
    return jnp.dot(a.astype(BF16), b.astype(BF16), preferred_element_type=F32)


def _dot_t(a, b):
    return lax.dot_general(a.astype(BF16), b.astype(BF16), (((1,), (1,)), ((), ())),
                           preferred_element_type=F32)


def _dot_hi(a, b):
    return jnp.dot(a, b, preferred_element_type=F32, precision=HI)


def _iota_f32(shape, dim):
    return lax.broadcasted_iota(jnp.int32, shape, dim).astype(F32)


def _const_spec(shape):
    nd = len(shape)
    return pl.BlockSpec(shape, lambda *_: (0,) * nd)


def _layer_spec(arr, l):
    nd = arr.ndim - 1
    return pl.BlockSpec((None,) + arr.shape[1:], lambda *_: (l,) + (0,) * nd, pipeline_mode=pl.Buffered(1))


def _cast_mixer_weights(w_int_ref, w_fu_ref, w_out_ref, wmain_s, wlr_s, wcv_s, wfu_s, wout_s):
    blk = 4 * LANES
    for r in range(0, N_MAIN, blk):
        wmain_s[:, r:r + blk] = w_int_ref[r:r + blk, :].T.astype(BF16)
    lane = lax.broadcasted_iota(jnp.int32, (D_MODEL, LANES), 1)
    wlr_s[...] = jnp.where(lane < GLA_LOWRANK, w_int_ref[N_MAIN:N_MAIN + LANES, :].T, 0.0).astype(BF16)
    cv0 = N_MAIN + GLA_LOWRANK
    for r in range(0, 2 * CONV_WIDTH, blk):
        wcv_s[:, r:r + blk] = w_int_ref[cv0 + r:cv0 + r + blk, :].T.astype(BF16)
    wfu_s[...] = jnp.zeros_like(wfu_s)
    wfu_s[0:GLA_LOWRANK, :] = w_fu_ref[...].astype(BF16)
    wout_s[...] = w_out_ref[...].astype(BF16)


def _project(x, g_mix, wmain_s, wlr_s, wfu_s, b_f, wcv_s):
    xn = _rmsnorm(x, g_mix).astype(BF16)
    cv = jnp.dot(xn, wcv_s[...], preferred_element_type=F32)
    u = cv[:, :CONV_WIDTH] * _sigmoid(cv[:, CONV_WIDTH:])
    lr = jnp.dot(xn, wlr_s[...], preferred_element_type=F32)
    zf = _dot(lr, wfu_s[...]) + b_f
    la = _log_sigmoid(zf) * (1.0 / GLA_TAU)
    zqk = jnp.dot(xn, wmain_s[:, :2 * QK_COLS], preferred_element_type=F32)
    q = zqk[:, :QK_COLS] * (GLA_DK ** -0.5)
    k = zqk[:, QK_COLS:]
    zvo = jnp.dot(xn, wmain_s[:, 2 * QK_COLS:], preferred_element_type=F32)
    v = zvo[:, :V_COLS]
    og = zvo[:, V_COLS:]
    return q, k, v, og, la, u


def _stack_heads(qd):
    lane = lax.broadcasted_iota(jnp.int32, qd.shape, 1)
    return jnp.concatenate(
        [jnp.where((lane >= h * GLA_DK) & (lane < (h + 1) * GLA_DK), qd, 0.0) for h in range(GLA_HEADS)],
        axis=0)


def _gated_head_norm(o, og, g_gla):
    outs = []
    for h in range(GLA_HEADS):
        sl = slice(h * GLA_DV, (h + 1) * GLA_DV)
        outs.append(_rmsnorm(o[:, sl], g_gla) * _silu(og[:, sl]))
    return jnp.concatenate(outs, axis=1)


def _causal_conv(win, w_conv, b_conv, n):
    acc = jnp.broadcast_to(b_conv, (n, CONV_WIDTH))
    for s in range(SUBLANES):
        taps = [j for j in range(CONV_K) if (CONV_OFF + j) % SUBLANES == s]
        if not taps:
            continue
        rows = n if s == 0 else n + SUBLANES
        part = None
        for j in taps:
            a = (CONV_OFF + j) - s
            term = w_conv[j:j + 1, :] * win[a:a + rows, :]
            part = term if part is None else part + term
        acc = acc + part[s:s + n, :]
    return acc


def _conv_ln_act(acc, g_ln, b_ln):
    mu = jnp.mean(acc, axis=-1, keepdims=True)
    xc = acc - mu
    y = xc * lax.rsqrt(jnp.mean(xc * xc, axis=-1, keepdims=True) + EPS) * g_ln + b_ln
    return _silu(y)


def _head_diag(upd):
    return jnp.concatenate([upd[h * GLA_DK:(h + 1) * GLA_DK, h * GLA_DV:(h + 1) * GLA_DV]
                            for h in range(GLA_HEADS)], axis=0)


def _mixer_weight_args(l, P):
    names = ("g_mix", "w_in_t", "w_forget_up", "b_forget", "g_gla_out", "w_conv", "b_conv", "g_conv_ln",
             "b_conv_ln", "w_out")
    arrs = [P[n] for n in names]
    return arrs, [_layer_spec(a, l) for a in arrs]


_MIXER_WEIGHT_SCRATCH = [pltpu.VMEM((D_MODEL, N_MAIN), BF16), pltpu.VMEM((D_MODEL, LANES), BF16),
                         pltpu.VMEM((D_MODEL, 2 * CONV_WIDTH), BF16), pltpu.VMEM((LANES, QK_COLS), BF16),
                         pltpu.VMEM((D_MODEL, D_MODEL), BF16)]


def _mixer_prompt_kernel(x_ref, g_mix_ref, w_int_ref, w_fu_ref, b_f_ref, g_gla_ref, w_conv_ref,
                         b_conv_ref, g_ln_ref, b_ln_ref, w_out_ref,
                         h_ref, sg_ref, sc_ref, risk_ref,
                         wmain_s, wlr_s, wcv_s, wfu_s, wout_s,
                         s_ref, ubuf_ref, qs_ref, kd_ref, klt_ref, dec_ref, mid_ref, v_ref, og_ref, mix_ref,
                         *stable_refs, tt, stable):
    t = pl.program_id(1)
    nt = pl.num_programs(1)
    C = GLA_CHUNK
    n_chunks = tt // C

    @pl.when((pl.program_id(0) == 0) & (t == 0))
    def _():
        _cast_mixer_weights(w_int_ref, w_fu_ref, w_out_ref, wmain_s, wlr_s, wcv_s, wfu_s, wout_s)

    @pl.when(t == 0)
    def _():
        s_ref[...] = jnp.zeros_like(s_ref)
        ubuf_ref[0:CONV_PAD, :] = jnp.zeros((CONV_PAD, CONV_WIDTH), F32)

    x = x_ref[...]
    q, k, v, og, la, u = _project(x, g_mix_ref[...], wmain_s, wlr_s, wfu_s, b_f_ref[...], wcv_s)
    ubuf_ref[CONV_PAD:CONV_PAD + tt, :] = u
    v_ref[...] = v.astype(BF16)
    og_ref[...] = _silu(og)

    w_conv = w_conv_ref[...]
    b_conv = b_conv_ref[...]
    g_ln = g_ln_ref[...]
    b_ln = b_ln_ref[...]
    row = lax.broadcasted_iota(jnp.int32, (C, C), 0)
    col = lax.broadcasted_iota(jnp.int32, (C, C), 1)
    tri = (col <= row).astype(F32)
    risk = jnp.zeros((1, QK_COLS), F32)
    for c in range(n_chunks):
        rows = slice(c * C, (c + 1) * C)
        win = ubuf_ref[c * C:c * C + C + CONV_PAD, :]
        cact = _conv_ln_act(_causal_conv(win, w_conv, b_conv, C), g_ln, b_ln)
        mix_ref[rows, V_COLS:] = cact.astype(BF16)
        bits = pltpu.bitcast(cact[C - SUBLANES:C, 0:QK_COLS], jnp.uint32)
        zero = pltpu.bitcast(lax.shift_right_logical(lax.shift_right_logical(bits, jnp.uint32(16)), jnp.uint32(16)),
                             F32)[0:1, :]
        qc = q[rows, :] + zero
        b = _dot_hi(tri, la[rows, :])
        b_last = b[C - 1:C, :]
        if stable:
            q_st, k_st, b_st = stable_refs
            q_st[rows, :] = qc
            k_st[rows, :] = k[rows, :]
            b_st[rows, :] = b
            b_mid = jnp.zeros_like(b_last)
        else:
            b_mid = b[C // 2 - 1:C // 2, :]
            risk = jnp.maximum(risk, jnp.maximum(-b_mid, b_mid - b_last))
        qs_ref[c] = _stack_heads(qc * jnp.exp(b - b_mid)).astype(BF16)
        kd_ref[rows, :] = (k[rows, :] * jnp.exp(jnp.minimum(b_mid - b, DECAY_SAFE))).astype(BF16)
        kl = k[rows, :] * jnp.exp(b_last - b)
        klt = jnp.concatenate([kl, jnp.broadcast_to(jnp.exp(b_last), (C // 2, QK_COLS)),
                               jnp.broadcast_to(jnp.exp(b_mid), (C // 2, QK_COLS))], axis=0).T
        klt_ref[c] = klt.astype(BF16)
        dec_ref[c] = jnp.broadcast_to(klt[:, C:C + 1], (QK_COLS, GLA_DV))
        mid_ref[c] = jnp.broadcast_to(klt[:, 3 * C // 2:3 * C // 2 + 1], (QK_COLS, GLA_DV))

    r4 = lax.broadcasted_iota(jnp.int32, (GLA_HEADS * C, C), 0)
    c4 = lax.broadcasted_iota(jnp.int32, (GLA_HEADS * C, C), 1)
    causal4 = c4 <= (r4 % C)
    g_gla = g_gla_ref[...]
    s = s_ref[...]
    for c in range(n_chunks):
        rows = slice(c * C, (c + 1) * C)
        qs = qs_ref[c]
        vc = v_ref[rows, :]
        if stable:
            scores = _direct_scores(*stable_refs, c * C, C).astype(BF16)
        else:
            scores = jnp.where(causal4, _dot_t(qs, kd_ref[rows, :]), 0.0).astype(BF16)
        o_inter = jnp.dot(qs, (mid_ref[c] * s).astype(BF16), preferred_element_type=F32)
        upd = jnp.dot(klt_ref[c][:, :C], vc, preferred_element_type=F32)
        s = dec_ref[c] * s + _head_diag(upd)
        o_parts = []
        for h in range(GLA_HEADS):
            vh = vc[:, h * GLA_DV:(h + 1) * GLA_DV]
            o_parts.append(jnp.dot(scores[h * C:(h + 1) * C, :], vh, preferred_element_type=F32)
                           + o_inter[h * C:(h + 1) * C, :])
        o = jnp.concatenate(o_parts, axis=1)
        gated = []
        for h in range(GLA_HEADS):
            sl = slice(h * GLA_DV, (h + 1) * GLA_DV)
            gated.append(_rmsnorm(o[:, sl], g_gla) * og_ref[rows, sl])
        mix_ref[rows, 0:V_COLS] = jnp.concatenate(gated, axis=1).astype(BF16)
    s_ref[...] = s

    h_ref[...] = x + jnp.dot(mix_ref[...], wout_s[...], preferred_element_type=F32)
    tail = ubuf_ref[tt:tt + CONV_PAD, :]
    ubuf_ref[0:CONV_PAD, :] = tail

    risk_ref[0] = jnp.broadcast_to(jnp.max(risk, axis=-1, keepdims=True), (SUBLANES, LANES))

    @pl.when(t == nt - 1)
    def _():
        sg_ref[0] = s
        sc_ref[0] = tail[CONV_OFF:, :]


def _direct_scores(q_st, k_st, b_st, r0, C):
    kc = k_st[r0:r0 + C, :]
    bc = b_st[r0:r0 + C, :]
    srow = lax.broadcasted_iota(jnp.int32, (C, QK_COLS), 0)
    lane = lax.broadcasted_iota(jnp.int32, (C, LANES), 1)
    head_of = lax.broadcasted_iota(jnp.int32, (QK_COLS, LANES), 0) // GLA_DK
    head_sum = (head_of == lax.broadcasted_iota(jnp.int32, (QK_COLS, LANES), 1)).astype(F32)

    def one_query(t, acc):
        d = b_st[pl.ds(r0 + t, 1), :] - bc
        w = jnp.where(srow <= t, jnp.exp(jnp.minimum(d, 0.0)), 0.0) * kc * q_st[pl.ds(r0 + t, 1), :]
        per_head = _dot_hi(w, head_sum)
        return tuple(jnp.where(lane == t, per_head[:, h:h + 1], acc[h]) for h in range(GLA_HEADS))

    acc = lax.fori_loop(0, C, one_query, tuple(jnp.zeros((C, LANES), F32) for _ in range(GLA_HEADS)))
    return jnp.concatenate([a.T[0:C, :] for a in acc], axis=0)


def _mixer_prompt(x, row_off, nb, seq, l, P, *, tt=512, stable=False):
    nt = seq // tt
    n_chunks = tt // GLA_CHUNK
    blk_off = row_off // tt
    weights, w_specs = _mixer_weight_args(l, P)
    in_specs = [pl.BlockSpec((tt, D_MODEL), lambda b, t: (blk_off + b * nt + t, 0))] + w_specs
    out_shape = (jax.ShapeDtypeStruct((nb * seq, D_MODEL), F32),
                 jax.ShapeDtypeStruct((nb, QK_COLS, GLA_DV), F32),
                 jax.ShapeDtypeStruct((nb, CONV_K - 1, CONV_WIDTH), F32),
                 jax.ShapeDtypeStruct((nb * nt, SUBLANES, LANES), F32))
    out_specs = (pl.BlockSpec((tt, D_MODEL), lambda b, t: (b * nt + t, 0)),
                 pl.BlockSpec((1, QK_COLS, GLA_DV), lambda b, t: (b, 0, 0)),
                 pl.BlockSpec((1, CONV_K - 1, CONV_WIDTH), lambda b, t: (b, 0, 0)),
                 pl.BlockSpec((1, SUBLANES, LANES), lambda b, t: (b * nt + t, 0, 0)))
    scratch = _MIXER_WEIGHT_SCRATCH + [
        pltpu.VMEM((QK_COLS, GLA_DV), F32),
        pltpu.VMEM((CONV_PAD + tt, CONV_WIDTH), F32),
        pltpu.VMEM((n_chunks, GLA_HEADS * GLA_CHUNK, QK_COLS), BF16), pltpu.VMEM((tt, QK_COLS), BF16),
        pltpu.VMEM((n_chunks, QK_COLS, 2 * GLA_CHUNK), BF16), pltpu.VMEM((n_chunks, QK_COLS, GLA_DV), F32),
        pltpu.VMEM((n_chunks, QK_COLS, GLA_DV), F32),
        pltpu.VMEM((tt, V_COLS), BF16), pltpu.VMEM((tt, V_COLS), F32),
        pltpu.VMEM((tt, D_MODEL), BF16)]
    if stable:
        scratch += [pltpu.VMEM((tt, QK_COLS), F32)] * 3
    return pl.pallas_call(
        functools.partial(_mixer_prompt_kernel, tt=tt, stable=stable),
        grid=(nb, nt), in_specs=in_specs, out_specs=out_specs, out_shape=out_shape,
        scratch_shapes=scratch,
        compiler_params=pltpu.CompilerParams(dimension_semantics=("arbitrary", "arbitrary"),
                                             vmem_limit_bytes=VMEM_LIMIT),
        name="mixer_prompt_stable" if stable else "mixer_prompt",
    )(x, *weights)


def _guarded(*mixers):
    fast = [mixer(stable=False) for mixer in mixers]
    risk = functools.reduce(jnp.maximum, [jnp.max(outs[-1]) for outs in fast])
    return lax.cond(risk > DECAY_SAFE, lambda: tuple(tuple(mixer(stable=True)[:-1]) for mixer in mixers),
                    lambda: tuple(tuple(outs[:-1]) for outs in fast))


def _mixer_sample_kernel(x_ref, s_in_ref, c_in_ref, g_mix_ref, w_int_ref, w_fu_ref, b_f_ref,
                         g_gla_ref, w_conv_ref, b_conv_ref, g_ln_ref, b_ln_ref, w_out_ref,
                         h_ref, sg_ref, sc_ref, risk_ref,
                         wmain_s, wlr_s, wcv_s, wfu_s, wout_s,
                         u4_ref, oi_ref, cacc4_ref, *, sb, seq, stable):
    R = sb * seq
    n_slabs = CONV_WIDTH // LANES

    @pl.when(pl.program_id(0) == 0)
    def _():
        _cast_mixer_weights(w_int_ref, w_fu_ref, w_out_ref, wmain_s, wlr_s, wcv_s, wfu_s, wout_s)

    x = x_ref[...]
    q, k, v, og, la, u = _project(x, g_mix_ref[...], wmain_s, wlr_s, wfu_s, b_f_ref[...], wcv_s)

    for kk in range(n_slabs):
        u4_ref[kk] = u[:, kk * LANES:(kk + 1) * LANES]
    full = [c_in_ref[j] for j in range(CONV_K - 1)]
    for t in range(seq):
        full.append(jnp.concatenate([u4_ref.at[kk][pl.ds(t, sb, stride=seq), :] for kk in range(n_slabs)], axis=1))
    w_conv = w_conv_ref[...]
    for t in range(seq):
        acc = jnp.broadcast_to(b_conv_ref[...], (sb, CONV_WIDTH))
        for j in range(CONV_K):
            acc = acc + w_conv[j:j + 1, :] * full[t + j]
        for kk in range(n_slabs):
            cacc4_ref.at[kk][pl.ds(t, sb, stride=seq), :] = acc[:, kk * LANES:(kk + 1) * LANES]
    for j in range(CONV_K - 1):
        sc_ref[j] = full[seq + j]

    row = lax.broadcasted_iota(jnp.int32, (R, R), 0)
    col = lax.broadcasted_iota(jnp.int32, (R, R), 1)
    same = (row // seq) == (col // seq)
    b = _dot_hi((same & (col <= row)).astype(F32), la)
    b_tot = _dot_hi(same.astype(F32), la)
    qd = q * jnp.exp(b)
    kl = k * jnp.exp(b_tot - b)
    qs = _stack_heads(qd)
    risk_ref[0] = jnp.broadcast_to(jnp.max(jnp.max(-b_tot, axis=-1, keepdims=True), axis=0, keepdims=True),
                                   (SUBLANES, LANES))
    if stable:
        head_of = lax.broadcasted_iota(jnp.int32, (QK_COLS, LANES), 0) // GLA_DK
        head_sum = (head_of == lax.broadcasted_iota(jnp.int32, (QK_COLS, LANES), 1)).astype(F32)
        t_in_seq = lax.broadcasted_iota(jnp.int32, (R, QK_COLS), 0) % seq
        parts = [jnp.zeros((R, R), F32) for _ in range(GLA_HEADS)]
        for d in range(seq):
            k_d = jnp.concatenate([jnp.zeros((d, QK_COLS), F32), k[:R - d, :]], axis=0) if d else k
            b_d = jnp.concatenate([jnp.zeros((d, QK_COLS), F32), b[:R - d, :]], axis=0) if d else b
            w = jnp.where(t_in_seq >= d, jnp.exp(jnp.minimum(b - b_d, 0.0)), 0.0) * k_d * q
            per_head = _dot_hi(w, head_sum)
            for h in range(GLA_HEADS):
                parts[h] = jnp.where(col == row - d, per_head[:, h:h + 1], parts[h])
        scores = jnp.concatenate(parts, axis=0)
    else:
        kd = k * jnp.exp(jnp.minimum(-b, DECAY_SAFE))
        r4 = lax.broadcasted_iota(jnp.int32, (GLA_HEADS * R, R), 0) % R
        c4 = lax.broadcasted_iota(jnp.int32, (GLA_HEADS * R, R), 1)
        mask4 = ((r4 // seq) == (c4 // seq)) & (c4 <= r4)
        scores = jnp.where(mask4, _dot_t(qs, kd), 0.0)

    klt = kl.T
    dect = jnp.exp(b_tot).T
    lane_h = lax.broadcasted_iota(jnp.int32, (GLA_DK, R), 1)
    lane_r = lax.broadcasted_iota(jnp.int32, (QK_COLS, R), 1)
    upd = []
    for h in range(GLA_HEADS):
        klt_h = klt[h * GLA_DK:(h + 1) * GLA_DK, :]
        lhs = jnp.concatenate([jnp.where((lane_h >= i * seq) & (lane_h < (i + 1) * seq), klt_h, 0.0)
                               for i in range(sb)], axis=0)
        upd.append(_dot(lhs, v[:, h * GLA_DV:(h + 1) * GLA_DV]))
    for i in range(sb):
        s_old = s_in_ref[i]
        qsel = jnp.concatenate([qs[h * R + i * seq:h * R + (i + 1) * seq, :] for h in range(GLA_HEADS)], axis=0)
        oi = _dot(qsel, s_old)
        for h in range(GLA_HEADS):
            oi_ref[h, i * seq:(i + 1) * seq, :] = oi[h * seq:(h + 1) * seq, :]
        smask = (lane_r >= i * seq) & (lane_r < (i + 1) * seq)
        dec = jnp.sum(jnp.where(smask, dect, 0.0), axis=1, keepdims=True) * (1.0 / seq)
        u_new = jnp.concatenate([upd[h][i * GLA_DK:(i + 1) * GLA_DK, :] for h in range(GLA_HEADS)], axis=0)
        sg_ref[i] = dec * s_old + u_new

    o_parts = []
    for h in range(GLA_HEADS):
        vh = v[:, h * GLA_DV:(h + 1) * GLA_DV]
        o_parts.append(_dot(scores[h * R:(h + 1) * R, :], vh) + oi_ref[h])
    o = jnp.concatenate(o_parts, axis=1)
    cacc = jnp.concatenate([cacc4_ref[kk] for kk in range(n_slabs)], axis=1)
    mix = jnp.concatenate([_gated_head_norm(o, og, g_gla_ref[...]),
                           _conv_ln_act(cacc, g_ln_ref[...], b_ln_ref[...])], axis=1)
    h_ref[...] = x + jnp.dot(mix.astype(BF16), wout_s[...], preferred_element_type=F32)


def _mixer_sample(x, row_off, nb, seq, l, s_in, c_in_t, P, *, sb=16, stable=False):
    R = sb * seq
    blk_off = row_off // R
    n_slabs = CONV_WIDTH // LANES
    weights, w_specs = _mixer_weight_args(l, P)
    in_specs = [pl.BlockSpec((R, D_MODEL), lambda i: (blk_off + i, 0)),
                pl.BlockSpec((None, sb, QK_COLS, GLA_DV), lambda i: (l, i, 0, 0)),
                pl.BlockSpec((None, CONV_K - 1, sb, CONV_WIDTH), lambda i: (l, 0, i, 0))] + w_specs
    out_shape = (jax.ShapeDtypeStruct((nb * seq, D_MODEL), F32),
                 jax.ShapeDtypeStruct((nb, QK_COLS, GLA_DV), F32),
                 jax.ShapeDtypeStruct((CONV_K - 1, nb, CONV_WIDTH), F32),
                 jax.ShapeDtypeStruct((nb // sb, SUBLANES, LANES), F32))
    out_specs = (pl.BlockSpec((R, D_MODEL), lambda i: (i, 0)),
                 pl.BlockSpec((sb, QK_COLS, GLA_DV), lambda i: (i, 0, 0)),
                 pl.BlockSpec((CONV_K - 1, sb, CONV_WIDTH), lambda i: (0, i, 0)),
                 pl.BlockSpec((1, SUBLANES, LANES), lambda i: (i, 0, 0)))
    scratch = _MIXER_WEIGHT_SCRATCH + [
        pltpu.VMEM((n_slabs, R, LANES), F32),
        pltpu.VMEM((GLA_HEADS, R, GLA_DV), F32),
        pltpu.VMEM((n_slabs, R, LANES), F32)]
    return pl.pallas_call(
        functools.partial(_mixer_sample_kernel, sb=sb, seq=seq, stable=stable),
        grid=(nb // sb,), in_specs=in_specs, out_specs=out_specs, out_shape=out_shape,
        scratch_shapes=scratch,
        compiler_params=pltpu.CompilerParams(dimension_semantics=("arbitrary",),
                                             vmem_limit_bytes=VMEM_LIMIT),
        name="mixer_sample_stable" if stable else "mixer_sample",
    )(x, s_in, c_in_t, *weights)


def _pair_specs(n_first_tiles, width):
    return [pl.BlockSpec((TOK_TILE, width), lambda t, *_: (jnp.minimum(t, n_first_tiles - 1), 0)),
            pl.BlockSpec((TOK_TILE, width), lambda t, *_: (jnp.maximum(t - n_first_tiles, 0), 0))]


def _pick(t, n_first_tiles, a_ref, b_ref):
    return jnp.where(t < n_first_tiles, a_ref[...], b_ref[...])


def _route(logits):
    row = lax.broadcasted_iota(jnp.int32, logits.shape, 0)
    row_f = row.astype(F32)
    neg = jnp.float32(-jnp.inf)
    big = jnp.float32(1e9)
    is_grp = (row >= N_EXPERTS) & (row < N_EXPERTS + N_GROUPS)
    gl = jnp.where(is_grp, logits, neg)
    gmax = jnp.max(gl, axis=0, keepdims=True)
    gidx = jnp.min(jnp.where(is_grp & (gl == gmax), row_f - N_EXPERTS, big), axis=0, keepdims=True)
    gsum = jnp.sum(jnp.where(is_grp, jnp.exp(gl - gmax), 0.0), axis=0, keepdims=True)
    g_w = 1.0 / gsum
    grp_of_row = jnp.floor(row_f * (1.0 / EXPERTS_PER_GROUP))
    in_grp = (row < N_EXPERTS) & (grp_of_row == gidx)
    ml = jnp.where(in_grp, logits, neg)
    v1 = jnp.max(ml, axis=0, keepdims=True)
    i1 = jnp.min(jnp.where(in_grp & (ml == v1), row_f, big), axis=0, keepdims=True)
    ml2 = jnp.where(row_f == i1, neg, ml)
    v2 = jnp.max(ml2, axis=0, keepdims=True)
    i2 = jnp.min(jnp.where(in_grp & (ml2 == v2), row_f, big), axis=0, keepdims=True)
    e2 = jnp.exp(v2 - v1)
    w1 = g_w / (1.0 + e2)
    w2 = g_w * e2 / (1.0 + e2)
    return i1, i2, w1, w2


def _route_kernel(hp_ref, hs_ref, w_rt_ref, b_rt_ref, pos_ref, cnt_ref, earlier_s, *, npt):
    T = TOK_TILE
    R = ROUTE_ROWS

    @pl.when(pl.program_id(0) == 0)
    def _():
        earlier_s[...] = (_iota_f32((T, T), 0) < _iota_f32((T, T), 1)).astype(BF16)

    x = _pick(pl.program_id(0), npt, hp_ref, hs_ref)
    scale = lax.rsqrt(jnp.mean(x * x, axis=-1, keepdims=True) + EPS)
    x_hi = x.astype(BF16)
    x_lo = (x - x_hi.astype(F32)).astype(BF16)
    w_split = w_rt_ref[...]
    both = jnp.dot(x_hi, w_split, preferred_element_type=F32)
    logits = ((both[:, :LANES] + both[:, LANES:] + jnp.dot(x_lo, w_split[:, :LANES], preferred_element_type=F32))
              * scale + b_rt_ref[...])
    i1, i2, w1, w2 = _route(logits.T[:R])
    row = _iota_f32((R, T), 0)
    a0 = (row == i1).astype(F32)
    a1 = (row == i2).astype(F32)
    a = a0 + a1
    cnt = jnp.sum(a, axis=1, keepdims=True)
    rank = jnp.dot(a.astype(BF16), earlier_s[...], preferred_element_type=F32)
    cnt_pad = jnp.maximum(jnp.ceil(cnt * (1.0 / CHUNK)), 1.0) * CHUNK
    no_rows = jnp.zeros((LANES - R, LANES), F32)
    before = (_iota_f32((LANES, LANES), 1) < _iota_f32((LANES, LANES), 0)).astype(F32)
    first = _dot_hi(before, jnp.concatenate([jnp.broadcast_to(cnt_pad, (R, LANES)), no_rows], axis=0))[:R, 0:1]
    base = first + rank
    pos0 = jnp.sum(a0 * base, axis=0, keepdims=True)
    pos1 = jnp.sum(a1 * base, axis=0, keepdims=True)
    r8 = _iota_f32((SUBLANES, T), 0)
    res = jnp.where(r8 == 0.0, pos0, jnp.where(r8 == 1.0, pos1, jnp.where(
        r8 == 2.0, w1, jnp.where(r8 == 3.0, w2, jnp.where(r8 == 4.0, i1, 0.0)))))
    pos_ref[...] = jnp.concatenate([res, jnp.zeros((LANES - SUBLANES, T), F32)], axis=0).T
    cnt_ref[0] = jnp.concatenate([jnp.broadcast_to(cnt, (R, LANES)), no_rows], axis=0).T[:SUBLANES]


def _route_call(hp, hs, l, P):
    npt = hp.shape[0] // TOK_TILE
    nt = npt + hs.shape[0] // TOK_TILE
    return pl.pallas_call(
        functools.partial(_route_kernel, npt=npt), grid=(nt,),
        in_specs=_pair_specs(npt, D_MODEL) + [_layer_spec(P[n], l) for n in ("w_rt", "b_rt")],
        scratch_shapes=[pltpu.VMEM((TOK_TILE, TOK_TILE), BF16)],
        out_specs=(pl.BlockSpec((TOK_TILE, LANES), lambda t: (t, 0)),
                   pl.BlockSpec((1, SUBLANES, LANES), lambda t: (t, 0, 0))),
        out_shape=(jax.ShapeDtypeStruct((nt * TOK_TILE, LANES), F32),
                   jax.ShapeDtypeStruct((nt, SUBLANES, LANES), F32)),
        compiler_params=pltpu.CompilerParams(dimension_semantics=("arbitrary",), vmem_limit_bytes=VMEM_LIMIT),
        name="moe_route",
    )(hp, hs, P["w_rt"], P["b_rt"])


def _chunk_plan(cnt, n_row_tiles):
    n16 = jnp.maximum((cnt + (CHUNK - 1)) // CHUNK, 1)
    lofs16 = jnp.cumsum(n16, axis=1) - n16
    tile_pref16 = jnp.cumsum(n16, axis=0) - n16
    tot16 = jnp.sum(n16, axis=0)
    per_tile = EXP_TILE // CHUNK
    seg16 = ((tot16 + per_tile - 1) // per_tile) * per_tile
    seg_end16 = jnp.cumsum(seg16)
    dst16 = (seg_end16 - seg16)[None, :] + tile_pref16
    n_tot = jnp.sum(n16, axis=1)
    gap16 = seg_end16 - seg16 + tot16
    gapn16 = seg16 - tot16
    tile_start16 = jnp.arange(n_row_tiles, dtype=jnp.int32) * per_tile
    n_valid = seg_end16[-1] // per_tile
    misc = n_valid.reshape(1)
    exp_of_tile = jnp.minimum(jnp.sum(seg_end16[None, :] <= tile_start16[:, None], axis=1), N_EXPERTS - 1)
    i32 = lambda a: a.astype(jnp.int32).reshape(-1)
    return (i32(dst16), i32(n16), i32(lofs16), i32(n_tot), i32(gap16), i32(gapn16), i32(misc), i32(exp_of_tile))


def _chunk_copy(src, dst, src_chunk, dst_chunk, sem, n_chunks=1):
    rows = n_chunks * CHUNK
    return pltpu.make_async_copy(src.at[pl.ds(pl.multiple_of(src_chunk * CHUNK, CHUNK), rows), :],
                                 dst.at[pl.ds(pl.multiple_of(dst_chunk * CHUNK, CHUNK), rows), :], sem)


def _slab_copies(src, dst, sem, tile, src_ofs_ref, dst_ofs_ref, n16_ref):
    for e in range(N_EXPERTS):
        k = tile * N_EXPERTS + e
        _chunk_copy(src, dst, src_ofs_ref[k], dst_ofs_ref[k], sem, n16_ref[k]).start()


def _wait_slabs(src, dst, sem, n_chunks):
    _chunk_copy(src, dst, 0, 0, sem, n_chunks).wait()


def _wait_fill(src, dst, sem, n_chunks):
    @pl.when(n_chunks > 0)
    def _():
        _wait_slabs(src, dst, sem, n_chunks)


def _tile_copy(src, dst, dst_tile, sem):
    return pltpu.make_async_copy(src, dst.at[pl.ds(pl.multiple_of(dst_tile * EXP_TILE, EXP_TILE), EXP_TILE), :], sem)


def _dispatch_kernel(dst16_ref, n16_ref, lofs16_ref, ntot_ref, gap16_ref, gapn16_ref, misc_ref,
                     hp_ref, hs_ref, pos_ref, g_ffn_ref, xs_hbm, cbuf, zbuf, sem, *, n_row_tiles, npt):
    t = pl.program_id(0)
    nt = pl.num_programs(0)
    slot = t % 2
    T = TOK_TILE
    n_tail = n_row_tiles - misc_ref[0]

    @pl.when(t == 0)
    def _():
        zbuf[...] = jnp.zeros_like(zbuf)
        for e in range(N_EXPERTS):
            g = gapn16_ref[e]

            @pl.when(g > 0)
            def _(e=e, g=g):
                _chunk_copy(zbuf, xs_hbm, 0, gap16_ref[e], sem.at[2], g).start()

        def fill_tile(i, carry):
            _tile_copy(zbuf, xs_hbm, misc_ref[0] + i, sem.at[2]).start()
            return carry
        lax.fori_loop(0, n_tail, fill_tile, 0)

    xn = _rmsnorm(_pick(t, npt, hp_ref, hs_ref), g_ffn_ref[...]).astype(BF16)
    pos = pos_ref[...]
    pos_t = pos.T
    lane = lax.broadcasted_iota(jnp.int32, (T, LANES), 1)
    extra = jnp.zeros((T, LANES), F32)
    for s in range(TOP_K):
        c = pos[:, 2 + s:3 + s]
        hi = c.astype(BF16).astype(F32)
        mid = (c - hi).astype(BF16).astype(F32)
        lo = c - hi - mid
        for j, piece in enumerate((hi, mid, lo)):
            extra = jnp.where(lane == 3 * s + j, piece, extra)
    extra = jnp.where(lane == 3 * TOP_K, pos[:, 4:5], extra)
    moved = jnp.concatenate([xn, extra.astype(BF16)], axis=1)
    for r0 in range(0, CBUF_ROWS, SORT_ROWS):
        rows = _iota_f32((SORT_ROWS, T), 0) + float(r0)
        onehot = jnp.where((rows == pos_t[0:1, :]) | (rows == pos_t[1:2, :]), 1.0, 0.0).astype(BF16)
        cbuf[slot, r0:r0 + SORT_ROWS] = jnp.dot(onehot, moved, preferred_element_type=F32).astype(BF16)

    src = cbuf.at[slot]
    _slab_copies(src, xs_hbm, sem.at[slot], t, lofs16_ref, dst16_ref, n16_ref)

    @pl.when(t > 0)
    def _():
        _wait_slabs(cbuf.at[1 - slot], xs_hbm, sem.at[1 - slot], ntot_ref[t - 1])

    @pl.when(t == nt - 1)
    def _():
        _wait_slabs(src, xs_hbm, sem.at[slot], ntot_ref[t])
        for e in range(N_EXPERTS):
            _wait_fill(zbuf, xs_hbm, sem.at[2], gapn16_ref[e])

        def wait_tile(_, carry):
            _tile_copy(zbuf, xs_hbm, 0, sem.at[2]).wait()
            return carry
        lax.fori_loop(0, n_tail, wait_tile, 0)


def _dispatch_call(plan, hp, hs, pos, l, P, n_sorted):
    npt = hp.shape[0] // TOK_TILE
    nt = npt + hs.shape[0] // TOK_TILE
    g_ffn = P["g_ffn"]
    grid_spec = pltpu.PrefetchScalarGridSpec(
        num_scalar_prefetch=7, grid=(nt,),
        in_specs=_pair_specs(npt, D_MODEL) + [
            pl.BlockSpec((TOK_TILE, LANES), lambda t, *_: (t, 0)),
            pl.BlockSpec((None,) + g_ffn.shape[1:], lambda t, *_: (l, 0, 0))],
        out_specs=pl.BlockSpec(memory_space=pl.ANY),
        scratch_shapes=[pltpu.VMEM((2, CBUF_ROWS, XS_COLS), BF16), pltpu.VMEM((EXP_TILE, XS_COLS), BF16),
                        pltpu.SemaphoreType.DMA((3,))])
    return pl.pallas_call(
        functools.partial(_dispatch_kernel, n_row_tiles=n_sorted // EXP_TILE, npt=npt), grid_spec=grid_spec,
        out_shape=jax.ShapeDtypeStruct((n_sorted, XS_COLS), BF16),
        compiler_params=pltpu.CompilerParams(dimension_semantics=("arbitrary",), vmem_limit_bytes=VMEM_LIMIT),
        name="moe_dispatch",
    )(*plan[:7], hp, hs, pos, g_ffn)


def _expert_kernel(eot_ref, misc_ref, xs_hbm, wg_hbm, wu_hbm, wd_hbm, y_ref, wg_s, wu_s, wd_s, xbuf, xsem,
                   wg_f, wu_f, wd_f, wsem, *, w_base):
    i = pl.program_id(0)
    n_valid = misc_ref[0]
    valid = i < n_valid

    def expert_weights(e):
        return [pltpu.make_async_copy(src.at[w_base + e], dst.at[e % 2], wsem.at[e % 2, k])
                for k, (src, dst) in enumerate(((wg_hbm, wg_f), (wu_hbm, wu_f), (wd_hbm, wd_f)))]

    def row_tile(tile):
        return pltpu.make_async_copy(xs_hbm.at[pl.ds(pl.multiple_of(tile * EXP_TILE, EXP_TILE), EXP_TILE), :],
                                     xbuf.at[tile % XS_SLOTS], xsem.at[tile % XS_SLOTS])

    @pl.when(i == 0)
    def _():
        for ahead in range(XS_SLOTS - 1):
            @pl.when(ahead < n_valid)
            def _(ahead=ahead):
                row_tile(ahead).start()
        for c in expert_weights(eot_ref[0]):
            c.start()

    @pl.when(i + (XS_SLOTS - 1) < n_valid)
    def _():
        row_tile(i + (XS_SLOTS - 1)).start()

    @pl.when(jnp.logical_not(valid))
    def _():
        y_ref[...] = jnp.zeros_like(y_ref)

    @pl.when(valid & ((i == 0) | (eot_ref[i] != eot_ref[jnp.maximum(i - 1, 0)])))
    def _():
        e = eot_ref[i]

        @pl.when(e + 1 < N_EXPERTS)
        def _():
            for c in expert_weights(e + 1):
                c.start()

        for c in expert_weights(e):
            c.wait()
        wg_s[...] = wg_f[e % 2].astype(BF16)
        wu_s[...] = wu_f[e % 2].astype(BF16)
        wd_s[...] = wd_f[e % 2].astype(BF16)

    @pl.when(valid)
    def _():
        row_tile(i).wait()
        xs = xbuf[i % XS_SLOTS]
        x = xs[:, :D_MODEL]
        ex = xs[:, D_MODEL:].astype(F32)
        lane = lax.broadcasted_iota(jnp.int32, ex.shape, 1)
        id0 = jnp.sum(jnp.where(lane == 3 * TOP_K, ex, 0.0), axis=-1, keepdims=True)
        first = id0 == eot_ref[i].astype(F32)
        mine = (first & (lane < 3)) | (jnp.logical_not(first) & (lane >= 3) & (lane < 3 * TOP_K))
        c = jnp.sum(jnp.where(mine, ex, 0.0), axis=-1, keepdims=True)
        hg = _silu(jnp.dot(x, wg_s[...], preferred_element_type=F32)) * jnp.dot(x, wu_s[...],
                                                                               preferred_element_type=F32)
        y_ref[...] = jnp.dot((hg * c).astype(BF16), wd_s[...], preferred_element_type=F32).astype(BF16)


def _expert_call(plan, xs, l, wg, wu, wd):
    misc, exp_of_tile = plan[6:]
    n_row_tiles = xs.shape[0] // EXP_TILE

    grid_spec = pltpu.PrefetchScalarGridSpec(
        num_scalar_prefetch=2, grid=(n_row_tiles,),
        in_specs=[pl.BlockSpec(memory_space=pl.ANY)] * 4,
        out_specs=pl.BlockSpec((EXP_TILE, D_MODEL), lambda i, eot, nv: (i, 0)),
        scratch_shapes=[pltpu.VMEM((D_MODEL, EXPERT_FF), BF16), pltpu.VMEM((D_MODEL, EXPERT_FF), BF16),
                        pltpu.VMEM((EXPERT_FF, D_MODEL), BF16), pltpu.VMEM((XS_SLOTS, EXP_TILE, XS_COLS), BF16),
                        pltpu.SemaphoreType.DMA((XS_SLOTS,)),
                        pltpu.VMEM((2, D_MODEL, EXPERT_FF), F32), pltpu.VMEM((2, D_MODEL, EXPERT_FF), F32),
                        pltpu.VMEM((2, EXPERT_FF, D_MODEL), F32), pltpu.SemaphoreType.DMA((2, 3))])
    return pl.pallas_call(
        functools.partial(_expert_kernel, w_base=l * N_EXPERTS), grid_spec=grid_spec,
        out_shape=jax.ShapeDtypeStruct((xs.shape[0], D_MODEL), BF16),
        compiler_params=pltpu.CompilerParams(dimension_semantics=("arbitrary",), vmem_limit_bytes=VMEM_LIMIT),
        name="moe_experts",
    )(exp_of_tile, misc, xs, wg, wu, wd)


def _combine_kernel(dst16_ref, n16_ref, lofs16_ref, ntot_ref, hp_ref, hs_ref, pp_ref, ps_ref, pos_ref, y_hbm,
                    g_ple_ref, w_pg_ref, w_pp_ref, g_fin_ref, *rest, final, npt, n_state):
    if final:
        st_in, (op_ref, os_ref, *st_out, ybuf, wpg_s, wpp_s, sem) = rest[:n_state], rest[n_state:]
        depth = n_state // len(st_out)
        for k, out in enumerate(st_out):
            for l in range(depth):
                out[l] = st_in[k * depth + l][...]
    else:
        o_ref, ybuf, wpg_s, wpp_s, sem = rest
    t = pl.program_id(0)
    nt = pl.num_programs(0)
    slot = t % 2

    def fetch(tile, sl):
        _slab_copies(y_hbm, ybuf.at[sl], sem.at[sl], tile, dst16_ref, lofs16_ref, n16_ref)

    @pl.when(t == 0)
    def _():
        ybuf[...] = jnp.zeros_like(ybuf)
        fetch(t, slot)
        wpg_s[...] = w_pg_ref[...].astype(BF16)
        wpp_s[...] = w_pp_ref[...].astype(BF16)

    @pl.when(t + 1 < nt)
    def _():
        fetch(t + 1, 1 - slot)

    _wait_slabs(y_hbm, ybuf.at[slot], sem.at[slot], ntot_ref[t])

    pos = pos_ref[...]
    cols = _iota_f32((TOK_TILE, CBUF_ROWS), 1)
    pick = jnp.where((cols == pos[:, 0:1]) | (cols == pos[:, 1:2]), 1.0, 0.0).astype(BF16)
    h2 = _pick(t, npt, hp_ref, hs_ref) + jnp.dot(pick, ybuf[slot], preferred_element_type=F32)
    xn2 = _rmsnorm(h2, g_ple_ref[...])
    gate = _sigmoid(_dot(xn2, wpg_s[...]))
    p = jnp.where(t < npt, pp_ref[...], ps_ref[...])
    h3 = h2 + gate * _dot(p, wpp_s[...])
    if final:
        h3 = _rmsnorm(h3, g_fin_ref[...])

        @pl.when(t < npt)
        def _():
            op_ref[...] = h3

        @pl.when(t >= npt)
        def _():
            os_ref[...] = h3
    else:
        o_ref[...] = h3


def _state_specs(arrs, axis, nt):
    shape = arrs[0].shape
    unit = SUBLANES if axis == len(shape) - 2 else 1
    per = -(-shape[axis] // (nt * unit)) * unit
    assert shape[axis] % per == 0, (shape, axis, nt)
    n_blocks = shape[axis] // per
    block = shape[:axis] + (per,) + shape[axis + 1:]

    def in_map(t, *_):
        return (0,) * axis + (jnp.minimum(t, n_blocks - 1),) + (0,) * (len(shape) - axis - 1)

    return [pl.BlockSpec(block, in_map)] * len(arrs), pl.BlockSpec((len(arrs),) + block, lambda t, *_: (0,) + in_map(t))


def _combine_call(plan, hp, hs, pp, ps, pos, y, l, P, g_final, *, states=None):
    final = states is not None
    st_in = [a for arrs, _ in states for a in arrs] if final else []
    npt = hp.shape[0] // TOK_TILE
    nst = hs.shape[0] // TOK_TILE
    nt = npt + nst

    def lmap(t, *_):
        return (l, 0, 0)

    in_specs = _pair_specs(npt, D_MODEL) + [
        pl.BlockSpec((None, TOK_TILE, PLE_DIM), lambda t, *_: (l, jnp.minimum(t, npt - 1), 0)),
        pl.BlockSpec((None, TOK_TILE, PLE_DIM), lambda t, *_: (l, jnp.maximum(t - npt, 0), 0)),
        pl.BlockSpec((TOK_TILE, LANES), lambda t, *_: (t, 0)),
        pl.BlockSpec(memory_space=pl.ANY),
        pl.BlockSpec((None,) + P["g_ple"].shape[1:], lmap),
        pl.BlockSpec((None,) + P["w_ple_gate"].shape[1:], lmap),
        pl.BlockSpec((None,) + P["w_ple_proj"].shape[1:], lmap),
        pl.BlockSpec(g_final.shape, lambda t, *_: (0, 0))]
    if final:
        st_specs = [_state_specs(arrs, axis, nt) for arrs, axis in states]
        in_specs += [s for ins, _ in st_specs for s in ins]
        out_specs = (pl.BlockSpec((TOK_TILE, D_MODEL), lambda t, *_: (jnp.minimum(t, npt - 1), 0)),
                     pl.BlockSpec((TOK_TILE, D_MODEL), lambda t, *_: (jnp.maximum(t - npt, 0), 0)),
                     ) + tuple(out for _, out in st_specs)
        out_shape = (jax.ShapeDtypeStruct(hp.shape, F32), jax.ShapeDtypeStruct(hs.shape, F32),
                     ) + tuple(jax.ShapeDtypeStruct((len(arrs),) + arrs[0].shape, F32) for arrs, _ in states)
    else:
        out_specs = pl.BlockSpec((TOK_TILE, D_MODEL), lambda t, *_: (t, 0))
        out_shape = jax.ShapeDtypeStruct((nt * TOK_TILE, D_MODEL), F32)
    grid_spec = pltpu.PrefetchScalarGridSpec(
        num_scalar_prefetch=4, grid=(nt,), in_specs=in_specs, out_specs=out_specs,
        scratch_shapes=[pltpu.VMEM((2, CBUF_ROWS, D_MODEL), BF16), pltpu.VMEM((D_MODEL, D_MODEL), BF16),
                        pltpu.VMEM((PLE_DIM, D_MODEL), BF16), pltpu.SemaphoreType.DMA((2,))])
    return pl.pallas_call(
        functools.partial(_combine_kernel, final=final, npt=npt, n_state=len(st_in)),
        grid_spec=grid_spec, out_shape=out_shape,
        compiler_params=pltpu.CompilerParams(dimension_semantics=("arbitrary",), vmem_limit_bytes=VMEM_LIMIT),
        name="moe_combine_final" if final else "moe_combine",
    )(*plan[:4], hp, hs, pp, ps, pos, y, P["g_ple"], P["w_ple_gate"], P["w_ple_proj"], g_final, *st_in)


def _ffn(hp, hs, pp, ps, l, P, g_final, *, states=None):
    n = hp.shape[0] + hs.shape[0]
    nt = n // TOK_TILE
    bound = TOP_K * n + nt * N_EXPERTS * CHUNK + N_EXPERTS * (EXP_TILE - 1)
    n_sorted = -(-bound // EXP_TILE) * EXP_TILE
    pos, cnt = _route_call(hp, hs, l, P)
    plan = _chunk_plan(cnt[:, 0, :N_EXPERTS].astype(jnp.int32), n_sorted // EXP_TILE)
    xs = _dispatch_call(plan, hp, hs, pos, l, P, n_sorted)
    y = _expert_call(plan, xs, l, P["wg"], P["wu"], P["wd"])
    return _combine_call(plan, hp, hs, pp, ps, pos, y, l, P, g_final, states=states)


def _router_weights(g_ffn, w_grp_router, b_grp_router, w_exp_router, b_exp_router):
    depth = w_grp_router.shape[0]
    n_pad = LANES - N_EXPERTS - N_GROUPS
    w_er = jnp.transpose(w_exp_router, (0, 2, 1, 3)).reshape(depth, D_MODEL, N_EXPERTS)
    w_rt = jnp.concatenate([w_er, w_grp_router, jnp.zeros((depth, D_MODEL, n_pad), F32)], axis=2) * g_ffn[:, :, None]
    b_rt = jnp.concatenate([b_exp_router.reshape(depth, N_EXPERTS), b_grp_router, jnp.zeros((depth, n_pad), F32)], axis=1)
    w_hi = w_rt.astype(BF16)
    w_lo = (w_rt - w_hi.astype(F32)).astype(BF16)
    return jnp.concatenate([w_hi, w_lo], axis=2), b_rt.reshape(depth, 1, LANES)


def kernel(x_prompt, x_sample, state_gla, state_conv, p_prompt, p_sample, g_mix, w_in, w_forget_up, b_forget,
           g_gla_out, w_conv, b_conv, g_conv_ln, b_conv_ln, w_out, g_ffn, w_grp_router, b_grp_router,
           w_exp_router, b_exp_router, w_exp_gate, w_exp_up, w_exp_down, g_ple, w_ple_gate, w_ple_proj, g_final):
    depth = w_in.shape[0]
    nbp, seq_p, _ = x_prompt.shape
    nbs, seq_s, _ = x_sample.shape
    n_p = nbp * seq_p
    n_s = nbs * seq_s

    def rows(v):
        return v.reshape(depth, 1, -1)

    P = {
        "g_mix": rows(g_mix), "w_in_t": jnp.swapaxes(w_in, 1, 2),
        "w_forget_up": w_forget_up, "b_forget": rows(b_forget), "g_gla_out": rows(g_gla_out),
        "w_conv": w_conv, "b_conv": rows(b_conv), "g_conv_ln": rows(g_conv_ln), "b_conv_ln": rows(b_conv_ln),
        "w_out": w_out, "g_ffn": rows(g_ffn), "g_ple": rows(g_ple),
        "w_ple_gate": w_ple_gate, "w_ple_proj": w_ple_proj,
        "wg": w_exp_gate.reshape(depth * N_EXPERTS, D_MODEL, EXPERT_FF),
        "wu": w_exp_up.reshape(depth * N_EXPERTS, D_MODEL, EXPERT_FF),
        "wd": w_exp_down.reshape(depth * N_EXPERTS, EXPERT_FF, D_MODEL),
    }
    P["w_rt"], P["b_rt"] = _router_weights(g_ffn, w_grp_router, b_grp_router, w_exp_router, b_exp_router)
    g_fin = g_final.reshape(1, -1)
    xp = x_prompt.reshape(n_p, D_MODEL)
    xs = x_sample.reshape(n_s, D_MODEL)
    pp = p_prompt.reshape(depth, n_p, PLE_DIM)
    ps = p_sample.reshape(depth, n_s, PLE_DIM)
    s_in = state_gla.reshape(depth, nbs, QK_COLS, GLA_DV)
    c_in_t = jnp.swapaxes(state_conv, 1, 2)

    h = None
    sg_p, sg_s, sc_p, sc_s = [], [], [], []
    for l in range(depth):
        src_p, src_s, off_s = (xp, xs, 0) if l == 0 else (h, h, n_p)
        (hp, sgp, scp), (hs, sgs, scs) = _guarded(
            functools.partial(_mixer_prompt, src_p, 0, nbp, seq_p, l, P),
            functools.partial(_mixer_sample, src_s, off_s, nbs, seq_s, l, s_in, c_in_t, P))
        sg_p.append(sgp)
        sg_s.append(sgs)
        sc_p.append(scp)
        sc_s.append(scs)
        h = _ffn(hp, hs, pp, ps, l, P, g_fin, states=((sg_p, 0), (sg_s, 0), (sc_s, 1)) if l == depth - 1 else None)

    y_prompt, y_sample, sg_p_all, sg_s_all, sc_s_all = h
    return (y_prompt.reshape(nbp, seq_p, D_MODEL), y_sample.reshape(nbs, seq_s, D_MODEL),
            sg_p_all.reshape(depth, nbp, GLA_HEADS, GLA_DK, GLA_DV), sg_s_all.reshape(depth, nbs, GLA_HEADS, GLA_DK, GLA_DV),
            jnp.stack(sc_p), jnp.swapaxes(sc_s_all, 1, 2))
```
